```python
import math
import jax, jax.numpy as jnp
from jax import lax
import numpy as np

D_MODEL = 2048
BATCH = 4
SEQ = 4096
DEPTH = 2

MEM_TOKENS = 256
EPS = 1e-6
ROPE_THETA = 500000.0
N_EVEN = (DEPTH + 1) // 2
N_ODD = DEPTH // 2

ML_HEADS = 4
ML_DV = D_MODEL // 2 // ML_HEADS
ML_DK = ML_DV // 2
ML_CHUNK = 64
ML_CONV = 4
ML_FORGET_BIAS = 3.0
RW_HEAD = 64
RW_HEADS = D_MODEL // 2 // RW_HEAD
RW_DIM = RW_HEADS * RW_HEAD
RW_LORA_W = 64
RW_LORA_A = 64
RW_LORA_G = 128
RW_LN_EPS = 64e-5
ML_SPLITS = (2 * ML_HEADS * ML_DK, ML_HEADS * ML_DV, ML_HEADS * ML_DV, 2 * ML_HEADS)
RW_SPLITS = (RW_DIM, RW_DIM, RW_DIM, RW_LORA_W, RW_LORA_A, RW_LORA_G)
ML_COLS = sum(ML_SPLITS)
RW_COLS = sum(RW_SPLITS)
AB_COLS = ML_COLS + RW_COLS

NSA_HEADS = 16
NSA_KV = 4
NSA_REP = NSA_HEADS // NSA_KV
NSA_HD = D_MODEL // NSA_HEADS
ROPE_DIM = NSA_HD // 4
CMP_BLOCK = 32
CMP_STRIDE = 16
SEL_BLOCK = 64
SEL_TOPK = 16
WINDOW = 512
NSA_QCHUNK = 32
NSA_SPLITS = (NSA_HEADS * NSA_HD,) + (NSA_KV * NSA_HD,) * 6 + (3 * NSA_HEADS,)
NSA_COLS = sum(NSA_SPLITS)

CA_HEADS = 4
CA_HD = D_MODEL // CA_HEADS

MOE_GROUPS = 8
MOE_PER_GROUP = 8
MOE_EXPERTS = MOE_GROUPS * MOE_PER_GROUP
MOE_TOPK = 2
MOE_FF = D_MODEL // 4
MOE_ROWS = 128

kernel_name = 'hybrid_mlstm_rwkv7_nsa_hmoe'

F32 = jnp.float32


def _split(x, sizes):
    return jnp.split(x, [int(s) for s in np.cumsum(sizes)[:-1]], axis=-1)


def _rmsnorm(x, w):
    xf = x.astype(F32)
    y = xf * lax.rsqrt(jnp.mean(xf * xf, axis=-1, keepdims=True) + EPS)
    return (y * w.astype(F32)).astype(x.dtype)


def _masked_softmax(s, mask):
    s = jnp.where(mask, s.astype(F32), -jnp.inf)
    m = jnp.max(s, axis=-1, keepdims=True)
    m = jnp.where(jnp.isfinite(m), m, 0.0)
    e = jnp.exp(s - m)
    d = jnp.sum(e, axis=-1, keepdims=True)
    return e / jnp.where(d > 0, d, 1.0)


def _rope(x, pos):
    half = ROPE_DIM // 2
    inv = 1.0 / (ROPE_THETA ** (jnp.arange(half, dtype=F32) / half))
    ang = pos.astype(F32)[:, None] * inv[None, :]
    shape = (ang.shape[0],) + (1,) * (x.ndim - 3) + (half,)
    cos = jnp.cos(ang).reshape(shape)
    sin = jnp.sin(ang).reshape(shape)
    xf = x.astype(F32)
    x1 = xf[..., :half]
    x2 = xf[..., half:ROPE_DIM]
    out = jnp.concatenate([x1 * cos - x2 * sin, x2 * cos + x1 * sin, xf[..., ROPE_DIM:]], axis=-1)
    return out.astype(x.dtype)


def _token_shift(p):
    return jnp.pad(p, ((0, 0), (1, 0), (0, 0)))[:, :-1]


def _causal_conv(x, w, b):
    k_len = w.shape[0]
    s_len = x.shape[1]
    xp = jnp.pad(x, ((0, 0), (k_len - 1, 0), (0, 0)))
    y = b
    for j in range(k_len):
        y = y + xp[:, j:j + s_len] * w[j]
    return y


def _mlstm(q, k, v, i_pre, f_pre):
    B, S, H, DK = q.shape
    DV = v.shape[-1]
    L = ML_CHUNK
    NC = S // L

    def chunks(t):
        t = t.astype(F32).reshape((B, NC, L, H) + t.shape[3:])
        return t.transpose((1, 0, 3, 2) + tuple(range(4, t.ndim)))

    qc = chunks(q) * (DK ** -0.5)
    kc = chunks(k)
    vc = chunks(v)
    ic = chunks(i_pre)
    lfc = chunks(jax.nn.log_sigmoid(f_pre.astype(F32)))
    causal = jnp.asarray(np.tril(np.ones((L, L), dtype=bool)))

    def step(carry, xs):
        C, n, m = carry
        q_, k_, v_, i_, lf = xs
        b = jnp.cumsum(lf, axis=-1)
        dmat = jnp.where(causal, b[..., :, None] - b[..., None, :] + i_[..., None, :], -jnp.inf)
        inter = b + m[..., None]
        m_row = jnp.maximum(inter, jnp.max(dmat, axis=-1))
        w_in = jnp.exp(dmat - m_row[..., None])
        w_st = jnp.exp(inter - m_row)
        s = jnp.einsum('bhjd,bhld->bhjl', q_, k_) * w_in
        num = w_st[..., None] * jnp.einsum('bhjd,bhde->bhje', q_, C) + jnp.einsum('bhjl,bhle->bhje', s, v_)
        den = w_st * jnp.einsum('bhjd,bhd->bhj', q_, n) + jnp.sum(s, axis=-1)
        h = num / jnp.maximum(jnp.abs(den), jnp.exp(-m_row))[..., None]
        b_last = b[..., -1]
        g_key = b_last[..., None] - b + i_
        m_new = jnp.maximum(b_last + m, jnp.max(g_key, axis=-1))
        wk = jnp.exp(g_key - m_new[..., None])
        decay = jnp.exp(b_last + m - m_new)
        C_new = decay[..., None, None] * C + jnp.einsum('bhld,bhle->bhde', k_ * wk[..., None], v_)
        n_new = decay[..., None] * n + jnp.einsum('bhl,bhld->bhd', wk, k_)
        return (C_new, n_new, m_new), h

    init = (jnp.zeros((B, H, DK, DV), F32), jnp.zeros((B, H, DK), F32), jnp.zeros((B, H), F32))
    _, hs = lax.scan(step, init, (qc, kc, vc, ic, lfc))
    return hs.transpose(1, 0, 3, 2, 4).reshape(B, S, H, DV)


def _rwkv7_scan(r, w, k, v, kk, a):
    def step(state, xs):
        r_t, w_t, k_t, v_t, kk_t, a_t = xs
        sk = jnp.einsum('bhvk,bhk->bhv', state, kk_t)
        state = (state * w_t[:, :, None, :] - sk[..., None] * (kk_t * a_t)[:, :, None, :]
                 + v_t[..., None] * k_t[:, :, None, :])
        return state, jnp.einsum('bhvk,bhk->bhv', state, r_t)

    B, S, H, N = r.shape
    xs = tuple(t.transpose(1, 0, 2, 3) for t in (r, w, k, v, kk, a))
    _, y = lax.scan(step, jnp.zeros((B, H, N, N), F32), xs)
    return y.transpose(1, 0, 2, 3)


def _ab_mixer(h, w_in, conv_w, conv_b, gate_b, mu, w0, w_up, a0, a_up, g_up, k_k, k_a, r_k, ln_w, ln_b, w_out):
    B, S, _ = h.shape
    p = h @ w_in
    ml_p, rw_p = p[..., :ML_COLS], p[..., ML_COLS:]
    qk, v_m, o_m, gif = _split(ml_p, ML_SPLITS)
    qk = jax.nn.silu(_causal_conv(qk, conv_w, conv_b))
    q_m, k_m = jnp.split(qk, 2, axis=-1)
    gif = (gif + gate_b).astype(F32)
    h_m = _mlstm(q_m.reshape(B, S, ML_HEADS, ML_DK), k_m.reshape(B, S, ML_HEADS, ML_DK),
                 v_m.reshape(B, S, ML_HEADS, ML_DV), gif[..., :ML_HEADS], gif[..., ML_HEADS:])
    y_m = jax.nn.sigmoid(o_m.astype(F32)) * h_m.reshape(B, S, ML_HEADS * ML_DV)
    rw_p = (rw_p + (_token_shift(rw_p) - rw_p) * mu).astype(F32)
    r, k, v, xw, xa, xg = _split(rw_p, RW_SPLITS)
    logw = -math.exp(-0.5) * jax.nn.sigmoid(w0 + jnp.tanh(xw) @ w_up)
    a = jax.nn.sigmoid(a0 + xa @ a_up)
    g = jax.nn.sigmoid(xg) @ g_up

    def heads(t):
        return t.reshape(B, S, RW_HEADS, RW_HEAD)

    kk = heads(k * k_k)
    kk = kk * lax.rsqrt(jnp.sum(kk * kk, axis=-1, keepdims=True) + 1e-12)
    k = k * (1.0 + (a - 1.0) * k_a)
    rh, kh, vh, ah = heads(r), heads(k), heads(v), heads(a)
    y = _rwkv7_scan(rh, jnp.exp(heads(logw)), kh, vh, kk, ah)
    y_mu = jnp.mean(y, axis=-1, keepdims=True)
    y_var = jnp.mean(jnp.square(y - y_mu), axis=-1, keepdims=True)
    yn = ((y - y_mu) * lax.rsqrt(y_var + RW_LN_EPS)).reshape(B, S, RW_DIM) * ln_w + ln_b
    bonus = (jnp.sum(rh * kh * r_k.reshape(RW_HEADS, RW_HEAD), axis=-1, keepdims=True) * vh).reshape(B, S, RW_DIM)
    y_r = (yn + bonus) * g
    y_cat = jnp.concatenate([y_m, y_r], axis=-1).astype(h.dtype)
    return y_cat @ w_out


def _nsa(h, w_in, gate_b, cmp_pos, cmp_w1, cmp_w2, w_out):
    B, S, _ = h.shape
    G, R, HD = NSA_KV, NSA_REP, NSA_HD
    QC = NSA_QCHUNK
    scale = HD ** -0.5
    pos = jnp.arange(S)
    q, kc, vc, ks, vs, kw, vw, gl = _split(h @ w_in, NSA_SPLITS)
    q = q.reshape(B, S, G, R, HD)

    def kvh(t):
        return t.reshape(B, S, G, HD)

    q_rot = _rope(q, pos)
    ks = _rope(kvh(ks), pos)
    kw = _rope(kvh(kw), pos)
    vs = kvh(vs)
    vw = kvh(vw)
    gates = jax.nn.sigmoid((gl + gate_b).astype(F32)).reshape(B, S, 3, G, R)

    n_cmp = (S - CMP_BLOCK) // CMP_STRIDE + 1
    cidx = np.arange(n_cmp)[:, None] * CMP_STRIDE + np.arange(CMP_BLOCK)[None, :]

    def compress(t, pe, w1, w2):
        blocks = t[:, cidx] + pe[None, None, :, None, :]
        return jax.nn.gelu(jnp.einsum('bjpgd,pde->bjge', blocks, w1)) @ w2

    k_cmp = compress(kvh(kc), cmp_pos[0], cmp_w1[0], cmp_w2[0])
    v_cmp = compress(kvh(vc), cmp_pos[1], cmp_w1[1], cmp_w2[1])
    cmp_end = jnp.asarray(cidx[:, -1])

    n_sel = S // SEL_BLOCK
    c0 = np.arange(n_cmp)[:, None] * CMP_STRIDE
    s0 = np.arange(n_sel)[None, :] * SEL_BLOCK
    ov = np.clip(np.minimum(c0 + CMP_BLOCK, s0 + SEL_BLOCK) - np.maximum(c0, s0), 0, None) / CMP_BLOCK
    ov = jnp.asarray(ov, dtype=F32)
    n_top = min(SEL_TOPK, n_sel)
    k_blk = ks.reshape(B, n_sel, SEL_BLOCK, G, HD).transpose(0, 3, 1, 2, 4)
    v_blk = vs.reshape(B, n_sel, SEL_BLOCK, G, HD).transpose(0, 3, 1, 2, 4)
    kw_pad = jnp.pad(kw, ((0, 0), (WINDOW, 0), (0, 0), (0, 0)))
    vw_pad = jnp.pad(vw, ((0, 0), (WINDOW, 0), (0, 0), (0, 0)))
    bi = jnp.arange(B)[:, None, None, None]
    gi = jnp.arange(G)[None, :, None, None]
    blk_ids = jnp.arange(n_sel)

    def chunk(c):
        t0 = c * QC
        qt = t0 + jnp.arange(QC)
        q_c = lax.dynamic_slice_in_dim(q, t0, QC, axis=1)
        qr_c = lax.dynamic_slice_in_dim(q_rot, t0, QC, axis=1)
        g_c = lax.dynamic_slice_in_dim(gates, t0, QC, axis=1)
        p_cmp = _masked_softmax(jnp.einsum('bqgrd,bjgd->bgrqj', q_c, k_cmp) * scale,
                                cmp_end[None, :] <= qt[:, None])
        o_cmp = jnp.einsum('bgrqj,bjgd->bqgrd', p_cmp.astype(v_cmp.dtype), v_cmp)
        imp = jnp.einsum('bgrqj,js->bgqs', p_cmp, ov)
        cur = (qt // SEL_BLOCK)[:, None]
        forced = (blk_ids[None, :] == 0) | (blk_ids[None, :] == cur) | (blk_ids[None, :] == cur - 1)
        score = jnp.where(forced, jnp.inf, jnp.where(blk_ids[None, :] <= cur, imp, -jnp.inf))
        top_v, top_i = lax.top_k(score, n_top)
        k_g = k_blk[bi, gi, top_i].reshape(B, G, QC, n_top * SEL_BLOCK, HD)
        v_g = v_blk[bi, gi, top_i].reshape(B, G, QC, n_top * SEL_BLOCK, HD)
        kpos = (top_i[..., None] * SEL_BLOCK + jnp.arange(SEL_BLOCK)).reshape(B, G, QC, n_top * SEL_BLOCK)
        kmask = jnp.repeat(top_v > -jnp.inf, SEL_BLOCK, axis=-1) & (kpos <= qt[None, None, :, None])
        p_slc = _masked_softmax(jnp.einsum('bqgrd,bgqkd->bgrqk', qr_c, k_g) * scale, kmask[:, :, None])
        o_slc = jnp.einsum('bgrqk,bgqkd->bqgrd', p_slc.astype(v_g.dtype), v_g)
        k_w = lax.dynamic_slice_in_dim(kw_pad, t0, QC + WINDOW, axis=1)
        v_w = lax.dynamic_slice_in_dim(vw_pad, t0, QC + WINDOW, axis=1)
        wpos = t0 - WINDOW + jnp.arange(QC + WINDOW)
        dist = qt[:, None] - wpos[None, :]
        wmask = (dist >= 0) & (dist < WINDOW) & (wpos[None, :] >= 0)
        p_win = _masked_softmax(jnp.einsum('bqgrd,bkgd->bgrqk', qr_c, k_w) * scale, wmask)
        o_win = jnp.einsum('bgrqk,bkgd->bqgrd', p_win.astype(v_w.dtype), v_w)
        return (g_c[:, :, 0, :, :, None] * o_cmp + g_c[:, :, 1, :, :, None] * o_slc
                + g_c[:, :, 2, :, :, None] * o_win)

    o = lax.map(chunk, jnp.arange(S // QC))
    o = o.transpose(1, 0, 2, 3, 4, 5).reshape(B, S, NSA_HEADS * HD).astype(h.dtype)
    return o @ w_out


def _cross_attn(h, m, wq, wk, wv, wo):
    B, S, D = h.shape
    M = m.shape[1]
    q = (h @ wq).reshape(B, S, CA_HEADS, CA_HD)
    k = (m @ wk).reshape(B, M, CA_HEADS, CA_HD)
    v = (m @ wv).reshape(B, M, CA_HEADS, CA_HD)
    s = jnp.einsum('bshd,bmhd->bhsm', q, k).astype(F32) * (CA_HD ** -0.5)
    p = jax.nn.softmax(s, axis=-1).astype(v.dtype)
    o = jnp.einsum('bhsm,bmhd->bshd', p, v).reshape(B, S, D)
    return o @ wo


def _hier_moe(h, wg, bg, we, be, w_gate, w_up, w_down):
    B, S, D = h.shape
    T = B * S
    A = T * MOE_TOPK
    xt = h.reshape(T, D)
    lg = (xt @ wg + bg).astype(F32)
    grp = jnp.argmax(lg, axis=-1)
    p_grp = jnp.take_along_axis(jax.nn.softmax(lg, axis=-1), grp[:, None], axis=-1)
    le = (xt @ we + be).astype(F32).reshape(T, MOE_GROUPS, MOE_PER_GROUP)
    le = jnp.take_along_axis(le, grp[:, None, None], axis=1)[:, 0]
    top_l, top_e = lax.top_k(le, MOE_TOPK)
    wts = (p_grp * jax.nn.softmax(top_l, axis=-1)).reshape(A)
    eid = (grp[:, None] * MOE_PER_GROUP + top_e).reshape(A)
    tok = jnp.repeat(jnp.arange(T, dtype=jnp.int32), MOE_TOPK)
    order = jnp.argsort(eid)
    e_s, tok_s, w_s = eid[order], tok[order], wts[order]
    counts = jax.ops.segment_sum(jnp.ones((A,), jnp.int32), eid, num_segments=MOE_EXPERTS)
    padded = (counts + MOE_ROWS - 1) // MOE_ROWS * MOE_ROWS
    ends = jnp.cumsum(padded)
    dest = (ends - padded)[e_s] + jnp.arange(A, dtype=jnp.int32) - (jnp.cumsum(counts) - counts)[e_s]
    n_chunks = -(-A // MOE_ROWS) + MOE_EXPERTS
    rows = n_chunks * MOE_ROWS
    row_tok = jnp.full((rows,), T, jnp.int32).at[dest].set(tok_s)
    row_w = jnp.zeros((rows,), F32).at[dest].set(w_s)
    chunk_e = jnp.minimum(jnp.searchsorted(ends, jnp.arange(n_chunks, dtype=jnp.int32) * MOE_ROWS, side='right'),
                          MOE_EXPERTS - 1)
    x_pad = jnp.concatenate([xt, jnp.zeros((1, D), xt.dtype)], axis=0)

    def expert_rows(args):
        t_idx, w_r, e = args
        xr = x_pad[t_idx]
        hid = jax.nn.silu(xr @ w_gate[e]) * (xr @ w_up[e])
        y = hid @ w_down[e]
        return y * w_r[:, None].astype(y.dtype)

    out = lax.map(expert_rows, (row_tok.reshape(n_chunks, MOE_ROWS), row_w.reshape(n_chunks, MOE_ROWS), chunk_e))
    y = jax.ops.segment_sum(out.reshape(rows, D), row_tok, num_segments=T + 1)[:T]
    return y.reshape(B, S, D).astype(h.dtype)


def setup_inputs(seed: int = 0) -> dict:
    key = jax.random.key(seed)
    ks = iter(jax.random.split(key, 48))
    D = D_MODEL

    def nrm(shape, scale):
        return scale * jax.random.normal(next(ks), shape, F32)

    def gain(shape):
        return 1.0 + nrm(shape, 0.05)

    gate_offset = jnp.concatenate([jnp.zeros((ML_HEADS,), F32), jnp.full((ML_HEADS,), ML_FORGET_BIAS, F32)])
    return {
        'x': nrm((BATCH, SEQ, D), 1.0),
        'mem': nrm((BATCH, MEM_TOKENS, D), 1.0),
        'norm_mix': gain((DEPTH, D)),
        'norm_cross': gain((DEPTH, D)),
        'norm_mem': gain((DEPTH, D)),
        'norm_ffn': gain((DEPTH, D)),
        'norm_final': gain((D,)),
        'ab_w_in': nrm((N_EVEN, D, AB_COLS), D ** -0.5),
        'ml_conv_w': nrm((N_EVEN, ML_CONV, 2 * ML_HEADS * ML_DK), 0.5),
        'ml_conv_b': nrm((N_EVEN, 2 * ML_HEADS * ML_DK), 0.02),
        'ml_gate_b': nrm((N_EVEN, 2 * ML_HEADS), 0.1) + gate_offset,
        'rw_mu': 0.5 + nrm((N_EVEN, RW_COLS), 0.15),
        'rw_w0': nrm((N_EVEN, RW_DIM), 0.5),
        'rw_w_up': nrm((N_EVEN, RW_LORA_W, RW_DIM), 0.1),
        'rw_a0': nrm((N_EVEN, RW_DIM), 0.1),
        'rw_a_up': nrm((N_EVEN, RW_LORA_A, RW_DIM), 0.5 * RW_LORA_A ** -0.5),
        'rw_g_up': nrm((N_EVEN, RW_LORA_G, RW_DIM), RW_LORA_G ** -0.5),
        'rw_k_k': 0.85 + nrm((N_EVEN, RW_DIM), 0.05),
        'rw_k_a': 1.0 + nrm((N_EVEN, RW_DIM), 0.05),
        'rw_r_k': nrm((N_EVEN, RW_DIM), 0.1),
        'rw_ln_w': gain((N_EVEN, RW_DIM)),
        'rw_ln_b': nrm((N_EVEN, RW_DIM), 0.02),
        'ab_w_out': nrm((N_EVEN, ML_HEADS * ML_DV + RW_DIM, D), (ML_HEADS * ML_DV + RW_DIM) ** -0.5),
        'nsa_w_in': nrm((N_ODD, D, NSA_COLS), D ** -0.5),
        'nsa_gate_b': nrm((N_ODD, 3 * NSA_HEADS), 0.1),
        'cmp_pos': nrm((N_ODD, 2, CMP_BLOCK, NSA_HD), 0.02),
        'cmp_w1': nrm((N_ODD, 2, CMP_BLOCK, NSA_HD, NSA_HD), (CMP_BLOCK * NSA_HD) ** -0.5),
        'cmp_w2': nrm((N_ODD, 2, NSA_HD, NSA_HD), NSA_HD ** -0.5),
        'nsa_w_out': nrm((N_ODD, NSA_HEADS * NSA_HD, D), (NSA_HEADS * NSA_HD) ** -0.5),
        'ca_wq': nrm((DEPTH, D, D), D ** -0.5),
        'ca_wk': nrm((DEPTH, D, D), D ** -0.5),
        'ca_wv': nrm((DEPTH, D, D), D ** -0.5),
        'ca_wo': nrm((DEPTH, D, D), D ** -0.5),
        'moe_wg': nrm((DEPTH, D, MOE_GROUPS), D ** -0.5),
        'moe_bg': nrm((DEPTH, MOE_GROUPS), 0.01),
        'moe_we': nrm((DEPTH, D, MOE_EXPERTS), D ** -0.5),
        'moe_be': nrm((DEPTH, MOE_EXPERTS), 0.01),
        'moe_w_gate': nrm((DEPTH, MOE_EXPERTS, D, MOE_FF), D ** -0.5),
        'moe_w_up': nrm((DEPTH, MOE_EXPERTS, D, MOE_FF), D ** -0.5),
        'moe_w_down': nrm((DEPTH, MOE_EXPERTS, MOE_FF, D), MOE_FF ** -0.5),
    }


def reference(x, mem, norm_mix, norm_cross, norm_mem, norm_ffn, norm_final,
              ab_w_in, ml_conv_w, ml_conv_b, ml_gate_b, rw_mu, rw_w0, rw_w_up, rw_a0, rw_a_up, rw_g_up,
              rw_k_k, rw_k_a, rw_r_k, rw_ln_w, rw_ln_b, ab_w_out,
              nsa_w_in, nsa_gate_b, cmp_pos, cmp_w1, cmp_w2, nsa_w_out,
              ca_wq, ca_wk, ca_wv, ca_wo,
              moe_wg, moe_bg, moe_we, moe_be, moe_w_gate, moe_w_up, moe_w_down):
    for l in range(DEPTH):
        j = l // 2
        hn = _rmsnorm(x, norm_mix[l])
        if l % 2 == 0:
            x = x + _ab_mixer(hn, ab_w_in[j], ml_conv_w[j], ml_conv_b[j], ml_gate_b[j], rw_mu[j], rw_w0[j],
                              rw_w_up[j], rw_a0[j], rw_a_up[j], rw_g_up[j], rw_k_k[j], rw_k_a[j], rw_r_k[j],
                              rw_ln_w[j], rw_ln_b[j], ab_w_out[j])
        else:
            x = x + _nsa(hn, nsa_w_in[j], nsa_gate_b[j], cmp_pos[j], cmp_w1[j], cmp_w2[j], nsa_w_out[j])
        x = x + _cross_attn(_rmsnorm(x, norm_cross[l]), _rmsnorm(mem, norm_mem[l]),
                            ca_wq[l], ca_wk[l], ca_wv[l], ca_wo[l])
        x = x + _hier_moe(_rmsnorm(x, norm_ffn[l]), moe_wg[l], moe_bg[l], moe_we[l], moe_be[l],
                          moe_w_gate[l], moe_w_up[l], moe_w_down[l])
    return _rmsnorm(x, norm_final)
```

```python
import functools
import math

import jax
import jax.numpy as jnp
import numpy as np
from jax import lax
from jax.experimental import pallas as pl
from jax.experimental.pallas import tpu as pltpu

F32 = jnp.float32
BF16 = jnp.bfloat16

D_MODEL = 2048
DEPTH = 2
EPS = 1e-6
ROPE_THETA = 500000.0

ML_HEADS = 4
ML_DV = D_MODEL // 2 // ML_HEADS
ML_DK = ML_DV // 2
ML_CHUNK = 64
ML_CONV = 4
RW_HEAD = 64
RW_HEADS = D_MODEL // 2 // RW_HEAD
RW_DIM = RW_HEADS * RW_HEAD
RW_LORA_W = 64
RW_LORA_A = 64
RW_LORA_G = 128
RW_LN_EPS = 64e-5
ML_SPLITS = (2 * ML_HEADS * ML_DK, ML_HEADS * ML_DV, ML_HEADS * ML_DV, 2 * ML_HEADS)
RW_SPLITS = (RW_DIM, RW_DIM, RW_DIM, RW_LORA_W, RW_LORA_A, RW_LORA_G)
ML_COLS = sum(ML_SPLITS)
RW_COLS = sum(RW_SPLITS)

NSA_HEADS = 16
NSA_KV = 4
NSA_REP = NSA_HEADS // NSA_KV
NSA_HD = D_MODEL // NSA_HEADS
ROPE_DIM = NSA_HD // 4
CMP_BLOCK = 32
CMP_STRIDE = 16
SEL_BLOCK = 64
SEL_TOPK = 16
WINDOW = 512
NSA_QCHUNK = 32
NSA_SPLITS = (NSA_HEADS * NSA_HD,) + (NSA_KV * NSA_HD,) * 6 + (3 * NSA_HEADS,)

CA_HEADS = 4
CA_HD = D_MODEL // CA_HEADS

MOE_GROUPS = 8
MOE_PER_GROUP = 8
MOE_EXPERTS = MOE_GROUPS * MOE_PER_GROUP
MOE_TOPK = 2
MOE_FF = D_MODEL // 4
MOE_ROWS = 128

V7X_VMEM_BYTES = 64 * 1024 * 1024
VMEM_LIMIT = 48 * 1024 * 1024
LANES = 128


def _cparams(sem):
    return pltpu.CompilerParams(dimension_semantics=sem, vmem_limit_bytes=VMEM_LIMIT)


def _mm_kernel(*refs, has_norm, has_res):
    it = iter(refs)
    x_ref = next(it)
    w_ref = next(it)
    nw_ref = next(it) if has_norm else None
    r_ref = next(it) if has_res else None
    o_ref = next(it)
    xs_ref = next(it)

    @pl.when(pl.program_id(1) == 0)
    def _():
        x = x_ref[...]
        if has_norm:
            ms = jnp.mean(x * x, axis=-1, keepdims=True)
            x = x * lax.rsqrt(ms + EPS) * nw_ref[...]
        xs_ref[...] = x.astype(BF16)

    acc = jnp.dot(xs_ref[...], w_ref[...], preferred_element_type=F32)
    if has_res:
        acc = acc + r_ref[...]
    o_ref[...] = acc


def _matmul(x, w_bf16, *, norm_w=None, residual=None, tm=1024, tn=512):
    M, K = x.shape
    N = w_bf16.shape[1]
    tm = min(tm, M)
    tn = min(tn, N)
    assert M % tm == 0 and N % tn == 0, (M, N, tm, tn)
    has_norm = norm_w is not None
    has_res = residual is not None
    in_specs = [pl.BlockSpec((tm, K), lambda i, j: (i, 0)),
                pl.BlockSpec((K, tn), lambda i, j: (0, j))]
    args = [x, w_bf16]
    if has_norm:
        in_specs.append(pl.BlockSpec((1, K), lambda i, j: (0, 0)))
        args.append(norm_w.reshape(1, K))
    if has_res:
        in_specs.append(pl.BlockSpec((tm, tn), lambda i, j: (i, j)))
        args.append(residual)
    return pl.pallas_call(
        functools.partial(_mm_kernel, has_norm=has_norm, has_res=has_res),
        grid=(M // tm, N // tn),
        in_specs=in_specs,
        out_specs=pl.BlockSpec((tm, tn), lambda i, j: (i, j)),
        out_shape=jax.ShapeDtypeStruct((M, N), F32),
        scratch_shapes=[pltpu.VMEM((tm, K), BF16)],
        compiler_params=_cparams(("parallel", "arbitrary")),
        name="mm_norm" if has_norm else "mm",
    )(*args)


def _pad_cols(w, mult):
    n = w.shape[1]
    pad = (-n) % mult
    if pad:
        w = jnp.pad(w, ((0, 0), (0, pad)))
    return w


def _rmsnorm_kernel(x_ref, w_ref, o_ref):
    x = x_ref[...]
    ms = jnp.mean(x * x, axis=-1, keepdims=True)
    o_ref[...] = x * lax.rsqrt(ms + EPS) * w_ref[...]


def _rmsnorm_rows(x, w, tm=512):
    M, K = x.shape
    return pl.pallas_call(
        _rmsnorm_kernel,
        grid=(M // tm,),
        in_specs=[pl.BlockSpec((tm, K), lambda i: (i, 0)), pl.BlockSpec((1, K), lambda i: (0, 0))],
        out_specs=pl.BlockSpec((tm, K), lambda i: (i, 0)),
        out_shape=jax.ShapeDtypeStruct((M, K), F32),
        compiler_params=_cparams(("parallel",)),
        name="rmsnorm",
    )(x, w.reshape(1, K))


def _split(x, sizes):
    return jnp.split(x, [int(s) for s in np.cumsum(sizes)[:-1]], axis=-1)


def _rmsnorm_j(x, w):
    return x * lax.rsqrt(jnp.mean(x * x, axis=-1, keepdims=True) + EPS) * w


def _masked_softmax(s, mask):
    s = jnp.where(mask, s.astype(F32), -jnp.inf)
    m = jnp.max(s, axis=-1, keepdims=True)
    m = jnp.where(jnp.isfinite(m), m, 0.0)
    e = jnp.exp(s - m)
    d = jnp.sum(e, axis=-1, keepdims=True)
    return e / jnp.where(d > 0, d, 1.0)


def _rope(x, pos):
    half = ROPE_DIM // 2
    inv = 1.0 / (ROPE_THETA ** (jnp.arange(half, dtype=F32) / half))
    ang = pos.astype(F32)[:, None] * inv[None, :]
    shape = (ang.shape[0],) + (1,) * (x.ndim - 3) + (half,)
    cos = jnp.cos(ang).reshape(shape)
    sin = jnp.sin(ang).reshape(shape)
    x1 = x[..., :half]
    x2 = x[..., half:ROPE_DIM]
    return jnp.concatenate([x1 * cos - x2 * sin, x2 * cos + x1 * sin, x[..., ROPE_DIM:]], axis=-1)


def _token_shift(p):
    return jnp.pad(p, ((0, 0), (1, 0), (0, 0)))[:, :-1]


def _causal_conv(x, w, b):
    k_len = w.shape[0]
    s_len = x.shape[1]
    xp = jnp.pad(x, ((0, 0), (k_len - 1, 0), (0, 0)))
    y = b
    for j in range(k_len):
        y = y + xp[:, j:j + s_len] * w[j]
    return y


def _mlstm_j(q, k, v, i_pre, f_pre):
    B, S, H, DK = q.shape
    DV = v.shape[-1]
    L = ML_CHUNK
    NC = S // L

    def chunks(t):
        t = t.astype(F32).reshape((B, NC, L, H) + t.shape[3:])
        return t.transpose((1, 0, 3, 2) + tuple(range(4, t.ndim)))

    qc = chunks(q) * (DK ** -0.5)
    kc = chunks(k)
    vc = chunks(v)
    ic = chunks(i_pre)
    lfc = chunks(jax.nn.log_sigmoid(f_pre.astype(F32)))
    causal = jnp.asarray(np.tril(np.ones((L, L), dtype=bool)))

    def step(carry, xs):
        C, n, m = carry
        q_, k_, v_, i_, lf = xs
        b = jnp.cumsum(lf, axis=-1)
        dmat = jnp.where(causal, b[..., :, None] - b[..., None, :] + i_[..., None, :], -jnp.inf)
        inter = b + m[..., None]
        m_row = jnp.maximum(inter, jnp.max(dmat, axis=-1))
        w_in = jnp.exp(dmat - m_row[..., None])
        w_st = jnp.exp(inter - m_row)
        s = jnp.einsum('bhjd,bhld->bhjl', q_, k_) * w_in
        num = w_st[..., None] * jnp.einsum('bhjd,bhde->bhje', q_, C) + jnp.einsum('bhjl,bhle->bhje', s, v_)
        den = w_st * jnp.einsum('bhjd,bhd->bhj', q_, n) + jnp.sum(s, axis=-1)
        h = num / jnp.maximum(jnp.abs(den), jnp.exp(-m_row))[..., None]
        b_last = b[..., -1]
        g_key = b_last[..., None] - b + i_
        m_new = jnp.maximum(b_last + m, jnp.max(g_key, axis=-1))
        wk = jnp.exp(g_key - m_new[..., None])
        decay = jnp.exp(b_last + m - m_new)
        C_new = decay[..., None, None] * C + jnp.einsum('bhld,bhle->bhde', k_ * wk[..., None], v_)
        n_new = decay[..., None] * n + jnp.einsum('bhl,bhld->bhd', wk, k_)
        return (C_new, n_new, m_new), h

    init = (jnp.zeros((B, H, DK, DV), F32), jnp.zeros((B, H, DK), F32), jnp.zeros((B, H), F32))
    _, hs = lax.scan(step, init, (qc, kc, vc, ic, lfc))
    return hs.transpose(1, 0, 3, 2, 4).reshape(B, S, H, DV)


def _rwkv7_scan_j(r, w, k, v, kk, a):
    def step(state, xs):
        r_t, w_t, k_t, v_t, kk_t, a_t = xs
        sk = jnp.einsum('bhvk,bhk->bhv', state, kk_t)
        state = (state * w_t[:, :, None, :] - sk[..., None] * (kk_t * a_t)[:, :, None, :]
                 + v_t[..., None] * k_t[:, :, None, :])
        return state, jnp.einsum('bhvk,bhk->bhv', state, r_t)

    B, S, H, N = r.shape
    xs = tuple(t.transpose(1, 0, 2, 3) for t in (r, w, k, v, kk, a))
    _, y = lax.scan(step, jnp.zeros((B, H, N, N), F32), xs)
    return y.transpose(1, 0, 2, 3)


def _ab_mixer(x, norm_w, w_in, conv_w, conv_b, gate_b, mu, w0, w_up, a0, a_up, g_up, k_k, k_a, r_k, ln_w, ln_b,
              w_out):
    B, S, D = x.shape
    T = B * S
    x2 = x.reshape(T, D)
    ncol = w_in.shape[1]
    p = _matmul(x2, _pad_cols(w_in, 512).astype(BF16), norm_w=norm_w)[:, :ncol].reshape(B, S, ncol)
    ml_p, rw_p = p[..., :ML_COLS], p[..., ML_COLS:]
    qk, v_m, o_m, gif = _split(ml_p, ML_SPLITS)
    qk = jax.nn.silu(_causal_conv(qk, conv_w, conv_b))
    q_m, k_m = jnp.split(qk, 2, axis=-1)
    gif = gif + gate_b
    h_m = _mlstm_j(q_m.reshape(B, S, ML_HEADS, ML_DK), k_m.reshape(B, S, ML_HEADS, ML_DK),
                   v_m.reshape(B, S, ML_HEADS, ML_DV), gif[..., :ML_HEADS], gif[..., ML_HEADS:])
    y_m = jax.nn.sigmoid(o_m) * h_m.reshape(B, S, ML_HEADS * ML_DV)
    rw_p = rw_p + (_token_shift(rw_p) - rw_p) * mu
    r, k, v, xw, xa, xg = _split(rw_p, RW_SPLITS)
    logw = -math.exp(-0.5) * jax.nn.sigmoid(w0 + jnp.tanh(xw) @ w_up)
    a = jax.nn.sigmoid(a0 + xa @ a_up)
    g = jax.nn.sigmoid(xg) @ g_up

    def heads(t):
        return t.reshape(B, S, RW_HEADS, RW_HEAD)

    kk = heads(k * k_k)
    kk = kk * lax.rsqrt(jnp.sum(kk * kk, axis=-1, keepdims=True) + 1e-12)
    k = k * (1.0 + (a - 1.0) * k_a)
    rh, kh, vh, ah = heads(r), heads(k), heads(v), heads(a)
    y = _rwkv7_scan_j(rh, jnp.exp(heads(logw)), kh, vh, kk, ah)
    y_mu = jnp.mean(y, axis=-1, keepdims=True)
    y_var = jnp.mean(jnp.square(y - y_mu), axis=-1, keepdims=True)
    yn = ((y - y_mu) * lax.rsqrt(y_var + RW_LN_EPS)).reshape(B, S, RW_DIM) * ln_w + ln_b
    bonus = (jnp.sum(rh * kh * r_k.reshape(RW_HEADS, RW_HEAD), axis=-1, keepdims=True) * vh).reshape(B, S, RW_DIM)
    y_r = (yn + bonus) * g
    y_cat = jnp.concatenate([y_m, y_r], axis=-1).reshape(T, D)
    return _matmul(y_cat, w_out.astype(BF16), residual=x2).reshape(B, S, D)


def _nsa(x, norm_w, w_in, gate_b, cmp_pos, cmp_w1, cmp_w2, w_out):
    B, S, D = x.shape
    T = B * S
    x2 = x.reshape(T, D)
    G, R, HD = NSA_KV, NSA_REP, NSA_HD
    QC = NSA_QCHUNK
    scale = HD ** -0.5
    pos = jnp.arange(S)
    ncol = w_in.shape[1]
    p = _matmul(x2, _pad_cols(w_in, 512).astype(BF16), norm_w=norm_w)[:, :ncol].reshape(B, S, ncol)
    q, kc, vc, ks, vs, kw, vw, gl = _split(p, NSA_SPLITS)
    q = q.reshape(B, S, G, R, HD)

    def kvh(t):
        return t.reshape(B, S, G, HD)

    q_rot = _rope(q, pos)
    ks = _rope(kvh(ks), pos)
    kw = _rope(kvh(kw), pos)
    vs = kvh(vs)
    vw = kvh(vw)
    gates = jax.nn.sigmoid(gl + gate_b).reshape(B, S, 3, G, R)
    n_cmp = (S - CMP_BLOCK) // CMP_STRIDE + 1
    cidx = np.arange(n_cmp)[:, None] * CMP_STRIDE + np.arange(CMP_BLOCK)[None, :]

    def compress(t, pe, w1, w2):
        blocks = t[:, cidx] + pe[None, None, :, None, :]
        return jax.nn.gelu(jnp.einsum('bjpgd,pde->bjge', blocks, w1)) @ w2

    k_cmp = compress(kvh(kc), cmp_pos[0], cmp_w1[0], cmp_w2[0])
    v_cmp = compress(kvh(vc), cmp_pos[1], cmp_w1[1], cmp_w2[1])
    cmp_end = jnp.asarray(cidx[:, -1])
    n_sel = S // SEL_BLOCK
    c0 = np.arange(n_cmp)[:, None] * CMP_STRIDE
    s0 = np.arange(n_sel)[None, :] * SEL_BLOCK
    ov = np.clip(np.minimum(c0 + CMP_BLOCK, s0 + SEL_BLOCK) - np.maximum(c0, s0), 0, None) / CMP_BLOCK
    ov = jnp.asarray(ov, dtype=F32)
    n_top = min(SEL_TOPK, n_sel)
    k_blk = ks.reshape(B, n_sel, SEL_BLOCK, G, HD).transpose(0, 3, 1, 2, 4)
    v_blk = vs.reshape(B, n_sel, SEL_BLOCK, G, HD).transpose(0, 3, 1, 2, 4)
    kw_pad = jnp.pad(kw, ((0, 0), (WINDOW, 0), (0, 0), (0, 0)))
    vw_pad = jnp.pad(vw, ((0, 0), (WINDOW, 0), (0, 0), (0, 0)))
    bi = jnp.arange(B)[:, None, None, None]
    gi = jnp.arange(G)[None, :, None, None]
    blk_ids = jnp.arange(n_sel)

    def chunk(c):
        t0 = c * QC
        qt = t0 + jnp.arange(QC)
        q_c = lax.dynamic_slice_in_dim(q, t0, QC, axis=1)
        qr_c = lax.dynamic_slice_in_dim(q_rot, t0, QC, axis=1)
        g_c = lax.dynamic_slice_in_dim(gates, t0, QC, axis=1)
        p_cmp = _masked_softmax(jnp.einsum('bqgrd,bjgd->bgrqj', q_c, k_cmp) * scale,
                                cmp_end[None, :] <= qt[:, None])
        o_cmp = jnp.einsum('bgrqj,bjgd->bqgrd', p_cmp, v_cmp)
        imp = jnp.einsum('bgrqj,js->bgqs', p_cmp, ov)
        cur = (qt // SEL_BLOCK)[:, None]
        forced = (blk_ids[None, :] == 0) | (blk_ids[None, :] == cur) | (blk_ids[None, :] == cur - 1)
        score = jnp.where(forced, jnp.inf, jnp.where(blk_ids[None, :] <= cur, imp, -jnp.inf))
        top_v, top_i = lax.top_k(score, n_top)
        k_g = k_blk[bi, gi, top_i].reshape(B, G, QC, n_top * SEL_BLOCK, HD)
        v_g = v_blk[bi, gi, top_i].reshape(B, G, QC, n_top * SEL_BLOCK, HD)
        kpos = (top_i[..., None] * SEL_BLOCK + jnp.arange(SEL_BLOCK)).reshape(B, G, QC, n_top * SEL_BLOCK)
        kmask = jnp.repeat(top_v > -jnp.inf, SEL_BLOCK, axis=-1) & (kpos <= qt[None, None, :, None])
        p_slc = _masked_softmax(jnp.einsum('bqgrd,bgqkd->bgrqk', qr_c, k_g) * scale, kmask[:, :, None])
        o_slc = jnp.einsum('bgrqk,bgqkd->bqgrd', p_slc, v_g)
        k_w = lax.dynamic_slice_in_dim(kw_pad, t0, QC + WINDOW, axis=1)
        v_w = lax.dynamic_slice_in_dim(vw_pad, t0, QC + WINDOW, axis=1)
        wpos = t0 - WINDOW + jnp.arange(QC + WINDOW)
        dist = qt[:, None] - wpos[None, :]
        wmask = (dist >= 0) & (dist < WINDOW) & (wpos[None, :] >= 0)
        p_win = _masked_softmax(jnp.einsum('bqgrd,bkgd->bgrqk', qr_c, k_w) * scale, wmask)
        o_win = jnp.einsum('bgrqk,bkgd->bqgrd', p_win, v_w)
        return (g_c[:, :, 0, :, :, None] * o_cmp + g_c[:, :, 1, :, :, None] * o_slc
                + g_c[:, :, 2, :, :, None] * o_win)

    o = lax.map(chunk, jnp.arange(S // QC))
    o = o.transpose(1, 0, 2, 3, 4, 5).reshape(T, NSA_HEADS * HD)
    return _matmul(o, w_out.astype(BF16), residual=x2).reshape(B, S, D)


def _cross_attn(x, norm_w, mem, norm_mem, wq, wk, wv, wo):
    B, S, D = x.shape
    T = B * S
    M = mem.shape[1]
    x2 = x.reshape(T, D)
    q = _matmul(x2, wq.astype(BF16), norm_w=norm_w).reshape(B, S, CA_HEADS, CA_HD)
    m2 = mem.reshape(B * M, D)
    kv = _matmul(m2, jnp.concatenate([wk, wv], axis=1).astype(BF16), norm_w=norm_mem)
    k = kv[:, :D].reshape(B, M, CA_HEADS, CA_HD)
    v = kv[:, D:].reshape(B, M, CA_HEADS, CA_HD)
    s = jnp.einsum('bshd,bmhd->bhsm', q, k) * (CA_HD ** -0.5)
    p = jax.nn.softmax(s, axis=-1)
    o = jnp.einsum('bhsm,bmhd->bshd', p, v).reshape(T, D)
    return _matmul(o, wo.astype(BF16), residual=x2).reshape(B, S, D)


def _hier_moe(x, norm_w, wg, bg, we, be, w_gate, w_up, w_down):
    B, S, D = x.shape
    h = _rmsnorm_j(x, norm_w)
    T = B * S
    A = T * MOE_TOPK
    xt = h.reshape(T, D)
    lg = xt @ wg + bg
    grp = jnp.argmax(lg, axis=-1)
    p_grp = jnp.take_along_axis(jax.nn.softmax(lg, axis=-1), grp[:, None], axis=-1)
    le = (xt @ we + be).reshape(T, MOE_GROUPS, MOE_PER_GROUP)
    le = jnp.take_along_axis(le, grp[:, None, None], axis=1)[:, 0]
    top_l, top_e = lax.top_k(le, MOE_TOPK)
    wts = (p_grp * jax.nn.softmax(top_l, axis=-1)).reshape(A)
    eid = (grp[:, None] * MOE_PER_GROUP + top_e).reshape(A)
    tok = jnp.repeat(jnp.arange(T, dtype=jnp.int32), MOE_TOPK)
    order = jnp.argsort(eid)
    e_s, tok_s, w_s = eid[order], tok[order], wts[order]
    counts = jax.ops.segment_sum(jnp.ones((A,), jnp.int32), eid, num_segments=MOE_EXPERTS)
    padded = (counts + MOE_ROWS - 1) // MOE_ROWS * MOE_ROWS
    ends = jnp.cumsum(padded)
    dest = (ends - padded)[e_s] + jnp.arange(A, dtype=jnp.int32) - (jnp.cumsum(counts) - counts)[e_s]
    n_chunks = -(-A // MOE_ROWS) + MOE_EXPERTS
    rows = n_chunks * MOE_ROWS
    row_tok = jnp.full((rows,), T, jnp.int32).at[dest].set(tok_s)
    row_w = jnp.zeros((rows,), F32).at[dest].set(w_s)
    chunk_e = jnp.minimum(jnp.searchsorted(ends, jnp.arange(n_chunks, dtype=jnp.int32) * MOE_ROWS, side='right'),
                          MOE_EXPERTS - 1)
    x_pad = jnp.concatenate([xt, jnp.zeros((1, D), xt.dtype)], axis=0)

    def expert_rows(args):
        t_idx, w_r, e = args
        xr = x_pad[t_idx]
        hid = jax.nn.silu(xr @ w_gate[e]) * (xr @ w_up[e])
        y = hid @ w_down[e]
        return y * w_r[:, None]

    out = lax.map(expert_rows, (row_tok.reshape(n_chunks, MOE_ROWS), row_w.reshape(n_chunks, MOE_ROWS), chunk_e))
    y = jax.ops.segment_sum(out.reshape(rows, D), row_tok, num_segments=T + 1)[:T]
    return x + y.reshape(B, S, D)


def kernel(x, mem, norm_mix, norm_cross, norm_mem, norm_ffn, norm_final,
           ab_w_in, ml_conv_w, ml_conv_b, ml_gate_b, rw_mu, rw_w0, rw_w_up, rw_a0, rw_a_up, rw_g_up,
           rw_k_k, rw_k_a, rw_r_k, rw_ln_w, rw_ln_b, ab_w_out,
           nsa_w_in, nsa_gate_b, cmp_pos, cmp_w1, cmp_w2, nsa_w_out,
           ca_wq, ca_wk, ca_wv, ca_wo,
           moe_wg, moe_bg, moe_we, moe_be, moe_w_gate, moe_w_up, moe_w_down):
    B, S, D = x.shape
    for l in range(DEPTH):
        j = l // 2
        if l % 2 == 0:
            x = _ab_mixer(x, norm_mix[l], ab_w_in[j], ml_conv_w[j], ml_conv_b[j], ml_gate_b[j], rw_mu[j], rw_w0[j],
                          rw_w_up[j], rw_a0[j], rw_a_up[j], rw_g_up[j], rw_k_k[j], rw_k_a[j], rw_r_k[j],
                          rw_ln_w[j], rw_ln_b[j], ab_w_out[j])
        else:
            x = _nsa(x, norm_mix[l], nsa_w_in[j], nsa_gate_b[j], cmp_pos[j], cmp_w1[j], cmp_w2[j], nsa_w_out[j])
        x = _cross_attn(x, norm_cross[l], mem, norm_mem[l], ca_wq[l], ca_wk[l], ca_wv[l], ca_wo[l])
        x = _hier_moe(x, norm_ffn[l], moe_wg[l], moe_bg[l], moe_we[l], moe_be[l],
                      moe_w_gate[l], moe_w_up[l], moe_w_down[l])
    return _rmsnorm_rows(x.reshape(B * S, D), norm_final).reshape(B, S, D)
```

```python
import functools
import math

import jax
import jax.numpy as jnp
import numpy as np
from jax import lax
from jax.experimental import pallas as pl
from jax.experimental.pallas import tpu as pltpu

F32 = jnp.float32
BF16 = jnp.bfloat16

D_MODEL = 2048
DEPTH = 2
EPS = 1e-6
ROPE_THETA = 500000.0

ML_HEADS = 4
ML_DV = D_MODEL // 2 // ML_HEADS
ML_DK = ML_DV // 2
ML_CHUNK = 64
ML_CONV = 4
RW_HEAD = 64
RW_HEADS = D_MODEL // 2 // RW_HEAD
RW_DIM = RW_HEADS * RW_HEAD
RW_LORA_W = 64
RW_LORA_A = 64
RW_LORA_G = 128
RW_LN_EPS = 64e-5
ML_SPLITS = (2 * ML_HEADS * ML_DK, ML_HEADS * ML_DV, ML_HEADS * ML_DV, 2 * ML_HEADS)
RW_SPLITS = (RW_DIM, RW_DIM, RW_DIM, RW_LORA_W, RW_LORA_A, RW_LORA_G)
ML_COLS = sum(ML_SPLITS)
RW_COLS = sum(RW_SPLITS)

NSA_HEADS = 16
NSA_KV = 4
NSA_REP = NSA_HEADS // NSA_KV
NSA_HD = D_MODEL // NSA_HEADS
ROPE_DIM = NSA_HD // 4
CMP_BLOCK = 32
CMP_STRIDE = 16
SEL_BLOCK = 64
SEL_TOPK = 16
WINDOW = 512
NSA_QCHUNK = 32
NSA_SPLITS = (NSA_HEADS * NSA_HD,) + (NSA_KV * NSA_HD,) * 6 + (3 * NSA_HEADS,)

CA_HEADS = 4
CA_HD = D_MODEL // CA_HEADS

MOE_GROUPS = 8
MOE_PER_GROUP = 8
MOE_EXPERTS = MOE_GROUPS * MOE_PER_GROUP
MOE_TOPK = 2
MOE_FF = D_MODEL // 4
MOE_ROWS = 128

V7X_VMEM_BYTES = 64 * 1024 * 1024
VMEM_LIMIT = 48 * 1024 * 1024
LANES = 128
SUBLANES = 8


def _cparams(sem):
    return pltpu.CompilerParams(dimension_semantics=sem, vmem_limit_bytes=VMEM_LIMIT)


def _mm_kernel(*refs, has_norm, has_res):
    it = iter(refs)
    x_ref = next(it)
    w_ref = next(it)
    nw_ref = next(it) if has_norm else None
    r_ref = next(it) if has_res else None
    o_ref = next(it)
    xs_ref = next(it)

    @pl.when(pl.program_id(1) == 0)
    def _():
        x = x_ref[...]
        if has_norm:
            ms = jnp.mean(x * x, axis=-1, keepdims=True)
            x = x * lax.rsqrt(ms + EPS) * nw_ref[...]
        xs_ref[...] = x.astype(BF16)

    acc = jnp.dot(xs_ref[...], w_ref[...], preferred_element_type=F32)
    if has_res:
        acc = acc + r_ref[...]
    o_ref[...] = acc


def _matmul(x, w_bf16, *, norm_w=None, residual=None, tm=1024, tn=512):
    M, K = x.shape
    N = w_bf16.shape[1]
    tm = min(tm, M)
    tn = min(tn, N)
    assert M % tm == 0 and N % tn == 0, (M, N, tm, tn)
    has_norm = norm_w is not None
    has_res = residual is not None
    in_specs = [pl.BlockSpec((tm, K), lambda i, j: (i, 0)),
                pl.BlockSpec((K, tn), lambda i, j: (0, j))]
    args = [x, w_bf16]
    if has_norm:
        in_specs.append(pl.BlockSpec((1, K), lambda i, j: (0, 0)))
        args.append(norm_w.reshape(1, K))
    if has_res:
        in_specs.append(pl.BlockSpec((tm, tn), lambda i, j: (i, j)))
        args.append(residual)
    return pl.pallas_call(
        functools.partial(_mm_kernel, has_norm=has_norm, has_res=has_res),
        grid=(M // tm, N // tn),
        in_specs=in_specs,
        out_specs=pl.BlockSpec((tm, tn), lambda i, j: (i, j)),
        out_shape=jax.ShapeDtypeStruct((M, N), F32),
        scratch_shapes=[pltpu.VMEM((tm, K), BF16)],
        compiler_params=_cparams(("parallel", "arbitrary")),
        name="mm_norm" if has_norm else "mm",
    )(*args)


def _pad_cols(w, mult):
    n = w.shape[1]
    pad = (-n) % mult
    if pad:
        w = jnp.pad(w, ((0, 0), (0, pad)))
    return w


def _rmsnorm_kernel(x_ref, w_ref, o_ref):
    x = x_ref[...]
    ms = jnp.mean(x * x, axis=-1, keepdims=True)
    o_ref[...] = x * lax.rsqrt(ms + EPS) * w_ref[...]


def _rmsnorm_rows(x, w, tm=512):
    M, K = x.shape
    return pl.pallas_call(
        _rmsnorm_kernel,
        grid=(M // tm,),
        in_specs=[pl.BlockSpec((tm, K), lambda i: (i, 0)), pl.BlockSpec((1, K), lambda i: (0, 0))],
        out_specs=pl.BlockSpec((tm, K), lambda i: (i, 0)),
        out_shape=jax.ShapeDtypeStruct((M, K), F32),
        compiler_params=_cparams(("parallel",)),
        name="rmsnorm",
    )(x, w.reshape(1, K))


def _rwkv_scan_kernel(r_ref, w_ref, k_ref, v_ref, kk_ref, kb_ref, y_ref, s_ref, *, tb, nb, npairs, gsz):
    @pl.when(pl.program_id(1) == 0)
    def _():
        s_ref[...] = jnp.zeros_like(s_ref)

    row = lax.broadcasted_iota(jnp.int32, (LANES, LANES), 0)
    col = lax.broadcasted_iota(jnp.int32, (LANES, LANES), 1)
    hmat = jnp.where((row // RW_HEAD) == (col // RW_HEAD), 1.0, 0.0).astype(BF16)
    vrow = lax.broadcasted_iota(jnp.int32, (RW_HEAD, LANES), 0)
    vcol = lax.broadcasted_iota(jnp.int32, (RW_HEAD, LANES), 1)
    diag = jnp.where((vcol % RW_HEAD) == vrow, 1.0, 0.0).astype(F32)

    def group_sum(parts):
        lhs = jnp.concatenate([q.astype(BF16) for q in parts], axis=0)
        out = jnp.dot(lhs, hmat, preferred_element_type=F32)
        return [out[i * RW_HEAD:(i + 1) * RW_HEAD] for i in range(len(parts))]

    def step(t8, carry):
        rows = pl.ds(pl.multiple_of(t8 * SUBLANES, SUBLANES), SUBLANES)
        for b in range(nb):
            for g in range(npairs // gsz):
                pairs = range(g * gsz, (g + 1) * gsz)
                lns = [pl.ds(p * LANES, LANES) for p in pairs]
                kk8 = [kk_ref[b, rows, ln] for ln in lns]
                w8 = [w_ref[b, rows, ln] for ln in lns]
                kb8 = [kb_ref[b, rows, ln] for ln in lns]
                k8 = [k_ref[b, rows, ln] for ln in lns]
                v8 = [v_ref[b, rows, ln] for ln in lns]
                r8 = [r_ref[b, rows, ln] for ln in lns]
                s = [s_ref[b, p] for p in pairs]
                ys = [[] for _ in pairs]
                for j in range(SUBLANES):
                    sl = slice(j, j + 1)
                    sk = group_sum([s[i] * kk8[i][sl] for i in range(gsz)])
                    vb = group_sum([diag * v8[i][sl] for i in range(gsz)])
                    s = [s[i] * w8[i][sl] - sk[i] * kb8[i][sl] + vb[i] * k8[i][sl] for i in range(gsz)]
                    yb = group_sum([s[i] * r8[i][sl] for i in range(gsz)])
                    for i in range(gsz):
                        ys[i].append(jnp.sum(yb[i] * diag, axis=0, keepdims=True))
                for i, p in enumerate(pairs):
                    s_ref[b, p] = s[i]
                    y_ref[b, rows, lns[i]] = jnp.concatenate(ys[i], axis=0)
        return carry

    lax.fori_loop(0, tb // SUBLANES, step, 0)


def _rwkv_scan(r, w, k, v, kk, kb, tb=256, nb=2, gsz=4):
    B, S, C = r.shape
    npairs = C // LANES
    tb = min(tb, S)
    nb = min(nb, B)
    spec = pl.BlockSpec((nb, tb, C), lambda b, t: (b, t, 0))
    return pl.pallas_call(
        functools.partial(_rwkv_scan_kernel, tb=tb, nb=nb, npairs=npairs, gsz=gsz),
        grid=(B // nb, S // tb),
        in_specs=[spec] * 6,
        out_specs=spec,
        out_shape=jax.ShapeDtypeStruct((B, S, C), F32),
        scratch_shapes=[pltpu.VMEM((nb, npairs, RW_HEAD, LANES), F32)],
        compiler_params=_cparams(("parallel", "arbitrary")),
        name="rwkv_scan",
    )(r, w, k, v, kk, kb)


NSA_NEG = -1e30
NSA_FORCED = 1e30
NSA_REMOVED = -3e30
NSA_COL_Q, NSA_COL_KC, NSA_COL_VC, NSA_COL_KS, NSA_COL_VS, NSA_COL_KW, NSA_COL_VW, NSA_COL_GL = (
    0, 16, 20, 24, 28, 32, 36, 40)


def _rope_tables(S):
    half = ROPE_DIM // 2
    inv = 1.0 / (ROPE_THETA ** (jnp.arange(half, dtype=F32) / half))
    ang = jnp.arange(S, dtype=F32)[:, None] * inv[None, :]
    cos, sin = jnp.cos(ang), jnp.sin(ang)
    one = jnp.ones((S, NSA_HD - ROPE_DIM), F32)
    zero = jnp.zeros((S, NSA_HD - ROPE_DIM), F32)
    zh = jnp.zeros((S, half), F32)
    return (jnp.concatenate([cos, cos, one], axis=1), jnp.concatenate([zh, sin, zero], axis=1),
            jnp.concatenate([-sin, zh, zero], axis=1))


def _rope_rows(x, c, s1, s2):
    half = ROPE_DIM // 2
    return x * c + pltpu.roll(x, half, 1) * s1 + pltpu.roll(x, NSA_HD - half, 1) * s2


def _nsa_prep_q_kernel(x_ref, c_ref, s1_ref, s2_ref, qt_ref, qrt_ref):
    x = x_ref[0] * (NSA_HD ** -0.5)
    xr = _rope_rows(x, c_ref[...], s1_ref[...], s2_ref[...])
    qt_ref[0, 0] = x.T.astype(BF16)
    qrt_ref[0, 0] = xr.T.astype(BF16)


def _nsa_prep_kv_kernel(ks_ref, kw_ref, vs_ref, vw_ref, c_ref, s1_ref, s2_ref, kso_ref, kwo_ref, vst_ref, vwt_ref):
    c, s1, s2 = c_ref[...], s1_ref[...], s2_ref[...]
    kso_ref[0, 0] = _rope_rows(ks_ref[0], c, s1, s2).astype(BF16)
    kwo_ref[0, 0] = _rope_rows(kw_ref[0], c, s1, s2).astype(BF16)
    vst_ref[0, 0] = vs_ref[0].T.astype(BF16)
    vwt_ref[0, 0] = vw_ref[0].T.astype(BF16)


def _nsa_compress_kernel(kc_ref, vc_ref, pe_ref, w1_ref, w2_ref, kcmp_ref, vcmpt_ref, *, ncp):
    for which, x_ref in enumerate((kc_ref, vc_ref)):
        za = jnp.zeros((ncp, NSA_HD), F32)
        zb = jnp.zeros((ncp, NSA_HD), F32)
        for p in range(CMP_STRIDE):
            xp = x_ref[pl.ds(p, ncp, stride=CMP_STRIDE), :]
            za = za + jnp.dot((xp + pe_ref[which, p:p + 1, :]).astype(BF16), w1_ref[which, p],
                              preferred_element_type=F32)
            zb = zb + jnp.dot((xp + pe_ref[which, CMP_STRIDE + p:CMP_STRIDE + p + 1, :]).astype(BF16),
                              w1_ref[which, CMP_STRIDE + p], preferred_element_type=F32)
        pre = za + pltpu.roll(zb, ncp - 1, 0)
        out = jnp.dot(jax.nn.gelu(pre).astype(BF16), w2_ref[which], preferred_element_type=F32)
        if which == 0:
            kcmp_ref[0, 0] = out.astype(BF16)
        else:
            vcmpt_ref[0, 0] = out.T.astype(BF16)


def _nsa_attn_kernel(qt_ref, qrt_ref, kcmp_ref, vcmpt_ref, ks_ref, kw_ref, vst_ref, vwt_ref, gl_ref, gb_ref, ovt_ref,
                     o_ref, sel_ref, ms_ref, ls_ref, accs_ref, mw_ref, lw_ref, accw_ref, *, tq, kt, ncp, nsel):
    R = NSA_REP
    g = pl.program_id(1)
    qi = pl.program_id(2)
    t0 = qi * tq
    q_t = jnp.concatenate([qt_ref[0, r] for r in range(R)], axis=1)
    qr_t = jnp.concatenate([qrt_ref[0, r] for r in range(R)], axis=1)
    qpos = t0 + lax.broadcasted_iota(jnp.int32, (1, tq), 1)

    def per_head(fn, a):
        return jnp.concatenate([fn(a[:, r * tq:(r + 1) * tq]) for r in range(R)], axis=1)

    s = jnp.dot(kcmp_ref[0, 0], q_t, preferred_element_type=F32)
    cend = lax.broadcasted_iota(jnp.int32, (ncp, 1), 0) * CMP_STRIDE + (CMP_BLOCK - 1)
    vis = cend <= qpos
    s = per_head(lambda a: jnp.where(vis, a, NSA_NEG), s)
    m = jnp.max(s, axis=0, keepdims=True)
    e = per_head(lambda a: jnp.where(vis, a, 0.0), jnp.exp(s - m))
    d = jnp.sum(e, axis=0, keepdims=True)
    p = e * (1.0 / jnp.where(d > 0, d, 1.0))
    ocmp_t = jnp.dot(vcmpt_ref[0, 0], p.astype(BF16), preferred_element_type=F32)

    psum = p[:, 0:tq]
    for r in range(1, R):
        psum = psum + p[:, r * tq:(r + 1) * tq]
    imp = jnp.dot(ovt_ref[...], psum, preferred_element_type=F32, precision=lax.Precision.HIGHEST)
    sidx = lax.broadcasted_iota(jnp.int32, (nsel, tq), 0)
    cur = qpos // SEL_BLOCK
    forced = (sidx == 0) | (sidx == cur) | (sidx == cur - 1)
    score = jnp.where(forced, NSA_FORCED, jnp.where(sidx <= cur, imp, NSA_NEG))
    sel = jnp.zeros((nsel, tq), F32)
    for _ in range(SEL_TOPK):
        mx = jnp.max(score, axis=0, keepdims=True)
        idx = jnp.min(jnp.where(score == mx, sidx, nsel), axis=0, keepdims=True)
        pick = (sidx == idx) & (mx > 0.5 * NSA_NEG)
        sel = jnp.where(pick, 1.0, sel)
        score = jnp.where(pick, NSA_REMOVED, score)
    sel_ref[...] = sel

    def online_update(s, valid, v_t, m_ref, l_ref, acc_ref):
        s = per_head(lambda a: jnp.where(valid, a, NSA_NEG), s)
        m_old = m_ref[...]
        m_new = jnp.maximum(m_old, jnp.max(s, axis=0, keepdims=True))
        alpha = jnp.exp(m_old - m_new)
        pexp = jnp.exp(s - m_new)
        l_ref[...] = alpha * l_ref[...] + jnp.sum(pexp, axis=0, keepdims=True)
        acc_ref[...] = alpha * acc_ref[...] + jnp.dot(v_t, pexp.astype(BF16), preferred_element_type=F32)
        m_ref[...] = m_new

    for m_ref, l_ref, acc_ref in ((ms_ref, ls_ref, accs_ref), (mw_ref, lw_ref, accw_ref)):
        m_ref[...] = jnp.full(m_ref.shape, NSA_NEG, F32)
        l_ref[...] = jnp.zeros(l_ref.shape, F32)
        acc_ref[...] = jnp.zeros(acc_ref.shape, F32)

    nblk = kt // SEL_BLOCK

    def sel_body(c, carry):
        k0 = pl.multiple_of(c * kt, kt)
        s = jnp.dot(ks_ref[0, 0, pl.ds(k0, kt), :], qr_t, preferred_element_type=F32)
        selrows = sel_ref[pl.ds(pl.multiple_of(c * nblk, nblk), nblk), :]
        selexp = jnp.concatenate(
            [jnp.broadcast_to(selrows[b:b + 1], (SEL_BLOCK, tq)) for b in range(nblk)], axis=0)
        kpos = k0 + lax.broadcasted_iota(jnp.int32, (kt, 1), 0)
        valid = (selexp > 0.5) & (kpos <= qpos)
        online_update(s, valid, vst_ref[0, 0, :, pl.ds(k0, kt)], ms_ref, ls_ref, accs_ref)
        return carry

    lax.fori_loop(0, t0 // kt + 1, sel_body, 0)

    nwin = WINDOW // tq
    for i in range(nwin + 1):
        c = qi - nwin + i

        @pl.when(c >= 0)
        def _():
            k0 = pl.multiple_of(c * tq, tq)
            s = jnp.dot(kw_ref[0, 0, pl.ds(k0, tq), :], qr_t, preferred_element_type=F32)
            dist = qpos - (k0 + lax.broadcasted_iota(jnp.int32, (tq, 1), 0))
            valid = (dist >= 0) & (dist < WINDOW)
            online_update(s, valid, vwt_ref[0, 0, :, pl.ds(k0, tq)], mw_ref, lw_ref, accw_ref)

    oslc_t = accs_ref[...] * (1.0 / ls_ref[...])
    owin_t = accw_ref[...] * (1.0 / lw_ref[...])

    gates_t = jax.nn.sigmoid(gl_ref[0] + gb_ref[...]).T
    rid = lax.broadcasted_iota(jnp.int32, (LANES, 1), 0)

    def gate_row(which, r):
        return jnp.sum(jnp.where(rid == which * NSA_HEADS + g * R + r, gates_t, 0.0), axis=0, keepdims=True)

    for r in range(R):
        cols = slice(r * tq, (r + 1) * tq)
        o_t = (gate_row(0, r) * ocmp_t[:, cols] + gate_row(1, r) * oslc_t[:, cols]
               + gate_row(2, r) * owin_t[:, cols])
        o_ref[0, :, r * NSA_HD:(r + 1) * NSA_HD] = o_t.T


def _nsa_attention(p3, gate_b, cmp_pos, cmp_w1, cmp_w2, *, tq=128, kt=512, tp=512):
    B, S, _ = p3.shape
    G, R, HD = NSA_KV, NSA_REP, NSA_HD
    assert CMP_BLOCK == 2 * CMP_STRIDE and S % kt == 0 and WINDOW % tq == 0 and tq == LANES
    ncp = S // CMP_STRIDE
    nsel = S // SEL_BLOCK
    n_cmp = (S - CMP_BLOCK) // CMP_STRIDE + 1
    c_tab, s1_tab, s2_tab = _rope_tables(S)
    tab_spec3 = pl.BlockSpec((tp, HD), lambda b, h, i: (i, 0))

    qt, qrt = pl.pallas_call(
        _nsa_prep_q_kernel,
        grid=(B, NSA_HEADS, S // tp),
        in_specs=[pl.BlockSpec((1, tp, HD), lambda b, h, i: (b, i, NSA_COL_Q + h)), tab_spec3, tab_spec3, tab_spec3],
        out_specs=[pl.BlockSpec((1, 1, HD, tp), lambda b, h, i: (b, h, 0, i))] * 2,
        out_shape=[jax.ShapeDtypeStruct((B, NSA_HEADS, HD, S), BF16)] * 2,
        compiler_params=_cparams(("parallel", "parallel", "parallel")),
        name="nsa_prep_q",
    )(p3, c_tab, s1_tab, s2_tab)

    def col_spec(col0):
        return pl.BlockSpec((1, tp, HD), lambda b, g, i: (b, i, col0 + g))

    ks_rot, kw_rot, vs_t, vw_t = pl.pallas_call(
        _nsa_prep_kv_kernel,
        grid=(B, G, S // tp),
        in_specs=[col_spec(NSA_COL_KS), col_spec(NSA_COL_KW), col_spec(NSA_COL_VS), col_spec(NSA_COL_VW),
                  tab_spec3, tab_spec3, tab_spec3],
        out_specs=[pl.BlockSpec((1, 1, tp, HD), lambda b, g, i: (b, g, i, 0))] * 2
        + [pl.BlockSpec((1, 1, HD, tp), lambda b, g, i: (b, g, 0, i))] * 2,
        out_shape=[jax.ShapeDtypeStruct((B, G, S, HD), BF16)] * 2 + [jax.ShapeDtypeStruct((B, G, HD, S), BF16)] * 2,
        compiler_params=_cparams(("parallel", "parallel", "parallel")),
        name="nsa_prep_kv",
    )(p3, p3, p3, p3, c_tab, s1_tab, s2_tab)

    k_cmp, v_cmp_t = pl.pallas_call(
        functools.partial(_nsa_compress_kernel, ncp=ncp),
        grid=(B, G),
        in_specs=[pl.BlockSpec((None, S, HD), lambda b, g: (b, 0, NSA_COL_KC + g)),
                  pl.BlockSpec((None, S, HD), lambda b, g: (b, 0, NSA_COL_VC + g)),
                  pl.BlockSpec((2, CMP_BLOCK, HD), lambda b, g: (0, 0, 0)),
                  pl.BlockSpec((2, CMP_BLOCK, HD, HD), lambda b, g: (0, 0, 0, 0)),
                  pl.BlockSpec((2, HD, HD), lambda b, g: (0, 0, 0))],
        out_specs=[pl.BlockSpec((1, 1, ncp, HD), lambda b, g: (b, g, 0, 0)),
                   pl.BlockSpec((1, 1, HD, ncp), lambda b, g: (b, g, 0, 0))],
        out_shape=[jax.ShapeDtypeStruct((B, G, ncp, HD), BF16), jax.ShapeDtypeStruct((B, G, HD, ncp), BF16)],
        compiler_params=_cparams(("parallel", "parallel")),
        name="nsa_compress",
    )(p3, p3, cmp_pos, cmp_w1.astype(BF16), cmp_w2.astype(BF16))

    c0 = np.arange(ncp)[None, :] * CMP_STRIDE
    s0 = np.arange(nsel)[:, None] * SEL_BLOCK
    ov_t = np.clip(np.minimum(c0 + CMP_BLOCK, s0 + SEL_BLOCK) - np.maximum(c0, s0), 0, None) / CMP_BLOCK
    ov_t = ov_t * (np.arange(ncp)[None, :] < n_cmp)
    gb = jnp.pad(gate_b, (0, LANES - gate_b.shape[0])).reshape(1, LANES)
    ncols = R * tq

    def full_kv(shape):
        return pl.BlockSpec((1, 1) + shape, lambda b, g, i: (b, g, 0, 0))

    return pl.pallas_call(
        functools.partial(_nsa_attn_kernel, tq=tq, kt=kt, ncp=ncp, nsel=nsel),
        grid=(B, G, S // tq),
        in_specs=[pl.BlockSpec((1, R, HD, tq), lambda b, g, i: (b, g, 0, i)),
                  pl.BlockSpec((1, R, HD, tq), lambda b, g, i: (b, g, 0, i)),
                  full_kv((ncp, HD)), full_kv((HD, ncp)),
                  full_kv((S, HD)), full_kv((S, HD)), full_kv((HD, S)), full_kv((HD, S)),
                  pl.BlockSpec((1, tq, LANES), lambda b, g, i: (b, i, NSA_COL_GL)),
                  pl.BlockSpec((1, LANES), lambda b, g, i: (0, 0)),
                  pl.BlockSpec((nsel, ncp), lambda b, g, i: (0, 0))],
        out_specs=pl.BlockSpec((1, tq, R * HD), lambda b, g, i: (b, i, g)),
        out_shape=jax.ShapeDtypeStruct((B, S, NSA_HEADS * HD), F32),
        scratch_shapes=[pltpu.VMEM((nsel, tq), F32),
                        pltpu.VMEM((1, ncols), F32), pltpu.VMEM((1, ncols), F32), pltpu.VMEM((HD, ncols), F32),
                        pltpu.VMEM((1, ncols), F32), pltpu.VMEM((1, ncols), F32), pltpu.VMEM((HD, ncols), F32)],
        compiler_params=_cparams(("parallel", "parallel", "arbitrary")),
        name="nsa_attn",
    )(qt, qrt, k_cmp, v_cmp_t, ks_rot, kw_rot, vs_t, vw_t, p3, gb, jnp.asarray(ov_t, F32))


def _split(x, sizes):
    return jnp.split(x, [int(s) for s in np.cumsum(sizes)[:-1]], axis=-1)


def _rmsnorm_j(x, w):
    return x * lax.rsqrt(jnp.mean(x * x, axis=-1, keepdims=True) + EPS) * w


def _masked_softmax(s, mask):
    s = jnp.where(mask, s.astype(F32), -jnp.inf)
    m = jnp.max(s, axis=-1, keepdims=True)
    m = jnp.where(jnp.isfinite(m), m, 0.0)
    e = jnp.exp(s - m)
    d = jnp.sum(e, axis=-1, keepdims=True)
    return e / jnp.where(d > 0, d, 1.0)


def _rope(x, pos):
    half = ROPE_DIM // 2
    inv = 1.0 / (ROPE_THETA ** (jnp.arange(half, dtype=F32) / half))
    ang = pos.astype(F32)[:, None] * inv[None, :]
    shape = (ang.shape[0],) + (1,) * (x.ndim - 3) + (half,)
    cos = jnp.cos(ang).reshape(shape)
    sin = jnp.sin(ang).reshape(shape)
    x1 = x[..., :half]
    x2 = x[..., half:ROPE_DIM]
    return jnp.concatenate([x1 * cos - x2 * sin, x2 * cos + x1 * sin, x[..., ROPE_DIM:]], axis=-1)


def _token_shift(p):
    return jnp.pad(p, ((0, 0), (1, 0), (0, 0)))[:, :-1]


def _causal_conv(x, w, b):
    k_len = w.shape[0]
    s_len = x.shape[1]
    xp = jnp.pad(x, ((0, 0), (k_len - 1, 0), (0, 0)))
    y = b
    for j in range(k_len):
        y = y + xp[:, j:j + s_len] * w[j]
    return y


def _mlstm_j(q, k, v, i_pre, f_pre):
    B, S, H, DK = q.shape
    DV = v.shape[-1]
    L = ML_CHUNK
    NC = S // L

    def chunks(t):
        t = t.astype(F32).reshape((B, NC, L, H) + t.shape[3:])
        return t.transpose((1, 0, 3, 2) + tuple(range(4, t.ndim)))

    qc = chunks(q) * (DK ** -0.5)
    kc = chunks(k)
    vc = chunks(v)
    ic = chunks(i_pre)
    lfc = chunks(jax.nn.log_sigmoid(f_pre.astype(F32)))
    causal = jnp.asarray(np.tril(np.ones((L, L), dtype=bool)))

    def step(carry, xs):
        C, n, m = carry
        q_, k_, v_, i_, lf = xs
        b = jnp.cumsum(lf, axis=-1)
        dmat = jnp.where(causal, b[..., :, None] - b[..., None, :] + i_[..., None, :], -jnp.inf)
        inter = b + m[..., None]
        m_row = jnp.maximum(inter, jnp.max(dmat, axis=-1))
        w_in = jnp.exp(dmat - m_row[..., None])
        w_st = jnp.exp(inter - m_row)
        s = jnp.einsum('bhjd,bhld->bhjl', q_, k_) * w_in
        num = w_st[..., None] * jnp.einsum('bhjd,bhde->bhje', q_, C) + jnp.einsum('bhjl,bhle->bhje', s, v_)
        den = w_st * jnp.einsum('bhjd,bhd->bhj', q_, n) + jnp.sum(s, axis=-1)
        h = num / jnp.maximum(jnp.abs(den), jnp.exp(-m_row))[..., None]
        b_last = b[..., -1]
        g_key = b_last[..., None] - b + i_
        m_new = jnp.maximum(b_last + m, jnp.max(g_key, axis=-1))
        wk = jnp.exp(g_key - m_new[..., None])
        decay = jnp.exp(b_last + m - m_new)
        C_new = decay[..., None, None] * C + jnp.einsum('bhld,bhle->bhde', k_ * wk[..., None], v_)
        n_new = decay[..., None] * n + jnp.einsum('bhl,bhld->bhd', wk, k_)
        return (C_new, n_new, m_new), h

    init = (jnp.zeros((B, H, DK, DV), F32), jnp.zeros((B, H, DK), F32), jnp.zeros((B, H), F32))
    _, hs = lax.scan(step, init, (qc, kc, vc, ic, lfc))
    return hs.transpose(1, 0, 3, 2, 4).reshape(B, S, H, DV)


def _rwkv7_scan_j(r, w, k, v, kk, a):
    def step(state, xs):
        r_t, w_t, k_t, v_t, kk_t, a_t = xs
        sk = jnp.einsum('bhvk,bhk->bhv', state, kk_t)
        state = (state * w_t[:, :, None, :] - sk[..., None] * (kk_t * a_t)[:, :, None, :]
                 + v_t[..., None] * k_t[:, :, None, :])
        return state, jnp.einsum('bhvk,bhk->bhv', state, r_t)

    B, S, H, N = r.shape
    xs = tuple(t.transpose(1, 0, 2, 3) for t in (r, w, k, v, kk, a))
    _, y = lax.scan(step, jnp.zeros((B, H, N, N), F32), xs)
    return y.transpose(1, 0, 2, 3)


def _ab_mixer(x, norm_w, w_in, conv_w, conv_b, gate_b, mu, w0, w_up, a0, a_up, g_up, k_k, k_a, r_k, ln_w, ln_b,
              w_out):
    B, S, D = x.shape
    T = B * S
    x2 = x.reshape(T, D)
    ncol = w_in.shape[1]
    p = _matmul(x2, _pad_cols(w_in, 512).astype(BF16), norm_w=norm_w)[:, :ncol].reshape(B, S, ncol)
    ml_p, rw_p = p[..., :ML_COLS], p[..., ML_COLS:]
    qk, v_m, o_m, gif = _split(ml_p, ML_SPLITS)
    qk = jax.nn.silu(_causal_conv(qk, conv_w, conv_b))
    q_m, k_m = jnp.split(qk, 2, axis=-1)
    gif = gif + gate_b
    h_m = _mlstm_j(q_m.reshape(B, S, ML_HEADS, ML_DK), k_m.reshape(B, S, ML_HEADS, ML_DK),
                   v_m.reshape(B, S, ML_HEADS, ML_DV), gif[..., :ML_HEADS], gif[..., ML_HEADS:])
    y_m = jax.nn.sigmoid(o_m) * h_m.reshape(B, S, ML_HEADS * ML_DV)
    rw_p = rw_p + (_token_shift(rw_p) - rw_p) * mu
    r, k, v, xw, xa, xg = _split(rw_p, RW_SPLITS)
    logw = -math.exp(-0.5) * jax.nn.sigmoid(w0 + jnp.tanh(xw) @ w_up)
    a = jax.nn.sigmoid(a0 + xa @ a_up)
    g = jax.nn.sigmoid(xg) @ g_up

    def heads(t):
        return t.reshape(B, S, RW_HEADS, RW_HEAD)

    kk = heads(k * k_k)
    kk = kk * lax.rsqrt(jnp.sum(kk * kk, axis=-1, keepdims=True) + 1e-12)
    k = k * (1.0 + (a - 1.0) * k_a)
    rh, kh, vh, ah = heads(r), heads(k), heads(v), heads(a)
    y = heads(_rwkv_scan(r, jnp.exp(logw), k, v, kk.reshape(B, S, RW_DIM), kk.reshape(B, S, RW_DIM) * a))
    y_mu = jnp.mean(y, axis=-1, keepdims=True)
    y_var = jnp.mean(jnp.square(y - y_mu), axis=-1, keepdims=True)
    yn = ((y - y_mu) * lax.rsqrt(y_var + RW_LN_EPS)).reshape(B, S, RW_DIM) * ln_w + ln_b
    bonus = (jnp.sum(rh * kh * r_k.reshape(RW_HEADS, RW_HEAD), axis=-1, keepdims=True) * vh).reshape(B, S, RW_DIM)
    y_r = (yn + bonus) * g
    y_cat = jnp.concatenate([y_m, y_r], axis=-1).reshape(T, D)
    return _matmul(y_cat, w_out.astype(BF16), residual=x2).reshape(B, S, D)


def _nsa(x, norm_w, w_in, gate_b, cmp_pos, cmp_w1, cmp_w2, w_out):
    B, S, D = x.shape
    T = B * S
    x2 = x.reshape(T, D)
    p = _matmul(x2, _pad_cols(w_in, 512).astype(BF16), norm_w=norm_w)
    o = _nsa_attention(p.reshape(B, S, -1), gate_b, cmp_pos, cmp_w1, cmp_w2)
    return _matmul(o.reshape(T, NSA_HEADS * NSA_HD), w_out.astype(BF16), residual=x2).reshape(B, S, D)


def _nsa_j(x, norm_w, w_in, gate_b, cmp_pos, cmp_w1, cmp_w2, w_out):
    B, S, D = x.shape
    T = B * S
    x2 = x.reshape(T, D)
    G, R, HD = NSA_KV, NSA_REP, NSA_HD
    QC = NSA_QCHUNK
    scale = HD ** -0.5
    pos = jnp.arange(S)
    ncol = w_in.shape[1]
    p = _matmul(x2, _pad_cols(w_in, 512).astype(BF16), norm_w=norm_w)[:, :ncol].reshape(B, S, ncol)
    q, kc, vc, ks, vs, kw, vw, gl = _split(p, NSA_SPLITS)
    q = q.reshape(B, S, G, R, HD)

    def kvh(t):
        return t.reshape(B, S, G, HD)

    q_rot = _rope(q, pos)
    ks = _rope(kvh(ks), pos)
    kw = _rope(kvh(kw), pos)
    vs = kvh(vs)
    vw = kvh(vw)
    gates = jax.nn.sigmoid(gl + gate_b).reshape(B, S, 3, G, R)
    n_cmp = (S - CMP_BLOCK) // CMP_STRIDE + 1
    cidx = np.arange(n_cmp)[:, None] * CMP_STRIDE + np.arange(CMP_BLOCK)[None, :]

    def compress(t, pe, w1, w2):
        blocks = t[:, cidx] + pe[None, None, :, None, :]
        return jax.nn.gelu(jnp.einsum('bjpgd,pde->bjge', blocks, w1)) @ w2

    k_cmp = compress(kvh(kc), cmp_pos[0], cmp_w1[0], cmp_w2[0])
    v_cmp = compress(kvh(vc), cmp_pos[1], cmp_w1[1], cmp_w2[1])
    cmp_end = jnp.asarray(cidx[:, -1])
    n_sel = S // SEL_BLOCK
    c0 = np.arange(n_cmp)[:, None] * CMP_STRIDE
    s0 = np.arange(n_sel)[None, :] * SEL_BLOCK
    ov = np.clip(np.minimum(c0 + CMP_BLOCK, s0 + SEL_BLOCK) - np.maximum(c0, s0), 0, None) / CMP_BLOCK
    ov = jnp.asarray(ov, dtype=F32)
    n_top = min(SEL_TOPK, n_sel)
    k_blk = ks.reshape(B, n_sel, SEL_BLOCK, G, HD).transpose(0, 3, 1, 2, 4)
    v_blk = vs.reshape(B, n_sel, SEL_BLOCK, G, HD).transpose(0, 3, 1, 2, 4)
    kw_pad = jnp.pad(kw, ((0, 0), (WINDOW, 0), (0, 0), (0, 0)))
    vw_pad = jnp.pad(vw, ((0, 0), (WINDOW, 0), (0, 0), (0, 0)))
    bi = jnp.arange(B)[:, None, None, None]
    gi = jnp.arange(G)[None, :, None, None]
    blk_ids = jnp.arange(n_sel)

    def chunk(c):
        t0 = c * QC
        qt = t0 + jnp.arange(QC)
        q_c = lax.dynamic_slice_in_dim(q, t0, QC, axis=1)
        qr_c = lax.dynamic_slice_in_dim(q_rot, t0, QC, axis=1)
        g_c = lax.dynamic_slice_in_dim(gates, t0, QC, axis=1)
        p_cmp = _masked_softmax(jnp.einsum('bqgrd,bjgd->bgrqj', q_c, k_cmp) * scale,
                                cmp_end[None, :] <= qt[:, None])
        o_cmp = jnp.einsum('bgrqj,bjgd->bqgrd', p_cmp, v_cmp)
        imp = jnp.einsum('bgrqj,js->bgqs', p_cmp, ov)
        cur = (qt // SEL_BLOCK)[:, None]
        forced = (blk_ids[None, :] == 0) | (blk_ids[None, :] == cur) | (blk_ids[None, :] == cur - 1)
        score = jnp.where(forced, jnp.inf, jnp.where(blk_ids[None, :] <= cur, imp, -jnp.inf))
        top_v, top_i = lax.top_k(score, n_top)
        k_g = k_blk[bi, gi, top_i].reshape(B, G, QC, n_top * SEL_BLOCK, HD)
        v_g = v_blk[bi, gi, top_i].reshape(B, G, QC, n_top * SEL_BLOCK, HD)
        kpos = (top_i[..., None] * SEL_BLOCK + jnp.arange(SEL_BLOCK)).reshape(B, G, QC, n_top * SEL_BLOCK)
        kmask = jnp.repeat(top_v > -jnp.inf, SEL_BLOCK, axis=-1) & (kpos <= qt[None, None, :, None])
        p_slc = _masked_softmax(jnp.einsum('bqgrd,bgqkd->bgrqk', qr_c, k_g) * scale, kmask[:, :, None])
        o_slc = jnp.einsum('bgrqk,bgqkd->bqgrd', p_slc, v_g)
        k_w = lax.dynamic_slice_in_dim(kw_pad, t0, QC + WINDOW, axis=1)
        v_w = lax.dynamic_slice_in_dim(vw_pad, t0, QC + WINDOW, axis=1)
        wpos = t0 - WINDOW + jnp.arange(QC + WINDOW)
        dist = qt[:, None] - wpos[None, :]
        wmask = (dist >= 0) & (dist < WINDOW) & (wpos[None, :] >= 0)
        p_win = _masked_softmax(jnp.einsum('bqgrd,bkgd->bgrqk', qr_c, k_w) * scale, wmask)
        o_win = jnp.einsum('bgrqk,bkgd->bqgrd', p_win, v_w)
        return (g_c[:, :, 0, :, :, None] * o_cmp + g_c[:, :, 1, :, :, None] * o_slc
                + g_c[:, :, 2, :, :, None] * o_win)

    o = lax.map(chunk, jnp.arange(S // QC))
    o = o.transpose(1, 0, 2, 3, 4, 5).reshape(T, NSA_HEADS * HD)
    return _matmul(o, w_out.astype(BF16), residual=x2).reshape(B, S, D)


def _cross_attn(x, norm_w, mem, norm_mem, wq, wk, wv, wo):
    B, S, D = x.shape
    T = B * S
    M = mem.shape[1]
    x2 = x.reshape(T, D)
    q = _matmul(x2, wq.astype(BF16), norm_w=norm_w).reshape(B, S, CA_HEADS, CA_HD)
    m2 = mem.reshape(B * M, D)
    kv = _matmul(m2, jnp.concatenate([wk, wv], axis=1).astype(BF16), norm_w=norm_mem)
    k = kv[:, :D].reshape(B, M, CA_HEADS, CA_HD)
    v = kv[:, D:].reshape(B, M, CA_HEADS, CA_HD)
    s = jnp.einsum('bshd,bmhd->bhsm', q, k) * (CA_HD ** -0.5)
    p = jax.nn.softmax(s, axis=-1)
    o = jnp.einsum('bhsm,bmhd->bshd', p, v).reshape(T, D)
    return _matmul(o, wo.astype(BF16), residual=x2).reshape(B, S, D)


def _hier_moe(x, norm_w, wg, bg, we, be, w_gate, w_up, w_down):
    B, S, D = x.shape
    h = _rmsnorm_j(x, norm_w)
    T = B * S
    A = T * MOE_TOPK
    xt = h.reshape(T, D)
    lg = xt @ wg + bg
    grp = jnp.argmax(lg, axis=-1)
    p_grp = jnp.take_along_axis(jax.nn.softmax(lg, axis=-1), grp[:, None], axis=-1)
    le = (xt @ we + be).reshape(T, MOE_GROUPS, MOE_PER_GROUP)
    le = jnp.take_along_axis(le, grp[:, None, None], axis=1)[:, 0]
    top_l, top_e = lax.top_k(le, MOE_TOPK)
    wts = (p_grp * jax.nn.softmax(top_l, axis=-1)).reshape(A)
    eid = (grp[:, None] * MOE_PER_GROUP + top_e).reshape(A)
    tok = jnp.repeat(jnp.arange(T, dtype=jnp.int32), MOE_TOPK)
    order = jnp.argsort(eid)
    e_s, tok_s, w_s = eid[order], tok[order], wts[order]
    counts = jax.ops.segment_sum(jnp.ones((A,), jnp.int32), eid, num_segments=MOE_EXPERTS)
    padded = (counts + MOE_ROWS - 1) // MOE_ROWS * MOE_ROWS
    ends = jnp.cumsum(padded)
    dest = (ends - padded)[e_s] + jnp.arange(A, dtype=jnp.int32) - (jnp.cumsum(counts) - counts)[e_s]
    n_chunks = -(-A // MOE_ROWS) + MOE_EXPERTS
    rows = n_chunks * MOE_ROWS
    row_tok = jnp.full((rows,), T, jnp.int32).at[dest].set(tok_s)
    row_w = jnp.zeros((rows,), F32).at[dest].set(w_s)
    chunk_e = jnp.minimum(jnp.searchsorted(ends, jnp.arange(n_chunks, dtype=jnp.int32) * MOE_ROWS, side='right'),
                          MOE_EXPERTS - 1)
    x_pad = jnp.concatenate([xt, jnp.zeros((1, D), xt.dtype)], axis=0)

    def expert_rows(args):
        t_idx, w_r, e = args
        xr = x_pad[t_idx]
        hid = jax.nn.silu(xr @ w_gate[e]) * (xr @ w_up[e])
        y = hid @ w_down[e]
        return y * w_r[:, None]

    out = lax.map(expert_rows, (row_tok.reshape(n_chunks, MOE_ROWS), row_w.reshape(n_chunks, MOE_ROWS), chunk_e))
    y = jax.ops.segment_sum(out.reshape(rows, D), row_tok, num_segments=T + 1)[:T]
    return x + y.reshape(B, S, D)


def kernel(x, mem, norm_mix, norm_cross, norm_mem, norm_ffn, norm_final,
           ab_w_in, ml_conv_w, ml_conv_b, ml_gate_b, rw_mu, rw_w0, rw_w_up, rw_a0, rw_a_up, rw_g_up,
           rw_k_k, rw_k_a, rw_r_k, rw_ln_w, rw_ln_b, ab_w_out,
           nsa_w_in, nsa_gate_b, cmp_pos, cmp_w1, cmp_w2, nsa_w_out,
           ca_wq, ca_wk, ca_wv, ca_wo,
           moe_wg, moe_bg, moe_we, moe_be, moe_w_gate, moe_w_up, moe_w_down):
    B, S, D = x.shape
    for l in range(DEPTH):
        j = l // 2
        if l % 2 == 0:
            x = _ab_mixer(x, norm_mix[l], ab_w_in[j], ml_conv_w[j], ml_conv_b[j], ml_gate_b[j], rw_mu[j], rw_w0[j],
                          rw_w_up[j], rw_a0[j], rw_a_up[j], rw_g_up[j], rw_k_k[j], rw_k_a[j], rw_r_k[j],
                          rw_ln_w[j], rw_ln_b[j], ab_w_out[j])
        else:
            x = _nsa(x, norm_mix[l], nsa_w_in[j], nsa_gate_b[j], cmp_pos[j], cmp_w1[j], cmp_w2[j], nsa_w_out[j])
        x = _cross_attn(x, norm_cross[l], mem, norm_mem[l], ca_wq[l], ca_wk[l], ca_wv[l], ca_wo[l])
        x = _hier_moe(x, norm_ffn[l], moe_wg[l], moe_bg[l], moe_we[l], moe_be[l],
                      moe_w_gate[l], moe_w_up[l], moe_w_down[l])
    return _rmsnorm_rows(x.reshape(B * S, D), norm_final).reshape(B, S, D)
```

```python
import functools
import math

import jax
import jax.numpy as jnp
import numpy as np
from jax import lax
from jax.experimental import pallas as pl
from jax.experimental.pallas import tpu as pltpu

F32 = jnp.float32
BF16 = jnp.bfloat16

D_MODEL = 2048
DEPTH = 2
EPS = 1e-6
ROPE_THETA = 500000.0

ML_HEADS = 4
ML_DV = D_MODEL // 2 // ML_HEADS
ML_DK = ML_DV // 2
ML_CHUNK = 64
ML_CONV = 4
RW_HEAD = 64
RW_HEADS = D_MODEL // 2 // RW_HEAD
RW_DIM = RW_HEADS * RW_HEAD
RW_LORA_W = 64
RW_LORA_A = 64
RW_LORA_G = 128
RW_LN_EPS = 64e-5
ML_SPLITS = (2 * ML_HEADS * ML_DK, ML_HEADS * ML_DV, ML_HEADS * ML_DV, 2 * ML_HEADS)
RW_SPLITS = (RW_DIM, RW_DIM, RW_DIM, RW_LORA_W, RW_LORA_A, RW_LORA_G)
ML_COLS = sum(ML_SPLITS)
RW_COLS = sum(RW_SPLITS)

NSA_HEADS = 16
NSA_KV = 4
NSA_REP = NSA_HEADS // NSA_KV
NSA_HD = D_MODEL // NSA_HEADS
ROPE_DIM = NSA_HD // 4
CMP_BLOCK = 32
CMP_STRIDE = 16
SEL_BLOCK = 64
SEL_TOPK = 16
WINDOW = 512
NSA_QCHUNK = 32
NSA_SPLITS = (NSA_HEADS * NSA_HD,) + (NSA_KV * NSA_HD,) * 6 + (3 * NSA_HEADS,)

CA_HEADS = 4
CA_HD = D_MODEL // CA_HEADS

MOE_GROUPS = 8
MOE_PER_GROUP = 8
MOE_EXPERTS = MOE_GROUPS * MOE_PER_GROUP
MOE_TOPK = 2
MOE_FF = D_MODEL // 4
MOE_ROWS = 128

V7X_VMEM_BYTES = 64 * 1024 * 1024
VMEM_LIMIT = 48 * 1024 * 1024
LANES = 128
SUBLANES = 8


def _cparams(sem):
    return pltpu.CompilerParams(dimension_semantics=sem, vmem_limit_bytes=VMEM_LIMIT)


def _mm_kernel(*refs, has_norm, has_res):
    it = iter(refs)
    x_ref = next(it)
    w_ref = next(it)
    nw_ref = next(it) if has_norm else None
    r_ref = next(it) if has_res else None
    o_ref = next(it)
    xs_ref = next(it)

    @pl.when(pl.program_id(1) == 0)
    def _():
        x = x_ref[...]
        if has_norm:
            ms = jnp.mean(x * x, axis=-1, keepdims=True)
            x = x * lax.rsqrt(ms + EPS) * nw_ref[...]
        xs_ref[...] = x.astype(BF16)

    acc = jnp.dot(xs_ref[...], w_ref[...], preferred_element_type=F32)
    if has_res:
        acc = acc + r_ref[...]
    o_ref[...] = acc


def _matmul(x, w_bf16, *, norm_w=None, residual=None, tm=1024, tn=512):
    M, K = x.shape
    N = w_bf16.shape[1]
    tm = min(tm, M)
    tn = min(tn, N)
    assert M % tm == 0 and N % tn == 0, (M, N, tm, tn)
    has_norm = norm_w is not None
    has_res = residual is not None
    in_specs = [pl.BlockSpec((tm, K), lambda i, j: (i, 0)),
                pl.BlockSpec((K, tn), lambda i, j: (0, j))]
    args = [x, w_bf16]
    if has_norm:
        in_specs.append(pl.BlockSpec((1, K), lambda i, j: (0, 0)))
        args.append(norm_w.reshape(1, K))
    if has_res:
        in_specs.append(pl.BlockSpec((tm, tn), lambda i, j: (i, j)))
        args.append(residual)
    return pl.pallas_call(
        functools.partial(_mm_kernel, has_norm=has_norm, has_res=has_res),
        grid=(M // tm, N // tn),
        in_specs=in_specs,
        out_specs=pl.BlockSpec((tm, tn), lambda i, j: (i, j)),
        out_shape=jax.ShapeDtypeStruct((M, N), F32),
        scratch_shapes=[pltpu.VMEM((tm, K), BF16)],
        compiler_params=_cparams(("parallel", "arbitrary")),
        name="mm_norm" if has_norm else "mm",
    )(*args)


def _pad_cols(w, mult):
    n = w.shape[1]
    pad = (-n) % mult
    if pad:
        w = jnp.pad(w, ((0, 0), (0, pad)))
    return w


def _rmsnorm_kernel(x_ref, w_ref, o_ref):
    x = x_ref[...]
    ms = jnp.mean(x * x, axis=-1, keepdims=True)
    o_ref[...] = x * lax.rsqrt(ms + EPS) * w_ref[...]


def _rmsnorm_rows(x, w, tm=512):
    M, K = x.shape
    return pl.pallas_call(
        _rmsnorm_kernel,
        grid=(M // tm,),
        in_specs=[pl.BlockSpec((tm, K), lambda i: (i, 0)), pl.BlockSpec((1, K), lambda i: (0, 0))],
        out_specs=pl.BlockSpec((tm, K), lambda i: (i, 0)),
        out_shape=jax.ShapeDtypeStruct((M, K), F32),
        compiler_params=_cparams(("parallel",)),
        name="rmsnorm",
    )(x, w.reshape(1, K))


def _rwkv_scan_kernel(r_ref, w_ref, k_ref, v_ref, kk_ref, kb_ref, y_ref, s_ref, *, tb, nb, npairs, gsz):
    @pl.when(pl.program_id(1) == 0)
    def _():
        s_ref[...] = jnp.zeros_like(s_ref)

    row = lax.broadcasted_iota(jnp.int32, (LANES, LANES), 0)
    col = lax.broadcasted_iota(jnp.int32, (LANES, LANES), 1)
    hmat = jnp.where((row // RW_HEAD) == (col // RW_HEAD), 1.0, 0.0).astype(BF16)
    vrow = lax.broadcasted_iota(jnp.int32, (RW_HEAD, LANES), 0)
    vcol = lax.broadcasted_iota(jnp.int32, (RW_HEAD, LANES), 1)
    diag = jnp.where((vcol % RW_HEAD) == vrow, 1.0, 0.0).astype(F32)

    def group_sum(parts):
        lhs = jnp.concatenate([q.astype(BF16) for q in parts], axis=0)
        out = jnp.dot(lhs, hmat, preferred_element_type=F32)
        return [out[i * RW_HEAD:(i + 1) * RW_HEAD] for i in range(len(parts))]

    def step(t8, carry):
        rows = pl.ds(pl.multiple_of(t8 * SUBLANES, SUBLANES), SUBLANES)
        for b in range(nb):
            for g in range(npairs // gsz):
                pairs = range(g * gsz, (g + 1) * gsz)
                lns = [pl.ds(p * LANES, LANES) for p in pairs]
                kk8 = [kk_ref[b, rows, ln] for ln in lns]
                w8 = [w_ref[b, rows, ln] for ln in lns]
                kb8 = [kb_ref[b, rows, ln] for ln in lns]
                k8 = [k_ref[b, rows, ln] for ln in lns]
                v8 = [v_ref[b, rows, ln] for ln in lns]
                r8 = [r_ref[b, rows, ln] for ln in lns]
                s = [s_ref[b, p] for p in pairs]
                ys = [[] for _ in pairs]
                for j in range(SUBLANES):
                    sl = slice(j, j + 1)
                    sk = group_sum([s[i] * kk8[i][sl] for i in range(gsz)])
                    vb = group_sum([diag * v8[i][sl] for i in range(gsz)])
                    s = [s[i] * w8[i][sl] - sk[i] * kb8[i][sl] + vb[i] * k8[i][sl] for i in range(gsz)]
                    yb = group_sum([s[i] * r8[i][sl] for i in range(gsz)])
                    for i in range(gsz):
                        ys[i].append(jnp.sum(yb[i] * diag, axis=0, keepdims=True))
                for i, p in enumerate(pairs):
                    s_ref[b, p] = s[i]
                    y_ref[b, rows, lns[i]] = jnp.concatenate(ys[i], axis=0)
        return carry

    lax.fori_loop(0, tb // SUBLANES, step, 0)


def _rwkv_scan(r, w, k, v, kk, kb, tb=256, nb=2, gsz=8):
    B, S, C = r.shape
    npairs = C // LANES
    tb = min(tb, S)
    nb = min(nb, B)
    spec = pl.BlockSpec((nb, tb, C), lambda b, t: (b, t, 0))
    return pl.pallas_call(
        functools.partial(_rwkv_scan_kernel, tb=tb, nb=nb, npairs=npairs, gsz=gsz),
        grid=(B // nb, S // tb),
        in_specs=[spec] * 6,
        out_specs=spec,
        out_shape=jax.ShapeDtypeStruct((B, S, C), F32),
        scratch_shapes=[pltpu.VMEM((nb, npairs, RW_HEAD, LANES), F32)],
        compiler_params=_cparams(("parallel", "arbitrary")),
        name="rwkv_scan",
    )(r, w, k, v, kk, kb)


NSA_NEG = -1e30
NSA_FORCED = 1e30
NSA_REMOVED = -3e30
NSA_COL_Q, NSA_COL_KC, NSA_COL_VC, NSA_COL_KS, NSA_COL_VS, NSA_COL_KW, NSA_COL_VW, NSA_COL_GL = (
    0, 16, 20, 24, 28, 32, 36, 40)


def _rope_tables(S):
    half = ROPE_DIM // 2
    inv = 1.0 / (ROPE_THETA ** (jnp.arange(half, dtype=F32) / half))
    ang = jnp.arange(S, dtype=F32)[:, None] * inv[None, :]
    cos, sin = jnp.cos(ang), jnp.sin(ang)
    one = jnp.ones((S, NSA_HD - ROPE_DIM), F32)
    zero = jnp.zeros((S, NSA_HD - ROPE_DIM), F32)
    zh = jnp.zeros((S, half), F32)
    return (jnp.concatenate([cos, cos, one], axis=1), jnp.concatenate([zh, sin, zero], axis=1),
            jnp.concatenate([-sin, zh, zero], axis=1))


def _rope_rows(x, c, s1, s2):
    half = ROPE_DIM // 2
    return x * c + pltpu.roll(x, half, 1) * s1 + pltpu.roll(x, NSA_HD - half, 1) * s2


def _nsa_prep_q_kernel(x_ref, c_ref, s1_ref, s2_ref, qt_ref, qrt_ref):
    x = x_ref[0] * (NSA_HD ** -0.5)
    xr = _rope_rows(x, c_ref[...], s1_ref[...], s2_ref[...])
    qt_ref[0, 0] = x.T.astype(BF16)
    qrt_ref[0, 0] = xr.T.astype(BF16)


def _nsa_prep_kv_kernel(ks_ref, kw_ref, vs_ref, vw_ref, c_ref, s1_ref, s2_ref, kso_ref, kwo_ref, vst_ref, vwt_ref):
    c, s1, s2 = c_ref[...], s1_ref[...], s2_ref[...]
    kso_ref[0, 0] = _rope_rows(ks_ref[0], c, s1, s2).astype(BF16)
    kwo_ref[0, 0] = _rope_rows(kw_ref[0], c, s1, s2).astype(BF16)
    vst_ref[0, 0] = vs_ref[0].T.astype(BF16)
    vwt_ref[0, 0] = vw_ref[0].T.astype(BF16)


def _nsa_compress_kernel(kc_ref, vc_ref, pe_ref, w1_ref, w2_ref, kcmp_ref, vcmpt_ref, *, ncp):
    for which, x_ref in enumerate((kc_ref, vc_ref)):
        za = jnp.zeros((ncp, NSA_HD), F32)
        zb = jnp.zeros((ncp, NSA_HD), F32)
        for p in range(CMP_STRIDE):
            xp = x_ref[pl.ds(p, ncp, stride=CMP_STRIDE), :]
            za = za + jnp.dot((xp + pe_ref[which, p:p + 1, :]).astype(BF16), w1_ref[which, p],
                              preferred_element_type=F32)
            zb = zb + jnp.dot((xp + pe_ref[which, CMP_STRIDE + p:CMP_STRIDE + p + 1, :]).astype(BF16),
                              w1_ref[which, CMP_STRIDE + p], preferred_element_type=F32)
        pre = za + pltpu.roll(zb, ncp - 1, 0)
        out = jnp.dot(jax.nn.gelu(pre).astype(BF16), w2_ref[which], preferred_element_type=F32)
        if which == 0:
            kcmp_ref[0, 0] = out.astype(BF16)
        else:
            vcmpt_ref[0, 0] = out.T.astype(BF16)


def _nsa_attn_kernel(qt_ref, qrt_ref, kcmp_ref, vcmpt_ref, ks_ref, kw_ref, vst_ref, vwt_ref, gl_ref, gb_ref, ovt_ref,
                     o_ref, sel_ref, ms_ref, ls_ref, accs_ref, *, tq, kt, ncp, nsel):
    R = NSA_REP
    g = pl.program_id(1)
    qi = pl.program_id(2)
    t0 = qi * tq
    q_t = jnp.concatenate([qt_ref[0, r] for r in range(R)], axis=1)
    qr_t = jnp.concatenate([qrt_ref[0, r] for r in range(R)], axis=1)
    qpos = t0 + lax.broadcasted_iota(jnp.int32, (1, tq), 1)

    def per_head(fn, a):
        return jnp.concatenate([fn(a[:, r * tq:(r + 1) * tq]) for r in range(R)], axis=1)

    s = jnp.dot(kcmp_ref[0, 0], q_t, preferred_element_type=F32)
    cend = lax.broadcasted_iota(jnp.int32, (ncp, 1), 0) * CMP_STRIDE + (CMP_BLOCK - 1)
    vis = cend <= qpos
    s = per_head(lambda a: jnp.where(vis, a, NSA_NEG), s)
    m = jnp.max(s, axis=0, keepdims=True)
    e = per_head(lambda a: jnp.where(vis, a, 0.0), jnp.exp(s - m))
    d = jnp.sum(e, axis=0, keepdims=True)
    p = e * (1.0 / jnp.where(d > 0, d, 1.0))
    ocmp_t = jnp.dot(vcmpt_ref[0, 0], p.astype(BF16), preferred_element_type=F32)

    psum = p[:, 0:tq]
    for r in range(1, R):
        psum = psum + p[:, r * tq:(r + 1) * tq]
    imp = jnp.dot(ovt_ref[...], psum, preferred_element_type=F32, precision=lax.Precision.HIGHEST)
    sidx = lax.broadcasted_iota(jnp.int32, (nsel, tq), 0)
    cur = qpos // SEL_BLOCK
    forced = (sidx == 0) | (sidx == cur) | (sidx == cur - 1)
    score = jnp.where(forced, NSA_FORCED, jnp.where(sidx <= cur, imp, NSA_NEG))
    sel = jnp.zeros((nsel, tq), F32)
    for _ in range(SEL_TOPK):
        mx = jnp.max(score, axis=0, keepdims=True)
        idx = jnp.min(jnp.where(score == mx, sidx, nsel), axis=0, keepdims=True)
        pick = (sidx == idx) & (mx > 0.5 * NSA_NEG)
        sel = jnp.where(pick, 1.0, sel)
        score = jnp.where(pick, NSA_REMOVED, score)
    sel_ref[...] = sel

    def online_update(s, valid, v_t, m_ref, l_ref, acc_ref):
        s = per_head(lambda a: jnp.where(valid, a, NSA_NEG), s)
        m_old = m_ref[...]
        m_new = jnp.maximum(m_old, jnp.max(s, axis=0, keepdims=True))
        alpha = jnp.exp(m_old - m_new)
        pexp = jnp.exp(s - m_new)
        l_ref[...] = alpha * l_ref[...] + jnp.sum(pexp, axis=0, keepdims=True)
        acc_ref[...] = alpha * acc_ref[...] + jnp.dot(v_t, pexp.astype(BF16), preferred_element_type=F32)
        m_ref[...] = m_new

    ms_ref[...] = jnp.full(ms_ref.shape, NSA_NEG, F32)
    ls_ref[...] = jnp.zeros(ls_ref.shape, F32)
    accs_ref[...] = jnp.zeros(accs_ref.shape, F32)

    nblk = kt // SEL_BLOCK

    def sel_body(c, carry):
        k0 = pl.multiple_of(c * kt, kt)
        s = jnp.dot(ks_ref[0, 0, pl.ds(k0, kt), :], qr_t, preferred_element_type=F32)
        selrows = sel_ref[pl.ds(pl.multiple_of(c * nblk, nblk), nblk), :]
        selexp = jnp.concatenate(
            [jnp.broadcast_to(selrows[b:b + 1], (SEL_BLOCK, tq)) for b in range(nblk)], axis=0)
        kpos = k0 + lax.broadcasted_iota(jnp.int32, (kt, 1), 0)
        valid = (selexp > 0.5) & (kpos <= qpos)
        online_update(s, valid, vst_ref[0, 0, :, pl.ds(k0, kt)], ms_ref, ls_ref, accs_ref)
        return carry

    lax.fori_loop(0, t0 // kt + 1, sel_body, 0)

    nwk = WINDOW + tq
    k0 = pl.multiple_of(jnp.maximum(t0 - WINDOW, 0), tq)
    s = jnp.dot(kw_ref[0, 0, pl.ds(k0, nwk), :], qr_t, preferred_element_type=F32)
    dist = qpos - (k0 + lax.broadcasted_iota(jnp.int32, (nwk, 1), 0))
    inwin = (dist >= 0) & (dist < WINDOW)
    s = per_head(lambda a: jnp.where(inwin, a, NSA_NEG), s)
    ew = jnp.exp(s - jnp.max(s, axis=0, keepdims=True))
    owin_t = (jnp.dot(vwt_ref[0, 0, :, pl.ds(k0, nwk)], ew.astype(BF16), preferred_element_type=F32)
              * (1.0 / jnp.sum(ew, axis=0, keepdims=True)))

    oslc_t = accs_ref[...] * (1.0 / ls_ref[...])

    gates_t = jax.nn.sigmoid(gl_ref[0] + gb_ref[...]).T
    rid = lax.broadcasted_iota(jnp.int32, (LANES, 1), 0)

    def gate_row(which, r):
        return jnp.sum(jnp.where(rid == which * NSA_HEADS + g * R + r, gates_t, 0.0), axis=0, keepdims=True)

    for r in range(R):
        cols = slice(r * tq, (r + 1) * tq)
        o_t = (gate_row(0, r) * ocmp_t[:, cols] + gate_row(1, r) * oslc_t[:, cols]
               + gate_row(2, r) * owin_t[:, cols])
        o_ref[0, :, r * NSA_HD:(r + 1) * NSA_HD] = o_t.T


def _nsa_attention(p3, gate_b, cmp_pos, cmp_w1, cmp_w2, *, tq=128, kt=512, tp=512):
    B, S, _ = p3.shape
    G, R, HD = NSA_KV, NSA_REP, NSA_HD
    assert CMP_BLOCK == 2 * CMP_STRIDE and S % kt == 0 and WINDOW % tq == 0 and tq == LANES
    ncp = S // CMP_STRIDE
    nsel = S // SEL_BLOCK
    n_cmp = (S - CMP_BLOCK) // CMP_STRIDE + 1
    c_tab, s1_tab, s2_tab = _rope_tables(S)
    tab_spec3 = pl.BlockSpec((tp, HD), lambda b, h, i: (i, 0))

    qt, qrt = pl.pallas_call(
        _nsa_prep_q_kernel,
        grid=(B, NSA_HEADS, S // tp),
        in_specs=[pl.BlockSpec((1, tp, HD), lambda b, h, i: (b, i, NSA_COL_Q + h)), tab_spec3, tab_spec3, tab_spec3],
        out_specs=[pl.BlockSpec((1, 1, HD, tp), lambda b, h, i: (b, h, 0, i))] * 2,
        out_shape=[jax.ShapeDtypeStruct((B, NSA_HEADS, HD, S), BF16)] * 2,
        compiler_params=_cparams(("parallel", "parallel", "parallel")),
        name="nsa_prep_q",
    )(p3, c_tab, s1_tab, s2_tab)

    def col_spec(col0):
        return pl.BlockSpec((1, tp, HD), lambda b, g, i: (b, i, col0 + g))

    ks_rot, kw_rot, vs_t, vw_t = pl.pallas_call(
        _nsa_prep_kv_kernel,
        grid=(B, G, S // tp),
        in_specs=[col_spec(NSA_COL_KS), col_spec(NSA_COL_KW), col_spec(NSA_COL_VS), col_spec(NSA_COL_VW),
                  tab_spec3, tab_spec3, tab_spec3],
        out_specs=[pl.BlockSpec((1, 1, tp, HD), lambda b, g, i: (b, g, i, 0))] * 2
        + [pl.BlockSpec((1, 1, HD, tp), lambda b, g, i: (b, g, 0, i))] * 2,
        out_shape=[jax.ShapeDtypeStruct((B, G, S, HD), BF16)] * 2 + [jax.ShapeDtypeStruct((B, G, HD, S), BF16)] * 2,
        compiler_params=_cparams(("parallel", "parallel", "parallel")),
        name="nsa_prep_kv",
    )(p3, p3, p3, p3, c_tab, s1_tab, s2_tab)

    k_cmp, v_cmp_t = pl.pallas_call(
        functools.partial(_nsa_compress_kernel, ncp=ncp),
        grid=(B, G),
        in_specs=[pl.BlockSpec((None, S, HD), lambda b, g: (b, 0, NSA_COL_KC + g)),
                  pl.BlockSpec((None, S, HD), lambda b, g: (b, 0, NSA_COL_VC + g)),
                  pl.BlockSpec((2, CMP_BLOCK, HD), lambda b, g: (0, 0, 0)),
                  pl.BlockSpec((2, CMP_BLOCK, HD, HD), lambda b, g: (0, 0, 0, 0)),
                  pl.BlockSpec((2, HD, HD), lambda b, g: (0, 0, 0))],
        out_specs=[pl.BlockSpec((1, 1, ncp, HD), lambda b, g: (b, g, 0, 0)),
                   pl.BlockSpec((1, 1, HD, ncp), lambda b, g: (b, g, 0, 0))],
        out_shape=[jax.ShapeDtypeStruct((B, G, ncp, HD), BF16), jax.ShapeDtypeStruct((B, G, HD, ncp), BF16)],
        compiler_params=_cparams(("parallel", "parallel")),
        name="nsa_compress",
    )(p3, p3, cmp_pos, cmp_w1.astype(BF16), cmp_w2.astype(BF16))

    c0 = np.arange(ncp)[None, :] * CMP_STRIDE
    s0 = np.arange(nsel)[:, None] * SEL_BLOCK
    ov_t = np.clip(np.minimum(c0 + CMP_BLOCK, s0 + SEL_BLOCK) - np.maximum(c0, s0), 0, None) / CMP_BLOCK
    ov_t = ov_t * (np.arange(ncp)[None, :] < n_cmp)
    gb = jnp.pad(gate_b, (0, LANES - gate_b.shape[0])).reshape(1, LANES)
    ncols = R * tq

    def full_kv(shape):
        return pl.BlockSpec((1, 1) + shape, lambda b, g, i: (b, g, 0, 0))

    return pl.pallas_call(
        functools.partial(_nsa_attn_kernel, tq=tq, kt=kt, ncp=ncp, nsel=nsel),
        grid=(B, G, S // tq),
        in_specs=[pl.BlockSpec((1, R, HD, tq), lambda b, g, i: (b, g, 0, i)),
                  pl.BlockSpec((1, R, HD, tq), lambda b, g, i: (b, g, 0, i)),
                  full_kv((ncp, HD)), full_kv((HD, ncp)),
                  full_kv((S, HD)), full_kv((S, HD)), full_kv((HD, S)), full_kv((HD, S)),
                  pl.BlockSpec((1, tq, LANES), lambda b, g, i: (b, i, NSA_COL_GL)),
                  pl.BlockSpec((1, LANES), lambda b, g, i: (0, 0)),
                  pl.BlockSpec((nsel, ncp), lambda b, g, i: (0, 0))],
        out_specs=pl.BlockSpec((1, tq, R * HD), lambda b, g, i: (b, i, g)),
        out_shape=jax.ShapeDtypeStruct((B, S, NSA_HEADS * HD), F32),
        scratch_shapes=[pltpu.VMEM((nsel, tq), F32),
                        pltpu.VMEM((1, ncols), F32), pltpu.VMEM((1, ncols), F32), pltpu.VMEM((HD, ncols), F32)],
        compiler_params=_cparams(("parallel", "parallel", "arbitrary")),
        name="nsa_attn",
    )(qt, qrt, k_cmp, v_cmp_t, ks_rot, kw_rot, vs_t, vw_t, p3, gb, jnp.asarray(ov_t, F32))


def _split(x, sizes):
    return jnp.split(x, [int(s) for s in np.cumsum(sizes)[:-1]], axis=-1)


def _rmsnorm_j(x, w):
    return x * lax.rsqrt(jnp.mean(x * x, axis=-1, keepdims=True) + EPS) * w


def _masked_softmax(s, mask):
    s = jnp.where(mask, s.astype(F32), -jnp.inf)
    m = jnp.max(s, axis=-1, keepdims=True)
    m = jnp.where(jnp.isfinite(m), m, 0.0)
    e = jnp.exp(s - m)
    d = jnp.sum(e, axis=-1, keepdims=True)
    return e / jnp.where(d > 0, d, 1.0)


def _rope(x, pos):
    half = ROPE_DIM // 2
    inv = 1.0 / (ROPE_THETA ** (jnp.arange(half, dtype=F32) / half))
    ang = pos.astype(F32)[:, None] * inv[None, :]
    shape = (ang.shape[0],) + (1,) * (x.ndim - 3) + (half,)
    cos = jnp.cos(ang).reshape(shape)
    sin = jnp.sin(ang).reshape(shape)
    x1 = x[..., :half]
    x2 = x[..., half:ROPE_DIM]
    return jnp.concatenate([x1 * cos - x2 * sin, x2 * cos + x1 * sin, x[..., ROPE_DIM:]], axis=-1)


def _token_shift(p):
    return jnp.pad(p, ((0, 0), (1, 0), (0, 0)))[:, :-1]


def _causal_conv(x, w, b):
    k_len = w.shape[0]
    s_len = x.shape[1]
    xp = jnp.pad(x, ((0, 0), (k_len - 1, 0), (0, 0)))
    y = b
    for j in range(k_len):
        y = y + xp[:, j:j + s_len] * w[j]
    return y


def _mlstm_j(q, k, v, i_pre, f_pre):
    B, S, H, DK = q.shape
    DV = v.shape[-1]
    L = ML_CHUNK
    NC = S // L

    def chunks(t):
        t = t.astype(F32).reshape((B, NC, L, H) + t.shape[3:])
        return t.transpose((1, 0, 3, 2) + tuple(range(4, t.ndim)))

    qc = chunks(q) * (DK ** -0.5)
    kc = chunks(k)
    vc = chunks(v)
    ic = chunks(i_pre)
    lfc = chunks(jax.nn.log_sigmoid(f_pre.astype(F32)))
    causal = jnp.asarray(np.tril(np.ones((L, L), dtype=bool)))

    def step(carry, xs):
        C, n, m = carry
        q_, k_, v_, i_, lf = xs
        b = jnp.cumsum(lf, axis=-1)
        dmat = jnp.where(causal, b[..., :, None] - b[..., None, :] + i_[..., None, :], -jnp.inf)
        inter = b + m[..., None]
        m_row = jnp.maximum(inter, jnp.max(dmat, axis=-1))
        w_in = jnp.exp(dmat - m_row[..., None])
        w_st = jnp.exp(inter - m_row)
        s = jnp.einsum('bhjd,bhld->bhjl', q_, k_) * w_in
        num = w_st[..., None] * jnp.einsum('bhjd,bhde->bhje', q_, C) + jnp.einsum('bhjl,bhle->bhje', s, v_)
        den = w_st * jnp.einsum('bhjd,bhd->bhj', q_, n) + jnp.sum(s, axis=-1)
        h = num / jnp.maximum(jnp.abs(den), jnp.exp(-m_row))[..., None]
        b_last = b[..., -1]
        g_key = b_last[..., None] - b + i_
        m_new = jnp.maximum(b_last + m, jnp.max(g_key, axis=-1))
        wk = jnp.exp(g_key - m_new[..., None])
        decay = jnp.exp(b_last + m - m_new)
        C_new = decay[..., None, None] * C + jnp.einsum('bhld,bhle->bhde', k_ * wk[..., None], v_)
        n_new = decay[..., None] * n + jnp.einsum('bhl,bhld->bhd', wk, k_)
        return (C_new, n_new, m_new), h

    init = (jnp.zeros((B, H, DK, DV), F32), jnp.zeros((B, H, DK), F32), jnp.zeros((B, H), F32))
    _, hs = lax.scan(step, init, (qc, kc, vc, ic, lfc))
    return hs.transpose(1, 0, 3, 2, 4).reshape(B, S, H, DV)


def _rwkv7_scan_j(r, w, k, v, kk, a):
    def step(state, xs):
        r_t, w_t, k_t, v_t, kk_t, a_t = xs
        sk = jnp.einsum('bhvk,bhk->bhv', state, kk_t)
        state = (state * w_t[:, :, None, :] - sk[..., None] * (kk_t * a_t)[:, :, None, :]
                 + v_t[..., None] * k_t[:, :, None, :])
        return state, jnp.einsum('bhvk,bhk->bhv', state, r_t)

    B, S, H, N = r.shape
    xs = tuple(t.transpose(1, 0, 2, 3) for t in (r, w, k, v, kk, a))
    _, y = lax.scan(step, jnp.zeros((B, H, N, N), F32), xs)
    return y.transpose(1, 0, 2, 3)


AB_COL_GATES = 50 * LANES
AB_COLS_PAD = 52 * LANES


def _ab_permute_cols(w_in):
    ml, rw = w_in[:, :ML_COLS], w_in[:, ML_COLS:]
    main, gif = ml[:, :ML_COLS - 2 * ML_HEADS], ml[:, ML_COLS - 2 * ML_HEADS:]
    w = jnp.concatenate([main, rw, gif], axis=1)
    return jnp.pad(w, ((0, 0), (0, AB_COLS_PAD - w.shape[1])))


def _mlstm_kernel(gb_ref, qk_ref, v_ref, o_ref, gcol_ref, grow_ref, cw_ref, cb_ref, y_ref,
                  halo_ref, c_ref, n_ref, m_ref, qc_ref, kc_ref, *, tb):
    H, DK, DV, L = ML_HEADS, ML_DK, ML_DV, ML_CHUNK
    HALO = SUBLANES

    @pl.when(pl.program_id(1) == 0)
    def _():
        halo_ref[...] = jnp.zeros_like(halo_ref)
        c_ref[...] = jnp.zeros_like(c_ref)
        n_ref[...] = jnp.zeros_like(n_ref)
        m_ref[...] = jnp.zeros_like(m_ref)

    x = qk_ref[0]
    xe = jnp.concatenate([halo_ref[...], x], axis=0)
    y = cb_ref[...]
    for j in range(ML_CONV):
        lo = HALO - (ML_CONV - 1) + j
        y = y + xe[lo:lo + tb] * cw_ref[j:j + 1, :]
    halo_ref[...] = x[tb - HALO:]
    y = y * jax.nn.sigmoid(y)
    qc_ref[...] = y[:, :H * DK] * (DK ** -0.5)
    kc_ref[...] = y[:, H * DK:]

    r_i = lax.broadcasted_iota(jnp.int32, (L, L), 0)
    c_i = lax.broadcasted_iota(jnp.int32, (L, L), 1)
    tri = c_i <= r_i

    def chunk(c, carry):
        rows = pl.ds(pl.multiple_of(c * L, L), L)
        for h in range(H):
            q = qc_ref[rows, h * DK:(h + 1) * DK]
            k = kc_ref[rows, h * DK:(h + 1) * DK]
            v = v_ref[0, rows, h * DV:(h + 1) * DV]
            gc = gcol_ref[0, h, rows, :]
            gr = grow_ref[0, h, c]
            i_col = gc[:, 0:1] + gb_ref[h]
            lf_col = jax.nn.log_sigmoid(gc[:, 1:2] + gb_ref[H + h])
            i_row = gr[0:1, :] + gb_ref[h]
            lf_row = jax.nn.log_sigmoid(gr[1:2, :] + gb_ref[H + h])
            b_col = jnp.sum(jnp.where(tri, lf_row, 0.0), axis=1, keepdims=True)
            b_row = jnp.sum(jnp.where(c_i >= r_i, lf_col, 0.0), axis=0, keepdims=True)
            dmat = jnp.where(tri, b_col - b_row + i_row, NSA_NEG)
            m_old = m_ref[h]
            inter = b_col + m_old
            m_row = jnp.maximum(inter, jnp.max(dmat, axis=1, keepdims=True))
            w_in = jnp.exp(dmat - m_row)
            w_st = jnp.exp(inter - m_row)
            qb = q.astype(BF16)
            s = lax.dot_general(qb, k.astype(BF16), (((1,), (1,)), ((), ())), preferred_element_type=F32) * w_in
            num = (w_st * jnp.dot(qb, c_ref[h].astype(BF16), preferred_element_type=F32)
                   + jnp.dot(s.astype(BF16), v.astype(BF16), preferred_element_type=F32))
            den = w_st * jnp.sum(q * n_ref[h], axis=1, keepdims=True) + jnp.sum(s, axis=1, keepdims=True)
            hid = num * (1.0 / jnp.maximum(jnp.abs(den), jnp.exp(-m_row)))
            y_ref[0, rows, h * DV:(h + 1) * DV] = jax.nn.sigmoid(o_ref[0, rows, h * DV:(h + 1) * DV]) * hid
            b_last = b_col[L - 1:L, :]
            g_key = b_last - b_col + i_col
            m_new = jnp.maximum(b_last + m_old, jnp.max(g_key, axis=0, keepdims=True))
            wk = jnp.exp(g_key - m_new)
            decay = jnp.exp(b_last + m_old - m_new)
            kw_t = (k * wk).T.astype(BF16)
            c_ref[h] = decay * c_ref[h] + jnp.dot(kw_t, v.astype(BF16), preferred_element_type=F32)
            n_ref[h] = decay * n_ref[h] + jnp.sum(wk * k, axis=0, keepdims=True)
            m_ref[h] = m_new
        return carry

    lax.fori_loop(0, tb // L, chunk, 0)


def _mlstm(p3, conv_w, conv_b, gate_b, tb=512):
    B, S, _ = p3.shape
    H, DK, DV, L = ML_HEADS, ML_DK, ML_DV, ML_CHUNK
    tb = min(tb, S)
    nc = S // L
    gif = p3[:, :, AB_COL_GATES:AB_COL_GATES + 2 * H].reshape(B, S, 2, H)
    gcol = gif.transpose(0, 3, 1, 2)
    grow = gif.reshape(B, nc, L, 2, H).transpose(0, 4, 1, 3, 2)
    qkw = 2 * H * DK
    return pl.pallas_call(
        functools.partial(_mlstm_kernel, tb=tb),
        grid_spec=pltpu.PrefetchScalarGridSpec(
            num_scalar_prefetch=1, grid=(B, S // tb),
            in_specs=[pl.BlockSpec((1, tb, qkw), lambda b, t, gb: (b, t, 0)),
                      pl.BlockSpec((1, tb, H * DV), lambda b, t, gb: (b, t, 1)),
                      pl.BlockSpec((1, tb, H * DV), lambda b, t, gb: (b, t, 2)),
                      pl.BlockSpec((1, H, tb, 2), lambda b, t, gb: (b, 0, t, 0)),
                      pl.BlockSpec((1, H, tb // L, 2, L), lambda b, t, gb: (b, 0, t, 0, 0)),
                      pl.BlockSpec((ML_CONV, qkw), lambda b, t, gb: (0, 0)),
                      pl.BlockSpec((1, qkw), lambda b, t, gb: (0, 0))],
            out_specs=pl.BlockSpec((1, tb, H * DV), lambda b, t, gb: (b, t, 0)),
            scratch_shapes=[pltpu.VMEM((SUBLANES, qkw), F32), pltpu.VMEM((H, DK, DV), F32),
                            pltpu.VMEM((H, 1, DK), F32), pltpu.VMEM((H, 1, 1), F32),
                            pltpu.VMEM((tb, H * DK), F32), pltpu.VMEM((tb, H * DK), F32)]),
        out_shape=jax.ShapeDtypeStruct((B, S, H * DV), F32),
        compiler_params=_cparams(("parallel", "arbitrary")),
        name="mlstm",
    )(gate_b, p3, p3, p3, gcol, grow, conv_w, conv_b.reshape(1, qkw))


def _head_sum_bcast(a, hmat):
    parts = [jnp.dot(a[:, i * LANES:(i + 1) * LANES], hmat, preferred_element_type=F32,
                     precision=lax.Precision.HIGHEST) for i in range(a.shape[1] // LANES)]
    return jnp.concatenate(parts, axis=1)


def _head_hmat():
    row = lax.broadcasted_iota(jnp.int32, (LANES, LANES), 0)
    col = lax.broadcasted_iota(jnp.int32, (LANES, LANES), 1)
    return jnp.where((row // RW_HEAD) == (col // RW_HEAD), 1.0, 0.0).astype(F32)


def _rwkv_prep_kernel(r_ref, k_ref, v_ref, lo_ref, mu_ref, w0_ref, wup_ref, a0_ref, aup_ref, gup_ref, kk_ref2, ka_ref,
                      rk_ref, ro_ref, wo_ref, ko_ref, vo_ref, kko_ref, kbo_ref, go_ref, bo_ref, prev_ref, *, tb):
    @pl.when(pl.program_id(1) == 0)
    def _():
        prev_ref[...] = jnp.zeros_like(prev_ref)

    C = RW_DIM
    x = jnp.concatenate([r_ref[0], k_ref[0], v_ref[0], lo_ref[0]], axis=1)
    prev = prev_ref[...]
    row0 = lax.broadcasted_iota(jnp.int32, (tb, 1), 0) == 0
    shifted = jnp.where(row0, prev[SUBLANES - 1:SUBLANES, :], pltpu.roll(x, 1, 0))
    prev_ref[...] = x[tb - SUBLANES:]
    xm = x + (shifted - x) * mu_ref[...]
    r, k, v, lo = xm[:, :C], xm[:, C:2 * C], xm[:, 2 * C:3 * C], xm[:, 3 * C:]
    xwa = lo[:, :LANES]
    xg = lo[:, LANES:]
    lw = jnp.dot(jnp.tanh(xwa).astype(BF16), wup_ref[...], preferred_element_type=F32)
    la = jnp.dot(xwa.astype(BF16), aup_ref[...], preferred_element_type=F32)
    g = jnp.dot(jax.nn.sigmoid(xg).astype(BF16), gup_ref[...], preferred_element_type=F32)
    w = jnp.exp(-math.exp(-0.5) * jax.nn.sigmoid(w0_ref[...] + lw))
    a = jax.nn.sigmoid(a0_ref[...] + la)
    hmat = _head_hmat()
    kk = k * kk_ref2[...]
    kk = kk * lax.rsqrt(_head_sum_bcast(kk * kk, hmat) + 1e-12)
    k2 = k * (1.0 + (a - 1.0) * ka_ref[...])
    ro_ref[0] = r
    wo_ref[0] = w
    ko_ref[0] = k2
    vo_ref[0] = v
    kko_ref[0] = kk
    kbo_ref[0] = kk * a
    go_ref[0] = g
    bo_ref[0] = _head_sum_bcast(r * k2 * rk_ref[...], hmat) * v


def _rwkv_post_kernel(y_ref, b_ref, g_ref, lnw_ref, lnb_ref, o_ref):
    y = y_ref[0]
    hmat = _head_hmat()
    mu = _head_sum_bcast(y, hmat) * (1.0 / RW_HEAD)
    yc = y - mu
    var = _head_sum_bcast(yc * yc, hmat) * (1.0 / RW_HEAD)
    yn = yc * lax.rsqrt(var + RW_LN_EPS) * lnw_ref[...] + lnb_ref[...]
    o_ref[0] = (yn + b_ref[0]) * g_ref[0]


def _rwkv_branch(p3, mu, w0, w_up, a0, a_up, g_up, k_k, k_a, r_k, ln_w, ln_b, tb=256):
    B, S, _ = p3.shape
    C = RW_DIM
    tb = min(tb, S)
    row = lambda t: t.reshape(1, -1)
    wup = jnp.concatenate([w_up, jnp.zeros_like(a_up)], axis=0).astype(BF16)
    aup = jnp.concatenate([jnp.zeros_like(w_up), a_up], axis=0).astype(BF16)
    seq = pl.BlockSpec((1, tb, C), lambda b, t: (b, t, 0))
    par = pl.BlockSpec((1, C), lambda b, t: (0, 0))
    r, w, k, v, kk, kb, g, bonus = pl.pallas_call(
        functools.partial(_rwkv_prep_kernel, tb=tb),
        grid=(B, S // tb),
        in_specs=[pl.BlockSpec((1, tb, C), lambda b, t: (b, t, 3)), pl.BlockSpec((1, tb, C), lambda b, t: (b, t, 4)),
                  pl.BlockSpec((1, tb, C), lambda b, t: (b, t, 5)),
                  pl.BlockSpec((1, tb, 2 * LANES), lambda b, t: (b, t, 24)),
                  pl.BlockSpec((1, RW_COLS), lambda b, t: (0, 0)), par,
                  pl.BlockSpec((LANES, C), lambda b, t: (0, 0)), par, pl.BlockSpec((LANES, C), lambda b, t: (0, 0)),
                  pl.BlockSpec((RW_LORA_G, C), lambda b, t: (0, 0)), par, par, par],
        out_specs=[seq] * 8,
        out_shape=[jax.ShapeDtypeStruct((B, S, C), F32)] * 8,
        scratch_shapes=[pltpu.VMEM((SUBLANES, RW_COLS), F32)],
        compiler_params=_cparams(("parallel", "arbitrary")),
        name="rwkv_prep",
    )(p3, p3, p3, p3, row(mu), row(w0), wup, row(a0), aup, g_up.astype(BF16), row(k_k), row(k_a), row(r_k))
    y = _rwkv_scan(r, w, k, v, kk, kb)
    return pl.pallas_call(
        _rwkv_post_kernel,
        grid=(B, S // tb),
        in_specs=[seq, seq, seq, par, par],
        out_specs=seq,
        out_shape=jax.ShapeDtypeStruct((B, S, C), F32),
        compiler_params=_cparams(("parallel", "parallel")),
        name="rwkv_post",
    )(y, bonus, g, row(ln_w), row(ln_b))


def _ab_mixer(x, norm_w, w_in, conv_w, conv_b, gate_b, mu, w0, w_up, a0, a_up, g_up, k_k, k_a, r_k, ln_w, ln_b,
              w_out):
    B, S, D = x.shape
    T = B * S
    x2 = x.reshape(T, D)
    p3 = _matmul(x2, _ab_permute_cols(w_in).astype(BF16), norm_w=norm_w).reshape(B, S, AB_COLS_PAD)
    y_m = _mlstm(p3, conv_w, conv_b, gate_b)
    y_r = _rwkv_branch(p3, mu, w0, w_up, a0, a_up, g_up, k_k, k_a, r_k, ln_w, ln_b)
    hm = ML_HEADS * ML_DV
    part = _matmul(y_m.reshape(T, hm), w_out[:hm].astype(BF16), residual=x2)
    return _matmul(y_r.reshape(T, RW_DIM), w_out[hm:].astype(BF16), residual=part).reshape(B, S, D)


def _ab_mixer_j(x, norm_w, w_in, conv_w, conv_b, gate_b, mu, w0, w_up, a0, a_up, g_up, k_k, k_a, r_k, ln_w, ln_b,
                w_out):
    B, S, D = x.shape
    T = B * S
    x2 = x.reshape(T, D)
    ncol = w_in.shape[1]
    p = _matmul(x2, _pad_cols(w_in, 512).astype(BF16), norm_w=norm_w)[:, :ncol].reshape(B, S, ncol)
    ml_p, rw_p = p[..., :ML_COLS], p[..., ML_COLS:]
    qk, v_m, o_m, gif = _split(ml_p, ML_SPLITS)
    qk = jax.nn.silu(_causal_conv(qk, conv_w, conv_b))
    q_m, k_m = jnp.split(qk, 2, axis=-1)
    gif = gif + gate_b
    h_m = _mlstm_j(q_m.reshape(B, S, ML_HEADS, ML_DK), k_m.reshape(B, S, ML_HEADS, ML_DK),
                   v_m.reshape(B, S, ML_HEADS, ML_DV), gif[..., :ML_HEADS], gif[..., ML_HEADS:])
    y_m = jax.nn.sigmoid(o_m) * h_m.reshape(B, S, ML_HEADS * ML_DV)
    rw_p = rw_p + (_token_shift(rw_p) - rw_p) * mu
    r, k, v, xw, xa, xg = _split(rw_p, RW_SPLITS)
    logw = -math.exp(-0.5) * jax.nn.sigmoid(w0 + jnp.tanh(xw) @ w_up)
    a = jax.nn.sigmoid(a0 + xa @ a_up)
    g = jax.nn.sigmoid(xg) @ g_up

    def heads(t):
        return t.reshape(B, S, RW_HEADS, RW_HEAD)

    kk = heads(k * k_k)
    kk = kk * lax.rsqrt(jnp.sum(kk * kk, axis=-1, keepdims=True) + 1e-12)
    k = k * (1.0 + (a - 1.0) * k_a)
    rh, kh, vh, ah = heads(r), heads(k), heads(v), heads(a)
    y = heads(_rwkv_scan(r, jnp.exp(logw), k, v, kk.reshape(B, S, RW_DIM), kk.reshape(B, S, RW_DIM) * a))
    y_mu = jnp.mean(y, axis=-1, keepdims=True)
    y_var = jnp.mean(jnp.square(y - y_mu), axis=-1, keepdims=True)
    yn = ((y - y_mu) * lax.rsqrt(y_var + RW_LN_EPS)).reshape(B, S, RW_DIM) * ln_w + ln_b
    bonus = (jnp.sum(rh * kh * r_k.reshape(RW_HEADS, RW_HEAD), axis=-1, keepdims=True) * vh).reshape(B, S, RW_DIM)
    y_r = (yn + bonus) * g
    y_cat = jnp.concatenate([y_m, y_r], axis=-1).reshape(T, D)
    return _matmul(y_cat, w_out.astype(BF16), residual=x2).reshape(B, S, D)


def _nsa(x, norm_w, w_in, gate_b, cmp_pos, cmp_w1, cmp_w2, w_out):
    B, S, D = x.shape
    T = B * S
    x2 = x.reshape(T, D)
    p = _matmul(x2, _pad_cols(w_in, 512).astype(BF16), norm_w=norm_w)
    o = _nsa_attention(p.reshape(B, S, -1), gate_b, cmp_pos, cmp_w1, cmp_w2)
    return _matmul(o.reshape(T, NSA_HEADS * NSA_HD), w_out.astype(BF16), residual=x2).reshape(B, S, D)


def _nsa_j(x, norm_w, w_in, gate_b, cmp_pos, cmp_w1, cmp_w2, w_out):
    B, S, D = x.shape
    T = B * S
    x2 = x.reshape(T, D)
    G, R, HD = NSA_KV, NSA_REP, NSA_HD
    QC = NSA_QCHUNK
    scale = HD ** -0.5
    pos = jnp.arange(S)
    ncol = w_in.shape[1]
    p = _matmul(x2, _pad_cols(w_in, 512).astype(BF16), norm_w=norm_w)[:, :ncol].reshape(B, S, ncol)
    q, kc, vc, ks, vs, kw, vw, gl = _split(p, NSA_SPLITS)
    q = q.reshape(B, S, G, R, HD)

    def kvh(t):
        return t.reshape(B, S, G, HD)

    q_rot = _rope(q, pos)
    ks = _rope(kvh(ks), pos)
    kw = _rope(kvh(kw), pos)
    vs = kvh(vs)
    vw = kvh(vw)
    gates = jax.nn.sigmoid(gl + gate_b).reshape(B, S, 3, G, R)
    n_cmp = (S - CMP_BLOCK) // CMP_STRIDE + 1
    cidx = np.arange(n_cmp)[:, None] * CMP_STRIDE + np.arange(CMP_BLOCK)[None, :]

    def compress(t, pe, w1, w2):
        blocks = t[:, cidx] + pe[None, None, :, None, :]
        return jax.nn.gelu(jnp.einsum('bjpgd,pde->bjge', blocks, w1)) @ w2

    k_cmp = compress(kvh(kc), cmp_pos[0], cmp_w1[0], cmp_w2[0])
    v_cmp = compress(kvh(vc), cmp_pos[1], cmp_w1[1], cmp_w2[1])
    cmp_end = jnp.asarray(cidx[:, -1])
    n_sel = S // SEL_BLOCK
    c0 = np.arange(n_cmp)[:, None] * CMP_STRIDE
    s0 = np.arange(n_sel)[None, :] * SEL_BLOCK
    ov = np.clip(np.minimum(c0 + CMP_BLOCK, s0 + SEL_BLOCK) - np.maximum(c0, s0), 0, None) / CMP_BLOCK
    ov = jnp.asarray(ov, dtype=F32)
    n_top = min(SEL_TOPK, n_sel)
    k_blk = ks.reshape(B, n_sel, SEL_BLOCK, G, HD).transpose(0, 3, 1, 2, 4)
    v_blk = vs.reshape(B, n_sel, SEL_BLOCK, G, HD).transpose(0, 3, 1, 2, 4)
    kw_pad = jnp.pad(kw, ((0, 0), (WINDOW, 0), (0, 0), (0, 0)))
    vw_pad = jnp.pad(vw, ((0, 0), (WINDOW, 0), (0, 0), (0, 0)))
    bi = jnp.arange(B)[:, None, None, None]
    gi = jnp.arange(G)[None, :, None, None]
    blk_ids = jnp.arange(n_sel)

    def chunk(c):
        t0 = c * QC
        qt = t0 + jnp.arange(QC)
        q_c = lax.dynamic_slice_in_dim(q, t0, QC, axis=1)
        qr_c = lax.dynamic_slice_in_dim(q_rot, t0, QC, axis=1)
        g_c = lax.dynamic_slice_in_dim(gates, t0, QC, axis=1)
        p_cmp = _masked_softmax(jnp.einsum('bqgrd,bjgd->bgrqj', q_c, k_cmp) * scale,
                                cmp_end[None, :] <= qt[:, None])
        o_cmp = jnp.einsum('bgrqj,bjgd->bqgrd', p_cmp, v_cmp)
        imp = jnp.einsum('bgrqj,js->bgqs', p_cmp, ov)
        cur = (qt // SEL_BLOCK)[:, None]
        forced = (blk_ids[None, :] == 0) | (blk_ids[None, :] == cur) | (blk_ids[None, :] == cur - 1)
        score = jnp.where(forced, jnp.inf, jnp.where(blk_ids[None, :] <= cur, imp, -jnp.inf))
        top_v, top_i = lax.top_k(score, n_top)
        k_g = k_blk[bi, gi, top_i].reshape(B, G, QC, n_top * SEL_BLOCK, HD)
        v_g = v_blk[bi, gi, top_i].reshape(B, G, QC, n_top * SEL_BLOCK, HD)
        kpos = (top_i[..., None] * SEL_BLOCK + jnp.arange(SEL_BLOCK)).reshape(B, G, QC, n_top * SEL_BLOCK)
        kmask = jnp.repeat(top_v > -jnp.inf, SEL_BLOCK, axis=-1) & (kpos <= qt[None, None, :, None])
        p_slc = _masked_softmax(jnp.einsum('bqgrd,bgqkd->bgrqk', qr_c, k_g) * scale, kmask[:, :, None])
        o_slc = jnp.einsum('bgrqk,bgqkd->bqgrd', p_slc, v_g)
        k_w = lax.dynamic_slice_in_dim(kw_pad, t0, QC + WINDOW, axis=1)
        v_w = lax.dynamic_slice_in_dim(vw_pad, t0, QC + WINDOW, axis=1)
        wpos = t0 - WINDOW + jnp.arange(QC + WINDOW)
        dist = qt[:, None] - wpos[None, :]
        wmask = (dist >= 0) & (dist < WINDOW) & (wpos[None, :] >= 0)
        p_win = _masked_softmax(jnp.einsum('bqgrd,bkgd->bgrqk', qr_c, k_w) * scale, wmask)
        o_win = jnp.einsum('bgrqk,bkgd->bqgrd', p_win, v_w)
        return (g_c[:, :, 0, :, :, None] * o_cmp + g_c[:, :, 1, :, :, None] * o_slc
                + g_c[:, :, 2, :, :, None] * o_win)

    o = lax.map(chunk, jnp.arange(S // QC))
    o = o.transpose(1, 0, 2, 3, 4, 5).reshape(T, NSA_HEADS * HD)
    return _matmul(o, w_out.astype(BF16), residual=x2).reshape(B, S, D)


def _cross_attn_kernel(q_ref, k_ref, v_ref, o_ref):
    for h in range(CA_HEADS):
        cols = slice(h * CA_HD, (h + 1) * CA_HD)
        q = q_ref[:, cols].astype(BF16)
        k = k_ref[:, cols].astype(BF16)
        s = lax.dot_general(q, k, (((1,), (1,)), ((), ())), preferred_element_type=F32) * (CA_HD ** -0.5)
        e = jnp.exp(s - jnp.max(s, axis=-1, keepdims=True))
        p = e * (1.0 / jnp.sum(e, axis=-1, keepdims=True))
        o_ref[:, cols] = jnp.dot(p.astype(BF16), v_ref[:, cols].astype(BF16), preferred_element_type=F32)


def _cross_attn(x, norm_w, mem, norm_mem, wq, wk, wv, wo, tq=512):
    B, S, D = x.shape
    T = B * S
    M = mem.shape[1]
    x2 = x.reshape(T, D)
    q = _matmul(x2, wq.astype(BF16), norm_w=norm_w)
    kv = _matmul(mem.reshape(B * M, D), jnp.concatenate([wk, wv], axis=1).astype(BF16), norm_w=norm_mem)
    nq = S // tq
    o = pl.pallas_call(
        _cross_attn_kernel,
        grid=(B, nq),
        in_specs=[pl.BlockSpec((tq, D), lambda b, i: (b * nq + i, 0)),
                  pl.BlockSpec((M, D), lambda b, i: (b, 0)), pl.BlockSpec((M, D), lambda b, i: (b, 1))],
        out_specs=pl.BlockSpec((tq, D), lambda b, i: (b * nq + i, 0)),
        out_shape=jax.ShapeDtypeStruct((T, D), F32),
        compiler_params=_cparams(("parallel", "parallel")),
        name="cross_attn",
    )(q, kv, kv)
    return _matmul(o, wo.astype(BF16), residual=x2).reshape(B, S, D)


def _cross_attn_j(x, norm_w, mem, norm_mem, wq, wk, wv, wo):
    B, S, D = x.shape
    T = B * S
    M = mem.shape[1]
    x2 = x.reshape(T, D)
    q = _matmul(x2, wq.astype(BF16), norm_w=norm_w).reshape(B, S, CA_HEADS, CA_HD)
    m2 = mem.reshape(B * M, D)
    kv = _matmul(m2, jnp.concatenate([wk, wv], axis=1).astype(BF16), norm_w=norm_mem)
    k = kv[:, :D].reshape(B, M, CA_HEADS, CA_HD)
    v = kv[:, D:].reshape(B, M, CA_HEADS, CA_HD)
    s = jnp.einsum('bshd,bmhd->bhsm', q, k) * (CA_HD ** -0.5)
    p = jax.nn.softmax(s, axis=-1)
    o = jnp.einsum('bhsm,bmhd->bshd', p, v).reshape(T, D)
    return _matmul(o, wo.astype(BF16), residual=x2).reshape(B, S, D)


MOE_TM = 256
MOE_NEG = -1e30


def _moe_route_kernel(x_ref, nw_ref, wr_ref, br_ref, xn_ref, ri_ref, rw_ref, cnt_ref, cnt_scr, *, tm):
    @pl.when(pl.program_id(0) == 0)
    def _():
        cnt_scr[...] = jnp.zeros_like(cnt_scr)

    x = x_ref[...]
    xn = x * lax.rsqrt(jnp.mean(x * x, axis=-1, keepdims=True) + EPS) * nw_ref[...]
    xn_ref[...] = xn
    logits = jnp.dot(xn, wr_ref[...], preferred_element_type=F32, precision=lax.Precision.HIGHEST) + br_ref[...]
    lane = lax.broadcasted_iota(jnp.int32, (tm, LANES), 1)
    gmask = lane < MOE_GROUPS
    lg = jnp.where(gmask, logits, MOE_NEG)
    gmax = jnp.max(lg, axis=-1, keepdims=True)
    grp = jnp.min(jnp.where(lg == gmax, lane, LANES), axis=-1, keepdims=True)
    p_grp = 1.0 / jnp.sum(jnp.where(gmask, jnp.exp(lg - gmax), 0.0), axis=-1, keepdims=True)
    lo = MOE_GROUPS + grp * MOE_PER_GROUP
    le = jnp.where((lane >= lo) & (lane < lo + MOE_PER_GROUP), logits, MOE_NEG)
    t1 = jnp.max(le, axis=-1, keepdims=True)
    i1 = jnp.min(jnp.where(le == t1, lane, LANES), axis=-1, keepdims=True)
    le2 = jnp.where(lane == i1, MOE_NEG, le)
    t2 = jnp.max(le2, axis=-1, keepdims=True)
    i2 = jnp.min(jnp.where(le2 == t2, lane, LANES), axis=-1, keepdims=True)
    e21 = jnp.exp(t2 - t1)
    w1 = p_grp / (1.0 + e21)
    w2 = w1 * e21
    e1 = i1 - MOE_GROUPS
    e2 = i2 - MOE_GROUPS
    oh1 = jnp.where(lane == e1, 1.0, 0.0)
    oh2 = jnp.where(lane == e2, 1.0, 0.0)
    both = oh1 + oh2
    r_i = lax.broadcasted_iota(jnp.int32, (tm, tm), 0)
    c_i = lax.broadcasted_iota(jnp.int32, (tm, tm), 1)
    ltri = jnp.where(c_i < r_i, 1.0, 0.0).astype(BF16)
    before = jnp.dot(ltri, both.astype(BF16), preferred_element_type=F32) + cnt_scr[...]
    pos1 = jnp.sum(oh1 * before, axis=-1, keepdims=True).astype(jnp.int32)
    pos2 = jnp.sum(oh2 * before, axis=-1, keepdims=True).astype(jnp.int32)
    cnt_scr[...] = cnt_scr[...] + jnp.sum(both, axis=0, keepdims=True)
    ri_ref[...] = jnp.where(lane == 0, e1, jnp.where(lane == 1, e2, jnp.where(lane == 2, pos1,
                            jnp.where(lane == 3, pos2, 0))))
    rw_ref[...] = jnp.where(lane == 0, w1, jnp.where(lane == 1, w2, 0.0))
    cnt_ref[...] = cnt_scr[...]


def _moe_dest(tok, e1_ref, e2_ref, p1_ref, p2_ref, off_ref):
    return off_ref[e1_ref[tok]] + p1_ref[tok], off_ref[e2_ref[tok]] + p2_ref[tok]


def _moe_dispatch_kernel(e1_ref, e2_ref, p1_ref, p2_ref, off_ref, xn_hbm, xs_in_hbm, xs_hbm, sem, *, tt, burst):
    del xs_in_hbm
    base = pl.program_id(0) * tt

    def row_copy(tok, dst):
        return pltpu.make_async_copy(xn_hbm.at[pl.ds(tok, 1)], xs_hbm.at[pl.ds(dst, 1)], sem)

    def chunk(c, carry):
        def issue(t, carry2):
            tok = base + c * burst + t
            d1, d2 = _moe_dest(tok, e1_ref, e2_ref, p1_ref, p2_ref, off_ref)
            row_copy(tok, d1).start()
            row_copy(tok, d2).start()
            return carry2

        def drain(t, carry2):
            row_copy(0, 0).wait()
            row_copy(0, 0).wait()
            return carry2

        lax.fori_loop(0, burst, issue, 0)
        lax.fori_loop(0, burst, drain, 0)
        return carry

    lax.fori_loop(0, tt // burst, chunk, 0)


def _moe_expert_kernel(te_ref, nu_ref, x_ref, wg_ref, wu_ref, wd_ref, y_ref):
    del te_ref

    @pl.when(pl.program_id(0) < nu_ref[0])
    def _():
        x = x_ref[...].astype(BF16)
        gate = jnp.dot(x, wg_ref[...].astype(BF16), preferred_element_type=F32)
        up = jnp.dot(x, wu_ref[...].astype(BF16), preferred_element_type=F32)
        hid = gate * jax.nn.sigmoid(gate) * up
        y_ref[...] = jnp.dot(hid.astype(BF16), wd_ref[...].astype(BF16), preferred_element_type=F32)

    @pl.when(pl.program_id(0) >= nu_ref[0])
    def _():
        y_ref[...] = jnp.zeros_like(y_ref)


def _moe_combine_kernel(e1_ref, e2_ref, p1_ref, p2_ref, off_ref, x_ref, rw_ref, ys_hbm, o_ref, buf1, buf2, sem, *, tm):
    base = pl.program_id(0) * tm

    def row_copy(src, t, buf):
        return pltpu.make_async_copy(ys_hbm.at[pl.ds(src, 1)], buf.at[pl.ds(t, 1)], sem)

    def issue(t, carry):
        d1, d2 = _moe_dest(base + t, e1_ref, e2_ref, p1_ref, p2_ref, off_ref)
        row_copy(d1, t, buf1).start()
        row_copy(d2, t, buf2).start()
        return carry

    def drain(t, carry):
        row_copy(0, 0, buf1).wait()
        row_copy(0, 0, buf2).wait()
        return carry

    lax.fori_loop(0, tm, issue, 0)
    lax.fori_loop(0, tm, drain, 0)
    w = rw_ref[...]
    o_ref[...] = x_ref[...] + w[:, 0:1] * buf1[...] + w[:, 1:2] * buf2[...]


def _hier_moe(x, norm_w, wg, bg, we, be, w_gate, w_up, w_down, layer):
    B, S, D = x.shape
    T = B * S
    x2 = x.reshape(T, D)
    FF = w_gate.shape[-1]
    tm_r = 512
    wr = jnp.pad(jnp.concatenate([wg, we], axis=1), ((0, 0), (0, LANES - MOE_GROUPS - MOE_EXPERTS)))
    br = jnp.pad(jnp.concatenate([bg, be]), (0, LANES - MOE_GROUPS - MOE_EXPERTS)).reshape(1, LANES)
    xn, ri, rw, cnt = pl.pallas_call(
        functools.partial(_moe_route_kernel, tm=tm_r),
        grid=(T // tm_r,),
        in_specs=[pl.BlockSpec((tm_r, D), lambda i: (i, 0)), pl.BlockSpec((1, D), lambda i: (0, 0)),
                  pl.BlockSpec((D, LANES), lambda i: (0, 0)), pl.BlockSpec((1, LANES), lambda i: (0, 0))],
        out_specs=[pl.BlockSpec((tm_r, D), lambda i: (i, 0)), pl.BlockSpec((tm_r, LANES), lambda i: (i, 0)),
                   pl.BlockSpec((tm_r, LANES), lambda i: (i, 0)), pl.BlockSpec((1, LANES), lambda i: (0, 0))],
        out_shape=[jax.ShapeDtypeStruct((T, D), F32), jax.ShapeDtypeStruct((T, LANES), jnp.int32),
                   jax.ShapeDtypeStruct((T, LANES), F32), jax.ShapeDtypeStruct((1, LANES), F32)],
        scratch_shapes=[pltpu.VMEM((1, LANES), F32)],
        compiler_params=_cparams(("arbitrary",)),
        name="moe_route",
    )(x2, norm_w.reshape(1, D), wr, br)

    counts = cnt[0, :MOE_EXPERTS].astype(jnp.int32)
    padded = (counts + MOE_TM - 1) // MOE_TM * MOE_TM
    ends = jnp.cumsum(padded)
    off = (ends - padded).astype(jnp.int32)
    n_tiles = (T * MOE_TOPK) // MOE_TM + MOE_EXPERTS
    rows = n_tiles * MOE_TM
    n_used = (ends[-1] // MOE_TM).astype(jnp.int32).reshape(1)
    tile_e = jnp.minimum(jnp.searchsorted(ends, jnp.arange(n_tiles, dtype=jnp.int32) * MOE_TM, side='right'),
                         MOE_EXPERTS - 1).astype(jnp.int32)
    e1, e2, p1, p2 = ri[:, 0], ri[:, 1], ri[:, 2], ri[:, 3]

    tt = min(2048, T)
    xs = pl.pallas_call(
        functools.partial(_moe_dispatch_kernel, tt=tt, burst=64),
        grid_spec=pltpu.PrefetchScalarGridSpec(
            num_scalar_prefetch=5, grid=(T // tt,),
            in_specs=[pl.BlockSpec(memory_space=pl.ANY), pl.BlockSpec(memory_space=pl.ANY)],
            out_specs=pl.BlockSpec(memory_space=pl.ANY),
            scratch_shapes=[pltpu.SemaphoreType.DMA(())]),
        out_shape=jax.ShapeDtypeStruct((rows, D), F32),
        input_output_aliases={6: 0},
        compiler_params=_cparams(("arbitrary",)),
        name="moe_dispatch",
    )(e1, e2, p1, p2, off, xn, jnp.zeros((rows, D), F32))

    def tile_ix(i, te, nu):
        return jnp.minimum(i, nu[0] - 1)

    ys = pl.pallas_call(
        _moe_expert_kernel,
        grid_spec=pltpu.PrefetchScalarGridSpec(
            num_scalar_prefetch=2, grid=(n_tiles,),
            in_specs=[pl.BlockSpec((MOE_TM, D), lambda i, te, nu: (tile_ix(i, te, nu), 0)),
                      pl.BlockSpec((None, None, D, FF), lambda i, te, nu: (layer, te[tile_ix(i, te, nu)], 0, 0)),
                      pl.BlockSpec((None, None, D, FF), lambda i, te, nu: (layer, te[tile_ix(i, te, nu)], 0, 0)),
                      pl.BlockSpec((None, None, FF, D), lambda i, te, nu: (layer, te[tile_ix(i, te, nu)], 0, 0))],
            out_specs=pl.BlockSpec((MOE_TM, D), lambda i, te, nu: (i, 0))),
        out_shape=jax.ShapeDtypeStruct((rows, D), F32),
        compiler_params=_cparams(("arbitrary",)),
        name="moe_experts",
    )(tile_e, n_used, xs, w_gate, w_up, w_down)

    tm_c = 256
    out = pl.pallas_call(
        functools.partial(_moe_combine_kernel, tm=tm_c),
        grid_spec=pltpu.PrefetchScalarGridSpec(
            num_scalar_prefetch=5, grid=(T // tm_c,),
            in_specs=[pl.BlockSpec((tm_c, D), lambda i, *_: (i, 0)), pl.BlockSpec((tm_c, LANES), lambda i, *_: (i, 0)),
                      pl.BlockSpec(memory_space=pl.ANY)],
            out_specs=pl.BlockSpec((tm_c, D), lambda i, *_: (i, 0)),
            scratch_shapes=[pltpu.VMEM((tm_c, D), F32), pltpu.VMEM((tm_c, D), F32), pltpu.SemaphoreType.DMA(())]),
        out_shape=jax.ShapeDtypeStruct((T, D), F32),
        compiler_params=_cparams(("arbitrary",)),
        name="moe_combine",
    )(e1, e2, p1, p2, off, x2, rw, ys)
    return out.reshape(B, S, D)


def _hier_moe_j(x, norm_w, wg, bg, we, be, w_gate, w_up, w_down):
    B, S, D = x.shape
    h = _rmsnorm_j(x, norm_w)
    T = B * S
    A = T * MOE_TOPK
    xt = h.reshape(T, D)
    lg = xt @ wg + bg
    grp = jnp.argmax(lg, axis=-1)
    p_grp = jnp.take_along_axis(jax.nn.softmax(lg, axis=-1), grp[:, None], axis=-1)
    le = (xt @ we + be).reshape(T, MOE_GROUPS, MOE_PER_GROUP)
    le = jnp.take_along_axis(le, grp[:, None, None], axis=1)[:, 0]
    top_l, top_e = lax.top_k(le, MOE_TOPK)
    wts = (p_grp * jax.nn.softmax(top_l, axis=-1)).reshape(A)
    eid = (grp[:, None] * MOE_PER_GROUP + top_e).reshape(A)
    tok = jnp.repeat(jnp.arange(T, dtype=jnp.int32), MOE_TOPK)
    order = jnp.argsort(eid)
    e_s, tok_s, w_s = eid[order], tok[order], wts[order]
    counts = jax.ops.segment_sum(jnp.ones((A,), jnp.int32), eid, num_segments=MOE_EXPERTS)
    padded = (counts + MOE_ROWS - 1) // MOE_ROWS * MOE_ROWS
    ends = jnp.cumsum(padded)
    dest = (ends - padded)[e_s] + jnp.arange(A, dtype=jnp.int32) - (jnp.cumsum(counts) - counts)[e_s]
    n_chunks = -(-A // MOE_ROWS) + MOE_EXPERTS
    rows = n_chunks * MOE_ROWS
    row_tok = jnp.full((rows,), T, jnp.int32).at[dest].set(tok_s)
    row_w = jnp.zeros((rows,), F32).at[dest].set(w_s)
    chunk_e = jnp.minimum(jnp.searchsorted(ends, jnp.arange(n_chunks, dtype=jnp.int32) * MOE_ROWS, side='right'),
                          MOE_EXPERTS - 1)
    x_pad = jnp.concatenate([xt, jnp.zeros((1, D), xt.dtype)], axis=0)

    def expert_rows(args):
        t_idx, w_r, e = args
        xr = x_pad[t_idx]
        hid = jax.nn.silu(xr @ w_gate[e]) * (xr @ w_up[e])
        y = hid @ w_down[e]
        return y * w_r[:, None]

    out = lax.map(expert_rows, (row_tok.reshape(n_chunks, MOE_ROWS), row_w.reshape(n_chunks, MOE_ROWS), chunk_e))
    y = jax.ops.segment_sum(out.reshape(rows, D), row_tok, num_segments=T + 1)[:T]
    return x + y.reshape(B, S, D)


def kernel(x, mem, norm_mix, norm_cross, norm_mem, norm_ffn, norm_final,
           ab_w_in, ml_conv_w, ml_conv_b, ml_gate_b, rw_mu, rw_w0, rw_w_up, rw_a0, rw_a_up, rw_g_up,
           rw_k_k, rw_k_a, rw_r_k, rw_ln_w, rw_ln_b, ab_w_out,
           nsa_w_in, nsa_gate_b, cmp_pos, cmp_w1, cmp_w2, nsa_w_out,
           ca_wq, ca_wk, ca_wv, ca_wo,
           moe_wg, moe_bg, moe_we, moe_be, moe_w_gate, moe_w_up, moe_w_down):
    B, S, D = x.shape
    for l in range(DEPTH):
        j = l // 2
        if l % 2 == 0:
            x = _ab_mixer(x, norm_mix[l], ab_w_in[j], ml_conv_w[j], ml_conv_b[j], ml_gate_b[j], rw_mu[j], rw_w0[j],
                          rw_w_up[j], rw_a0[j], rw_a_up[j], rw_g_up[j], rw_k_k[j], rw_k_a[j], rw_r_k[j],
                          rw_ln_w[j], rw_ln_b[j], ab_w_out[j])
        else:
            x = _nsa(x, norm_mix[l], nsa_w_in[j], nsa_gate_b[j], cmp_pos[j], cmp_w1[j], cmp_w2[j], nsa_w_out[j])
        x = _cross_attn(x, norm_cross[l], mem, norm_mem[l], ca_wq[l], ca_wk[l], ca_wv[l], ca_wo[l])
        x = _hier_moe(x, norm_ffn[l], moe_wg[l], moe_bg[l], moe_we[l], moe_be[l],
                      moe_w_gate, moe_w_up, moe_w_down, l)
    return _rmsnorm_rows(x.reshape(B * S, D), norm_final).reshape(B, S, D)
```

```python
import functools
import math

import jax
import jax.numpy as jnp
import numpy as np
from jax import lax
from jax.experimental import pallas as pl
from jax.experimental.pallas import tpu as pltpu

F32 = jnp.float32
BF16 = jnp.bfloat16

D_MODEL = 2048
DEPTH = 2
EPS = 1e-6
ROPE_THETA = 500000.0

ML_HEADS = 4
ML_DV = D_MODEL // 2 // ML_HEADS
ML_DK = ML_DV // 2
ML_CHUNK = 64
ML_CONV = 4
RW_HEAD = 64
RW_HEADS = D_MODEL // 2 // RW_HEAD
RW_DIM = RW_HEADS * RW_HEAD
RW_LORA_W = 64
RW_LORA_A = 64
RW_LORA_G = 128
RW_LN_EPS = 64e-5
ML_SPLITS = (2 * ML_HEADS * ML_DK, ML_HEADS * ML_DV, ML_HEADS * ML_DV, 2 * ML_HEADS)
RW_SPLITS = (RW_DIM, RW_DIM, RW_DIM, RW_LORA_W, RW_LORA_A, RW_LORA_G)
ML_COLS = sum(ML_SPLITS)
RW_COLS = sum(RW_SPLITS)

NSA_HEADS = 16
NSA_KV = 4
NSA_REP = NSA_HEADS // NSA_KV
NSA_HD = D_MODEL // NSA_HEADS
ROPE_DIM = NSA_HD // 4
CMP_BLOCK = 32
CMP_STRIDE = 16
SEL_BLOCK = 64
SEL_TOPK = 16
WINDOW = 512
NSA_QCHUNK = 32
NSA_SPLITS = (NSA_HEADS * NSA_HD,) + (NSA_KV * NSA_HD,) * 6 + (3 * NSA_HEADS,)

CA_HEADS = 4
CA_HD = D_MODEL // CA_HEADS

MOE_GROUPS = 8
MOE_PER_GROUP = 8
MOE_EXPERTS = MOE_GROUPS * MOE_PER_GROUP
MOE_TOPK = 2
MOE_FF = D_MODEL // 4
MOE_ROWS = 128

V7X_VMEM_BYTES = 64 * 1024 * 1024
VMEM_LIMIT = 48 * 1024 * 1024
LANES = 128
SUBLANES = 8


def _cparams(sem):
    return pltpu.CompilerParams(dimension_semantics=sem, vmem_limit_bytes=VMEM_LIMIT)


def _mm_kernel(*refs, has_norm, has_res):
    it = iter(refs)
    x_ref = next(it)
    w_ref = next(it)
    nw_ref = next(it) if has_norm else None
    r_ref = next(it) if has_res else None
    o_ref = next(it)
    xs_ref = next(it)

    @pl.when(pl.program_id(1) == 0)
    def _():
        x = x_ref[...]
        if has_norm:
            ms = jnp.mean(x * x, axis=-1, keepdims=True)
            x = x * lax.rsqrt(ms + EPS) * nw_ref[...]
        xs_ref[...] = x.astype(BF16)

    acc = jnp.dot(xs_ref[...], w_ref[...], preferred_element_type=F32)
    if has_res:
        acc = acc + r_ref[...]
    o_ref[...] = acc


def _matmul(x, w_bf16, *, norm_w=None, residual=None, tm=1024, tn=512):
    M, K = x.shape
    N = w_bf16.shape[1]
    tm = min(tm, M)
    tn = min(tn, N)
    assert M % tm == 0 and N % tn == 0, (M, N, tm, tn)
    has_norm = norm_w is not None
    has_res = residual is not None
    in_specs = [pl.BlockSpec((tm, K), lambda i, j: (i, 0)),
                pl.BlockSpec((K, tn), lambda i, j: (0, j))]
    args = [x, w_bf16]
    if has_norm:
        in_specs.append(pl.BlockSpec((1, K), lambda i, j: (0, 0)))
        args.append(norm_w.reshape(1, K))
    if has_res:
        in_specs.append(pl.BlockSpec((tm, tn), lambda i, j: (i, j)))
        args.append(residual)
    return pl.pallas_call(
        functools.partial(_mm_kernel, has_norm=has_norm, has_res=has_res),
        grid=(M // tm, N // tn),
        in_specs=in_specs,
        out_specs=pl.BlockSpec((tm, tn), lambda i, j: (i, j)),
        out_shape=jax.ShapeDtypeStruct((M, N), F32),
        scratch_shapes=[pltpu.VMEM((tm, K), BF16)],
        compiler_params=_cparams(("parallel", "arbitrary")),
        name="mm_norm" if has_norm else "mm",
    )(*args)


def _pad_cols(w, mult):
    n = w.shape[1]
    pad = (-n) % mult
    if pad:
        w = jnp.pad(w, ((0, 0), (0, pad)))
    return w


def _rmsnorm_kernel(x_ref, w_ref, o_ref):
    x = x_ref[...]
    ms = jnp.mean(x * x, axis=-1, keepdims=True)
    o_ref[...] = x * lax.rsqrt(ms + EPS) * w_ref[...]


def _rmsnorm_rows(x, w, tm=512):
    M, K = x.shape
    return pl.pallas_call(
        _rmsnorm_kernel,
        grid=(M // tm,),
        in_specs=[pl.BlockSpec((tm, K), lambda i: (i, 0)), pl.BlockSpec((1, K), lambda i: (0, 0))],
        out_specs=pl.BlockSpec((tm, K), lambda i: (i, 0)),
        out_shape=jax.ShapeDtypeStruct((M, K), F32),
        compiler_params=_cparams(("parallel",)),
        name="rmsnorm",
    )(x, w.reshape(1, K))


def _rwkv_scan_kernel(r_ref, w_ref, k_ref, v_ref, kk_ref, kb_ref, y_ref, s_ref, *, tb, nb, npairs, gsz):
    @pl.when(pl.program_id(1) == 0)
    def _():
        s_ref[...] = jnp.zeros_like(s_ref)

    row = lax.broadcasted_iota(jnp.int32, (LANES, LANES), 0)
    col = lax.broadcasted_iota(jnp.int32, (LANES, LANES), 1)
    hmat = jnp.where((row // RW_HEAD) == (col // RW_HEAD), 1.0, 0.0).astype(BF16)
    vrow = lax.broadcasted_iota(jnp.int32, (RW_HEAD, LANES), 0)
    vcol = lax.broadcasted_iota(jnp.int32, (RW_HEAD, LANES), 1)
    diag = jnp.where((vcol % RW_HEAD) == vrow, 1.0, 0.0).astype(F32)

    def group_sum(parts):
        lhs = jnp.concatenate([q.astype(BF16) for q in parts], axis=0)
        out = jnp.dot(lhs, hmat, preferred_element_type=F32)
        return [out[i * RW_HEAD:(i + 1) * RW_HEAD] for i in range(len(parts))]

    def step(t8, carry):
        rows = pl.ds(pl.multiple_of(t8 * SUBLANES, SUBLANES), SUBLANES)
        for b in range(nb):
            for g in range(npairs // gsz):
                pairs = range(g * gsz, (g + 1) * gsz)
                lns = [pl.ds(p * LANES, LANES) for p in pairs]
                kk8 = [kk_ref[b, rows, ln] for ln in lns]
                w8 = [w_ref[b, rows, ln] for ln in lns]
                kb8 = [kb_ref[b, rows, ln] for ln in lns]
                k8 = [k_ref[b, rows, ln] for ln in lns]
                v8 = [v_ref[b, rows, ln] for ln in lns]
                r8 = [r_ref[b, rows, ln] for ln in lns]
                s = [s_ref[b, p] for p in pairs]
                ys = [[] for _ in pairs]
                for j in range(SUBLANES):
                    sl = slice(j, j + 1)
                    sk = group_sum([s[i] * kk8[i][sl] for i in range(gsz)])
                    vb = group_sum([diag * v8[i][sl] for i in range(gsz)])
                    s = [s[i] * w8[i][sl] - sk[i] * kb8[i][sl] + vb[i] * k8[i][sl] for i in range(gsz)]
                    yb = group_sum([s[i] * r8[i][sl] for i in range(gsz)])
                    for i in range(gsz):
                        ys[i].append(jnp.sum(yb[i] * diag, axis=0, keepdims=True))
                for i, p in enumerate(pairs):
                    s_ref[b, p] = s[i]
                    y_ref[b, rows, lns[i]] = jnp.concatenate(ys[i], axis=0)
        return carry

    lax.fori_loop(0, tb // SUBLANES, step, 0)


def _rwkv_scan(r, w, k, v, kk, kb, tb=256, nb=2, gsz=8):
    B, S, C = r.shape
    npairs = C // LANES
    tb = min(tb, S)
    nb = min(nb, B)
    spec = pl.BlockSpec((nb, tb, C), lambda b, t: (b, t, 0))
    return pl.pallas_call(
        functools.partial(_rwkv_scan_kernel, tb=tb, nb=nb, npairs=npairs, gsz=gsz),
        grid=(B // nb, S // tb),
        in_specs=[spec] * 6,
        out_specs=spec,
        out_shape=jax.ShapeDtypeStruct((B, S, C), F32),
        scratch_shapes=[pltpu.VMEM((nb, npairs, RW_HEAD, LANES), F32)],
        compiler_params=_cparams(("parallel", "arbitrary")),
        name="rwkv_scan",
    )(r, w, k, v, kk, kb)


NSA_NEG = -1e30
NSA_FORCED = 1e30
NSA_REMOVED = -3e30
NSA_COL_Q, NSA_COL_KC, NSA_COL_VC, NSA_COL_KS, NSA_COL_VS, NSA_COL_KW, NSA_COL_VW, NSA_COL_GL = (
    0, 16, 20, 24, 28, 32, 36, 40)


def _rope_tables(S):
    half = ROPE_DIM // 2
    inv = 1.0 / (ROPE_THETA ** (jnp.arange(half, dtype=F32) / half))
    ang = jnp.arange(S, dtype=F32)[:, None] * inv[None, :]
    cos, sin = jnp.cos(ang), jnp.sin(ang)
    one = jnp.ones((S, NSA_HD - ROPE_DIM), F32)
    zero = jnp.zeros((S, NSA_HD - ROPE_DIM), F32)
    zh = jnp.zeros((S, half), F32)
    return (jnp.concatenate([cos, cos, one], axis=1), jnp.concatenate([zh, sin, zero], axis=1),
            jnp.concatenate([-sin, zh, zero], axis=1))


def _rope_rows(x, c, s1, s2):
    half = ROPE_DIM // 2
    return x * c + pltpu.roll(x, half, 1) * s1 + pltpu.roll(x, NSA_HD - half, 1) * s2


def _nsa_prep_q_kernel(x_ref, c_ref, s1_ref, s2_ref, qt_ref, qrt_ref):
    x = x_ref[0] * (NSA_HD ** -0.5)
    xr = _rope_rows(x, c_ref[...], s1_ref[...], s2_ref[...])
    qt_ref[0, 0] = x.T.astype(BF16)
    qrt_ref[0, 0] = xr.T.astype(BF16)


def _nsa_prep_kv_kernel(ks_ref, kw_ref, vs_ref, vw_ref, c_ref, s1_ref, s2_ref, kso_ref, kwo_ref, vst_ref, vwt_ref):
    c, s1, s2 = c_ref[...], s1_ref[...], s2_ref[...]
    kso_ref[0, 0] = _rope_rows(ks_ref[0], c, s1, s2).astype(BF16)
    kwo_ref[0, 0] = _rope_rows(kw_ref[0], c, s1, s2).astype(BF16)
    vst_ref[0, 0] = vs_ref[0].T.astype(BF16)
    vwt_ref[0, 0] = vw_ref[0].T.astype(BF16)


def _nsa_compress_kernel(kc_ref, vc_ref, pe_ref, w1_ref, w2_ref, kcmp_ref, vcmpt_ref, *, ncp):
    for which, x_ref in enumerate((kc_ref, vc_ref)):
        za = jnp.zeros((ncp, NSA_HD), F32)
        zb = jnp.zeros((ncp, NSA_HD), F32)
        for p in range(CMP_STRIDE):
            xp = x_ref[pl.ds(p, ncp, stride=CMP_STRIDE), :]
            za = za + jnp.dot((xp + pe_ref[which, p:p + 1, :]).astype(BF16), w1_ref[which, p],
                              preferred_element_type=F32)
            zb = zb + jnp.dot((xp + pe_ref[which, CMP_STRIDE + p:CMP_STRIDE + p + 1, :]).astype(BF16),
                              w1_ref[which, CMP_STRIDE + p], preferred_element_type=F32)
        pre = za + pltpu.roll(zb, ncp - 1, 0)
        out = jnp.dot(jax.nn.gelu(pre).astype(BF16), w2_ref[which], preferred_element_type=F32)
        if which == 0:
            kcmp_ref[0, 0] = out.astype(BF16)
        else:
            vcmpt_ref[0, 0] = out.T.astype(BF16)


def _nsa_attn_kernel(qt_ref, qrt_ref, kcmp_ref, vcmpt_ref, ks_ref, kw_ref, vst_ref, vwt_ref, gl_ref, gb_ref, ovt_ref,
                     o_ref, sel_ref, ms_ref, ls_ref, accs_ref, *, tq, kt, ncp, nsel):
    R = NSA_REP
    g = pl.program_id(1)
    qi = pl.program_id(2)
    t0 = qi * tq
    q_t = jnp.concatenate([qt_ref[0, r] for r in range(R)], axis=1)
    qr_t = jnp.concatenate([qrt_ref[0, r] for r in range(R)], axis=1)
    qpos = t0 + lax.broadcasted_iota(jnp.int32, (1, tq), 1)

    def per_head(fn, a):
        return jnp.concatenate([fn(a[:, r * tq:(r + 1) * tq]) for r in range(R)], axis=1)

    s = jnp.dot(kcmp_ref[0, 0], q_t, preferred_element_type=F32)
    cend = lax.broadcasted_iota(jnp.int32, (ncp, 1), 0) * CMP_STRIDE + (CMP_BLOCK - 1)
    vis = cend <= qpos
    s = per_head(lambda a: jnp.where(vis, a, NSA_NEG), s)
    m = jnp.max(s, axis=0, keepdims=True)
    e = per_head(lambda a: jnp.where(vis, a, 0.0), jnp.exp(s - m))
    d = jnp.sum(e, axis=0, keepdims=True)
    p = e * (1.0 / jnp.where(d > 0, d, 1.0))
    ocmp_t = jnp.dot(vcmpt_ref[0, 0], p.astype(BF16), preferred_element_type=F32)

    psum = p[:, 0:tq]
    for r in range(1, R):
        psum = psum + p[:, r * tq:(r + 1) * tq]
    imp = jnp.dot(ovt_ref[...], psum, preferred_element_type=F32, precision=lax.Precision.HIGHEST)
    sidx = lax.broadcasted_iota(jnp.int32, (nsel, tq), 0)
    cur = qpos // SEL_BLOCK
    forced = (sidx == 0) | (sidx == cur) | (sidx == cur - 1)
    score = jnp.where(forced, NSA_FORCED, jnp.where(sidx <= cur, imp, NSA_NEG))
    sel = jnp.zeros((nsel, tq), F32)
    for _ in range(SEL_TOPK):
        mx = jnp.max(score, axis=0, keepdims=True)
        idx = jnp.min(jnp.where(score == mx, sidx, nsel), axis=0, keepdims=True)
        pick = (sidx == idx) & (mx > 0.5 * NSA_NEG)
        sel = jnp.where(pick, 1.0, sel)
        score = jnp.where(pick, NSA_REMOVED, score)
    sel_ref[...] = sel

    def online_update(s, valid, v_t, m_ref, l_ref, acc_ref):
        s = per_head(lambda a: jnp.where(valid, a, NSA_NEG), s)
        m_old = m_ref[...]
        m_new = jnp.maximum(m_old, jnp.max(s, axis=0, keepdims=True))
        alpha = jnp.exp(m_old - m_new)
        pexp = jnp.exp(s - m_new)
        l_ref[...] = alpha * l_ref[...] + jnp.sum(pexp, axis=0, keepdims=True)
        acc_ref[...] = alpha * acc_ref[...] + jnp.dot(v_t, pexp.astype(BF16), preferred_element_type=F32)
        m_ref[...] = m_new

    ms_ref[...] = jnp.full(ms_ref.shape, NSA_NEG, F32)
    ls_ref[...] = jnp.zeros(ls_ref.shape, F32)
    accs_ref[...] = jnp.zeros(accs_ref.shape, F32)

    nblk = kt // SEL_BLOCK

    def sel_body(c, carry):
        k0 = pl.multiple_of(c * kt, kt)
        s = jnp.dot(ks_ref[0, 0, pl.ds(k0, kt), :], qr_t, preferred_element_type=F32)
        selrows = sel_ref[pl.ds(pl.multiple_of(c * nblk, nblk), nblk), :]
        selexp = jnp.concatenate(
            [jnp.broadcast_to(selrows[b:b + 1], (SEL_BLOCK, tq)) for b in range(nblk)], axis=0)
        kpos = k0 + lax.broadcasted_iota(jnp.int32, (kt, 1), 0)
        valid = (selexp > 0.5) & (kpos <= qpos)
        online_update(s, valid, vst_ref[0, 0, :, pl.ds(k0, kt)], ms_ref, ls_ref, accs_ref)
        return carry

    lax.fori_loop(0, t0 // kt + 1, sel_body, 0)

    nwk = WINDOW + tq
    k0 = pl.multiple_of(jnp.maximum(t0 - WINDOW, 0), tq)
    s = jnp.dot(kw_ref[0, 0, pl.ds(k0, nwk), :], qr_t, preferred_element_type=F32)
    dist = qpos - (k0 + lax.broadcasted_iota(jnp.int32, (nwk, 1), 0))
    inwin = (dist >= 0) & (dist < WINDOW)
    s = per_head(lambda a: jnp.where(inwin, a, NSA_NEG), s)
    ew = jnp.exp(s - jnp.max(s, axis=0, keepdims=True))
    owin_t = (jnp.dot(vwt_ref[0, 0, :, pl.ds(k0, nwk)], ew.astype(BF16), preferred_element_type=F32)
              * (1.0 / jnp.sum(ew, axis=0, keepdims=True)))

    oslc_t = accs_ref[...] * (1.0 / ls_ref[...])

    gates_t = jax.nn.sigmoid(gl_ref[0] + gb_ref[...]).T
    rid = lax.broadcasted_iota(jnp.int32, (LANES, 1), 0)

    def gate_row(which, r):
        return jnp.sum(jnp.where(rid == which * NSA_HEADS + g * R + r, gates_t, 0.0), axis=0, keepdims=True)

    for r in range(R):
        cols = slice(r * tq, (r + 1) * tq)
        o_t = (gate_row(0, r) * ocmp_t[:, cols] + gate_row(1, r) * oslc_t[:, cols]
               + gate_row(2, r) * owin_t[:, cols])
        o_ref[0, :, r * NSA_HD:(r + 1) * NSA_HD] = o_t.T


def _nsa_attention(p3, gate_b, cmp_pos, cmp_w1, cmp_w2, *, tq=128, kt=512, tp=512):
    B, S, _ = p3.shape
    G, R, HD = NSA_KV, NSA_REP, NSA_HD
    assert CMP_BLOCK == 2 * CMP_STRIDE and S % kt == 0 and WINDOW % tq == 0 and tq == LANES
    ncp = S // CMP_STRIDE
    nsel = S // SEL_BLOCK
    n_cmp = (S - CMP_BLOCK) // CMP_STRIDE + 1
    c_tab, s1_tab, s2_tab = _rope_tables(S)
    tab_spec3 = pl.BlockSpec((tp, HD), lambda b, h, i: (i, 0))

    qt, qrt = pl.pallas_call(
        _nsa_prep_q_kernel,
        grid=(B, NSA_HEADS, S // tp),
        in_specs=[pl.BlockSpec((1, tp, HD), lambda b, h, i: (b, i, NSA_COL_Q + h)), tab_spec3, tab_spec3, tab_spec3],
        out_specs=[pl.BlockSpec((1, 1, HD, tp), lambda b, h, i: (b, h, 0, i))] * 2,
        out_shape=[jax.ShapeDtypeStruct((B, NSA_HEADS, HD, S), BF16)] * 2,
        compiler_params=_cparams(("parallel", "parallel", "parallel")),
        name="nsa_prep_q",
    )(p3, c_tab, s1_tab, s2_tab)

    def col_spec(col0):
        return pl.BlockSpec((1, tp, HD), lambda b, g, i: (b, i, col0 + g))

    ks_rot, kw_rot, vs_t, vw_t = pl.pallas_call(
        _nsa_prep_kv_kernel,
        grid=(B, G, S // tp),
        in_specs=[col_spec(NSA_COL_KS), col_spec(NSA_COL_KW), col_spec(NSA_COL_VS), col_spec(NSA_COL_VW),
                  tab_spec3, tab_spec3, tab_spec3],
        out_specs=[pl.BlockSpec((1, 1, tp, HD), lambda b, g, i: (b, g, i, 0))] * 2
        + [pl.BlockSpec((1, 1, HD, tp), lambda b, g, i: (b, g, 0, i))] * 2,
        out_shape=[jax.ShapeDtypeStruct((B, G, S, HD), BF16)] * 2 + [jax.ShapeDtypeStruct((B, G, HD, S), BF16)] * 2,
        compiler_params=_cparams(("parallel", "parallel", "parallel")),
        name="nsa_prep_kv",
    )(p3, p3, p3, p3, c_tab, s1_tab, s2_tab)

    k_cmp, v_cmp_t = pl.pallas_call(
        functools.partial(_nsa_compress_kernel, ncp=ncp),
        grid=(B, G),
        in_specs=[pl.BlockSpec((None, S, HD), lambda b, g: (b, 0, NSA_COL_KC + g)),
                  pl.BlockSpec((None, S, HD), lambda b, g: (b, 0, NSA_COL_VC + g)),
                  pl.BlockSpec((2, CMP_BLOCK, HD), lambda b, g: (0, 0, 0)),
                  pl.BlockSpec((2, CMP_BLOCK, HD, HD), lambda b, g: (0, 0, 0, 0)),
                  pl.BlockSpec((2, HD, HD), lambda b, g: (0, 0, 0))],
        out_specs=[pl.BlockSpec((1, 1, ncp, HD), lambda b, g: (b, g, 0, 0)),
                   pl.BlockSpec((1, 1, HD, ncp), lambda b, g: (b, g, 0, 0))],
        out_shape=[jax.ShapeDtypeStruct((B, G, ncp, HD), BF16), jax.ShapeDtypeStruct((B, G, HD, ncp), BF16)],
        compiler_params=_cparams(("parallel", "parallel")),
        name="nsa_compress",
    )(p3, p3, cmp_pos, cmp_w1.astype(BF16), cmp_w2.astype(BF16))

    c0 = np.arange(ncp)[None, :] * CMP_STRIDE
    s0 = np.arange(nsel)[:, None] * SEL_BLOCK
    ov_t = np.clip(np.minimum(c0 + CMP_BLOCK, s0 + SEL_BLOCK) - np.maximum(c0, s0), 0, None) / CMP_BLOCK
    ov_t = ov_t * (np.arange(ncp)[None, :] < n_cmp)
    gb = jnp.pad(gate_b, (0, LANES - gate_b.shape[0])).reshape(1, LANES)
    ncols = R * tq

    def full_kv(shape):
        return pl.BlockSpec((1, 1) + shape, lambda b, g, i: (b, g, 0, 0))

    return pl.pallas_call(
        functools.partial(_nsa_attn_kernel, tq=tq, kt=kt, ncp=ncp, nsel=nsel),
        grid=(B, G, S // tq),
        in_specs=[pl.BlockSpec((1, R, HD, tq), lambda b, g, i: (b, g, 0, i)),
                  pl.BlockSpec((1, R, HD, tq), lambda b, g, i: (b, g, 0, i)),
                  full_kv((ncp, HD)), full_kv((HD, ncp)),
                  full_kv((S, HD)), full_kv((S, HD)), full_kv((HD, S)), full_kv((HD, S)),
                  pl.BlockSpec((1, tq, LANES), lambda b, g, i: (b, i, NSA_COL_GL)),
                  pl.BlockSpec((1, LANES), lambda b, g, i: (0, 0)),
                  pl.BlockSpec((nsel, ncp), lambda b, g, i: (0, 0))],
        out_specs=pl.BlockSpec((1, tq, R * HD), lambda b, g, i: (b, i, g)),
        out_shape=jax.ShapeDtypeStruct((B, S, NSA_HEADS * HD), F32),
        scratch_shapes=[pltpu.VMEM((nsel, tq), F32),
                        pltpu.VMEM((1, ncols), F32), pltpu.VMEM((1, ncols), F32), pltpu.VMEM((HD, ncols), F32)],
        compiler_params=_cparams(("parallel", "parallel", "arbitrary")),
        name="nsa_attn",
    )(qt, qrt, k_cmp, v_cmp_t, ks_rot, kw_rot, vs_t, vw_t, p3, gb, jnp.asarray(ov_t, F32))


def _split(x, sizes):
    return jnp.split(x, [int(s) for s in np.cumsum(sizes)[:-1]], axis=-1)


def _rmsnorm_j(x, w):
    return x * lax.rsqrt(jnp.mean(x * x, axis=-1, keepdims=True) + EPS) * w


def _masked_softmax(s, mask):
    s = jnp.where(mask, s.astype(F32), -jnp.inf)
    m = jnp.max(s, axis=-1, keepdims=True)
    m = jnp.where(jnp.isfinite(m), m, 0.0)
    e = jnp.exp(s - m)
    d = jnp.sum(e, axis=-1, keepdims=True)
    return e / jnp.where(d > 0, d, 1.0)


def _rope(x, pos):
    half = ROPE_DIM // 2
    inv = 1.0 / (ROPE_THETA ** (jnp.arange(half, dtype=F32) / half))
    ang = pos.astype(F32)[:, None] * inv[None, :]
    shape = (ang.shape[0],) + (1,) * (x.ndim - 3) + (half,)
    cos = jnp.cos(ang).reshape(shape)
    sin = jnp.sin(ang).reshape(shape)
    x1 = x[..., :half]
    x2 = x[..., half:ROPE_DIM]
    return jnp.concatenate([x1 * cos - x2 * sin, x2 * cos + x1 * sin, x[..., ROPE_DIM:]], axis=-1)


def _token_shift(p):
    return jnp.pad(p, ((0, 0), (1, 0), (0, 0)))[:, :-1]


def _causal_conv(x, w, b):
    k_len = w.shape[0]
    s_len = x.shape[1]
    xp = jnp.pad(x, ((0, 0), (k_len - 1, 0), (0, 0)))
    y = b
    for j in range(k_len):
        y = y + xp[:, j:j + s_len] * w[j]
    return y


def _mlstm_j(q, k, v, i_pre, f_pre):
    B, S, H, DK = q.shape
    DV = v.shape[-1]
    L = ML_CHUNK
    NC = S // L

    def chunks(t):
        t = t.astype(F32).reshape((B, NC, L, H) + t.shape[3:])
        return t.transpose((1, 0, 3, 2) + tuple(range(4, t.ndim)))

    qc = chunks(q) * (DK ** -0.5)
    kc = chunks(k)
    vc = chunks(v)
    ic = chunks(i_pre)
    lfc = chunks(jax.nn.log_sigmoid(f_pre.astype(F32)))
    causal = jnp.asarray(np.tril(np.ones((L, L), dtype=bool)))

    def step(carry, xs):
        C, n, m = carry
        q_, k_, v_, i_, lf = xs
        b = jnp.cumsum(lf, axis=-1)
        dmat = jnp.where(causal, b[..., :, None] - b[..., None, :] + i_[..., None, :], -jnp.inf)
        inter = b + m[..., None]
        m_row = jnp.maximum(inter, jnp.max(dmat, axis=-1))
        w_in = jnp.exp(dmat - m_row[..., None])
        w_st = jnp.exp(inter - m_row)
        s = jnp.einsum('bhjd,bhld->bhjl', q_, k_) * w_in
        num = w_st[..., None] * jnp.einsum('bhjd,bhde->bhje', q_, C) + jnp.einsum('bhjl,bhle->bhje', s, v_)
        den = w_st * jnp.einsum('bhjd,bhd->bhj', q_, n) + jnp.sum(s, axis=-1)
        h = num / jnp.maximum(jnp.abs(den), jnp.exp(-m_row))[..., None]
        b_last = b[..., -1]
        g_key = b_last[..., None] - b + i_
        m_new = jnp.maximum(b_last + m, jnp.max(g_key, axis=-1))
        wk = jnp.exp(g_key - m_new[..., None])
        decay = jnp.exp(b_last + m - m_new)
        C_new = decay[..., None, None] * C + jnp.einsum('bhld,bhle->bhde', k_ * wk[..., None], v_)
        n_new = decay[..., None] * n + jnp.einsum('bhl,bhld->bhd', wk, k_)
        return (C_new, n_new, m_new), h

    init = (jnp.zeros((B, H, DK, DV), F32), jnp.zeros((B, H, DK), F32), jnp.zeros((B, H), F32))
    _, hs = lax.scan(step, init, (qc, kc, vc, ic, lfc))
    return hs.transpose(1, 0, 3, 2, 4).reshape(B, S, H, DV)


def _rwkv7_scan_j(r, w, k, v, kk, a):
    def step(state, xs):
        r_t, w_t, k_t, v_t, kk_t, a_t = xs
        sk = jnp.einsum('bhvk,bhk->bhv', state, kk_t)
        state = (state * w_t[:, :, None, :] - sk[..., None] * (kk_t * a_t)[:, :, None, :]
                 + v_t[..., None] * k_t[:, :, None, :])
        return state, jnp.einsum('bhvk,bhk->bhv', state, r_t)

    B, S, H, N = r.shape
    xs = tuple(t.transpose(1, 0, 2, 3) for t in (r, w, k, v, kk, a))
    _, y = lax.scan(step, jnp.zeros((B, H, N, N), F32), xs)
    return y.transpose(1, 0, 2, 3)


AB_COL_GATES = 50 * LANES
AB_COLS_PAD = 52 * LANES


def _ab_permute_cols(w_in):
    ml, rw = w_in[:, :ML_COLS], w_in[:, ML_COLS:]
    main, gif = ml[:, :ML_COLS - 2 * ML_HEADS], ml[:, ML_COLS - 2 * ML_HEADS:]
    w = jnp.concatenate([main, rw, gif], axis=1)
    return jnp.pad(w, ((0, 0), (0, AB_COLS_PAD - w.shape[1])))


def _mlstm_kernel(gb_ref, qk_ref, v_ref, o_ref, gcol_ref, grow_ref, cw_ref, cb_ref, y_ref,
                  halo_ref, c_ref, n_ref, m_ref, qc_ref, kc_ref, *, tb):
    H, DK, DV, L = ML_HEADS, ML_DK, ML_DV, ML_CHUNK
    HALO = SUBLANES

    @pl.when(pl.program_id(1) == 0)
    def _():
        halo_ref[...] = jnp.zeros_like(halo_ref)
        c_ref[...] = jnp.zeros_like(c_ref)
        n_ref[...] = jnp.zeros_like(n_ref)
        m_ref[...] = jnp.zeros_like(m_ref)

    x = qk_ref[0]
    xe = jnp.concatenate([halo_ref[...], x], axis=0)
    y = cb_ref[...]
    for j in range(ML_CONV):
        lo = HALO - (ML_CONV - 1) + j
        y = y + xe[lo:lo + tb] * cw_ref[j:j + 1, :]
    halo_ref[...] = x[tb - HALO:]
    y = y * jax.nn.sigmoid(y)
    qc_ref[...] = y[:, :H * DK] * (DK ** -0.5)
    kc_ref[...] = y[:, H * DK:]

    r_i = lax.broadcasted_iota(jnp.int32, (L, L), 0)
    c_i = lax.broadcasted_iota(jnp.int32, (L, L), 1)
    tri = c_i <= r_i

    def chunk(c, carry):
        rows = pl.ds(pl.multiple_of(c * L, L), L)
        for h in range(H):
            q = qc_ref[rows, h * DK:(h + 1) * DK]
            k = kc_ref[rows, h * DK:(h + 1) * DK]
            v = v_ref[0, rows, h * DV:(h + 1) * DV]
            gc = gcol_ref[0, h, rows, :]
            gr = grow_ref[0, h, c]
            i_col = gc[:, 0:1] + gb_ref[h]
            lf_col = jax.nn.log_sigmoid(gc[:, 1:2] + gb_ref[H + h])
            i_row = gr[0:1, :] + gb_ref[h]
            lf_row = jax.nn.log_sigmoid(gr[1:2, :] + gb_ref[H + h])
            b_col = jnp.sum(jnp.where(tri, lf_row, 0.0), axis=1, keepdims=True)
            b_row = jnp.sum(jnp.where(c_i >= r_i, lf_col, 0.0), axis=0, keepdims=True)
            dmat = jnp.where(tri, b_col - b_row + i_row, NSA_NEG)
            m_old = m_ref[h]
            inter = b_col + m_old
            m_row = jnp.maximum(inter, jnp.max(dmat, axis=1, keepdims=True))
            w_in = jnp.exp(dmat - m_row)
            w_st = jnp.exp(inter - m_row)
            qb = q.astype(BF16)
            s = lax.dot_general(qb, k.astype(BF16), (((1,), (1,)), ((), ())), preferred_element_type=F32) * w_in
            num = (w_st * jnp.dot(qb, c_ref[h].astype(BF16), preferred_element_type=F32)
                   + jnp.dot(s.astype(BF16), v.astype(BF16), preferred_element_type=F32))
            den = w_st * jnp.sum(q * n_ref[h], axis=1, keepdims=True) + jnp.sum(s, axis=1, keepdims=True)
            hid = num * (1.0 / jnp.maximum(jnp.abs(den), jnp.exp(-m_row)))
            y_ref[0, rows, h * DV:(h + 1) * DV] = jax.nn.sigmoid(o_ref[0, rows, h * DV:(h + 1) * DV]) * hid
            b_last = b_col[L - 1:L, :]
            g_key = b_last - b_col + i_col
            m_new = jnp.maximum(b_last + m_old, jnp.max(g_key, axis=0, keepdims=True))
            wk = jnp.exp(g_key - m_new)
            decay = jnp.exp(b_last + m_old - m_new)
            kw_t = (k * wk).T.astype(BF16)
            c_ref[h] = decay * c_ref[h] + jnp.dot(kw_t, v.astype(BF16), preferred_element_type=F32)
            n_ref[h] = decay * n_ref[h] + jnp.sum(wk * k, axis=0, keepdims=True)
            m_ref[h] = m_new
        return carry

    lax.fori_loop(0, tb // L, chunk, 0)


def _mlstm(p3, conv_w, conv_b, gate_b, tb=512):
    B, S, _ = p3.shape
    H, DK, DV, L = ML_HEADS, ML_DK, ML_DV, ML_CHUNK
    tb = min(tb, S)
    nc = S // L
    gif = p3[:, :, AB_COL_GATES:AB_COL_GATES + 2 * H].reshape(B, S, 2, H)
    gcol = gif.transpose(0, 3, 1, 2)
    grow = gif.reshape(B, nc, L, 2, H).transpose(0, 4, 1, 3, 2)
    qkw = 2 * H * DK
    return pl.pallas_call(
        functools.partial(_mlstm_kernel, tb=tb),
        grid_spec=pltpu.PrefetchScalarGridSpec(
            num_scalar_prefetch=1, grid=(B, S // tb),
            in_specs=[pl.BlockSpec((1, tb, qkw), lambda b, t, gb: (b, t, 0)),
                      pl.BlockSpec((1, tb, H * DV), lambda b, t, gb: (b, t, 1)),
                      pl.BlockSpec((1, tb, H * DV), lambda b, t, gb: (b, t, 2)),
                      pl.BlockSpec((1, H, tb, 2), lambda b, t, gb: (b, 0, t, 0)),
                      pl.BlockSpec((1, H, tb // L, 2, L), lambda b, t, gb: (b, 0, t, 0, 0)),
                      pl.BlockSpec((ML_CONV, qkw), lambda b, t, gb: (0, 0)),
                      pl.BlockSpec((1, qkw), lambda b, t, gb: (0, 0))],
            out_specs=pl.BlockSpec((1, tb, H * DV), lambda b, t, gb: (b, t, 0)),
            scratch_shapes=[pltpu.VMEM((SUBLANES, qkw), F32), pltpu.VMEM((H, DK, DV), F32),
                            pltpu.VMEM((H, 1, DK), F32), pltpu.VMEM((H, 1, 1), F32),
                            pltpu.VMEM((tb, H * DK), F32), pltpu.VMEM((tb, H * DK), F32)]),
        out_shape=jax.ShapeDtypeStruct((B, S, H * DV), F32),
        compiler_params=_cparams(("parallel", "arbitrary")),
        name="mlstm",
    )(gate_b, p3, p3, p3, gcol, grow, conv_w, conv_b.reshape(1, qkw))


def _head_sum_bcast(a, hmat):
    parts = [jnp.dot(a[:, i * LANES:(i + 1) * LANES], hmat, preferred_element_type=F32,
                     precision=lax.Precision.HIGHEST) for i in range(a.shape[1] // LANES)]
    return jnp.concatenate(parts, axis=1)


def _head_hmat():
    row = lax.broadcasted_iota(jnp.int32, (LANES, LANES), 0)
    col = lax.broadcasted_iota(jnp.int32, (LANES, LANES), 1)
    return jnp.where((row // RW_HEAD) == (col // RW_HEAD), 1.0, 0.0).astype(F32)


def _rwkv_prep_kernel(r_ref, k_ref, v_ref, lo_ref, mu_ref, w0_ref, wup_ref, a0_ref, aup_ref, gup_ref, kk_ref2, ka_ref,
                      rk_ref, ro_ref, wo_ref, ko_ref, vo_ref, kko_ref, kbo_ref, go_ref, bo_ref, prev_ref, *, tb):
    @pl.when(pl.program_id(1) == 0)
    def _():
        prev_ref[...] = jnp.zeros_like(prev_ref)

    C = RW_DIM
    x = jnp.concatenate([r_ref[0], k_ref[0], v_ref[0], lo_ref[0]], axis=1)
    prev = prev_ref[...]
    row0 = lax.broadcasted_iota(jnp.int32, (tb, 1), 0) == 0
    shifted = jnp.where(row0, prev[SUBLANES - 1:SUBLANES, :], pltpu.roll(x, 1, 0))
    prev_ref[...] = x[tb - SUBLANES:]
    xm = x + (shifted - x) * mu_ref[...]
    r, k, v, lo = xm[:, :C], xm[:, C:2 * C], xm[:, 2 * C:3 * C], xm[:, 3 * C:]
    xwa = lo[:, :LANES]
    xg = lo[:, LANES:]
    lw = jnp.dot(jnp.tanh(xwa).astype(BF16), wup_ref[...], preferred_element_type=F32)
    la = jnp.dot(xwa.astype(BF16), aup_ref[...], preferred_element_type=F32)
    g = jnp.dot(jax.nn.sigmoid(xg).astype(BF16), gup_ref[...], preferred_element_type=F32)
    w = jnp.exp(-math.exp(-0.5) * jax.nn.sigmoid(w0_ref[...] + lw))
    a = jax.nn.sigmoid(a0_ref[...] + la)
    hmat = _head_hmat()
    kk = k * kk_ref2[...]
    kk = kk * lax.rsqrt(_head_sum_bcast(kk * kk, hmat) + 1e-12)
    k2 = k * (1.0 + (a - 1.0) * ka_ref[...])
    ro_ref[0] = r
    wo_ref[0] = w
    ko_ref[0] = k2
    vo_ref[0] = v
    kko_ref[0] = kk
    kbo_ref[0] = kk * a
    go_ref[0] = g
    bo_ref[0] = _head_sum_bcast(r * k2 * rk_ref[...], hmat) * v


def _rwkv_post_kernel(y_ref, b_ref, g_ref, lnw_ref, lnb_ref, o_ref):
    y = y_ref[0]
    hmat = _head_hmat()
    mu = _head_sum_bcast(y, hmat) * (1.0 / RW_HEAD)
    yc = y - mu
    var = _head_sum_bcast(yc * yc, hmat) * (1.0 / RW_HEAD)
    yn = yc * lax.rsqrt(var + RW_LN_EPS) * lnw_ref[...] + lnb_ref[...]
    o_ref[0] = (yn + b_ref[0]) * g_ref[0]


def _rwkv_branch(p3, mu, w0, w_up, a0, a_up, g_up, k_k, k_a, r_k, ln_w, ln_b, tb=256):
    B, S, _ = p3.shape
    C = RW_DIM
    tb = min(tb, S)
    row = lambda t: t.reshape(1, -1)
    wup = jnp.concatenate([w_up, jnp.zeros_like(a_up)], axis=0).astype(BF16)
    aup = jnp.concatenate([jnp.zeros_like(w_up), a_up], axis=0).astype(BF16)
    seq = pl.BlockSpec((1, tb, C), lambda b, t: (b, t, 0))
    par = pl.BlockSpec((1, C), lambda b, t: (0, 0))
    r, w, k, v, kk, kb, g, bonus = pl.pallas_call(
        functools.partial(_rwkv_prep_kernel, tb=tb),
        grid=(B, S // tb),
        in_specs=[pl.BlockSpec((1, tb, C), lambda b, t: (b, t, 3)), pl.BlockSpec((1, tb, C), lambda b, t: (b, t, 4)),
                  pl.BlockSpec((1, tb, C), lambda b, t: (b, t, 5)),
                  pl.BlockSpec((1, tb, 2 * LANES), lambda b, t: (b, t, 24)),
                  pl.BlockSpec((1, RW_COLS), lambda b, t: (0, 0)), par,
                  pl.BlockSpec((LANES, C), lambda b, t: (0, 0)), par, pl.BlockSpec((LANES, C), lambda b, t: (0, 0)),
                  pl.BlockSpec((RW_LORA_G, C), lambda b, t: (0, 0)), par, par, par],
        out_specs=[seq] * 8,
        out_shape=[jax.ShapeDtypeStruct((B, S, C), F32)] * 8,
        scratch_shapes=[pltpu.VMEM((SUBLANES, RW_COLS), F32)],
        compiler_params=_cparams(("parallel", "arbitrary")),
        name="rwkv_prep",
    )(p3, p3, p3, p3, row(mu), row(w0), wup, row(a0), aup, g_up.astype(BF16), row(k_k), row(k_a), row(r_k))
    y = _rwkv_scan(r, w, k, v, kk, kb)
    return pl.pallas_call(
        _rwkv_post_kernel,
        grid=(B, S // tb),
        in_specs=[seq, seq, seq, par, par],
        out_specs=seq,
        out_shape=jax.ShapeDtypeStruct((B, S, C), F32),
        compiler_params=_cparams(("parallel", "parallel")),
        name="rwkv_post",
    )(y, bonus, g, row(ln_w), row(ln_b))


def _ab_mixer(x, norm_w, w_in, conv_w, conv_b, gate_b, mu, w0, w_up, a0, a_up, g_up, k_k, k_a, r_k, ln_w, ln_b,
              w_out):
    B, S, D = x.shape
    T = B * S
    x2 = x.reshape(T, D)
    p3 = _matmul(x2, _ab_permute_cols(w_in).astype(BF16), norm_w=norm_w).reshape(B, S, AB_COLS_PAD)
    y_m = _mlstm(p3, conv_w, conv_b, gate_b)
    y_r = _rwkv_branch(p3, mu, w0, w_up, a0, a_up, g_up, k_k, k_a, r_k, ln_w, ln_b)
    hm = ML_HEADS * ML_DV
    part = _matmul(y_m.reshape(T, hm), w_out[:hm].astype(BF16), residual=x2)
    return _matmul(y_r.reshape(T, RW_DIM), w_out[hm:].astype(BF16), residual=part).reshape(B, S, D)


def _ab_mixer_j(x, norm_w, w_in, conv_w, conv_b, gate_b, mu, w0, w_up, a0, a_up, g_up, k_k, k_a, r_k, ln_w, ln_b,
                w_out):
    B, S, D = x.shape
    T = B * S
    x2 = x.reshape(T, D)
    ncol = w_in.shape[1]
    p = _matmul(x2, _pad_cols(w_in, 512).astype(BF16), norm_w=norm_w)[:, :ncol].reshape(B, S, ncol)
    ml_p, rw_p = p[..., :ML_COLS], p[..., ML_COLS:]
    qk, v_m, o_m, gif = _split(ml_p, ML_SPLITS)
    qk = jax.nn.silu(_causal_conv(qk, conv_w, conv_b))
    q_m, k_m = jnp.split(qk, 2, axis=-1)
    gif = gif + gate_b
    h_m = _mlstm_j(q_m.reshape(B, S, ML_HEADS, ML_DK), k_m.reshape(B, S, ML_HEADS, ML_DK),
                   v_m.reshape(B, S, ML_HEADS, ML_DV), gif[..., :ML_HEADS], gif[..., ML_HEADS:])
    y_m = jax.nn.sigmoid(o_m) * h_m.reshape(B, S, ML_HEADS * ML_DV)
    rw_p = rw_p + (_token_shift(rw_p) - rw_p) * mu
    r, k, v, xw, xa, xg = _split(rw_p, RW_SPLITS)
    logw = -math.exp(-0.5) * jax.nn.sigmoid(w0 + jnp.tanh(xw) @ w_up)
    a = jax.nn.sigmoid(a0 + xa @ a_up)
    g = jax.nn.sigmoid(xg) @ g_up

    def heads(t):
        return t.reshape(B, S, RW_HEADS, RW_HEAD)

    kk = heads(k * k_k)
    kk = kk * lax.rsqrt(jnp.sum(kk * kk, axis=-1, keepdims=True) + 1e-12)
    k = k * (1.0 + (a - 1.0) * k_a)
    rh, kh, vh, ah = heads(r), heads(k), heads(v), heads(a)
    y = heads(_rwkv_scan(r, jnp.exp(logw), k, v, kk.reshape(B, S, RW_DIM), kk.reshape(B, S, RW_DIM) * a))
    y_mu = jnp.mean(y, axis=-1, keepdims=True)
    y_var = jnp.mean(jnp.square(y - y_mu), axis=-1, keepdims=True)
    yn = ((y - y_mu) * lax.rsqrt(y_var + RW_LN_EPS)).reshape(B, S, RW_DIM) * ln_w + ln_b
    bonus = (jnp.sum(rh * kh * r_k.reshape(RW_HEADS, RW_HEAD), axis=-1, keepdims=True) * vh).reshape(B, S, RW_DIM)
    y_r = (yn + bonus) * g
    y_cat = jnp.concatenate([y_m, y_r], axis=-1).reshape(T, D)
    return _matmul(y_cat, w_out.astype(BF16), residual=x2).reshape(B, S, D)


def _nsa(x, norm_w, w_in, gate_b, cmp_pos, cmp_w1, cmp_w2, w_out):
    B, S, D = x.shape
    T = B * S
    x2 = x.reshape(T, D)
    p = _matmul(x2, _pad_cols(w_in, 512).astype(BF16), norm_w=norm_w)
    o = _nsa_attention(p.reshape(B, S, -1), gate_b, cmp_pos, cmp_w1, cmp_w2)
    return _matmul(o.reshape(T, NSA_HEADS * NSA_HD), w_out.astype(BF16), residual=x2).reshape(B, S, D)


def _nsa_j(x, norm_w, w_in, gate_b, cmp_pos, cmp_w1, cmp_w2, w_out):
    B, S, D = x.shape
    T = B * S
    x2 = x.reshape(T, D)
    G, R, HD = NSA_KV, NSA_REP, NSA_HD
    QC = NSA_QCHUNK
    scale = HD ** -0.5
    pos = jnp.arange(S)
    ncol = w_in.shape[1]
    p = _matmul(x2, _pad_cols(w_in, 512).astype(BF16), norm_w=norm_w)[:, :ncol].reshape(B, S, ncol)
    q, kc, vc, ks, vs, kw, vw, gl = _split(p, NSA_SPLITS)
    q = q.reshape(B, S, G, R, HD)

    def kvh(t):
        return t.reshape(B, S, G, HD)

    q_rot = _rope(q, pos)
    ks = _rope(kvh(ks), pos)
    kw = _rope(kvh(kw), pos)
    vs = kvh(vs)
    vw = kvh(vw)
    gates = jax.nn.sigmoid(gl + gate_b).reshape(B, S, 3, G, R)
    n_cmp = (S - CMP_BLOCK) // CMP_STRIDE + 1
    cidx = np.arange(n_cmp)[:, None] * CMP_STRIDE + np.arange(CMP_BLOCK)[None, :]

    def compress(t, pe, w1, w2):
        blocks = t[:, cidx] + pe[None, None, :, None, :]
        return jax.nn.gelu(jnp.einsum('bjpgd,pde->bjge', blocks, w1)) @ w2

    k_cmp = compress(kvh(kc), cmp_pos[0], cmp_w1[0], cmp_w2[0])
    v_cmp = compress(kvh(vc), cmp_pos[1], cmp_w1[1], cmp_w2[1])
    cmp_end = jnp.asarray(cidx[:, -1])
    n_sel = S // SEL_BLOCK
    c0 = np.arange(n_cmp)[:, None] * CMP_STRIDE
    s0 = np.arange(n_sel)[None, :] * SEL_BLOCK
    ov = np.clip(np.minimum(c0 + CMP_BLOCK, s0 + SEL_BLOCK) - np.maximum(c0, s0), 0, None) / CMP_BLOCK
    ov = jnp.asarray(ov, dtype=F32)
    n_top = min(SEL_TOPK, n_sel)
    k_blk = ks.reshape(B, n_sel, SEL_BLOCK, G, HD).transpose(0, 3, 1, 2, 4)
    v_blk = vs.reshape(B, n_sel, SEL_BLOCK, G, HD).transpose(0, 3, 1, 2, 4)
    kw_pad = jnp.pad(kw, ((0, 0), (WINDOW, 0), (0, 0), (0, 0)))
    vw_pad = jnp.pad(vw, ((0, 0), (WINDOW, 0), (0, 0), (0, 0)))
    bi = jnp.arange(B)[:, None, None, None]
    gi = jnp.arange(G)[None, :, None, None]
    blk_ids = jnp.arange(n_sel)

    def chunk(c):
        t0 = c * QC
        qt = t0 + jnp.arange(QC)
        q_c = lax.dynamic_slice_in_dim(q, t0, QC, axis=1)
        qr_c = lax.dynamic_slice_in_dim(q_rot, t0, QC, axis=1)
        g_c = lax.dynamic_slice_in_dim(gates, t0, QC, axis=1)
        p_cmp = _masked_softmax(jnp.einsum('bqgrd,bjgd->bgrqj', q_c, k_cmp) * scale,
                                cmp_end[None, :] <= qt[:, None])
        o_cmp = jnp.einsum('bgrqj,bjgd->bqgrd', p_cmp, v_cmp)
        imp = jnp.einsum('bgrqj,js->bgqs', p_cmp, ov)
        cur = (qt // SEL_BLOCK)[:, None]
        forced = (blk_ids[None, :] == 0) | (blk_ids[None, :] == cur) | (blk_ids[None, :] == cur - 1)
        score = jnp.where(forced, jnp.inf, jnp.where(blk_ids[None, :] <= cur, imp, -jnp.inf))
        top_v, top_i = lax.top_k(score, n_top)
        k_g = k_blk[bi, gi, top_i].reshape(B, G, QC, n_top * SEL_BLOCK, HD)
        v_g = v_blk[bi, gi, top_i].reshape(B, G, QC, n_top * SEL_BLOCK, HD)
        kpos = (top_i[..., None] * SEL_BLOCK + jnp.arange(SEL_BLOCK)).reshape(B, G, QC, n_top * SEL_BLOCK)
        kmask = jnp.repeat(top_v > -jnp.inf, SEL_BLOCK, axis=-1) & (kpos <= qt[None, None, :, None])
        p_slc = _masked_softmax(jnp.einsum('bqgrd,bgqkd->bgrqk', qr_c, k_g) * scale, kmask[:, :, None])
        o_slc = jnp.einsum('bgrqk,bgqkd->bqgrd', p_slc, v_g)
        k_w = lax.dynamic_slice_in_dim(kw_pad, t0, QC + WINDOW, axis=1)
        v_w = lax.dynamic_slice_in_dim(vw_pad, t0, QC + WINDOW, axis=1)
        wpos = t0 - WINDOW + jnp.arange(QC + WINDOW)
        dist = qt[:, None] - wpos[None, :]
        wmask = (dist >= 0) & (dist < WINDOW) & (wpos[None, :] >= 0)
        p_win = _masked_softmax(jnp.einsum('bqgrd,bkgd->bgrqk', qr_c, k_w) * scale, wmask)
        o_win = jnp.einsum('bgrqk,bkgd->bqgrd', p_win, v_w)
        return (g_c[:, :, 0, :, :, None] * o_cmp + g_c[:, :, 1, :, :, None] * o_slc
                + g_c[:, :, 2, :, :, None] * o_win)

    o = lax.map(chunk, jnp.arange(S // QC))
    o = o.transpose(1, 0, 2, 3, 4, 5).reshape(T, NSA_HEADS * HD)
    return _matmul(o, w_out.astype(BF16), residual=x2).reshape(B, S, D)


def _cross_attn_kernel(q_ref, k_ref, v_ref, o_ref):
    for h in range(CA_HEADS):
        cols = slice(h * CA_HD, (h + 1) * CA_HD)
        q = q_ref[:, cols].astype(BF16)
        k = k_ref[:, cols].astype(BF16)
        s = lax.dot_general(q, k, (((1,), (1,)), ((), ())), preferred_element_type=F32) * (CA_HD ** -0.5)
        e = jnp.exp(s - jnp.max(s, axis=-1, keepdims=True))
        p = e * (1.0 / jnp.sum(e, axis=-1, keepdims=True))
        o_ref[:, cols] = jnp.dot(p.astype(BF16), v_ref[:, cols].astype(BF16), preferred_element_type=F32)


def _cross_attn(x, norm_w, mem, norm_mem, wq, wk, wv, wo, tq=512):
    B, S, D = x.shape
    T = B * S
    M = mem.shape[1]
    x2 = x.reshape(T, D)
    q = _matmul(x2, wq.astype(BF16), norm_w=norm_w)
    kv = _matmul(mem.reshape(B * M, D), jnp.concatenate([wk, wv], axis=1).astype(BF16), norm_w=norm_mem)
    nq = S // tq
    o = pl.pallas_call(
        _cross_attn_kernel,
        grid=(B, nq),
        in_specs=[pl.BlockSpec((tq, D), lambda b, i: (b * nq + i, 0)),
                  pl.BlockSpec((M, D), lambda b, i: (b, 0)), pl.BlockSpec((M, D), lambda b, i: (b, 1))],
        out_specs=pl.BlockSpec((tq, D), lambda b, i: (b * nq + i, 0)),
        out_shape=jax.ShapeDtypeStruct((T, D), F32),
        compiler_params=_cparams(("parallel", "parallel")),
        name="cross_attn",
    )(q, kv, kv)
    return _matmul(o, wo.astype(BF16), residual=x2).reshape(B, S, D)


def _cross_attn_j(x, norm_w, mem, norm_mem, wq, wk, wv, wo):
    B, S, D = x.shape
    T = B * S
    M = mem.shape[1]
    x2 = x.reshape(T, D)
    q = _matmul(x2, wq.astype(BF16), norm_w=norm_w).reshape(B, S, CA_HEADS, CA_HD)
    m2 = mem.reshape(B * M, D)
    kv = _matmul(m2, jnp.concatenate([wk, wv], axis=1).astype(BF16), norm_w=norm_mem)
    k = kv[:, :D].reshape(B, M, CA_HEADS, CA_HD)
    v = kv[:, D:].reshape(B, M, CA_HEADS, CA_HD)
    s = jnp.einsum('bshd,bmhd->bhsm', q, k) * (CA_HD ** -0.5)
    p = jax.nn.softmax(s, axis=-1)
    o = jnp.einsum('bhsm,bmhd->bshd', p, v).reshape(T, D)
    return _matmul(o, wo.astype(BF16), residual=x2).reshape(B, S, D)


MOE_TM = 256
MOE_NEG = -1e30


def _moe_route_kernel(x_ref, nw_ref, wr_ref, br_ref, xn_ref, ri_ref, rw_ref, cnt_ref, cnt_scr, *, tm):
    @pl.when(pl.program_id(0) == 0)
    def _():
        cnt_scr[...] = jnp.zeros_like(cnt_scr)

    x = x_ref[...]
    xn = x * lax.rsqrt(jnp.mean(x * x, axis=-1, keepdims=True) + EPS) * nw_ref[...]
    xn_ref[...] = xn
    logits = jnp.dot(xn, wr_ref[...], preferred_element_type=F32, precision=lax.Precision.HIGHEST) + br_ref[...]
    lane = lax.broadcasted_iota(jnp.int32, (tm, LANES), 1)
    gmask = lane < MOE_GROUPS
    lg = jnp.where(gmask, logits, MOE_NEG)
    gmax = jnp.max(lg, axis=-1, keepdims=True)
    grp = jnp.min(jnp.where(lg == gmax, lane, LANES), axis=-1, keepdims=True)
    p_grp = 1.0 / jnp.sum(jnp.where(gmask, jnp.exp(lg - gmax), 0.0), axis=-1, keepdims=True)
    lo = MOE_GROUPS + grp * MOE_PER_GROUP
    le = jnp.where((lane >= lo) & (lane < lo + MOE_PER_GROUP), logits, MOE_NEG)
    t1 = jnp.max(le, axis=-1, keepdims=True)
    i1 = jnp.min(jnp.where(le == t1, lane, LANES), axis=-1, keepdims=True)
    le2 = jnp.where(lane == i1, MOE_NEG, le)
    t2 = jnp.max(le2, axis=-1, keepdims=True)
    i2 = jnp.min(jnp.where(le2 == t2, lane, LANES), axis=-1, keepdims=True)
    e21 = jnp.exp(t2 - t1)
    w1 = p_grp / (1.0 + e21)
    w2 = w1 * e21
    e1 = i1 - MOE_GROUPS
    e2 = i2 - MOE_GROUPS
    oh1 = jnp.where(lane == e1, 1.0, 0.0)
    oh2 = jnp.where(lane == e2, 1.0, 0.0)
    both = oh1 + oh2
    r_i = lax.broadcasted_iota(jnp.int32, (tm, tm), 0)
    c_i = lax.broadcasted_iota(jnp.int32, (tm, tm), 1)
    ltri = jnp.where(c_i < r_i, 1.0, 0.0).astype(BF16)
    before = jnp.dot(ltri, both.astype(BF16), preferred_element_type=F32) + cnt_scr[...]
    pos1 = jnp.sum(oh1 * before, axis=-1, keepdims=True).astype(jnp.int32)
    pos2 = jnp.sum(oh2 * before, axis=-1, keepdims=True).astype(jnp.int32)
    cnt_scr[...] = cnt_scr[...] + jnp.sum(both, axis=0, keepdims=True)
    ri_ref[...] = jnp.where(lane == 0, e1, jnp.where(lane == 1, e2, jnp.where(lane == 2, pos1,
                            jnp.where(lane == 3, pos2, 0))))
    rw_ref[...] = jnp.where(lane == 0, w1, jnp.where(lane == 1, w2, 0.0))
    cnt_ref[...] = cnt_scr[...]


def _moe_dest(tok, e1_ref, e2_ref, p1_ref, p2_ref, off_ref):
    return off_ref[e1_ref[tok]] + p1_ref[tok], off_ref[e2_ref[tok]] + p2_ref[tok]


def _moe_rowmap_kernel(e1_ref, e2_ref, p1_ref, p2_ref, off_ref, cnt_ref, rt_ref, *, n_tok, rows):
    def clear(i, carry):
        rt_ref[i] = 0
        return carry

    def clear_pad(e, carry):
        used = off_ref[e] + cnt_ref[e]
        end = jnp.where(e + 1 < MOE_EXPERTS, off_ref[jnp.minimum(e + 1, MOE_EXPERTS - 1)], rows)
        lax.fori_loop(used, end, clear, 0)
        return carry

    def place(t, carry):
        d1, d2 = _moe_dest(t, e1_ref, e2_ref, p1_ref, p2_ref, off_ref)
        rt_ref[d1] = t
        rt_ref[d2] = t
        return carry

    lax.fori_loop(0, MOE_EXPERTS, clear_pad, 0)
    lax.fori_loop(0, n_tok, place, 0)


def _moe_expert_kernel(te_ref, nu_ref, rt_ref, xn_hbm, wg_ref, wu_ref, wd_ref, y_ref, xbuf, sems):
    del te_ref
    i = pl.program_id(0)
    nu = nu_ref[0]
    slot = i % 2

    def row_copy(tile, r, s):
        return pltpu.make_async_copy(xn_hbm.at[pl.ds(rt_ref[tile * MOE_TM + r], 1)], xbuf.at[s, pl.ds(r, 1)],
                                     sems.at[s])

    def issue(tile, s):
        def body(r, carry):
            row_copy(tile, r, s).start()
            return carry
        lax.fori_loop(0, MOE_TM, body, 0, unroll=8)

    def drain(s):
        def body(r, carry):
            row_copy(0, 0, s).wait()
            return carry
        lax.fori_loop(0, MOE_TM, body, 0, unroll=8)

    @pl.when(i == 0)
    def _():
        issue(0, 0)

    @pl.when(i < nu)
    def _():
        drain(slot)

        @pl.when(i + 1 < nu)
        def _():
            issue(i + 1, 1 - slot)

        x = xbuf[slot].astype(BF16)
        gate = jnp.dot(x, wg_ref[...].astype(BF16), preferred_element_type=F32)
        up = jnp.dot(x, wu_ref[...].astype(BF16), preferred_element_type=F32)
        hid = gate * jax.nn.sigmoid(gate) * up
        y_ref[...] = jnp.dot(hid.astype(BF16), wd_ref[...].astype(BF16), preferred_element_type=F32)

    @pl.when(pl.program_id(0) >= nu_ref[0])
    def _():
        y_ref[...] = jnp.zeros_like(y_ref)


def _moe_combine_kernel(e1_ref, e2_ref, p1_ref, p2_ref, off_ref, x_ref, rw_ref, ys_hbm, o_ref, buf1, buf2, sem, *, tm):
    base = pl.program_id(0) * tm

    def row_copy(src, t, buf):
        return pltpu.make_async_copy(ys_hbm.at[pl.ds(src, 1)], buf.at[pl.ds(t, 1)], sem)

    def issue(t, carry):
        d1, d2 = _moe_dest(base + t, e1_ref, e2_ref, p1_ref, p2_ref, off_ref)
        row_copy(d1, t, buf1).start()
        row_copy(d2, t, buf2).start()
        return carry

    def drain(t, carry):
        row_copy(0, 0, buf1).wait()
        row_copy(0, 0, buf2).wait()
        return carry

    lax.fori_loop(0, tm, issue, 0)
    lax.fori_loop(0, tm, drain, 0)
    w = rw_ref[...]
    o_ref[...] = x_ref[...] + w[:, 0:1] * buf1[...] + w[:, 1:2] * buf2[...]


def _hier_moe(x, norm_w, wg, bg, we, be, w_gate, w_up, w_down, layer):
    B, S, D = x.shape
    T = B * S
    x2 = x.reshape(T, D)
    FF = w_gate.shape[-1]
    tm_r = 512
    wr = jnp.pad(jnp.concatenate([wg, we], axis=1), ((0, 0), (0, LANES - MOE_GROUPS - MOE_EXPERTS)))
    br = jnp.pad(jnp.concatenate([bg, be]), (0, LANES - MOE_GROUPS - MOE_EXPERTS)).reshape(1, LANES)
    xn, ri, rw, cnt = pl.pallas_call(
        functools.partial(_moe_route_kernel, tm=tm_r),
        grid=(T // tm_r,),
        in_specs=[pl.BlockSpec((tm_r, D), lambda i: (i, 0)), pl.BlockSpec((1, D), lambda i: (0, 0)),
                  pl.BlockSpec((D, LANES), lambda i: (0, 0)), pl.BlockSpec((1, LANES), lambda i: (0, 0))],
        out_specs=[pl.BlockSpec((tm_r, D), lambda i: (i, 0)), pl.BlockSpec((tm_r, LANES), lambda i: (i, 0)),
                   pl.BlockSpec((tm_r, LANES), lambda i: (i, 0)), pl.BlockSpec((1, LANES), lambda i: (0, 0))],
        out_shape=[jax.ShapeDtypeStruct((T, D), F32), jax.ShapeDtypeStruct((T, LANES), jnp.int32),
                   jax.ShapeDtypeStruct((T, LANES), F32), jax.ShapeDtypeStruct((1, LANES), F32)],
        scratch_shapes=[pltpu.VMEM((1, LANES), F32)],
        compiler_params=_cparams(("arbitrary",)),
        name="moe_route",
    )(x2, norm_w.reshape(1, D), wr, br)

    counts = cnt[0, :MOE_EXPERTS].astype(jnp.int32)
    padded = (counts + MOE_TM - 1) // MOE_TM * MOE_TM
    ends = jnp.cumsum(padded)
    off = (ends - padded).astype(jnp.int32)
    n_tiles = (T * MOE_TOPK) // MOE_TM + MOE_EXPERTS
    rows = n_tiles * MOE_TM
    n_used = (ends[-1] // MOE_TM).astype(jnp.int32).reshape(1)
    tile_e = jnp.minimum(jnp.searchsorted(ends, jnp.arange(n_tiles, dtype=jnp.int32) * MOE_TM, side='right'),
                         MOE_EXPERTS - 1).astype(jnp.int32)
    e1, e2, p1, p2 = ri[:, 0], ri[:, 1], ri[:, 2], ri[:, 3]

    row_tok = pl.pallas_call(
        functools.partial(_moe_rowmap_kernel, n_tok=T, rows=rows),
        grid_spec=pltpu.PrefetchScalarGridSpec(
            num_scalar_prefetch=6, grid=(1,), in_specs=[],
            out_specs=pl.BlockSpec(memory_space=pltpu.SMEM)),
        out_shape=jax.ShapeDtypeStruct((rows,), jnp.int32),
        compiler_params=_cparams(("arbitrary",)),
        name="moe_rowmap",
    )(e1, e2, p1, p2, off, counts)

    def w_ix(i, te, nu, rt):
        return (layer, te[jnp.minimum(i, nu[0] - 1)], 0, 0)

    ys = pl.pallas_call(
        _moe_expert_kernel,
        grid_spec=pltpu.PrefetchScalarGridSpec(
            num_scalar_prefetch=3, grid=(n_tiles,),
            in_specs=[pl.BlockSpec(memory_space=pl.ANY),
                      pl.BlockSpec((None, None, D, FF), w_ix), pl.BlockSpec((None, None, D, FF), w_ix),
                      pl.BlockSpec((None, None, FF, D), w_ix)],
            out_specs=pl.BlockSpec((MOE_TM, D), lambda i, te, nu, rt: (i, 0)),
            scratch_shapes=[pltpu.VMEM((2, MOE_TM, D), F32), pltpu.SemaphoreType.DMA((2,))]),
        out_shape=jax.ShapeDtypeStruct((rows, D), F32),
        compiler_params=_cparams(("arbitrary",)),
        name="moe_experts",
    )(tile_e, n_used, row_tok, xn, w_gate, w_up, w_down)

    tm_c = 256
    out = pl.pallas_call(
        functools.partial(_moe_combine_kernel, tm=tm_c),
        grid_spec=pltpu.PrefetchScalarGridSpec(
            num_scalar_prefetch=5, grid=(T // tm_c,),
            in_specs=[pl.BlockSpec((tm_c, D), lambda i, *_: (i, 0)), pl.BlockSpec((tm_c, LANES), lambda i, *_: (i, 0)),
                      pl.BlockSpec(memory_space=pl.ANY)],
            out_specs=pl.BlockSpec((tm_c, D), lambda i, *_: (i, 0)),
            scratch_shapes=[pltpu.VMEM((tm_c, D), F32), pltpu.VMEM((tm_c, D), F32), pltpu.SemaphoreType.DMA(())]),
        out_shape=jax.ShapeDtypeStruct((T, D), F32),
        compiler_params=_cparams(("arbitrary",)),
        name="moe_combine",
    )(e1, e2, p1, p2, off, x2, rw, ys)
    return out.reshape(B, S, D)


def _hier_moe_j(x, norm_w, wg, bg, we, be, w_gate, w_up, w_down):
    B, S, D = x.shape
    h = _rmsnorm_j(x, norm_w)
    T = B * S
    A = T * MOE_TOPK
    xt = h.reshape(T, D)
    lg = xt @ wg + bg
    grp = jnp.argmax(lg, axis=-1)
    p_grp = jnp.take_along_axis(jax.nn.softmax(lg, axis=-1), grp[:, None], axis=-1)
    le = (xt @ we + be).reshape(T, MOE_GROUPS, MOE_PER_GROUP)
    le = jnp.take_along_axis(le, grp[:, None, None], axis=1)[:, 0]
    top_l, top_e = lax.top_k(le, MOE_TOPK)
    wts = (p_grp * jax.nn.softmax(top_l, axis=-1)).reshape(A)
    eid = (grp[:, None] * MOE_PER_GROUP + top_e).reshape(A)
    tok = jnp.repeat(jnp.arange(T, dtype=jnp.int32), MOE_TOPK)
    order = jnp.argsort(eid)
    e_s, tok_s, w_s = eid[order], tok[order], wts[order]
    counts = jax.ops.segment_sum(jnp.ones((A,), jnp.int32), eid, num_segments=MOE_EXPERTS)
    padded = (counts + MOE_ROWS - 1) // MOE_ROWS * MOE_ROWS
    ends = jnp.cumsum(padded)
    dest = (ends - padded)[e_s] + jnp.arange(A, dtype=jnp.int32) - (jnp.cumsum(counts) - counts)[e_s]
    n_chunks = -(-A // MOE_ROWS) + MOE_EXPERTS
    rows = n_chunks * MOE_ROWS
    row_tok = jnp.full((rows,), T, jnp.int32).at[dest].set(tok_s)
    row_w = jnp.zeros((rows,), F32).at[dest].set(w_s)
    chunk_e = jnp.minimum(jnp.searchsorted(ends, jnp.arange(n_chunks, dtype=jnp.int32) * MOE_ROWS, side='right'),
                          MOE_EXPERTS - 1)
    x_pad = jnp.concatenate([xt, jnp.zeros((1, D), xt.dtype)], axis=0)

    def expert_rows(args):
        t_idx, w_r, e = args
        xr = x_pad[t_idx]
        hid = jax.nn.silu(xr @ w_gate[e]) * (xr @ w_up[e])
        y = hid @ w_down[e]
        return y * w_r[:, None]

    out = lax.map(expert_rows, (row_tok.reshape(n_chunks, MOE_ROWS), row_w.reshape(n_chunks, MOE_ROWS), chunk_e))
    y = jax.ops.segment_sum(out.reshape(rows, D), row_tok, num_segments=T + 1)[:T]
    return x + y.reshape(B, S, D)


def kernel(x, mem, norm_mix, norm_cross, norm_mem, norm_ffn, norm_final,
           ab_w_in, ml_conv_w, ml_conv_b, ml_gate_b, rw_mu, rw_w0, rw_w_up, rw_a0, rw_a_up, rw_g_up,
           rw_k_k, rw_k_a, rw_r_k, rw_ln_w, rw_ln_b, ab_w_out,
           nsa_w_in, nsa_gate_b, cmp_pos, cmp_w1, cmp_w2, nsa_w_out,
           ca_wq, ca_wk, ca_wv, ca_wo,
           moe_wg, moe_bg, moe_we, moe_be, moe_w_gate, moe_w_up, moe_w_down):
    B, S, D = x.shape
    for l in range(DEPTH):
        j = l // 2
        if l % 2 == 0:
            x = _ab_mixer(x, norm_mix[l], ab_w_in[j], ml_conv_w[j], ml_conv_b[j], ml_gate_b[j], rw_mu[j], rw_w0[j],
                          rw_w_up[j], rw_a0[j], rw_a_up[j], rw_g_up[j], rw_k_k[j], rw_k_a[j], rw_r_k[j],
                          rw_ln_w[j], rw_ln_b[j], ab_w_out[j])
        else:
            x = _nsa(x, norm_mix[l], nsa_w_in[j], nsa_gate_b[j], cmp_pos[j], cmp_w1[j], cmp_w2[j], nsa_w_out[j])
        x = _cross_attn(x, norm_cross[l], mem, norm_mem[l], ca_wq[l], ca_wk[l], ca_wv[l], ca_wo[l])
        x = _hier_moe(x, norm_ffn[l], moe_wg[l], moe_bg[l], moe_we[l], moe_be[l],
                      moe_w_gate, moe_w_up, moe_w_down, l)
    return _rmsnorm_rows(x.reshape(B * S, D), norm_final).reshape(B, S, D)
```

```python
import functools
import math

import jax
import jax.numpy as jnp
import numpy as np
from jax import lax
from jax.experimental import pallas as pl
from jax.experimental.pallas import tpu as pltpu

F32 = jnp.float32
BF16 = jnp.bfloat16

D_MODEL = 2048
DEPTH = 2
EPS = 1e-6
ROPE_THETA = 500000.0

ML_HEADS = 4
ML_DV = D_MODEL // 2 // ML_HEADS
ML_DK = ML_DV // 2
ML_CHUNK = 64
ML_CONV = 4
RW_HEAD = 64
RW_HEADS = D_MODEL // 2 // RW_HEAD
RW_DIM = RW_HEADS * RW_HEAD
RW_LORA_W = 64
RW_LORA_A = 64
RW_LORA_G = 128
RW_LN_EPS = 64e-5
ML_SPLITS = (2 * ML_HEADS * ML_DK, ML_HEADS * ML_DV, ML_HEADS * ML_DV, 2 * ML_HEADS)
RW_SPLITS = (RW_DIM, RW_DIM, RW_DIM, RW_LORA_W, RW_LORA_A, RW_LORA_G)
ML_COLS = sum(ML_SPLITS)
RW_COLS = sum(RW_SPLITS)

NSA_HEADS = 16
NSA_KV = 4
NSA_REP = NSA_HEADS // NSA_KV
NSA_HD = D_MODEL // NSA_HEADS
ROPE_DIM = NSA_HD // 4
CMP_BLOCK = 32
CMP_STRIDE = 16
SEL_BLOCK = 64
SEL_TOPK = 16
WINDOW = 512
NSA_QCHUNK = 32
NSA_SPLITS = (NSA_HEADS * NSA_HD,) + (NSA_KV * NSA_HD,) * 6 + (3 * NSA_HEADS,)

CA_HEADS = 4
CA_HD = D_MODEL // CA_HEADS

MOE_GROUPS = 8
MOE_PER_GROUP = 8
MOE_EXPERTS = MOE_GROUPS * MOE_PER_GROUP
MOE_TOPK = 2
MOE_FF = D_MODEL // 4
MOE_ROWS = 128

V7X_VMEM_BYTES = 64 * 1024 * 1024
VMEM_LIMIT = 48 * 1024 * 1024
LANES = 128
SUBLANES = 8
NT_DIMS = (((1,), (1,)), ((), ()))


def _cparams(sem):
    return pltpu.CompilerParams(dimension_semantics=sem, vmem_limit_bytes=VMEM_LIMIT)


def _mm_kernel(*refs, has_norm, has_res):
    it = iter(refs)
    x_ref = next(it)
    w_ref = next(it)
    nw_ref = next(it) if has_norm else None
    r_ref = next(it) if has_res else None
    o_ref = next(it)
    xs_ref = next(it)

    @pl.when(pl.program_id(1) == 0)
    def _():
        x = x_ref[...]
        if has_norm:
            ms = jnp.mean(x * x, axis=-1, keepdims=True)
            x = x * lax.rsqrt(ms + EPS) * nw_ref[...]
        xs_ref[...] = x.astype(BF16)

    acc = jnp.dot(xs_ref[...], w_ref[...], preferred_element_type=F32)
    if has_res:
        acc = acc + r_ref[...]
    o_ref[...] = acc


def _matmul(x, w_bf16, *, norm_w=None, residual=None, tm=1024, tn=512):
    M, K = x.shape
    N = w_bf16.shape[1]
    tm = min(tm, M)
    tn = min(tn, N)
    assert M % tm == 0 and N % tn == 0, (M, N, tm, tn)
    has_norm = norm_w is not None
    has_res = residual is not None
    in_specs = [pl.BlockSpec((tm, K), lambda i, j: (i, 0)),
                pl.BlockSpec((K, tn), lambda i, j: (0, j))]
    args = [x, w_bf16]
    if has_norm:
        in_specs.append(pl.BlockSpec((1, K), lambda i, j: (0, 0)))
        args.append(norm_w.reshape(1, K))
    if has_res:
        in_specs.append(pl.BlockSpec((tm, tn), lambda i, j: (i, j)))
        args.append(residual)
    return pl.pallas_call(
        functools.partial(_mm_kernel, has_norm=has_norm, has_res=has_res),
        grid=(M // tm, N // tn),
        in_specs=in_specs,
        out_specs=pl.BlockSpec((tm, tn), lambda i, j: (i, j)),
        out_shape=jax.ShapeDtypeStruct((M, N), F32),
        scratch_shapes=[pltpu.VMEM((tm, K), BF16)],
        compiler_params=_cparams(("parallel", "arbitrary")),
        name="mm_norm" if has_norm else "mm",
    )(*args)


def _pad_cols(w, mult):
    n = w.shape[1]
    pad = (-n) % mult
    if pad:
        w = jnp.pad(w, ((0, 0), (0, pad)))
    return w


def _rmsnorm_kernel(x_ref, w_ref, o_ref):
    x = x_ref[...]
    ms = jnp.mean(x * x, axis=-1, keepdims=True)
    o_ref[...] = x * lax.rsqrt(ms + EPS) * w_ref[...]


def _rmsnorm_rows(x, w, tm=512):
    M, K = x.shape
    return pl.pallas_call(
        _rmsnorm_kernel,
        grid=(M // tm,),
        in_specs=[pl.BlockSpec((tm, K), lambda i: (i, 0)), pl.BlockSpec((1, K), lambda i: (0, 0))],
        out_specs=pl.BlockSpec((tm, K), lambda i: (i, 0)),
        out_shape=jax.ShapeDtypeStruct((M, K), F32),
        compiler_params=_cparams(("parallel",)),
        name="rmsnorm",
    )(x, w.reshape(1, K))


def _rwkv_scan_kernel(r_ref, w_ref, k_ref, v_ref, kk_ref, kb_ref, y_ref, s_ref, *, tb, nb, npairs):
    @pl.when(pl.program_id(1) == 0)
    def _():
        s_ref[...] = jnp.zeros_like(s_ref)

    row = lax.broadcasted_iota(jnp.int32, (LANES, LANES), 0)
    col = lax.broadcasted_iota(jnp.int32, (LANES, LANES), 1)
    hmat = jnp.where((row // RW_HEAD) == (col // RW_HEAD), 1.0, 0.0).astype(BF16)
    vrow = lax.broadcasted_iota(jnp.int32, (RW_HEAD, LANES), 0)
    vcol = lax.broadcasted_iota(jnp.int32, (RW_HEAD, LANES), 1)
    diag = jnp.where((vcol % RW_HEAD) == vrow, 1.0, 0.0).astype(BF16)
    row16 = lax.broadcasted_iota(jnp.int32, (2 * SUBLANES, LANES), 0)
    lane16 = lax.broadcasted_iota(jnp.int32, (2 * SUBLANES, LANES), 1)
    head16 = (row16 // SUBLANES) == (lane16 // RW_HEAD)
    row8 = lax.broadcasted_iota(jnp.int32, (SUBLANES, LANES), 0)
    lane8 = lax.broadcasted_iota(jnp.int32, (SUBLANES, LANES), 1)
    nq = npairs // 2
    lns = [pl.ds(p * LANES, LANES) for p in range(npairs)]

    def group_sum(parts):
        lhs = jnp.concatenate([q.astype(BF16) for q in parts], axis=0)
        out = jnp.dot(lhs, hmat, preferred_element_type=F32)
        return [out[i * RW_HEAD:(i + 1) * RW_HEAD] for i in range(len(parts))]

    def step(t8, carry):
        rows = pl.ds(pl.multiple_of(t8 * SUBLANES, SUBLANES), SUBLANES)

        def tiles(ref):
            return [[ref[b, rows, ln] for ln in lns] for b in range(nb)]

        kk8, w8, kb8, k8, v8, r8 = (tiles(kk_ref), tiles(w_ref), tiles(kb_ref), tiles(k_ref), tiles(v_ref),
                                    tiles(r_ref))
        s = [[s_ref[b, p] for p in range(npairs)] for b in range(nb)]
        vcols = [[lax.dot_general(diag, jnp.where(head16, jnp.concatenate([v8[b][p]] * 2, axis=0), 0.0).astype(BF16),
                                  NT_DIMS, preferred_element_type=F32).astype(BF16)
                  for p in range(npairs)] for b in range(nb)]
        vk = [[None] * npairs for _ in range(nb)]
        for b in range(nb):
            for p in range(npairs):
                ksel = jnp.where(head16, jnp.concatenate([k8[b][p]] * 2, axis=0), 0.0)
                rhs = jnp.concatenate([jnp.where((row16 % SUBLANES) == j, ksel, 0.0) for j in range(SUBLANES)],
                                      axis=1).astype(BF16)
                vk[b][p] = jnp.dot(vcols[b][p], rhs, preferred_element_type=F32)
        for j in range(SUBLANES):
            sl = slice(j, j + 1)
            for b in range(nb):
                sk = group_sum([s[b][i] * kk8[b][i][sl] for i in range(npairs)])
                s[b] = [s[b][i] * w8[b][i][sl] - sk[i] * kb8[b][i][sl] + vk[b][i][:, j * LANES:(j + 1) * LANES]
                        for i in range(npairs)]
            for b in range(nb):
                for q in range(nq):
                    p0, p1 = 2 * q, 2 * q + 1
                    a = jnp.where((row8 == 0) & (lane8 < RW_HEAD), r8[b][p0][sl],
                        jnp.where((row8 == 1) & (lane8 >= RW_HEAD), r8[b][p0][sl],
                        jnp.where((row8 == 2) & (lane8 < RW_HEAD), r8[b][p1][sl],
                        jnp.where((row8 == 3) & (lane8 >= RW_HEAD), r8[b][p1][sl], 0.0))))
                    st = jnp.concatenate([s[b][p0], s[b][p1]], axis=0).astype(BF16)
                    yq = lax.dot_general(a.astype(BF16), st, NT_DIMS, preferred_element_type=F32)
                    base = ((t8 * SUBLANES + j) * nq + q) * SUBLANES
                    y_ref[b, pl.ds(pl.multiple_of(base, SUBLANES), SUBLANES), :] = yq
        for b in range(nb):
            for p in range(npairs):
                s_ref[b, p] = s[b][p]
        return carry

    lax.fori_loop(0, tb // SUBLANES, step, 0)


def _rwkv_scan(r, w, k, v, kk, kb, tb=64, nb=4):
    B, S, C = r.shape
    npairs = C // LANES
    tb = min(tb, S)
    nb = min(nb, B)
    spec = pl.BlockSpec((nb, tb, C), lambda b, t: (b, t, 0))
    yrows = (npairs // 2) * SUBLANES
    return pl.pallas_call(
        functools.partial(_rwkv_scan_kernel, tb=tb, nb=nb, npairs=npairs),
        grid=(B // nb, S // tb),
        in_specs=[spec] * 6,
        out_specs=pl.BlockSpec((nb, tb * yrows, LANES), lambda b, t: (b, t, 0)),
        out_shape=jax.ShapeDtypeStruct((B, S * yrows, LANES), F32),
        scratch_shapes=[pltpu.VMEM((nb, npairs, RW_HEAD, LANES), F32)],
        compiler_params=_cparams(("parallel", "arbitrary")),
        name="rwkv_scan",
    )(r, w, k, v, kk, kb)


NSA_NEG = -1e30
NSA_FORCED = 1e30
NSA_REMOVED = -3e30
NSA_COL_Q, NSA_COL_KC, NSA_COL_VC, NSA_COL_KS, NSA_COL_VS, NSA_COL_KW, NSA_COL_VW, NSA_COL_GL = (
    0, 16, 20, 24, 28, 32, 36, 40)


def _rope_tables(S):
    half = ROPE_DIM // 2
    inv = 1.0 / (ROPE_THETA ** (jnp.arange(half, dtype=F32) / half))
    ang = jnp.arange(S, dtype=F32)[:, None] * inv[None, :]
    cos, sin = jnp.cos(ang), jnp.sin(ang)
    one = jnp.ones((S, NSA_HD - ROPE_DIM), F32)
    zero = jnp.zeros((S, NSA_HD - ROPE_DIM), F32)
    zh = jnp.zeros((S, half), F32)
    return (jnp.concatenate([cos, cos, one], axis=1), jnp.concatenate([zh, sin, zero], axis=1),
            jnp.concatenate([-sin, zh, zero], axis=1))


def _rope_rows(x, c, s1, s2):
    half = ROPE_DIM // 2
    return x * c + pltpu.roll(x, half, 1) * s1 + pltpu.roll(x, NSA_HD - half, 1) * s2


def _nsa_prep_q_kernel(x_ref, c_ref, s1_ref, s2_ref, qt_ref, qrt_ref):
    x = x_ref[0] * (NSA_HD ** -0.5)
    xr = _rope_rows(x, c_ref[...], s1_ref[...], s2_ref[...])
    qt_ref[0, 0] = x.T.astype(BF16)
    qrt_ref[0, 0] = xr.T.astype(BF16)


def _nsa_prep_kv_kernel(ks_ref, kw_ref, vs_ref, vw_ref, c_ref, s1_ref, s2_ref, kso_ref, kwo_ref, vst_ref, vwt_ref):
    c, s1, s2 = c_ref[...], s1_ref[...], s2_ref[...]
    kso_ref[0, 0] = _rope_rows(ks_ref[0], c, s1, s2).astype(BF16)
    kwo_ref[0, 0] = _rope_rows(kw_ref[0], c, s1, s2).astype(BF16)
    vst_ref[0, 0] = vs_ref[0].T.astype(BF16)
    vwt_ref[0, 0] = vw_ref[0].T.astype(BF16)


def _nsa_compress_kernel(kc_ref, vc_ref, pe_ref, w1_ref, w2_ref, kcmp_ref, vcmpt_ref, *, ncp):
    for which, x_ref in enumerate((kc_ref, vc_ref)):
        za = jnp.zeros((ncp, NSA_HD), F32)
        zb = jnp.zeros((ncp, NSA_HD), F32)
        for p in range(CMP_STRIDE):
            xp = x_ref[pl.ds(p, ncp, stride=CMP_STRIDE), :]
            za = za + jnp.dot((xp + pe_ref[which, p:p + 1, :]).astype(BF16), w1_ref[which, p],
                              preferred_element_type=F32)
            zb = zb + jnp.dot((xp + pe_ref[which, CMP_STRIDE + p:CMP_STRIDE + p + 1, :]).astype(BF16),
                              w1_ref[which, CMP_STRIDE + p], preferred_element_type=F32)
        pre = za + pltpu.roll(zb, ncp - 1, 0)
        out = jnp.dot(jax.nn.gelu(pre).astype(BF16), w2_ref[which], preferred_element_type=F32)
        if which == 0:
            kcmp_ref[0, 0] = out.astype(BF16)
        else:
            vcmpt_ref[0, 0] = out.T.astype(BF16)


def _nsa_attn_kernel(qt_ref, qrt_ref, kcmp_ref, vcmpt_ref, ks_ref, kw_ref, vst_ref, vwt_ref, gl_ref, gb_ref, ovt_ref,
                     o_ref, sel_ref, ms_ref, ls_ref, accs_ref, *, tq, kt, ncp, nsel):
    R = NSA_REP
    g = pl.program_id(1)
    qi = pl.program_id(2)
    t0 = qi * tq
    q_t = jnp.concatenate([qt_ref[0, r] for r in range(R)], axis=1)
    qr_t = jnp.concatenate([qrt_ref[0, r] for r in range(R)], axis=1)
    qpos = t0 + lax.broadcasted_iota(jnp.int32, (1, tq), 1)

    def per_head(fn, a):
        return jnp.concatenate([fn(a[:, r * tq:(r + 1) * tq]) for r in range(R)], axis=1)

    s = jnp.dot(kcmp_ref[0, 0], q_t, preferred_element_type=F32)
    cend = lax.broadcasted_iota(jnp.int32, (ncp, 1), 0) * CMP_STRIDE + (CMP_BLOCK - 1)
    vis = cend <= qpos
    s = per_head(lambda a: jnp.where(vis, a, NSA_NEG), s)
    m = jnp.max(s, axis=0, keepdims=True)
    e = per_head(lambda a: jnp.where(vis, a, 0.0), jnp.exp(s - m))
    d = jnp.sum(e, axis=0, keepdims=True)
    p = e * (1.0 / jnp.where(d > 0, d, 1.0))
    ocmp_t = jnp.dot(vcmpt_ref[0, 0], p.astype(BF16), preferred_element_type=F32)

    psum = p[:, 0:tq]
    for r in range(1, R):
        psum = psum + p[:, r * tq:(r + 1) * tq]
    imp = jnp.dot(ovt_ref[...], psum, preferred_element_type=F32, precision=lax.Precision.HIGHEST)
    sidx = lax.broadcasted_iota(jnp.int32, (nsel, tq), 0)
    cur = qpos // SEL_BLOCK
    forced = (sidx == 0) | (sidx == cur) | (sidx == cur - 1)
    score = jnp.where(forced, NSA_FORCED, jnp.where(sidx <= cur, imp, NSA_NEG))
    sel = jnp.zeros((nsel, tq), F32)
    for _ in range(SEL_TOPK):
        mx = jnp.max(score, axis=0, keepdims=True)
        idx = jnp.min(jnp.where(score == mx, sidx, nsel), axis=0, keepdims=True)
        pick = (sidx == idx) & (mx > 0.5 * NSA_NEG)
        sel = jnp.where(pick, 1.0, sel)
        score = jnp.where(pick, NSA_REMOVED, score)
    sel_ref[...] = sel

    def online_update(s, valid, v_t, m_ref, l_ref, acc_ref):
        s = per_head(lambda a: jnp.where(valid, a, NSA_NEG), s)
        m_old = m_ref[...]
        m_new = jnp.maximum(m_old, jnp.max(s, axis=0, keepdims=True))
        alpha = jnp.exp(m_old - m_new)
        pexp = jnp.exp(s - m_new)
        l_ref[...] = alpha * l_ref[...] + jnp.sum(pexp, axis=0, keepdims=True)
        acc_ref[...] = alpha * acc_ref[...] + jnp.dot(v_t, pexp.astype(BF16), preferred_element_type=F32)
        m_ref[...] = m_new

    ms_ref[...] = jnp.full(ms_ref.shape, NSA_NEG, F32)
    ls_ref[...] = jnp.zeros(ls_ref.shape, F32)
    accs_ref[...] = jnp.zeros(accs_ref.shape, F32)

    nblk = kt // SEL_BLOCK

    def sel_body(c, carry):
        k0 = pl.multiple_of(c * kt, kt)
        s = jnp.dot(ks_ref[0, 0, pl.ds(k0, kt), :], qr_t, preferred_element_type=F32)
        selrows = sel_ref[pl.ds(pl.multiple_of(c * nblk, nblk), nblk), :]
        selexp = jnp.concatenate(
            [jnp.broadcast_to(selrows[b:b + 1], (SEL_BLOCK, tq)) for b in range(nblk)], axis=0)
        kpos = k0 + lax.broadcasted_iota(jnp.int32, (kt, 1), 0)
        valid = (selexp > 0.5) & (kpos <= qpos)
        online_update(s, valid, vst_ref[0, 0, :, pl.ds(k0, kt)], ms_ref, ls_ref, accs_ref)
        return carry

    lax.fori_loop(0, t0 // kt + 1, sel_body, 0)

    nwk = WINDOW + tq
    k0 = pl.multiple_of(jnp.maximum(t0 - WINDOW, 0), tq)
    s = jnp.dot(kw_ref[0, 0, pl.ds(k0, nwk), :], qr_t, preferred_element_type=F32)
    dist = qpos - (k0 + lax.broadcasted_iota(jnp.int32, (nwk, 1), 0))
    inwin = (dist >= 0) & (dist < WINDOW)
    s = per_head(lambda a: jnp.where(inwin, a, NSA_NEG), s)
    ew = jnp.exp(s - jnp.max(s, axis=0, keepdims=True))
    owin_t = (jnp.dot(vwt_ref[0, 0, :, pl.ds(k0, nwk)], ew.astype(BF16), preferred_element_type=F32)
              * (1.0 / jnp.sum(ew, axis=0, keepdims=True)))

    oslc_t = accs_ref[...] * (1.0 / ls_ref[...])

    gates_t = jax.nn.sigmoid(gl_ref[0] + gb_ref[...]).T
    rid = lax.broadcasted_iota(jnp.int32, (LANES, 1), 0)

    def gate_row(which, r):
        return jnp.sum(jnp.where(rid == which * NSA_HEADS + g * R + r, gates_t, 0.0), axis=0, keepdims=True)

    for r in range(R):
        cols = slice(r * tq, (r + 1) * tq)
        o_t = (gate_row(0, r) * ocmp_t[:, cols] + gate_row(1, r) * oslc_t[:, cols]
               + gate_row(2, r) * owin_t[:, cols])
        o_ref[0, :, r * NSA_HD:(r + 1) * NSA_HD] = o_t.T


def _nsa_attention(p3, gate_b, cmp_pos, cmp_w1, cmp_w2, *, tq=128, kt=512, tp=512):
    B, S, _ = p3.shape
    G, R, HD = NSA_KV, NSA_REP, NSA_HD
    assert CMP_BLOCK == 2 * CMP_STRIDE and S % kt == 0 and WINDOW % tq == 0 and tq == LANES
    ncp = S // CMP_STRIDE
    nsel = S // SEL_BLOCK
    n_cmp = (S - CMP_BLOCK) // CMP_STRIDE + 1
    c_tab, s1_tab, s2_tab = _rope_tables(S)
    tab_spec3 = pl.BlockSpec((tp, HD), lambda b, h, i: (i, 0))

    qt, qrt = pl.pallas_call(
        _nsa_prep_q_kernel,
        grid=(B, NSA_HEADS, S // tp),
        in_specs=[pl.BlockSpec((1, tp, HD), lambda b, h, i: (b, i, NSA_COL_Q + h)), tab_spec3, tab_spec3, tab_spec3],
        out_specs=[pl.BlockSpec((1, 1, HD, tp), lambda b, h, i: (b, h, 0, i))] * 2,
        out_shape=[jax.ShapeDtypeStruct((B, NSA_HEADS, HD, S), BF16)] * 2,
        compiler_params=_cparams(("parallel", "parallel", "parallel")),
        name="nsa_prep_q",
    )(p3, c_tab, s1_tab, s2_tab)

    def col_spec(col0):
        return pl.BlockSpec((1, tp, HD), lambda b, g, i: (b, i, col0 + g))

    ks_rot, kw_rot, vs_t, vw_t = pl.pallas_call(
        _nsa_prep_kv_kernel,
        grid=(B, G, S // tp),
        in_specs=[col_spec(NSA_COL_KS), col_spec(NSA_COL_KW), col_spec(NSA_COL_VS), col_spec(NSA_COL_VW),
                  tab_spec3, tab_spec3, tab_spec3],
        out_specs=[pl.BlockSpec((1, 1, tp, HD), lambda b, g, i: (b, g, i, 0))] * 2
        + [pl.BlockSpec((1, 1, HD, tp), lambda b, g, i: (b, g, 0, i))] * 2,
        out_shape=[jax.ShapeDtypeStruct((B, G, S, HD), BF16)] * 2 + [jax.ShapeDtypeStruct((B, G, HD, S), BF16)] * 2,
        compiler_params=_cparams(("parallel", "parallel", "parallel")),
        name="nsa_prep_kv",
    )(p3, p3, p3, p3, c_tab, s1_tab, s2_tab)

    k_cmp, v_cmp_t = pl.pallas_call(
        functools.partial(_nsa_compress_kernel, ncp=ncp),
        grid=(B, G),
        in_specs=[pl.BlockSpec((None, S, HD), lambda b, g: (b, 0, NSA_COL_KC + g)),
                  pl.BlockSpec((None, S, HD), lambda b, g: (b, 0, NSA_COL_VC + g)),
                  pl.BlockSpec((2, CMP_BLOCK, HD), lambda b, g: (0, 0, 0)),
                  pl.BlockSpec((2, CMP_BLOCK, HD, HD), lambda b, g: (0, 0, 0, 0)),
                  pl.BlockSpec((2, HD, HD), lambda b, g: (0, 0, 0))],
        out_specs=[pl.BlockSpec((1, 1, ncp, HD), lambda b, g: (b, g, 0, 0)),
                   pl.BlockSpec((1, 1, HD, ncp), lambda b, g: (b, g, 0, 0))],
        out_shape=[jax.ShapeDtypeStruct((B, G, ncp, HD), BF16), jax.ShapeDtypeStruct((B, G, HD, ncp), BF16)],
        compiler_params=_cparams(("parallel", "parallel")),
        name="nsa_compress",
    )(p3, p3, cmp_pos, cmp_w1.astype(BF16), cmp_w2.astype(BF16))

    c0 = np.arange(ncp)[None, :] * CMP_STRIDE
    s0 = np.arange(nsel)[:, None] * SEL_BLOCK
    ov_t = np.clip(np.minimum(c0 + CMP_BLOCK, s0 + SEL_BLOCK) - np.maximum(c0, s0), 0, None) / CMP_BLOCK
    ov_t = ov_t * (np.arange(ncp)[None, :] < n_cmp)
    gb = jnp.pad(gate_b, (0, LANES - gate_b.shape[0])).reshape(1, LANES)
    ncols = R * tq

    def full_kv(shape):
        return pl.BlockSpec((1, 1) + shape, lambda b, g, i: (b, g, 0, 0))

    return pl.pallas_call(
        functools.partial(_nsa_attn_kernel, tq=tq, kt=kt, ncp=ncp, nsel=nsel),
        grid=(B, G, S // tq),
        in_specs=[pl.BlockSpec((1, R, HD, tq), lambda b, g, i: (b, g, 0, i)),
                  pl.BlockSpec((1, R, HD, tq), lambda b, g, i: (b, g, 0, i)),
                  full_kv((ncp, HD)), full_kv((HD, ncp)),
                  full_kv((S, HD)), full_kv((S, HD)), full_kv((HD, S)), full_kv((HD, S)),
                  pl.BlockSpec((1, tq, LANES), lambda b, g, i: (b, i, NSA_COL_GL)),
                  pl.BlockSpec((1, LANES), lambda b, g, i: (0, 0)),
                  pl.BlockSpec((nsel, ncp), lambda b, g, i: (0, 0))],
        out_specs=pl.BlockSpec((1, tq, R * HD), lambda b, g, i: (b, i, g)),
        out_shape=jax.ShapeDtypeStruct((B, S, NSA_HEADS * HD), F32),
        scratch_shapes=[pltpu.VMEM((nsel, tq), F32),
                        pltpu.VMEM((1, ncols), F32), pltpu.VMEM((1, ncols), F32), pltpu.VMEM((HD, ncols), F32)],
        compiler_params=_cparams(("parallel", "parallel", "arbitrary")),
        name="nsa_attn",
    )(qt, qrt, k_cmp, v_cmp_t, ks_rot, kw_rot, vs_t, vw_t, p3, gb, jnp.asarray(ov_t, F32))


def _split(x, sizes):
    return jnp.split(x, [int(s) for s in np.cumsum(sizes)[:-1]], axis=-1)


def _rmsnorm_j(x, w):
    return x * lax.rsqrt(jnp.mean(x * x, axis=-1, keepdims=True) + EPS) * w


def _masked_softmax(s, mask):
    s = jnp.where(mask, s.astype(F32), -jnp.inf)
    m = jnp.max(s, axis=-1, keepdims=True)
    m = jnp.where(jnp.isfinite(m), m, 0.0)
    e = jnp.exp(s - m)
    d = jnp.sum(e, axis=-1, keepdims=True)
    return e / jnp.where(d > 0, d, 1.0)


def _rope(x, pos):
    half = ROPE_DIM // 2
    inv = 1.0 / (ROPE_THETA ** (jnp.arange(half, dtype=F32) / half))
    ang = pos.astype(F32)[:, None] * inv[None, :]
    shape = (ang.shape[0],) + (1,) * (x.ndim - 3) + (half,)
    cos = jnp.cos(ang).reshape(shape)
    sin = jnp.sin(ang).reshape(shape)
    x1 = x[..., :half]
    x2 = x[..., half:ROPE_DIM]
    return jnp.concatenate([x1 * cos - x2 * sin, x2 * cos + x1 * sin, x[..., ROPE_DIM:]], axis=-1)


def _token_shift(p):
    return jnp.pad(p, ((0, 0), (1, 0), (0, 0)))[:, :-1]


def _causal_conv(x, w, b):
    k_len = w.shape[0]
    s_len = x.shape[1]
    xp = jnp.pad(x, ((0, 0), (k_len - 1, 0), (0, 0)))
    y = b
    for j in range(k_len):
        y = y + xp[:, j:j + s_len] * w[j]
    return y


def _mlstm_j(q, k, v, i_pre, f_pre):
    B, S, H, DK = q.shape
    DV = v.shape[-1]
    L = ML_CHUNK
    NC = S // L

    def chunks(t):
        t = t.astype(F32).reshape((B, NC, L, H) + t.shape[3:])
        return t.transpose((1, 0, 3, 2) + tuple(range(4, t.ndim)))

    qc = chunks(q) * (DK ** -0.5)
    kc = chunks(k)
    vc = chunks(v)
    ic = chunks(i_pre)
    lfc = chunks(jax.nn.log_sigmoid(f_pre.astype(F32)))
    causal = jnp.asarray(np.tril(np.ones((L, L), dtype=bool)))

    def step(carry, xs):
        C, n, m = carry
        q_, k_, v_, i_, lf = xs
        b = jnp.cumsum(lf, axis=-1)
        dmat = jnp.where(causal, b[..., :, None] - b[..., None, :] + i_[..., None, :], -jnp.inf)
        inter = b + m[..., None]
        m_row = jnp.maximum(inter, jnp.max(dmat, axis=-1))
        w_in = jnp.exp(dmat - m_row[..., None])
        w_st = jnp.exp(inter - m_row)
        s = jnp.einsum('bhjd,bhld->bhjl', q_, k_) * w_in
        num = w_st[..., None] * jnp.einsum('bhjd,bhde->bhje', q_, C) + jnp.einsum('bhjl,bhle->bhje', s, v_)
        den = w_st * jnp.einsum('bhjd,bhd->bhj', q_, n) + jnp.sum(s, axis=-1)
        h = num / jnp.maximum(jnp.abs(den), jnp.exp(-m_row))[..., None]
        b_last = b[..., -1]
        g_key = b_last[..., None] - b + i_
        m_new = jnp.maximum(b_last + m, jnp.max(g_key, axis=-1))
        wk = jnp.exp(g_key - m_new[..., None])
        decay = jnp.exp(b_last + m - m_new)
        C_new = decay[..., None, None] * C + jnp.einsum('bhld,bhle->bhde', k_ * wk[..., None], v_)
        n_new = decay[..., None] * n + jnp.einsum('bhl,bhld->bhd', wk, k_)
        return (C_new, n_new, m_new), h

    init = (jnp.zeros((B, H, DK, DV), F32), jnp.zeros((B, H, DK), F32), jnp.zeros((B, H), F32))
    _, hs = lax.scan(step, init, (qc, kc, vc, ic, lfc))
    return hs.transpose(1, 0, 3, 2, 4).reshape(B, S, H, DV)


def _rwkv7_scan_j(r, w, k, v, kk, a):
    def step(state, xs):
        r_t, w_t, k_t, v_t, kk_t, a_t = xs
        sk = jnp.einsum('bhvk,bhk->bhv', state, kk_t)
        state = (state * w_t[:, :, None, :] - sk[..., None] * (kk_t * a_t)[:, :, None, :]
                 + v_t[..., None] * k_t[:, :, None, :])
        return state, jnp.einsum('bhvk,bhk->bhv', state, r_t)

    B, S, H, N = r.shape
    xs = tuple(t.transpose(1, 0, 2, 3) for t in (r, w, k, v, kk, a))
    _, y = lax.scan(step, jnp.zeros((B, H, N, N), F32), xs)
    return y.transpose(1, 0, 2, 3)


AB_COL_GATES = 50 * LANES
AB_COLS_PAD = 52 * LANES


def _ab_permute_cols(w_in):
    ml, rw = w_in[:, :ML_COLS], w_in[:, ML_COLS:]
    main, gif = ml[:, :ML_COLS - 2 * ML_HEADS], ml[:, ML_COLS - 2 * ML_HEADS:]
    w = jnp.concatenate([main, rw, gif], axis=1)
    return jnp.pad(w, ((0, 0), (0, AB_COLS_PAD - w.shape[1])))


def _mlstm_kernel(gb_ref, qk_ref, v_ref, o_ref, gcol_ref, grow_ref, cw_ref, cb_ref, y_ref,
                  halo_ref, c_ref, n_ref, m_ref, qc_ref, kc_ref, *, tb):
    H, DK, DV, L = ML_HEADS, ML_DK, ML_DV, ML_CHUNK
    HALO = SUBLANES

    @pl.when(pl.program_id(1) == 0)
    def _():
        halo_ref[...] = jnp.zeros_like(halo_ref)
        c_ref[...] = jnp.zeros_like(c_ref)
        n_ref[...] = jnp.zeros_like(n_ref)
        m_ref[...] = jnp.zeros_like(m_ref)

    x = qk_ref[0]
    xe = jnp.concatenate([halo_ref[...], x], axis=0)
    y = cb_ref[...]
    for j in range(ML_CONV):
        lo = HALO - (ML_CONV - 1) + j
        y = y + xe[lo:lo + tb] * cw_ref[j:j + 1, :]
    halo_ref[...] = x[tb - HALO:]
    y = y * jax.nn.sigmoid(y)
    qc_ref[...] = y[:, :H * DK] * (DK ** -0.5)
    kc_ref[...] = y[:, H * DK:]

    r_i = lax.broadcasted_iota(jnp.int32, (L, L), 0)
    c_i = lax.broadcasted_iota(jnp.int32, (L, L), 1)
    tri = c_i <= r_i

    def chunk(c, carry):
        rows = pl.ds(pl.multiple_of(c * L, L), L)
        for h in range(H):
            q = qc_ref[rows, h * DK:(h + 1) * DK]
            k = kc_ref[rows, h * DK:(h + 1) * DK]
            v = v_ref[0, rows, h * DV:(h + 1) * DV]
            gc = gcol_ref[0, h, rows, :]
            gr = grow_ref[0, h, c]
            i_col = gc[:, 0:1] + gb_ref[h]
            lf_col = jax.nn.log_sigmoid(gc[:, 1:2] + gb_ref[H + h])
            i_row = gr[0:1, :] + gb_ref[h]
            lf_row = jax.nn.log_sigmoid(gr[1:2, :] + gb_ref[H + h])
            b_col = jnp.sum(jnp.where(tri, lf_row, 0.0), axis=1, keepdims=True)
            b_row = jnp.sum(jnp.where(c_i >= r_i, lf_col, 0.0), axis=0, keepdims=True)
            dmat = jnp.where(tri, b_col - b_row + i_row, NSA_NEG)
            m_old = m_ref[h]
            inter = b_col + m_old
            m_row = jnp.maximum(inter, jnp.max(dmat, axis=1, keepdims=True))
            w_in = jnp.exp(dmat - m_row)
            w_st = jnp.exp(inter - m_row)
            qb = q.astype(BF16)
            s = lax.dot_general(qb, k.astype(BF16), (((1,), (1,)), ((), ())), preferred_element_type=F32) * w_in
            num = (w_st * jnp.dot(qb, c_ref[h].astype(BF16), preferred_element_type=F32)
                   + jnp.dot(s.astype(BF16), v.astype(BF16), preferred_element_type=F32))
            den = w_st * jnp.sum(q * n_ref[h], axis=1, keepdims=True) + jnp.sum(s, axis=1, keepdims=True)
            hid = num * (1.0 / jnp.maximum(jnp.abs(den), jnp.exp(-m_row)))
            y_ref[0, rows, h * DV:(h + 1) * DV] = jax.nn.sigmoid(o_ref[0, rows, h * DV:(h + 1) * DV]) * hid
            b_last = b_col[L - 1:L, :]
            g_key = b_last - b_col + i_col
            m_new = jnp.maximum(b_last + m_old, jnp.max(g_key, axis=0, keepdims=True))
            wk = jnp.exp(g_key - m_new)
            decay = jnp.exp(b_last + m_old - m_new)
            kw_t = (k * wk).T.astype(BF16)
            c_ref[h] = decay * c_ref[h] + jnp.dot(kw_t, v.astype(BF16), preferred_element_type=F32)
            n_ref[h] = decay * n_ref[h] + jnp.sum(wk * k, axis=0, keepdims=True)
            m_ref[h] = m_new
        return carry

    lax.fori_loop(0, tb // L, chunk, 0)


def _mlstm(p3, conv_w, conv_b, gate_b, tb=512):
    B, S, _ = p3.shape
    H, DK, DV, L = ML_HEADS, ML_DK, ML_DV, ML_CHUNK
    tb = min(tb, S)
    nc = S // L
    gif = p3[:, :, AB_COL_GATES:AB_COL_GATES + 2 * H].reshape(B, S, 2, H)
    gcol = gif.transpose(0, 3, 1, 2)
    grow = gif.reshape(B, nc, L, 2, H).transpose(0, 4, 1, 3, 2)
    qkw = 2 * H * DK
    return pl.pallas_call(
        functools.partial(_mlstm_kernel, tb=tb),
        grid_spec=pltpu.PrefetchScalarGridSpec(
            num_scalar_prefetch=1, grid=(B, S // tb),
            in_specs=[pl.BlockSpec((1, tb, qkw), lambda b, t, gb: (b, t, 0)),
                      pl.BlockSpec((1, tb, H * DV), lambda b, t, gb: (b, t, 1)),
                      pl.BlockSpec((1, tb, H * DV), lambda b, t, gb: (b, t, 2)),
                      pl.BlockSpec((1, H, tb, 2), lambda b, t, gb: (b, 0, t, 0)),
                      pl.BlockSpec((1, H, tb // L, 2, L), lambda b, t, gb: (b, 0, t, 0, 0)),
                      pl.BlockSpec((ML_CONV, qkw), lambda b, t, gb: (0, 0)),
                      pl.BlockSpec((1, qkw), lambda b, t, gb: (0, 0))],
            out_specs=pl.BlockSpec((1, tb, H * DV), lambda b, t, gb: (b, t, 0)),
            scratch_shapes=[pltpu.VMEM((SUBLANES, qkw), F32), pltpu.VMEM((H, DK, DV), F32),
                            pltpu.VMEM((H, 1, DK), F32), pltpu.VMEM((H, 1, 1), F32),
                            pltpu.VMEM((tb, H * DK), F32), pltpu.VMEM((tb, H * DK), F32)]),
        out_shape=jax.ShapeDtypeStruct((B, S, H * DV), F32),
        compiler_params=_cparams(("parallel", "arbitrary")),
        name="mlstm",
    )(gate_b, p3, p3, p3, gcol, grow, conv_w, conv_b.reshape(1, qkw))


def _head_sum_bcast(a, hmat):
    parts = [jnp.dot(a[:, i * LANES:(i + 1) * LANES], hmat, preferred_element_type=F32,
                     precision=lax.Precision.HIGHEST) for i in range(a.shape[1] // LANES)]
    return jnp.concatenate(parts, axis=1)


def _head_hmat():
    row = lax.broadcasted_iota(jnp.int32, (LANES, LANES), 0)
    col = lax.broadcasted_iota(jnp.int32, (LANES, LANES), 1)
    return jnp.where((row // RW_HEAD) == (col // RW_HEAD), 1.0, 0.0).astype(F32)


def _rwkv_prep_kernel(r_ref, k_ref, v_ref, lo_ref, mu_ref, w0_ref, wup_ref, a0_ref, aup_ref, gup_ref, kk_ref2, ka_ref,
                      rk_ref, ro_ref, wo_ref, ko_ref, vo_ref, kko_ref, kbo_ref, go_ref, bo_ref, prev_ref, *, tb):
    @pl.when(pl.program_id(1) == 0)
    def _():
        prev_ref[...] = jnp.zeros_like(prev_ref)

    C = RW_DIM
    x = jnp.concatenate([r_ref[0], k_ref[0], v_ref[0], lo_ref[0]], axis=1)
    prev = prev_ref[...]
    row0 = lax.broadcasted_iota(jnp.int32, (tb, 1), 0) == 0
    shifted = jnp.where(row0, prev[SUBLANES - 1:SUBLANES, :], pltpu.roll(x, 1, 0))
    prev_ref[...] = x[tb - SUBLANES:]
    xm = x + (shifted - x) * mu_ref[...]
    r, k, v, lo = xm[:, :C], xm[:, C:2 * C], xm[:, 2 * C:3 * C], xm[:, 3 * C:]
    xwa = lo[:, :LANES]
    xg = lo[:, LANES:]
    lw = jnp.dot(jnp.tanh(xwa).astype(BF16), wup_ref[...], preferred_element_type=F32)
    la = jnp.dot(xwa.astype(BF16), aup_ref[...], preferred_element_type=F32)
    g = jnp.dot(jax.nn.sigmoid(xg).astype(BF16), gup_ref[...], preferred_element_type=F32)
    w = jnp.exp(-math.exp(-0.5) * jax.nn.sigmoid(w0_ref[...] + lw))
    a = jax.nn.sigmoid(a0_ref[...] + la)
    hmat = _head_hmat()
    kk = k * kk_ref2[...]
    kk = kk * lax.rsqrt(_head_sum_bcast(kk * kk, hmat) + 1e-12)
    k2 = k * (1.0 + (a - 1.0) * ka_ref[...])
    ro_ref[0] = r
    wo_ref[0] = w
    ko_ref[0] = k2
    vo_ref[0] = v
    kko_ref[0] = kk
    kbo_ref[0] = kk * a
    go_ref[0] = g
    bo_ref[0] = _head_sum_bcast(r * k2 * rk_ref[...], hmat) * v


def _rwkv_post_kernel(y_ref, b_ref, g_ref, lnw_ref, lnb_ref, o_ref, *, tb):
    nq = RW_DIM // LANES // 2
    yrows = nq * SUBLANES
    lane = lax.broadcasted_iota(jnp.int32, (tb, LANES), 1)
    parts = []
    for q in range(nq):
        a0, a1, a2, a3 = [y_ref[pl.ds(q * SUBLANES + i, tb, stride=yrows), :] for i in range(4)]
        parts.append(jnp.where(lane < RW_HEAD, a0, pltpu.roll(a1, RW_HEAD, 1)))
        parts.append(jnp.where(lane < RW_HEAD, pltpu.roll(a2, RW_HEAD, 1), a3))
    y = jnp.concatenate(parts, axis=1)
    hmat = _head_hmat()
    mu = _head_sum_bcast(y, hmat) * (1.0 / RW_HEAD)
    yc = y - mu
    var = _head_sum_bcast(yc * yc, hmat) * (1.0 / RW_HEAD)
    yn = yc * lax.rsqrt(var + RW_LN_EPS) * lnw_ref[...] + lnb_ref[...]
    o_ref[0] = (yn + b_ref[0]) * g_ref[0]


def _rwkv_branch(p3, mu, w0, w_up, a0, a_up, g_up, k_k, k_a, r_k, ln_w, ln_b, tb=256):
    B, S, _ = p3.shape
    C = RW_DIM
    tb = min(tb, S)
    row = lambda t: t.reshape(1, -1)
    wup = jnp.concatenate([w_up, jnp.zeros_like(a_up)], axis=0).astype(BF16)
    aup = jnp.concatenate([jnp.zeros_like(w_up), a_up], axis=0).astype(BF16)
    seq = pl.BlockSpec((1, tb, C), lambda b, t: (b, t, 0))
    par = pl.BlockSpec((1, C), lambda b, t: (0, 0))
    r, w, k, v, kk, kb, g, bonus = pl.pallas_call(
        functools.partial(_rwkv_prep_kernel, tb=tb),
        grid=(B, S // tb),
        in_specs=[pl.BlockSpec((1, tb, C), lambda b, t: (b, t, 3)), pl.BlockSpec((1, tb, C), lambda b, t: (b, t, 4)),
                  pl.BlockSpec((1, tb, C), lambda b, t: (b, t, 5)),
                  pl.BlockSpec((1, tb, 2 * LANES), lambda b, t: (b, t, 24)),
                  pl.BlockSpec((1, RW_COLS), lambda b, t: (0, 0)), par,
                  pl.BlockSpec((LANES, C), lambda b, t: (0, 0)), par, pl.BlockSpec((LANES, C), lambda b, t: (0, 0)),
                  pl.BlockSpec((RW_LORA_G, C), lambda b, t: (0, 0)), par, par, par],
        out_specs=[seq] * 8,
        out_shape=[jax.ShapeDtypeStruct((B, S, C), F32)] * 8,
        scratch_shapes=[pltpu.VMEM((SUBLANES, RW_COLS), F32)],
        compiler_params=_cparams(("parallel", "arbitrary")),
        name="rwkv_prep",
    )(p3, p3, p3, p3, row(mu), row(w0), wup, row(a0), aup, g_up.astype(BF16), row(k_k), row(k_a), row(r_k))
    y_raw = _rwkv_scan(r, w, k, v, kk, kb)
    yrows = y_raw.shape[1] // S
    return pl.pallas_call(
        functools.partial(_rwkv_post_kernel, tb=tb),
        grid=(B, S // tb),
        in_specs=[pl.BlockSpec((None, tb * yrows, LANES), lambda b, t: (b, t, 0)), seq, seq, par, par],
        out_specs=seq,
        out_shape=jax.ShapeDtypeStruct((B, S, C), F32),
        compiler_params=_cparams(("parallel", "parallel")),
        name="rwkv_post",
    )(y_raw, bonus, g, row(ln_w), row(ln_b))


def _ab_mixer(x, norm_w, w_in, conv_w, conv_b, gate_b, mu, w0, w_up, a0, a_up, g_up, k_k, k_a, r_k, ln_w, ln_b,
              w_out):
    B, S, D = x.shape
    T = B * S
    x2 = x.reshape(T, D)
    p3 = _matmul(x2, _ab_permute_cols(w_in).astype(BF16), norm_w=norm_w).reshape(B, S, AB_COLS_PAD)
    y_m = _mlstm(p3, conv_w, conv_b, gate_b)
    y_r = _rwkv_branch(p3, mu, w0, w_up, a0, a_up, g_up, k_k, k_a, r_k, ln_w, ln_b)
    hm = ML_HEADS * ML_DV
    part = _matmul(y_m.reshape(T, hm), w_out[:hm].astype(BF16), residual=x2)
    return _matmul(y_r.reshape(T, RW_DIM), w_out[hm:].astype(BF16), residual=part).reshape(B, S, D)


def _ab_mixer_j(x, norm_w, w_in, conv_w, conv_b, gate_b, mu, w0, w_up, a0, a_up, g_up, k_k, k_a, r_k, ln_w, ln_b,
                w_out):
    B, S, D = x.shape
    T = B * S
    x2 = x.reshape(T, D)
    ncol = w_in.shape[1]
    p = _matmul(x2, _pad_cols(w_in, 512).astype(BF16), norm_w=norm_w)[:, :ncol].reshape(B, S, ncol)
    ml_p, rw_p = p[..., :ML_COLS], p[..., ML_COLS:]
    qk, v_m, o_m, gif = _split(ml_p, ML_SPLITS)
    qk = jax.nn.silu(_causal_conv(qk, conv_w, conv_b))
    q_m, k_m = jnp.split(qk, 2, axis=-1)
    gif = gif + gate_b
    h_m = _mlstm_j(q_m.reshape(B, S, ML_HEADS, ML_DK), k_m.reshape(B, S, ML_HEADS, ML_DK),
                   v_m.reshape(B, S, ML_HEADS, ML_DV), gif[..., :ML_HEADS], gif[..., ML_HEADS:])
    y_m = jax.nn.sigmoid(o_m) * h_m.reshape(B, S, ML_HEADS * ML_DV)
    rw_p = rw_p + (_token_shift(rw_p) - rw_p) * mu
    r, k, v, xw, xa, xg = _split(rw_p, RW_SPLITS)
    logw = -math.exp(-0.5) * jax.nn.sigmoid(w0 + jnp.tanh(xw) @ w_up)
    a = jax.nn.sigmoid(a0 + xa @ a_up)
    g = jax.nn.sigmoid(xg) @ g_up

    def heads(t):
        return t.reshape(B, S, RW_HEADS, RW_HEAD)

    kk = heads(k * k_k)
    kk = kk * lax.rsqrt(jnp.sum(kk * kk, axis=-1, keepdims=True) + 1e-12)
    k = k * (1.0 + (a - 1.0) * k_a)
    rh, kh, vh, ah = heads(r), heads(k), heads(v), heads(a)
    y = _rwkv7_scan_j(rh, jnp.exp(heads(logw)), kh, vh, kk, ah)
    y_mu = jnp.mean(y, axis=-1, keepdims=True)
    y_var = jnp.mean(jnp.square(y - y_mu), axis=-1, keepdims=True)
    yn = ((y - y_mu) * lax.rsqrt(y_var + RW_LN_EPS)).reshape(B, S, RW_DIM) * ln_w + ln_b
    bonus = (jnp.sum(rh * kh * r_k.reshape(RW_HEADS, RW_HEAD), axis=-1, keepdims=True) * vh).reshape(B, S, RW_DIM)
    y_r = (yn + bonus) * g
    y_cat = jnp.concatenate([y_m, y_r], axis=-1).reshape(T, D)
    return _matmul(y_cat, w_out.astype(BF16), residual=x2).reshape(B, S, D)


def _nsa(x, norm_w, w_in, gate_b, cmp_pos, cmp_w1, cmp_w2, w_out):
    B, S, D = x.shape
    T = B * S
    x2 = x.reshape(T, D)
    p = _matmul(x2, _pad_cols(w_in, 512).astype(BF16), norm_w=norm_w)
    o = _nsa_attention(p.reshape(B, S, -1), gate_b, cmp_pos, cmp_w1, cmp_w2)
    return _matmul(o.reshape(T, NSA_HEADS * NSA_HD), w_out.astype(BF16), residual=x2).reshape(B, S, D)


def _nsa_j(x, norm_w, w_in, gate_b, cmp_pos, cmp_w1, cmp_w2, w_out):
    B, S, D = x.shape
    T = B * S
    x2 = x.reshape(T, D)
    G, R, HD = NSA_KV, NSA_REP, NSA_HD
    QC = NSA_QCHUNK
    scale = HD ** -0.5
    pos = jnp.arange(S)
    ncol = w_in.shape[1]
    p = _matmul(x2, _pad_cols(w_in, 512).astype(BF16), norm_w=norm_w)[:, :ncol].reshape(B, S, ncol)
    q, kc, vc, ks, vs, kw, vw, gl = _split(p, NSA_SPLITS)
    q = q.reshape(B, S, G, R, HD)

    def kvh(t):
        return t.reshape(B, S, G, HD)

    q_rot = _rope(q, pos)
    ks = _rope(kvh(ks), pos)
    kw = _rope(kvh(kw), pos)
    vs = kvh(vs)
    vw = kvh(vw)
    gates = jax.nn.sigmoid(gl + gate_b).reshape(B, S, 3, G, R)
    n_cmp = (S - CMP_BLOCK) // CMP_STRIDE + 1
    cidx = np.arange(n_cmp)[:, None] * CMP_STRIDE + np.arange(CMP_BLOCK)[None, :]

    def compress(t, pe, w1, w2):
        blocks = t[:, cidx] + pe[None, None, :, None, :]
        return jax.nn.gelu(jnp.einsum('bjpgd,pde->bjge', blocks, w1)) @ w2

    k_cmp = compress(kvh(kc), cmp_pos[0], cmp_w1[0], cmp_w2[0])
    v_cmp = compress(kvh(vc), cmp_pos[1], cmp_w1[1], cmp_w2[1])
    cmp_end = jnp.asarray(cidx[:, -1])
    n_sel = S // SEL_BLOCK
    c0 = np.arange(n_cmp)[:, None] * CMP_STRIDE
    s0 = np.arange(n_sel)[None, :] * SEL_BLOCK
    ov = np.clip(np.minimum(c0 + CMP_BLOCK, s0 + SEL_BLOCK) - np.maximum(c0, s0), 0, None) / CMP_BLOCK
    ov = jnp.asarray(ov, dtype=F32)
    n_top = min(SEL_TOPK, n_sel)
    k_blk = ks.reshape(B, n_sel, SEL_BLOCK, G, HD).transpose(0, 3, 1, 2, 4)
    v_blk = vs.reshape(B, n_sel, SEL_BLOCK, G, HD).transpose(0, 3, 1, 2, 4)
    kw_pad = jnp.pad(kw, ((0, 0), (WINDOW, 0), (0, 0), (0, 0)))
    vw_pad = jnp.pad(vw, ((0, 0), (WINDOW, 0), (0, 0), (0, 0)))
    bi = jnp.arange(B)[:, None, None, None]
    gi = jnp.arange(G)[None, :, None, None]
    blk_ids = jnp.arange(n_sel)

    def chunk(c):
        t0 = c * QC
        qt = t0 + jnp.arange(QC)
        q_c = lax.dynamic_slice_in_dim(q, t0, QC, axis=1)
        qr_c = lax.dynamic_slice_in_dim(q_rot, t0, QC, axis=1)
        g_c = lax.dynamic_slice_in_dim(gates, t0, QC, axis=1)
        p_cmp = _masked_softmax(jnp.einsum('bqgrd,bjgd->bgrqj', q_c, k_cmp) * scale,
                                cmp_end[None, :] <= qt[:, None])
        o_cmp = jnp.einsum('bgrqj,bjgd->bqgrd', p_cmp, v_cmp)
        imp = jnp.einsum('bgrqj,js->bgqs', p_cmp, ov)
        cur = (qt // SEL_BLOCK)[:, None]
        forced = (blk_ids[None, :] == 0) | (blk_ids[None, :] == cur) | (blk_ids[None, :] == cur - 1)
        score = jnp.where(forced, jnp.inf, jnp.where(blk_ids[None, :] <= cur, imp, -jnp.inf))
        top_v, top_i = lax.top_k(score, n_top)
        k_g = k_blk[bi, gi, top_i].reshape(B, G, QC, n_top * SEL_BLOCK, HD)
        v_g = v_blk[bi, gi, top_i].reshape(B, G, QC, n_top * SEL_BLOCK, HD)
        kpos = (top_i[..., None] * SEL_BLOCK + jnp.arange(SEL_BLOCK)).reshape(B, G, QC, n_top * SEL_BLOCK)
        kmask = jnp.repeat(top_v > -jnp.inf, SEL_BLOCK, axis=-1) & (kpos <= qt[None, None, :, None])
        p_slc = _masked_softmax(jnp.einsum('bqgrd,bgqkd->bgrqk', qr_c, k_g) * scale, kmask[:, :, None])
        o_slc = jnp.einsum('bgrqk,bgqkd->bqgrd', p_slc, v_g)
        k_w = lax.dynamic_slice_in_dim(kw_pad, t0, QC + WINDOW, axis=1)
        v_w = lax.dynamic_slice_in_dim(vw_pad, t0, QC + WINDOW, axis=1)
        wpos = t0 - WINDOW + jnp.arange(QC + WINDOW)
        dist = qt[:, None] - wpos[None, :]
        wmask = (dist >= 0) & (dist < WINDOW) & (wpos[None, :] >= 0)
        p_win = _masked_softmax(jnp.einsum('bqgrd,bkgd->bgrqk', qr_c, k_w) * scale, wmask)
        o_win = jnp.einsum('bgrqk,bkgd->bqgrd', p_win, v_w)
        return (g_c[:, :, 0, :, :, None] * o_cmp + g_c[:, :, 1, :, :, None] * o_slc
                + g_c[:, :, 2, :, :, None] * o_win)

    o = lax.map(chunk, jnp.arange(S // QC))
    o = o.transpose(1, 0, 2, 3, 4, 5).reshape(T, NSA_HEADS * HD)
    return _matmul(o, w_out.astype(BF16), residual=x2).reshape(B, S, D)


def _cross_attn_kernel(q_ref, k_ref, v_ref, o_ref):
    for h in range(CA_HEADS):
        cols = slice(h * CA_HD, (h + 1) * CA_HD)
        q = q_ref[:, cols].astype(BF16)
        k = k_ref[:, cols].astype(BF16)
        s = lax.dot_general(q, k, (((1,), (1,)), ((), ())), preferred_element_type=F32) * (CA_HD ** -0.5)
        e = jnp.exp(s - jnp.max(s, axis=-1, keepdims=True))
        p = e * (1.0 / jnp.sum(e, axis=-1, keepdims=True))
        o_ref[:, cols] = jnp.dot(p.astype(BF16), v_ref[:, cols].astype(BF16), preferred_element_type=F32)


def _cross_attn(x, norm_w, mem, norm_mem, wq, wk, wv, wo, tq=512):
    B, S, D = x.shape
    T = B * S
    M = mem.shape[1]
    x2 = x.reshape(T, D)
    q = _matmul(x2, wq.astype(BF16), norm_w=norm_w)
    kv = _matmul(mem.reshape(B * M, D), jnp.concatenate([wk, wv], axis=1).astype(BF16), norm_w=norm_mem)
    nq = S // tq
    o = pl.pallas_call(
        _cross_attn_kernel,
        grid=(B, nq),
        in_specs=[pl.BlockSpec((tq, D), lambda b, i: (b * nq + i, 0)),
                  pl.BlockSpec((M, D), lambda b, i: (b, 0)), pl.BlockSpec((M, D), lambda b, i: (b, 1))],
        out_specs=pl.BlockSpec((tq, D), lambda b, i: (b * nq + i, 0)),
        out_shape=jax.ShapeDtypeStruct((T, D), F32),
        compiler_params=_cparams(("parallel", "parallel")),
        name="cross_attn",
    )(q, kv, kv)
    return _matmul(o, wo.astype(BF16), residual=x2).reshape(B, S, D)


def _cross_attn_j(x, norm_w, mem, norm_mem, wq, wk, wv, wo):
    B, S, D = x.shape
    T = B * S
    M = mem.shape[1]
    x2 = x.reshape(T, D)
    q = _matmul(x2, wq.astype(BF16), norm_w=norm_w).reshape(B, S, CA_HEADS, CA_HD)
    m2 = mem.reshape(B * M, D)
    kv = _matmul(m2, jnp.concatenate([wk, wv], axis=1).astype(BF16), norm_w=norm_mem)
    k = kv[:, :D].reshape(B, M, CA_HEADS, CA_HD)
    v = kv[:, D:].reshape(B, M, CA_HEADS, CA_HD)
    s = jnp.einsum('bshd,bmhd->bhsm', q, k) * (CA_HD ** -0.5)
    p = jax.nn.softmax(s, axis=-1)
    o = jnp.einsum('bhsm,bmhd->bshd', p, v).reshape(T, D)
    return _matmul(o, wo.astype(BF16), residual=x2).reshape(B, S, D)


MOE_TM = 256
MOE_NEG = -1e30


def _moe_route_kernel(x_ref, nw_ref, wr_ref, br_ref, xn_ref, ri_ref, rw_ref, cnt_ref, cnt_scr, *, tm):
    @pl.when(pl.program_id(0) == 0)
    def _():
        cnt_scr[...] = jnp.zeros_like(cnt_scr)

    x = x_ref[...]
    xn = x * lax.rsqrt(jnp.mean(x * x, axis=-1, keepdims=True) + EPS) * nw_ref[...]
    xn_ref[...] = xn
    logits = jnp.dot(xn, wr_ref[...], preferred_element_type=F32, precision=lax.Precision.HIGHEST) + br_ref[...]
    lane = lax.broadcasted_iota(jnp.int32, (tm, LANES), 1)
    gmask = lane < MOE_GROUPS
    lg = jnp.where(gmask, logits, MOE_NEG)
    gmax = jnp.max(lg, axis=-1, keepdims=True)
    grp = jnp.min(jnp.where(lg == gmax, lane, LANES), axis=-1, keepdims=True)
    p_grp = 1.0 / jnp.sum(jnp.where(gmask, jnp.exp(lg - gmax), 0.0), axis=-1, keepdims=True)
    lo = MOE_GROUPS + grp * MOE_PER_GROUP
    le = jnp.where((lane >= lo) & (lane < lo + MOE_PER_GROUP), logits, MOE_NEG)
    t1 = jnp.max(le, axis=-1, keepdims=True)
    i1 = jnp.min(jnp.where(le == t1, lane, LANES), axis=-1, keepdims=True)
    le2 = jnp.where(lane == i1, MOE_NEG, le)
    t2 = jnp.max(le2, axis=-1, keepdims=True)
    i2 = jnp.min(jnp.where(le2 == t2, lane, LANES), axis=-1, keepdims=True)
    e21 = jnp.exp(t2 - t1)
    w1 = p_grp / (1.0 + e21)
    w2 = w1 * e21
    e1 = i1 - MOE_GROUPS
    e2 = i2 - MOE_GROUPS
    oh1 = jnp.where(lane == e1, 1.0, 0.0)
    oh2 = jnp.where(lane == e2, 1.0, 0.0)
    both = oh1 + oh2
    r_i = lax.broadcasted_iota(jnp.int32, (tm, tm), 0)
    c_i = lax.broadcasted_iota(jnp.int32, (tm, tm), 1)
    ltri = jnp.where(c_i < r_i, 1.0, 0.0).astype(BF16)
    before = jnp.dot(ltri, both.astype(BF16), preferred_element_type=F32) + cnt_scr[...]
    pos1 = jnp.sum(oh1 * before, axis=-1, keepdims=True).astype(jnp.int32)
    pos2 = jnp.sum(oh2 * before, axis=-1, keepdims=True).astype(jnp.int32)
    cnt_scr[...] = cnt_scr[...] + jnp.sum(both, axis=0, keepdims=True)
    ri_ref[...] = jnp.where(lane == 0, e1, jnp.where(lane == 1, e2, jnp.where(lane == 2, pos1,
                            jnp.where(lane == 3, pos2, 0))))
    rw_ref[...] = jnp.where(lane == 0, w1, jnp.where(lane == 1, w2, 0.0))
    cnt_ref[...] = cnt_scr[...]


def _moe_dest(tok, e1_ref, e2_ref, p1_ref, p2_ref, off_ref):
    return off_ref[e1_ref[tok]] + p1_ref[tok], off_ref[e2_ref[tok]] + p2_ref[tok]


def _moe_rowmap_kernel(e1_ref, e2_ref, p1_ref, p2_ref, off_ref, cnt_ref, rt_ref, *, n_tok, rows):
    def clear(i, carry):
        rt_ref[i] = 0
        return carry

    def clear_pad(e, carry):
        used = off_ref[e] + cnt_ref[e]
        end = jnp.where(e + 1 < MOE_EXPERTS, off_ref[jnp.minimum(e + 1, MOE_EXPERTS - 1)], rows)
        lax.fori_loop(used, end, clear, 0)
        return carry

    def place(t, carry):
        d1, d2 = _moe_dest(t, e1_ref, e2_ref, p1_ref, p2_ref, off_ref)
        rt_ref[d1] = t
        rt_ref[d2] = t
        return carry

    lax.fori_loop(0, MOE_EXPERTS, clear_pad, 0)
    lax.fori_loop(0, n_tok, place, 0, unroll=8)


def _moe_expert_kernel(te_ref, nu_ref, rt_ref, xn_hbm, wg_ref, wu_ref, wd_ref, y_ref, xbuf, sems):
    del te_ref
    i = pl.program_id(0)
    nu = nu_ref[0]
    slot = i % 2

    def row_copy(tile, r, s):
        return pltpu.make_async_copy(xn_hbm.at[pl.ds(rt_ref[tile * MOE_TM + r], 1)], xbuf.at[s, pl.ds(r, 1)],
                                     sems.at[s])

    def drain(s):
        def body(r, carry):
            row_copy(0, 0, s).wait()
            return carry
        lax.fori_loop(0, MOE_TM, body, 0, unroll=8)

    @pl.when(i == 0)
    def _():
        def body(r, carry):
            row_copy(0, r, 0).start()
            return carry
        lax.fori_loop(0, MOE_TM, body, 0, unroll=8)

    @pl.when(i < nu)
    def _():
        drain(slot)
        nxt = jnp.minimum(i + 1, nu - 1)
        for r in range(MOE_TM):
            row_copy(nxt, r, 1 - slot).start()
        x = xbuf[slot].astype(BF16)
        gate = jnp.dot(x, wg_ref[...].astype(BF16), preferred_element_type=F32)
        up = jnp.dot(x, wu_ref[...].astype(BF16), preferred_element_type=F32)
        hid = gate * jax.nn.sigmoid(gate) * up
        y_ref[...] = jnp.dot(hid.astype(BF16), wd_ref[...].astype(BF16), preferred_element_type=F32)

        @pl.when(i == nu - 1)
        def _():
            drain(1 - slot)

    @pl.when(i >= nu)
    def _():
        y_ref[...] = jnp.zeros_like(y_ref)


def _moe_combine_kernel(e1_ref, e2_ref, p1_ref, p2_ref, off_ref, x_ref, rw_ref, ys_hbm, o_ref, buf1, buf2, sems, *, tm):
    i = pl.program_id(0)
    slot = i % 2

    def row_copy(src, t, buf, s):
        return pltpu.make_async_copy(ys_hbm.at[pl.ds(src, 1)], buf.at[s, pl.ds(t, 1)], sems.at[s])

    def issue_tile(tile, s):
        def body(t, carry):
            d1, d2 = _moe_dest(tile * tm + t, e1_ref, e2_ref, p1_ref, p2_ref, off_ref)
            row_copy(d1, t, buf1, s).start()
            row_copy(d2, t, buf2, s).start()
            return carry
        lax.fori_loop(0, tm, body, 0, unroll=4)

    def drain(t, carry):
        row_copy(0, 0, buf1, slot).wait()
        row_copy(0, 0, buf2, slot).wait()
        return carry

    @pl.when(i == 0)
    def _():
        issue_tile(0, 0)

    lax.fori_loop(0, tm, drain, 0, unroll=8)

    @pl.when(i + 1 < pl.num_programs(0))
    def _():
        issue_tile(i + 1, 1 - slot)

    w = rw_ref[...]
    o_ref[...] = x_ref[...] + w[:, 0:1] * buf1[slot] + w[:, 1:2] * buf2[slot]


def _hier_moe(x, norm_w, wg, bg, we, be, w_gate, w_up, w_down, layer):
    B, S, D = x.shape
    T = B * S
    x2 = x.reshape(T, D)
    FF = w_gate.shape[-1]
    tm_r = 512
    wr = jnp.pad(jnp.concatenate([wg, we], axis=1), ((0, 0), (0, LANES - MOE_GROUPS - MOE_EXPERTS)))
    br = jnp.pad(jnp.concatenate([bg, be]), (0, LANES - MOE_GROUPS - MOE_EXPERTS)).reshape(1, LANES)
    xn, ri, rw, cnt = pl.pallas_call(
        functools.partial(_moe_route_kernel, tm=tm_r),
        grid=(T // tm_r,),
        in_specs=[pl.BlockSpec((tm_r, D), lambda i: (i, 0)), pl.BlockSpec((1, D), lambda i: (0, 0)),
                  pl.BlockSpec((D, LANES), lambda i: (0, 0)), pl.BlockSpec((1, LANES), lambda i: (0, 0))],
        out_specs=[pl.BlockSpec((tm_r, D), lambda i: (i, 0)), pl.BlockSpec((tm_r, LANES), lambda i: (i, 0)),
                   pl.BlockSpec((tm_r, LANES), lambda i: (i, 0)), pl.BlockSpec((1, LANES), lambda i: (0, 0))],
        out_shape=[jax.ShapeDtypeStruct((T, D), F32), jax.ShapeDtypeStruct((T, LANES), jnp.int32),
                   jax.ShapeDtypeStruct((T, LANES), F32), jax.ShapeDtypeStruct((1, LANES), F32)],
        scratch_shapes=[pltpu.VMEM((1, LANES), F32)],
        compiler_params=_cparams(("arbitrary",)),
        name="moe_route",
    )(x2, norm_w.reshape(1, D), wr, br)

    counts = cnt[0, :MOE_EXPERTS].astype(jnp.int32)
    padded = (counts + MOE_TM - 1) // MOE_TM * MOE_TM
    ends = jnp.cumsum(padded)
    off = (ends - padded).astype(jnp.int32)
    n_tiles = (T * MOE_TOPK) // MOE_TM + MOE_EXPERTS
    rows = n_tiles * MOE_TM
    n_used = (ends[-1] // MOE_TM).astype(jnp.int32).reshape(1)
    tile_e = jnp.minimum(jnp.searchsorted(ends, jnp.arange(n_tiles, dtype=jnp.int32) * MOE_TM, side='right'),
                         MOE_EXPERTS - 1).astype(jnp.int32)
    e1, e2, p1, p2 = ri[:, 0], ri[:, 1], ri[:, 2], ri[:, 3]

    row_tok = pl.pallas_call(
        functools.partial(_moe_rowmap_kernel, n_tok=T, rows=rows),
        grid_spec=pltpu.PrefetchScalarGridSpec(
            num_scalar_prefetch=6, grid=(1,), in_specs=[],
            out_specs=pl.BlockSpec(memory_space=pltpu.SMEM)),
        out_shape=jax.ShapeDtypeStruct((rows,), jnp.int32),
        compiler_params=_cparams(("arbitrary",)),
        name="moe_rowmap",
    )(e1, e2, p1, p2, off, counts)

    def w_ix(i, te, nu, rt):
        return (layer, te[jnp.minimum(i, nu[0] - 1)], 0, 0)

    ys = pl.pallas_call(
        _moe_expert_kernel,
        grid_spec=pltpu.PrefetchScalarGridSpec(
            num_scalar_prefetch=3, grid=(n_tiles,),
            in_specs=[pl.BlockSpec(memory_space=pl.ANY),
                      pl.BlockSpec((None, None, D, FF), w_ix), pl.BlockSpec((None, None, D, FF), w_ix),
                      pl.BlockSpec((None, None, FF, D), w_ix)],
            out_specs=pl.BlockSpec((MOE_TM, D), lambda i, te, nu, rt: (i, 0)),
            scratch_shapes=[pltpu.VMEM((2, MOE_TM, D), F32), pltpu.SemaphoreType.DMA((2,))]),
        out_shape=jax.ShapeDtypeStruct((rows, D), F32),
        compiler_params=_cparams(("arbitrary",)),
        name="moe_experts",
    )(tile_e, n_used, row_tok, xn, w_gate, w_up, w_down)

    tm_c = 256
    out = pl.pallas_call(
        functools.partial(_moe_combine_kernel, tm=tm_c),
        grid_spec=pltpu.PrefetchScalarGridSpec(
            num_scalar_prefetch=5, grid=(T // tm_c,),
            in_specs=[pl.BlockSpec((tm_c, D), lambda i, *_: (i, 0)), pl.BlockSpec((tm_c, LANES), lambda i, *_: (i, 0)),
                      pl.BlockSpec(memory_space=pl.ANY)],
            out_specs=pl.BlockSpec((tm_c, D), lambda i, *_: (i, 0)),
            scratch_shapes=[pltpu.VMEM((2, tm_c, D), F32), pltpu.VMEM((2, tm_c, D), F32),
                            pltpu.SemaphoreType.DMA((2,))]),
        out_shape=jax.ShapeDtypeStruct((T, D), F32),
        compiler_params=_cparams(("arbitrary",)),
        name="moe_combine",
    )(e1, e2, p1, p2, off, x2, rw, ys)
    return out.reshape(B, S, D)


def _hier_moe_j(x, norm_w, wg, bg, we, be, w_gate, w_up, w_down):
    B, S, D = x.shape
    h = _rmsnorm_j(x, norm_w)
    T = B * S
    A = T * MOE_TOPK
    xt = h.reshape(T, D)
    lg = xt @ wg + bg
    grp = jnp.argmax(lg, axis=-1)
    p_grp = jnp.take_along_axis(jax.nn.softmax(lg, axis=-1), grp[:, None], axis=-1)
    le = (xt @ we + be).reshape(T, MOE_GROUPS, MOE_PER_GROUP)
    le = jnp.take_along_axis(le, grp[:, None, None], axis=1)[:, 0]
    top_l, top_e = lax.top_k(le, MOE_TOPK)
    wts = (p_grp * jax.nn.softmax(top_l, axis=-1)).reshape(A)
    eid = (grp[:, None] * MOE_PER_GROUP + top_e).reshape(A)
    tok = jnp.repeat(jnp.arange(T, dtype=jnp.int32), MOE_TOPK)
    order = jnp.argsort(eid)
    e_s, tok_s, w_s = eid[order], tok[order], wts[order]
    counts = jax.ops.segment_sum(jnp.ones((A,), jnp.int32), eid, num_segments=MOE_EXPERTS)
    padded = (counts + MOE_ROWS - 1) // MOE_ROWS * MOE_ROWS
    ends = jnp.cumsum(padded)
    dest = (ends - padded)[e_s] + jnp.arange(A, dtype=jnp.int32) - (jnp.cumsum(counts) - counts)[e_s]
    n_chunks = -(-A // MOE_ROWS) + MOE_EXPERTS
    rows = n_chunks * MOE_ROWS
    row_tok = jnp.full((rows,), T, jnp.int32).at[dest].set(tok_s)
    row_w = jnp.zeros((rows,), F32).at[dest].set(w_s)
    chunk_e = jnp.minimum(jnp.searchsorted(ends, jnp.arange(n_chunks, dtype=jnp.int32) * MOE_ROWS, side='right'),
                          MOE_EXPERTS - 1)
    x_pad = jnp.concatenate([xt, jnp.zeros((1, D), xt.dtype)], axis=0)

    def expert_rows(args):
        t_idx, w_r, e = args
        xr = x_pad[t_idx]
        hid = jax.nn.silu(xr @ w_gate[e]) * (xr @ w_up[e])
        y = hid @ w_down[e]
        return y * w_r[:, None]

    out = lax.map(expert_rows, (row_tok.reshape(n_chunks, MOE_ROWS), row_w.reshape(n_chunks, MOE_ROWS), chunk_e))
    y = jax.ops.segment_sum(out.reshape(rows, D), row_tok, num_segments=T + 1)[:T]
    return x + y.reshape(B, S, D)


def kernel(x, mem, norm_mix, norm_cross, norm_mem, norm_ffn, norm_final,
           ab_w_in, ml_conv_w, ml_conv_b, ml_gate_b, rw_mu, rw_w0, rw_w_up, rw_a0, rw_a_up, rw_g_up,
           rw_k_k, rw_k_a, rw_r_k, rw_ln_w, rw_ln_b, ab_w_out,
           nsa_w_in, nsa_gate_b, cmp_pos, cmp_w1, cmp_w2, nsa_w_out,
           ca_wq, ca_wk, ca_wv, ca_wo,
           moe_wg, moe_bg, moe_we, moe_be, moe_w_gate, moe_w_up, moe_w_down):
    B, S, D = x.shape
    for l in range(DEPTH):
        j = l // 2
        if l % 2 == 0:
            x = _ab_mixer(x, norm_mix[l], ab_w_in[j], ml_conv_w[j], ml_conv_b[j], ml_gate_b[j], rw_mu[j], rw_w0[j],
                          rw_w_up[j], rw_a0[j], rw_a_up[j], rw_g_up[j], rw_k_k[j], rw_k_a[j], rw_r_k[j],
                          rw_ln_w[j], rw_ln_b[j], ab_w_out[j])
        else:
            x = _nsa(x, norm_mix[l], nsa_w_in[j], nsa_gate_b[j], cmp_pos[j], cmp_w1[j], cmp_w2[j], nsa_w_out[j])
        x = _cross_attn(x, norm_cross[l], mem, norm_mem[l], ca_wq[l], ca_wk[l], ca_wv[l], ca_wo[l])
        x = _hier_moe(x, norm_ffn[l], moe_wg[l], moe_bg[l], moe_we[l], moe_be[l],
                      moe_w_gate, moe_w_up, moe_w_down, l)
    return _rmsnorm_rows(x.reshape(B * S, D), norm_final).reshape(B, S, D)
```

```python
import functools
import math

import jax
import jax.numpy as jnp
import numpy as np
from jax import lax
from jax.experimental import pallas as pl
from jax.experimental.pallas import tpu as pltpu

F32 = jnp.float32
BF16 = jnp.bfloat16

D_MODEL = 2048
DEPTH = 2
EPS = 1e-6
ROPE_THETA = 500000.0

ML_HEADS = 4
ML_DV = D_MODEL // 2 // ML_HEADS
ML_DK = ML_DV // 2
ML_CHUNK = 64
ML_CONV = 4
RW_HEAD = 64
RW_HEADS = D_MODEL // 2 // RW_HEAD
RW_DIM = RW_HEADS * RW_HEAD
RW_LORA_W = 64
RW_LORA_A = 64
RW_LORA_G = 128
RW_LN_EPS = 64e-5
ML_SPLITS = (2 * ML_HEADS * ML_DK, ML_HEADS * ML_DV, ML_HEADS * ML_DV, 2 * ML_HEADS)
RW_SPLITS = (RW_DIM, RW_DIM, RW_DIM, RW_LORA_W, RW_LORA_A, RW_LORA_G)
ML_COLS = sum(ML_SPLITS)
RW_COLS = sum(RW_SPLITS)

NSA_HEADS = 16
NSA_KV = 4
NSA_REP = NSA_HEADS // NSA_KV
NSA_HD = D_MODEL // NSA_HEADS
ROPE_DIM = NSA_HD // 4
CMP_BLOCK = 32
CMP_STRIDE = 16
SEL_BLOCK = 64
SEL_TOPK = 16
WINDOW = 512
NSA_QCHUNK = 32
NSA_SPLITS = (NSA_HEADS * NSA_HD,) + (NSA_KV * NSA_HD,) * 6 + (3 * NSA_HEADS,)

CA_HEADS = 4
CA_HD = D_MODEL // CA_HEADS

MOE_GROUPS = 8
MOE_PER_GROUP = 8
MOE_EXPERTS = MOE_GROUPS * MOE_PER_GROUP
MOE_TOPK = 2
MOE_FF = D_MODEL // 4
MOE_ROWS = 128

V7X_VMEM_BYTES = 64 * 1024 * 1024
VMEM_LIMIT = 48 * 1024 * 1024
LANES = 128
SUBLANES = 8
NT_DIMS = (((1,), (1,)), ((), ()))


def _cparams(sem):
    return pltpu.CompilerParams(dimension_semantics=sem, vmem_limit_bytes=VMEM_LIMIT)


def _mm_kernel(*refs, has_norm, has_res):
    it = iter(refs)
    x_ref = next(it)
    w_ref = next(it)
    nw_ref = next(it) if has_norm else None
    r_ref = next(it) if has_res else None
    o_ref = next(it)
    xs_ref = next(it)

    @pl.when(pl.program_id(1) == 0)
    def _():
        x = x_ref[...]
        if has_norm:
            ms = jnp.mean(x * x, axis=-1, keepdims=True)
            x = x * lax.rsqrt(ms + EPS) * nw_ref[...]
        xs_ref[...] = x.astype(BF16)

    acc = jnp.dot(xs_ref[...], w_ref[...], preferred_element_type=F32)
    if has_res:
        acc = acc + r_ref[...]
    o_ref[...] = acc


def _matmul(x, w_bf16, *, norm_w=None, residual=None, tm=1024, tn=512):
    M, K = x.shape
    N = w_bf16.shape[1]
    tm = min(tm, M)
    tn = min(tn, N)
    assert M % tm == 0 and N % tn == 0, (M, N, tm, tn)
    has_norm = norm_w is not None
    has_res = residual is not None
    in_specs = [pl.BlockSpec((tm, K), lambda i, j: (i, 0)),
                pl.BlockSpec((K, tn), lambda i, j: (0, j))]
    args = [x, w_bf16]
    if has_norm:
        in_specs.append(pl.BlockSpec((1, K), lambda i, j: (0, 0)))
        args.append(norm_w.reshape(1, K))
    if has_res:
        in_specs.append(pl.BlockSpec((tm, tn), lambda i, j: (i, j)))
        args.append(residual)
    return pl.pallas_call(
        functools.partial(_mm_kernel, has_norm=has_norm, has_res=has_res),
        grid=(M // tm, N // tn),
        in_specs=in_specs,
        out_specs=pl.BlockSpec((tm, tn), lambda i, j: (i, j)),
        out_shape=jax.ShapeDtypeStruct((M, N), F32),
        scratch_shapes=[pltpu.VMEM((tm, K), BF16)],
        compiler_params=_cparams(("parallel", "arbitrary")),
        name="mm_norm" if has_norm else "mm",
    )(*args)


def _pad_cols(w, mult):
    n = w.shape[1]
    pad = (-n) % mult
    if pad:
        w = jnp.pad(w, ((0, 0), (0, pad)))
    return w


def _rmsnorm_kernel(x_ref, w_ref, o_ref):
    x = x_ref[...]
    ms = jnp.mean(x * x, axis=-1, keepdims=True)
    o_ref[...] = x * lax.rsqrt(ms + EPS) * w_ref[...]


def _rmsnorm_rows(x, w, tm=512):
    M, K = x.shape
    return pl.pallas_call(
        _rmsnorm_kernel,
        grid=(M // tm,),
        in_specs=[pl.BlockSpec((tm, K), lambda i: (i, 0)), pl.BlockSpec((1, K), lambda i: (0, 0))],
        out_specs=pl.BlockSpec((tm, K), lambda i: (i, 0)),
        out_shape=jax.ShapeDtypeStruct((M, K), F32),
        compiler_params=_cparams(("parallel",)),
        name="rmsnorm",
    )(x, w.reshape(1, K))


def _rwkv_scan_kernel(r_ref, w_ref, k_ref, v_ref, kk_ref, kb_ref, y_ref, s_ref, *, tb, nb, npairs):
    @pl.when(pl.program_id(1) == 0)
    def _():
        s_ref[...] = jnp.zeros_like(s_ref)

    row = lax.broadcasted_iota(jnp.int32, (LANES, LANES), 0)
    col = lax.broadcasted_iota(jnp.int32, (LANES, LANES), 1)
    hmat = jnp.where((row // RW_HEAD) == (col // RW_HEAD), 1.0, 0.0).astype(BF16)
    vrow = lax.broadcasted_iota(jnp.int32, (RW_HEAD, LANES), 0)
    vcol = lax.broadcasted_iota(jnp.int32, (RW_HEAD, LANES), 1)
    diag = jnp.where((vcol % RW_HEAD) == vrow, 1.0, 0.0).astype(BF16)
    row16 = lax.broadcasted_iota(jnp.int32, (2 * SUBLANES, LANES), 0)
    lane16 = lax.broadcasted_iota(jnp.int32, (2 * SUBLANES, LANES), 1)
    head16 = (row16 // SUBLANES) == (lane16 // RW_HEAD)
    row8 = lax.broadcasted_iota(jnp.int32, (SUBLANES, LANES), 0)
    lane8 = lax.broadcasted_iota(jnp.int32, (SUBLANES, LANES), 1)
    nq = npairs // 2
    lns = [pl.ds(p * LANES, LANES) for p in range(npairs)]

    def group_sum(parts):
        lhs = jnp.concatenate([q.astype(BF16) for q in parts], axis=0)
        out = jnp.dot(lhs, hmat, preferred_element_type=F32)
        return [out[i * RW_HEAD:(i + 1) * RW_HEAD] for i in range(len(parts))]

    def step(t8, carry):
        rows = pl.ds(pl.multiple_of(t8 * SUBLANES, SUBLANES), SUBLANES)

        def tiles(ref):
            return [[ref[b, rows, ln] for ln in lns] for b in range(nb)]

        kk8, w8, kb8, k8, v8, r8 = (tiles(kk_ref), tiles(w_ref), tiles(kb_ref), tiles(k_ref), tiles(v_ref),
                                    tiles(r_ref))
        s = [[s_ref[b, p] for p in range(npairs)] for b in range(nb)]
        vcols = [[lax.dot_general(diag, jnp.where(head16, jnp.concatenate([v8[b][p]] * 2, axis=0), 0.0).astype(BF16),
                                  NT_DIMS, preferred_element_type=F32).astype(BF16)
                  for p in range(npairs)] for b in range(nb)]
        vk = [[None] * npairs for _ in range(nb)]
        for b in range(nb):
            for p in range(npairs):
                ksel = jnp.where(head16, jnp.concatenate([k8[b][p]] * 2, axis=0), 0.0)
                rhs = jnp.concatenate([jnp.where((row16 % SUBLANES) == j, ksel, 0.0) for j in range(SUBLANES)],
                                      axis=1).astype(BF16)
                vk[b][p] = jnp.dot(vcols[b][p], rhs, preferred_element_type=F32)
        for j in range(SUBLANES):
            sl = slice(j, j + 1)
            for b in range(nb):
                sk = group_sum([s[b][i] * kk8[b][i][sl] for i in range(npairs)])
                s[b] = [s[b][i] * w8[b][i][sl] - sk[i] * kb8[b][i][sl] + vk[b][i][:, j * LANES:(j + 1) * LANES]
                        for i in range(npairs)]
            for b in range(nb):
                for q in range(nq):
                    p0, p1 = 2 * q, 2 * q + 1
                    a = jnp.where((row8 == 0) & (lane8 < RW_HEAD), r8[b][p0][sl],
                        jnp.where((row8 == 1) & (lane8 >= RW_HEAD), r8[b][p0][sl],
                        jnp.where((row8 == 2) & (lane8 < RW_HEAD), r8[b][p1][sl],
                        jnp.where((row8 == 3) & (lane8 >= RW_HEAD), r8[b][p1][sl], 0.0))))
                    st = jnp.concatenate([s[b][p0], s[b][p1]], axis=0).astype(BF16)
                    yq = lax.dot_general(a.astype(BF16), st, NT_DIMS, preferred_element_type=F32)
                    base = ((t8 * SUBLANES + j) * nq + q) * SUBLANES
                    y_ref[b, pl.ds(pl.multiple_of(base, SUBLANES), SUBLANES), :] = yq
        for b in range(nb):
            for p in range(npairs):
                s_ref[b, p] = s[b][p]
        return carry

    lax.fori_loop(0, tb // SUBLANES, step, 0)


def _rwkv_scan(r, w, k, v, kk, kb, tb=64, nb=4):
    B, S, C = r.shape
    npairs = C // LANES
    tb = min(tb, S)
    nb = min(nb, B)
    spec = pl.BlockSpec((nb, tb, C), lambda b, t: (b, t, 0))
    yrows = (npairs // 2) * SUBLANES
    return pl.pallas_call(
        functools.partial(_rwkv_scan_kernel, tb=tb, nb=nb, npairs=npairs),
        grid=(B // nb, S // tb),
        in_specs=[spec] * 6,
        out_specs=pl.BlockSpec((nb, tb * yrows, LANES), lambda b, t: (b, t, 0)),
        out_shape=jax.ShapeDtypeStruct((B, S * yrows, LANES), F32),
        scratch_shapes=[pltpu.VMEM((nb, npairs, RW_HEAD, LANES), F32)],
        compiler_params=_cparams(("parallel", "arbitrary")),
        name="rwkv_scan",
    )(r, w, k, v, kk, kb)


NSA_NEG = -1e30
NSA_FORCED = 1e30
NSA_REMOVED = -3e30
NSA_COL_Q, NSA_COL_KC, NSA_COL_VC, NSA_COL_KS, NSA_COL_VS, NSA_COL_KW, NSA_COL_VW, NSA_COL_GL = (
    0, 16, 20, 24, 28, 32, 36, 40)


def _rope_tables(S):
    half = ROPE_DIM // 2
    inv = 1.0 / (ROPE_THETA ** (jnp.arange(half, dtype=F32) / half))
    ang = jnp.arange(S, dtype=F32)[:, None] * inv[None, :]
    cos, sin = jnp.cos(ang), jnp.sin(ang)
    one = jnp.ones((S, NSA_HD - ROPE_DIM), F32)
    zero = jnp.zeros((S, NSA_HD - ROPE_DIM), F32)
    zh = jnp.zeros((S, half), F32)
    return (jnp.concatenate([cos, cos, one], axis=1), jnp.concatenate([zh, sin, zero], axis=1),
            jnp.concatenate([-sin, zh, zero], axis=1))


def _rope_rows(x, c, s1, s2):
    half = ROPE_DIM // 2
    return x * c + pltpu.roll(x, half, 1) * s1 + pltpu.roll(x, NSA_HD - half, 1) * s2


def _nsa_prep_q_kernel(x_ref, c_ref, s1_ref, s2_ref, q_ref, qr_ref):
    c, s1, s2 = c_ref[...], s1_ref[...], s2_ref[...]
    for h in range(NSA_HEADS):
        cols = slice(h * NSA_HD, (h + 1) * NSA_HD)
        x = x_ref[0, :, cols] * (NSA_HD ** -0.5)
        q_ref[0, :, cols] = x.astype(BF16)
        qr_ref[0, :, cols] = _rope_rows(x, c, s1, s2).astype(BF16)


def _nsa_prep_kv_kernel(ks_ref, kw_ref, vs_ref, vw_ref, c_ref, s1_ref, s2_ref, kso_ref, kwo_ref, vst_ref, vwt_ref):
    c, s1, s2 = c_ref[...], s1_ref[...], s2_ref[...]
    kso_ref[0, 0] = _rope_rows(ks_ref[0], c, s1, s2).astype(BF16)
    kwo_ref[0, 0] = _rope_rows(kw_ref[0], c, s1, s2).astype(BF16)
    vst_ref[0, 0] = vs_ref[0].T.astype(BF16)
    vwt_ref[0, 0] = vw_ref[0].T.astype(BF16)


def _nsa_compress_kernel(kc_ref, vc_ref, pe_ref, w1_ref, w2_ref, kcmp_ref, vcmpt_ref, *, ncp):
    for which, x_ref in enumerate((kc_ref, vc_ref)):
        za = jnp.zeros((ncp, NSA_HD), F32)
        zb = jnp.zeros((ncp, NSA_HD), F32)
        for p in range(CMP_STRIDE):
            xp = x_ref[pl.ds(p, ncp, stride=CMP_STRIDE), :]
            za = za + jnp.dot((xp + pe_ref[which, p:p + 1, :]).astype(BF16), w1_ref[which, p],
                              preferred_element_type=F32)
            zb = zb + jnp.dot((xp + pe_ref[which, CMP_STRIDE + p:CMP_STRIDE + p + 1, :]).astype(BF16),
                              w1_ref[which, CMP_STRIDE + p], preferred_element_type=F32)
        pre = za + pltpu.roll(zb, ncp - 1, 0)
        out = jnp.dot(jax.nn.gelu(pre).astype(BF16), w2_ref[which], preferred_element_type=F32)
        if which == 0:
            kcmp_ref[0, 0] = out.astype(BF16)
        else:
            vcmpt_ref[0, 0] = out.T.astype(BF16)


def _nsa_attn_kernel(qt_ref, qrt_ref, kcmp_ref, vcmpt_ref, ks_ref, kw_ref, vst_ref, vwt_ref, gl_ref, gb_ref, ovt_ref,
                     o_ref, sel_ref, ms_ref, ls_ref, accs_ref, *, tq, kt, ncp, nsel):
    R = NSA_REP
    g = pl.program_id(1)
    qi = pl.program_id(2)
    t0 = qi * tq
    q_rows = jnp.concatenate([qt_ref[0, :, r * NSA_HD:(r + 1) * NSA_HD] for r in range(R)], axis=0)
    qr_rows = jnp.concatenate([qrt_ref[0, :, r * NSA_HD:(r + 1) * NSA_HD] for r in range(R)], axis=0)
    qpos = t0 + lax.broadcasted_iota(jnp.int32, (1, tq), 1)

    def per_head(fn, a):
        return jnp.concatenate([fn(a[:, r * tq:(r + 1) * tq]) for r in range(R)], axis=1)

    s = lax.dot_general(kcmp_ref[0, 0], q_rows, NT_DIMS, preferred_element_type=F32)
    cend = lax.broadcasted_iota(jnp.int32, (ncp, 1), 0) * CMP_STRIDE + (CMP_BLOCK - 1)
    vis = cend <= qpos
    s = per_head(lambda a: jnp.where(vis, a, NSA_NEG), s)
    m = jnp.max(s, axis=0, keepdims=True)
    e = per_head(lambda a: jnp.where(vis, a, 0.0), jnp.exp(s - m))
    d = jnp.sum(e, axis=0, keepdims=True)
    p = e * (1.0 / jnp.where(d > 0, d, 1.0))
    ocmp_t = jnp.dot(vcmpt_ref[0, 0], p.astype(BF16), preferred_element_type=F32)

    psum = p[:, 0:tq]
    for r in range(1, R):
        psum = psum + p[:, r * tq:(r + 1) * tq]
    imp = jnp.dot(ovt_ref[...], psum, preferred_element_type=F32, precision=lax.Precision.HIGHEST)
    sidx = lax.broadcasted_iota(jnp.int32, (nsel, tq), 0)
    cur = qpos // SEL_BLOCK
    forced = (sidx == 0) | (sidx == cur) | (sidx == cur - 1)
    score = jnp.where(forced, NSA_FORCED, jnp.where(sidx <= cur, imp, NSA_NEG))
    sel = jnp.zeros((nsel, tq), F32)
    for _ in range(SEL_TOPK):
        mx = jnp.max(score, axis=0, keepdims=True)
        idx = jnp.min(jnp.where(score == mx, sidx, nsel), axis=0, keepdims=True)
        pick = (sidx == idx) & (mx > 0.5 * NSA_NEG)
        sel = jnp.where(pick, 1.0, sel)
        score = jnp.where(pick, NSA_REMOVED, score)
    sel_ref[...] = sel

    def online_update(s, valid, v_t, m_ref, l_ref, acc_ref):
        s = per_head(lambda a: jnp.where(valid, a, NSA_NEG), s)
        m_old = m_ref[...]
        m_new = jnp.maximum(m_old, jnp.max(s, axis=0, keepdims=True))
        alpha = jnp.exp(m_old - m_new)
        pexp = jnp.exp(s - m_new)
        l_ref[...] = alpha * l_ref[...] + jnp.sum(pexp, axis=0, keepdims=True)
        acc_ref[...] = alpha * acc_ref[...] + jnp.dot(v_t, pexp.astype(BF16), preferred_element_type=F32)
        m_ref[...] = m_new

    ms_ref[...] = jnp.full(ms_ref.shape, NSA_NEG, F32)
    ls_ref[...] = jnp.zeros(ls_ref.shape, F32)
    accs_ref[...] = jnp.zeros(accs_ref.shape, F32)

    nblk = kt // SEL_BLOCK

    def sel_body(c, carry):
        k0 = pl.multiple_of(c * kt, kt)
        s = lax.dot_general(ks_ref[0, 0, pl.ds(k0, kt), :], qr_rows, NT_DIMS, preferred_element_type=F32)
        selrows = sel_ref[pl.ds(pl.multiple_of(c * nblk, nblk), nblk), :]
        selexp = jnp.concatenate(
            [jnp.broadcast_to(selrows[b:b + 1], (SEL_BLOCK, tq)) for b in range(nblk)], axis=0)
        kpos = k0 + lax.broadcasted_iota(jnp.int32, (kt, 1), 0)
        valid = (selexp > 0.5) & (kpos <= qpos)
        online_update(s, valid, vst_ref[0, 0, :, pl.ds(k0, kt)], ms_ref, ls_ref, accs_ref)
        return carry

    lax.fori_loop(0, t0 // kt + 1, sel_body, 0)


    nwk = WINDOW + tq
    k0 = pl.multiple_of(jnp.maximum(t0 - WINDOW, 0), tq)
    s = lax.dot_general(kw_ref[0, 0, pl.ds(k0, nwk), :], qr_rows, NT_DIMS, preferred_element_type=F32)
    dist = qpos - (k0 + lax.broadcasted_iota(jnp.int32, (nwk, 1), 0))
    inwin = (dist >= 0) & (dist < WINDOW)
    s = per_head(lambda a: jnp.where(inwin, a, NSA_NEG), s)
    ew = jnp.exp(s - jnp.max(s, axis=0, keepdims=True))
    owin_t = (jnp.dot(vwt_ref[0, 0, :, pl.ds(k0, nwk)], ew.astype(BF16), preferred_element_type=F32)
              * (1.0 / jnp.sum(ew, axis=0, keepdims=True)))

    oslc_t = accs_ref[...] * (1.0 / ls_ref[...])

    gates_t = jax.nn.sigmoid(gl_ref[0] + gb_ref[...]).T
    rid = lax.broadcasted_iota(jnp.int32, (LANES, 1), 0)

    def gate_row(which, r):
        return jnp.sum(jnp.where(rid == which * NSA_HEADS + g * R + r, gates_t, 0.0), axis=0, keepdims=True)

    for r in range(R):
        cols = slice(r * tq, (r + 1) * tq)
        o_t = (gate_row(0, r) * ocmp_t[:, cols] + gate_row(1, r) * oslc_t[:, cols]
               + gate_row(2, r) * owin_t[:, cols])
        o_ref[0, :, r * NSA_HD:(r + 1) * NSA_HD] = o_t.T


def _nsa_attention(p3, gate_b, cmp_pos, cmp_w1, cmp_w2, *, tq=128, kt=512, tp=512):
    B, S, _ = p3.shape
    G, R, HD = NSA_KV, NSA_REP, NSA_HD
    assert CMP_BLOCK == 2 * CMP_STRIDE and S % kt == 0 and WINDOW % tq == 0 and tq == LANES
    ncp = S // CMP_STRIDE
    nsel = S // SEL_BLOCK
    n_cmp = (S - CMP_BLOCK) // CMP_STRIDE + 1
    c_tab, s1_tab, s2_tab = _rope_tables(S)
    tab_spec3 = pl.BlockSpec((tp, HD), lambda b, h, i: (i, 0))

    qw = NSA_HEADS * HD
    tab_spec2 = pl.BlockSpec((tp, HD), lambda b, i: (i, 0))
    q_s, q_r = pl.pallas_call(
        _nsa_prep_q_kernel,
        grid=(B, S // tp),
        in_specs=[pl.BlockSpec((1, tp, qw), lambda b, i: (b, i, 0)), tab_spec2, tab_spec2, tab_spec2],
        out_specs=[pl.BlockSpec((1, tp, qw), lambda b, i: (b, i, 0))] * 2,
        out_shape=[jax.ShapeDtypeStruct((B, S, qw), BF16)] * 2,
        compiler_params=_cparams(("parallel", "parallel")),
        name="nsa_prep_q",
    )(p3, c_tab, s1_tab, s2_tab)

    def col_spec(col0):
        return pl.BlockSpec((1, tp, HD), lambda b, g, i: (b, i, col0 + g))

    ks_rot, kw_rot, vs_t, vw_t = pl.pallas_call(
        _nsa_prep_kv_kernel,
        grid=(B, G, S // tp),
        in_specs=[col_spec(NSA_COL_KS), col_spec(NSA_COL_KW), col_spec(NSA_COL_VS), col_spec(NSA_COL_VW),
                  tab_spec3, tab_spec3, tab_spec3],
        out_specs=[pl.BlockSpec((1, 1, tp, HD), lambda b, g, i: (b, g, i, 0))] * 2
        + [pl.BlockSpec((1, 1, HD, tp), lambda b, g, i: (b, g, 0, i))] * 2,
        out_shape=[jax.ShapeDtypeStruct((B, G, S, HD), BF16)] * 2 + [jax.ShapeDtypeStruct((B, G, HD, S), BF16)] * 2,
        compiler_params=_cparams(("parallel", "parallel", "parallel")),
        name="nsa_prep_kv",
    )(p3, p3, p3, p3, c_tab, s1_tab, s2_tab)

    k_cmp, v_cmp_t = pl.pallas_call(
        functools.partial(_nsa_compress_kernel, ncp=ncp),
        grid=(B, G),
        in_specs=[pl.BlockSpec((None, S, HD), lambda b, g: (b, 0, NSA_COL_KC + g)),
                  pl.BlockSpec((None, S, HD), lambda b, g: (b, 0, NSA_COL_VC + g)),
                  pl.BlockSpec((2, CMP_BLOCK, HD), lambda b, g: (0, 0, 0)),
                  pl.BlockSpec((2, CMP_BLOCK, HD, HD), lambda b, g: (0, 0, 0, 0)),
                  pl.BlockSpec((2, HD, HD), lambda b, g: (0, 0, 0))],
        out_specs=[pl.BlockSpec((1, 1, ncp, HD), lambda b, g: (b, g, 0, 0)),
                   pl.BlockSpec((1, 1, HD, ncp), lambda b, g: (b, g, 0, 0))],
        out_shape=[jax.ShapeDtypeStruct((B, G, ncp, HD), BF16), jax.ShapeDtypeStruct((B, G, HD, ncp), BF16)],
        compiler_params=_cparams(("parallel", "parallel")),
        name="nsa_compress",
    )(p3, p3, cmp_pos, cmp_w1.astype(BF16), cmp_w2.astype(BF16))

    c0 = np.arange(ncp)[None, :] * CMP_STRIDE
    s0 = np.arange(nsel)[:, None] * SEL_BLOCK
    ov_t = np.clip(np.minimum(c0 + CMP_BLOCK, s0 + SEL_BLOCK) - np.maximum(c0, s0), 0, None) / CMP_BLOCK
    ov_t = ov_t * (np.arange(ncp)[None, :] < n_cmp)
    gb = jnp.pad(gate_b, (0, LANES - gate_b.shape[0])).reshape(1, LANES)
    ncols = R * tq

    def full_kv(shape):
        return pl.BlockSpec((1, 1) + shape, lambda b, g, i: (b, g, 0, 0))

    return pl.pallas_call(
        functools.partial(_nsa_attn_kernel, tq=tq, kt=kt, ncp=ncp, nsel=nsel),
        grid=(B, G, S // tq),
        in_specs=[pl.BlockSpec((1, tq, R * HD), lambda b, g, i: (b, i, g)),
                  pl.BlockSpec((1, tq, R * HD), lambda b, g, i: (b, i, g)),
                  full_kv((ncp, HD)), full_kv((HD, ncp)),
                  full_kv((S, HD)), full_kv((S, HD)), full_kv((HD, S)), full_kv((HD, S)),
                  pl.BlockSpec((1, tq, LANES), lambda b, g, i: (b, i, NSA_COL_GL)),
                  pl.BlockSpec((1, LANES), lambda b, g, i: (0, 0)),
                  pl.BlockSpec((nsel, ncp), lambda b, g, i: (0, 0))],
        out_specs=pl.BlockSpec((1, tq, R * HD), lambda b, g, i: (b, i, g)),
        out_shape=jax.ShapeDtypeStruct((B, S, NSA_HEADS * HD), F32),
        scratch_shapes=[pltpu.VMEM((nsel, tq), F32),
                        pltpu.VMEM((1, ncols), F32), pltpu.VMEM((1, ncols), F32), pltpu.VMEM((HD, ncols), F32)],
        compiler_params=_cparams(("parallel", "parallel", "arbitrary")),
        name="nsa_attn",
    )(q_s, q_r, k_cmp, v_cmp_t, ks_rot, kw_rot, vs_t, vw_t, p3, gb, jnp.asarray(ov_t, F32))


def _split(x, sizes):
    return jnp.split(x, [int(s) for s in np.cumsum(sizes)[:-1]], axis=-1)


def _rmsnorm_j(x, w):
    return x * lax.rsqrt(jnp.mean(x * x, axis=-1, keepdims=True) + EPS) * w


def _masked_softmax(s, mask):
    s = jnp.where(mask, s.astype(F32), -jnp.inf)
    m = jnp.max(s, axis=-1, keepdims=True)
    m = jnp.where(jnp.isfinite(m), m, 0.0)
    e = jnp.exp(s - m)
    d = jnp.sum(e, axis=-1, keepdims=True)
    return e / jnp.where(d > 0, d, 1.0)


def _rope(x, pos):
    half = ROPE_DIM // 2
    inv = 1.0 / (ROPE_THETA ** (jnp.arange(half, dtype=F32) / half))
    ang = pos.astype(F32)[:, None] * inv[None, :]
    shape = (ang.shape[0],) + (1,) * (x.ndim - 3) + (half,)
    cos = jnp.cos(ang).reshape(shape)
    sin = jnp.sin(ang).reshape(shape)
    x1 = x[..., :half]
    x2 = x[..., half:ROPE_DIM]
    return jnp.concatenate([x1 * cos - x2 * sin, x2 * cos + x1 * sin, x[..., ROPE_DIM:]], axis=-1)


def _token_shift(p):
    return jnp.pad(p, ((0, 0), (1, 0), (0, 0)))[:, :-1]


def _causal_conv(x, w, b):
    k_len = w.shape[0]
    s_len = x.shape[1]
    xp = jnp.pad(x, ((0, 0), (k_len - 1, 0), (0, 0)))
    y = b
    for j in range(k_len):
        y = y + xp[:, j:j + s_len] * w[j]
    return y


def _mlstm_j(q, k, v, i_pre, f_pre):
    B, S, H, DK = q.shape
    DV = v.shape[-1]
    L = ML_CHUNK
    NC = S // L

    def chunks(t):
        t = t.astype(F32).reshape((B, NC, L, H) + t.shape[3:])
        return t.transpose((1, 0, 3, 2) + tuple(range(4, t.ndim)))

    qc = chunks(q) * (DK ** -0.5)
    kc = chunks(k)
    vc = chunks(v)
    ic = chunks(i_pre)
    lfc = chunks(jax.nn.log_sigmoid(f_pre.astype(F32)))
    causal = jnp.asarray(np.tril(np.ones((L, L), dtype=bool)))

    def step(carry, xs):
        C, n, m = carry
        q_, k_, v_, i_, lf = xs
        b = jnp.cumsum(lf, axis=-1)
        dmat = jnp.where(causal, b[..., :, None] - b[..., None, :] + i_[..., None, :], -jnp.inf)
        inter = b + m[..., None]
        m_row = jnp.maximum(inter, jnp.max(dmat, axis=-1))
        w_in = jnp.exp(dmat - m_row[..., None])
        w_st = jnp.exp(inter - m_row)
        s = jnp.einsum('bhjd,bhld->bhjl', q_, k_) * w_in
        num = w_st[..., None] * jnp.einsum('bhjd,bhde->bhje', q_, C) + jnp.einsum('bhjl,bhle->bhje', s, v_)
        den = w_st * jnp.einsum('bhjd,bhd->bhj', q_, n) + jnp.sum(s, axis=-1)
        h = num / jnp.maximum(jnp.abs(den), jnp.exp(-m_row))[..., None]
        b_last = b[..., -1]
        g_key = b_last[..., None] - b + i_
        m_new = jnp.maximum(b_last + m, jnp.max(g_key, axis=-1))
        wk = jnp.exp(g_key - m_new[..., None])
        decay = jnp.exp(b_last + m - m_new)
        C_new = decay[..., None, None] * C + jnp.einsum('bhld,bhle->bhde', k_ * wk[..., None], v_)
        n_new = decay[..., None] * n + jnp.einsum('bhl,bhld->bhd', wk, k_)
        return (C_new, n_new, m_new), h

    init = (jnp.zeros((B, H, DK, DV), F32), jnp.zeros((B, H, DK), F32), jnp.zeros((B, H), F32))
    _, hs = lax.scan(step, init, (qc, kc, vc, ic, lfc))
    return hs.transpose(1, 0, 3, 2, 4).reshape(B, S, H, DV)


def _rwkv7_scan_j(r, w, k, v, kk, a):
    def step(state, xs):
        r_t, w_t, k_t, v_t, kk_t, a_t = xs
        sk = jnp.einsum('bhvk,bhk->bhv', state, kk_t)
        state = (state * w_t[:, :, None, :] - sk[..., None] * (kk_t * a_t)[:, :, None, :]
                 + v_t[..., None] * k_t[:, :, None, :])
        return state, jnp.einsum('bhvk,bhk->bhv', state, r_t)

    B, S, H, N = r.shape
    xs = tuple(t.transpose(1, 0, 2, 3) for t in (r, w, k, v, kk, a))
    _, y = lax.scan(step, jnp.zeros((B, H, N, N), F32), xs)
    return y.transpose(1, 0, 2, 3)


AB_COL_GATES = 50 * LANES
AB_COLS_PAD = 52 * LANES


def _ab_permute_cols(w_in):
    ml, rw = w_in[:, :ML_COLS], w_in[:, ML_COLS:]
    main, gif = ml[:, :ML_COLS - 2 * ML_HEADS], ml[:, ML_COLS - 2 * ML_HEADS:]
    w = jnp.concatenate([main, rw, gif], axis=1)
    return jnp.pad(w, ((0, 0), (0, AB_COLS_PAD - w.shape[1])))


def _mlstm_kernel(gb_ref, qk_ref, v_ref, o_ref, gcol_ref, grow_ref, cw_ref, cb_ref, y_ref,
                  halo_ref, c_ref, n_ref, m_ref, qc_ref, kc_ref, *, tb):
    H, DK, DV, L = ML_HEADS, ML_DK, ML_DV, ML_CHUNK
    HALO = SUBLANES

    @pl.when(pl.program_id(1) == 0)
    def _():
        halo_ref[...] = jnp.zeros_like(halo_ref)
        c_ref[...] = jnp.zeros_like(c_ref)
        n_ref[...] = jnp.zeros_like(n_ref)
        m_ref[...] = jnp.zeros_like(m_ref)

    x = qk_ref[0]
    xe = jnp.concatenate([halo_ref[...], x], axis=0)
    y = cb_ref[...]
    for j in range(ML_CONV):
        lo = HALO - (ML_CONV - 1) + j
        y = y + xe[lo:lo + tb] * cw_ref[j:j + 1, :]
    halo_ref[...] = x[tb - HALO:]
    y = y * jax.nn.sigmoid(y)
    qc_ref[...] = y[:, :H * DK] * (DK ** -0.5)
    kc_ref[...] = y[:, H * DK:]

    r_i = lax.broadcasted_iota(jnp.int32, (L, L), 0)
    c_i = lax.broadcasted_iota(jnp.int32, (L, L), 1)
    tri = c_i <= r_i

    def chunk(c, carry):
        rows = pl.ds(pl.multiple_of(c * L, L), L)
        for h in range(H):
            q = qc_ref[rows, h * DK:(h + 1) * DK]
            k = kc_ref[rows, h * DK:(h + 1) * DK]
            v = v_ref[0, rows, h * DV:(h + 1) * DV]
            gc = gcol_ref[0, h, rows, :]
            gr = grow_ref[0, h, c]
            i_col = gc[:, 0:1] + gb_ref[h]
            lf_col = jax.nn.log_sigmoid(gc[:, 1:2] + gb_ref[H + h])
            i_row = gr[0:1, :] + gb_ref[h]
            lf_row = jax.nn.log_sigmoid(gr[1:2, :] + gb_ref[H + h])
            b_col = jnp.sum(jnp.where(tri, lf_row, 0.0), axis=1, keepdims=True)
            b_row = jnp.sum(jnp.where(c_i >= r_i, lf_col, 0.0), axis=0, keepdims=True)
            dmat = jnp.where(tri, b_col - b_row + i_row, NSA_NEG)
            m_old = m_ref[h]
            inter = b_col + m_old
            m_row = jnp.maximum(inter, jnp.max(dmat, axis=1, keepdims=True))
            w_in = jnp.exp(dmat - m_row)
            w_st = jnp.exp(inter - m_row)
            qb = q.astype(BF16)
            s = lax.dot_general(qb, k.astype(BF16), (((1,), (1,)), ((), ())), preferred_element_type=F32) * w_in
            num = (w_st * jnp.dot(qb, c_ref[h].astype(BF16), preferred_element_type=F32)
                   + jnp.dot(s.astype(BF16), v.astype(BF16), preferred_element_type=F32))
            den = w_st * jnp.sum(q * n_ref[h], axis=1, keepdims=True) + jnp.sum(s, axis=1, keepdims=True)
            hid = num * (1.0 / jnp.maximum(jnp.abs(den), jnp.exp(-m_row)))
            y_ref[0, rows, h * DV:(h + 1) * DV] = jax.nn.sigmoid(o_ref[0, rows, h * DV:(h + 1) * DV]) * hid
            b_last = b_col[L - 1:L, :]
            g_key = b_last - b_col + i_col
            m_new = jnp.maximum(b_last + m_old, jnp.max(g_key, axis=0, keepdims=True))
            wk = jnp.exp(g_key - m_new)
            decay = jnp.exp(b_last + m_old - m_new)
            kw_t = (k * wk).T.astype(BF16)
            c_ref[h] = decay * c_ref[h] + jnp.dot(kw_t, v.astype(BF16), preferred_element_type=F32)
            n_ref[h] = decay * n_ref[h] + jnp.sum(wk * k, axis=0, keepdims=True)
            m_ref[h] = m_new
        return carry

    lax.fori_loop(0, tb // L, chunk, 0)


def _mlstm(p3, conv_w, conv_b, gate_b, tb=512):
    B, S, _ = p3.shape
    H, DK, DV, L = ML_HEADS, ML_DK, ML_DV, ML_CHUNK
    tb = min(tb, S)
    nc = S // L
    gif = p3[:, :, AB_COL_GATES:AB_COL_GATES + 2 * H].reshape(B, S, 2, H)
    gcol = gif.transpose(0, 3, 1, 2)
    grow = gif.reshape(B, nc, L, 2, H).transpose(0, 4, 1, 3, 2)
    qkw = 2 * H * DK
    return pl.pallas_call(
        functools.partial(_mlstm_kernel, tb=tb),
        grid_spec=pltpu.PrefetchScalarGridSpec(
            num_scalar_prefetch=1, grid=(B, S // tb),
            in_specs=[pl.BlockSpec((1, tb, qkw), lambda b, t, gb: (b, t, 0)),
                      pl.BlockSpec((1, tb, H * DV), lambda b, t, gb: (b, t, 1)),
                      pl.BlockSpec((1, tb, H * DV), lambda b, t, gb: (b, t, 2)),
                      pl.BlockSpec((1, H, tb, 2), lambda b, t, gb: (b, 0, t, 0)),
                      pl.BlockSpec((1, H, tb // L, 2, L), lambda b, t, gb: (b, 0, t, 0, 0)),
                      pl.BlockSpec((ML_CONV, qkw), lambda b, t, gb: (0, 0)),
                      pl.BlockSpec((1, qkw), lambda b, t, gb: (0, 0))],
            out_specs=pl.BlockSpec((1, tb, H * DV), lambda b, t, gb: (b, t, 0)),
            scratch_shapes=[pltpu.VMEM((SUBLANES, qkw), F32), pltpu.VMEM((H, DK, DV), F32),
                            pltpu.VMEM((H, 1, DK), F32), pltpu.VMEM((H, 1, 1), F32),
                            pltpu.VMEM((tb, H * DK), F32), pltpu.VMEM((tb, H * DK), F32)]),
        out_shape=jax.ShapeDtypeStruct((B, S, H * DV), F32),
        compiler_params=_cparams(("parallel", "arbitrary")),
        name="mlstm",
    )(gate_b, p3, p3, p3, gcol, grow, conv_w, conv_b.reshape(1, qkw))


def _head_sum_bcast(a, hmat):
    parts = [jnp.dot(a[:, i * LANES:(i + 1) * LANES], hmat, preferred_element_type=F32,
                     precision=lax.Precision.HIGHEST) for i in range(a.shape[1] // LANES)]
    return jnp.concatenate(parts, axis=1)


def _head_hmat():
    row = lax.broadcasted_iota(jnp.int32, (LANES, LANES), 0)
    col = lax.broadcasted_iota(jnp.int32, (LANES, LANES), 1)
    return jnp.where((row // RW_HEAD) == (col // RW_HEAD), 1.0, 0.0).astype(F32)


def _rwkv_prep_kernel(r_ref, k_ref, v_ref, lo_ref, mu_ref, w0_ref, wup_ref, a0_ref, aup_ref, gup_ref, kk_ref2, ka_ref,
                      rk_ref, ro_ref, wo_ref, ko_ref, vo_ref, kko_ref, kbo_ref, go_ref, bo_ref, prev_ref, *, tb):
    @pl.when(pl.program_id(1) == 0)
    def _():
        prev_ref[...] = jnp.zeros_like(prev_ref)

    C = RW_DIM
    x = jnp.concatenate([r_ref[0], k_ref[0], v_ref[0], lo_ref[0]], axis=1)
    prev = prev_ref[...]
    row0 = lax.broadcasted_iota(jnp.int32, (tb, 1), 0) == 0
    shifted = jnp.where(row0, prev[SUBLANES - 1:SUBLANES, :], pltpu.roll(x, 1, 0))
    prev_ref[...] = x[tb - SUBLANES:]
    xm = x + (shifted - x) * mu_ref[...]
    r, k, v, lo = xm[:, :C], xm[:, C:2 * C], xm[:, 2 * C:3 * C], xm[:, 3 * C:]
    xwa = lo[:, :LANES]
    xg = lo[:, LANES:]
    lw = jnp.dot(jnp.tanh(xwa).astype(BF16), wup_ref[...], preferred_element_type=F32)
    la = jnp.dot(xwa.astype(BF16), aup_ref[...], preferred_element_type=F32)
    g = jnp.dot(jax.nn.sigmoid(xg).astype(BF16), gup_ref[...], preferred_element_type=F32)
    w = jnp.exp(-math.exp(-0.5) * jax.nn.sigmoid(w0_ref[...] + lw))
    a = jax.nn.sigmoid(a0_ref[...] + la)
    hmat = _head_hmat()
    kk = k * kk_ref2[...]
    kk = kk * lax.rsqrt(_head_sum_bcast(kk * kk, hmat) + 1e-12)
    k2 = k * (1.0 + (a - 1.0) * ka_ref[...])
    ro_ref[0] = r
    wo_ref[0] = w
    ko_ref[0] = k2
    vo_ref[0] = v
    kko_ref[0] = kk
    kbo_ref[0] = kk * a
    go_ref[0] = g
    bo_ref[0] = _head_sum_bcast(r * k2 * rk_ref[...], hmat) * v


def _rwkv_post_kernel(y_ref, b_ref, g_ref, lnw_ref, lnb_ref, o_ref, *, tb):
    nq = RW_DIM // LANES // 2
    yrows = nq * SUBLANES
    lane = lax.broadcasted_iota(jnp.int32, (tb, LANES), 1)
    parts = []
    for q in range(nq):
        a0, a1, a2, a3 = [y_ref[pl.ds(q * SUBLANES + i, tb, stride=yrows), :] for i in range(4)]
        parts.append(jnp.where(lane < RW_HEAD, a0, pltpu.roll(a1, RW_HEAD, 1)))
        parts.append(jnp.where(lane < RW_HEAD, pltpu.roll(a2, RW_HEAD, 1), a3))
    y = jnp.concatenate(parts, axis=1)
    hmat = _head_hmat()
    mu = _head_sum_bcast(y, hmat) * (1.0 / RW_HEAD)
    yc = y - mu
    var = _head_sum_bcast(yc * yc, hmat) * (1.0 / RW_HEAD)
    yn = yc * lax.rsqrt(var + RW_LN_EPS) * lnw_ref[...] + lnb_ref[...]
    o_ref[0] = (yn + b_ref[0]) * g_ref[0]


def _rwkv_branch(p3, mu, w0, w_up, a0, a_up, g_up, k_k, k_a, r_k, ln_w, ln_b, tb=256):
    B, S, _ = p3.shape
    C = RW_DIM
    tb = min(tb, S)
    row = lambda t: t.reshape(1, -1)
    wup = jnp.concatenate([w_up, jnp.zeros_like(a_up)], axis=0).astype(BF16)
    aup = jnp.concatenate([jnp.zeros_like(w_up), a_up], axis=0).astype(BF16)
    seq = pl.BlockSpec((1, tb, C), lambda b, t: (b, t, 0))
    par = pl.BlockSpec((1, C), lambda b, t: (0, 0))
    r, w, k, v, kk, kb, g, bonus = pl.pallas_call(
        functools.partial(_rwkv_prep_kernel, tb=tb),
        grid=(B, S // tb),
        in_specs=[pl.BlockSpec((1, tb, C), lambda b, t: (b, t, 3)), pl.BlockSpec((1, tb, C), lambda b, t: (b, t, 4)),
                  pl.BlockSpec((1, tb, C), lambda b, t: (b, t, 5)),
                  pl.BlockSpec((1, tb, 2 * LANES), lambda b, t: (b, t, 24)),
                  pl.BlockSpec((1, RW_COLS), lambda b, t: (0, 0)), par,
                  pl.BlockSpec((LANES, C), lambda b, t: (0, 0)), par, pl.BlockSpec((LANES, C), lambda b, t: (0, 0)),
                  pl.BlockSpec((RW_LORA_G, C), lambda b, t: (0, 0)), par, par, par],
        out_specs=[seq] * 8,
        out_shape=[jax.ShapeDtypeStruct((B, S, C), F32)] * 8,
        scratch_shapes=[pltpu.VMEM((SUBLANES, RW_COLS), F32)],
        compiler_params=_cparams(("parallel", "arbitrary")),
        name="rwkv_prep",
    )(p3, p3, p3, p3, row(mu), row(w0), wup, row(a0), aup, g_up.astype(BF16), row(k_k), row(k_a), row(r_k))
    y_raw = _rwkv_scan(r, w, k, v, kk, kb)
    yrows = y_raw.shape[1] // S
    return pl.pallas_call(
        functools.partial(_rwkv_post_kernel, tb=tb),
        grid=(B, S // tb),
        in_specs=[pl.BlockSpec((None, tb * yrows, LANES), lambda b, t: (b, t, 0)), seq, seq, par, par],
        out_specs=seq,
        out_shape=jax.ShapeDtypeStruct((B, S, C), F32),
        compiler_params=_cparams(("parallel", "parallel")),
        name="rwkv_post",
    )(y_raw, bonus, g, row(ln_w), row(ln_b))


def _ab_mixer(x, norm_w, w_in, conv_w, conv_b, gate_b, mu, w0, w_up, a0, a_up, g_up, k_k, k_a, r_k, ln_w, ln_b,
              w_out):
    B, S, D = x.shape
    T = B * S
    x2 = x.reshape(T, D)
    p3 = _matmul(x2, _ab_permute_cols(w_in).astype(BF16), norm_w=norm_w).reshape(B, S, AB_COLS_PAD)
    y_m = _mlstm(p3, conv_w, conv_b, gate_b)
    y_r = _rwkv_branch(p3, mu, w0, w_up, a0, a_up, g_up, k_k, k_a, r_k, ln_w, ln_b)
    hm = ML_HEADS * ML_DV
    part = _matmul(y_m.reshape(T, hm), w_out[:hm].astype(BF16), residual=x2)
    return _matmul(y_r.reshape(T, RW_DIM), w_out[hm:].astype(BF16), residual=part).reshape(B, S, D)


def _ab_mixer_j(x, norm_w, w_in, conv_w, conv_b, gate_b, mu, w0, w_up, a0, a_up, g_up, k_k, k_a, r_k, ln_w, ln_b,
                w_out):
    B, S, D = x.shape
    T = B * S
    x2 = x.reshape(T, D)
    ncol = w_in.shape[1]
    p = _matmul(x2, _pad_cols(w_in, 512).astype(BF16), norm_w=norm_w)[:, :ncol].reshape(B, S, ncol)
    ml_p, rw_p = p[..., :ML_COLS], p[..., ML_COLS:]
    qk, v_m, o_m, gif = _split(ml_p, ML_SPLITS)
    qk = jax.nn.silu(_causal_conv(qk, conv_w, conv_b))
    q_m, k_m = jnp.split(qk, 2, axis=-1)
    gif = gif + gate_b
    h_m = _mlstm_j(q_m.reshape(B, S, ML_HEADS, ML_DK), k_m.reshape(B, S, ML_HEADS, ML_DK),
                   v_m.reshape(B, S, ML_HEADS, ML_DV), gif[..., :ML_HEADS], gif[..., ML_HEADS:])
    y_m = jax.nn.sigmoid(o_m) * h_m.reshape(B, S, ML_HEADS * ML_DV)
    rw_p = rw_p + (_token_shift(rw_p) - rw_p) * mu
    r, k, v, xw, xa, xg = _split(rw_p, RW_SPLITS)
    logw = -math.exp(-0.5) * jax.nn.sigmoid(w0 + jnp.tanh(xw) @ w_up)
    a = jax.nn.sigmoid(a0 + xa @ a_up)
    g = jax.nn.sigmoid(xg) @ g_up

    def heads(t):
        return t.reshape(B, S, RW_HEADS, RW_HEAD)

    kk = heads(k * k_k)
    kk = kk * lax.rsqrt(jnp.sum(kk * kk, axis=-1, keepdims=True) + 1e-12)
    k = k * (1.0 + (a - 1.0) * k_a)
    rh, kh, vh, ah = heads(r), heads(k), heads(v), heads(a)
    y = _rwkv7_scan_j(rh, jnp.exp(heads(logw)), kh, vh, kk, ah)
    y_mu = jnp.mean(y, axis=-1, keepdims=True)
    y_var = jnp.mean(jnp.square(y - y_mu), axis=-1, keepdims=True)
    yn = ((y - y_mu) * lax.rsqrt(y_var + RW_LN_EPS)).reshape(B, S, RW_DIM) * ln_w + ln_b
    bonus = (jnp.sum(rh * kh * r_k.reshape(RW_HEADS, RW_HEAD), axis=-1, keepdims=True) * vh).reshape(B, S, RW_DIM)
    y_r = (yn + bonus) * g
    y_cat = jnp.concatenate([y_m, y_r], axis=-1).reshape(T, D)
    return _matmul(y_cat, w_out.astype(BF16), residual=x2).reshape(B, S, D)


def _nsa(x, norm_w, w_in, gate_b, cmp_pos, cmp_w1, cmp_w2, w_out):
    B, S, D = x.shape
    T = B * S
    x2 = x.reshape(T, D)
    p = _matmul(x2, _pad_cols(w_in, 512).astype(BF16), norm_w=norm_w)
    o = _nsa_attention(p.reshape(B, S, -1), gate_b, cmp_pos, cmp_w1, cmp_w2)
    return _matmul(o.reshape(T, NSA_HEADS * NSA_HD), w_out.astype(BF16), residual=x2).reshape(B, S, D)


def _nsa_j(x, norm_w, w_in, gate_b, cmp_pos, cmp_w1, cmp_w2, w_out):
    B, S, D = x.shape
    T = B * S
    x2 = x.reshape(T, D)
    G, R, HD = NSA_KV, NSA_REP, NSA_HD
    QC = NSA_QCHUNK
    scale = HD ** -0.5
    pos = jnp.arange(S)
    ncol = w_in.shape[1]
    p = _matmul(x2, _pad_cols(w_in, 512).astype(BF16), norm_w=norm_w)[:, :ncol].reshape(B, S, ncol)
    q, kc, vc, ks, vs, kw, vw, gl = _split(p, NSA_SPLITS)
    q = q.reshape(B, S, G, R, HD)

    def kvh(t):
        return t.reshape(B, S, G, HD)

    q_rot = _rope(q, pos)
    ks = _rope(kvh(ks), pos)
    kw = _rope(kvh(kw), pos)
    vs = kvh(vs)
    vw = kvh(vw)
    gates = jax.nn.sigmoid(gl + gate_b).reshape(B, S, 3, G, R)
    n_cmp = (S - CMP_BLOCK) // CMP_STRIDE + 1
    cidx = np.arange(n_cmp)[:, None] * CMP_STRIDE + np.arange(CMP_BLOCK)[None, :]

    def compress(t, pe, w1, w2):
        blocks = t[:, cidx] + pe[None, None, :, None, :]
        return jax.nn.gelu(jnp.einsum('bjpgd,pde->bjge', blocks, w1)) @ w2

    k_cmp = compress(kvh(kc), cmp_pos[0], cmp_w1[0], cmp_w2[0])
    v_cmp = compress(kvh(vc), cmp_pos[1], cmp_w1[1], cmp_w2[1])
    cmp_end = jnp.asarray(cidx[:, -1])
    n_sel = S // SEL_BLOCK
    c0 = np.arange(n_cmp)[:, None] * CMP_STRIDE
    s0 = np.arange(n_sel)[None, :] * SEL_BLOCK
    ov = np.clip(np.minimum(c0 + CMP_BLOCK, s0 + SEL_BLOCK) - np.maximum(c0, s0), 0, None) / CMP_BLOCK
    ov = jnp.asarray(ov, dtype=F32)
    n_top = min(SEL_TOPK, n_sel)
    k_blk = ks.reshape(B, n_sel, SEL_BLOCK, G, HD).transpose(0, 3, 1, 2, 4)
    v_blk = vs.reshape(B, n_sel, SEL_BLOCK, G, HD).transpose(0, 3, 1, 2, 4)
    kw_pad = jnp.pad(kw, ((0, 0), (WINDOW, 0), (0, 0), (0, 0)))
    vw_pad = jnp.pad(vw, ((0, 0), (WINDOW, 0), (0, 0), (0, 0)))
    bi = jnp.arange(B)[:, None, None, None]
    gi = jnp.arange(G)[None, :, None, None]
    blk_ids = jnp.arange(n_sel)

    def chunk(c):
        t0 = c * QC
        qt = t0 + jnp.arange(QC)
        q_c = lax.dynamic_slice_in_dim(q, t0, QC, axis=1)
        qr_c = lax.dynamic_slice_in_dim(q_rot, t0, QC, axis=1)
        g_c = lax.dynamic_slice_in_dim(gates, t0, QC, axis=1)
        p_cmp = _masked_softmax(jnp.einsum('bqgrd,bjgd->bgrqj', q_c, k_cmp) * scale,
                                cmp_end[None, :] <= qt[:, None])
        o_cmp = jnp.einsum('bgrqj,bjgd->bqgrd', p_cmp, v_cmp)
        imp = jnp.einsum('bgrqj,js->bgqs', p_cmp, ov)
        cur = (qt // SEL_BLOCK)[:, None]
        forced = (blk_ids[None, :] == 0) | (blk_ids[None, :] == cur) | (blk_ids[None, :] == cur - 1)
        score = jnp.where(forced, jnp.inf, jnp.where(blk_ids[None, :] <= cur, imp, -jnp.inf))
        top_v, top_i = lax.top_k(score, n_top)
        k_g = k_blk[bi, gi, top_i].reshape(B, G, QC, n_top * SEL_BLOCK, HD)
        v_g = v_blk[bi, gi, top_i].reshape(B, G, QC, n_top * SEL_BLOCK, HD)
        kpos = (top_i[..., None] * SEL_BLOCK + jnp.arange(SEL_BLOCK)).reshape(B, G, QC, n_top * SEL_BLOCK)
        kmask = jnp.repeat(top_v > -jnp.inf, SEL_BLOCK, axis=-1) & (kpos <= qt[None, None, :, None])
        p_slc = _masked_softmax(jnp.einsum('bqgrd,bgqkd->bgrqk', qr_c, k_g) * scale, kmask[:, :, None])
        o_slc = jnp.einsum('bgrqk,bgqkd->bqgrd', p_slc, v_g)
        k_w = lax.dynamic_slice_in_dim(kw_pad, t0, QC + WINDOW, axis=1)
        v_w = lax.dynamic_slice_in_dim(vw_pad, t0, QC + WINDOW, axis=1)
        wpos = t0 - WINDOW + jnp.arange(QC + WINDOW)
        dist = qt[:, None] - wpos[None, :]
        wmask = (dist >= 0) & (dist < WINDOW) & (wpos[None, :] >= 0)
        p_win = _masked_softmax(jnp.einsum('bqgrd,bkgd->bgrqk', qr_c, k_w) * scale, wmask)
        o_win = jnp.einsum('bgrqk,bkgd->bqgrd', p_win, v_w)
        return (g_c[:, :, 0, :, :, None] * o_cmp + g_c[:, :, 1, :, :, None] * o_slc
                + g_c[:, :, 2, :, :, None] * o_win)

    o = lax.map(chunk, jnp.arange(S // QC))
    o = o.transpose(1, 0, 2, 3, 4, 5).reshape(T, NSA_HEADS * HD)
    return _matmul(o, w_out.astype(BF16), residual=x2).reshape(B, S, D)


def _cross_attn_kernel(q_ref, k_ref, v_ref, o_ref):
    for h in range(CA_HEADS):
        cols = slice(h * CA_HD, (h + 1) * CA_HD)
        q = q_ref[:, cols].astype(BF16)
        k = k_ref[:, cols].astype(BF16)
        s = lax.dot_general(q, k, (((1,), (1,)), ((), ())), preferred_element_type=F32) * (CA_HD ** -0.5)
        e = jnp.exp(s - jnp.max(s, axis=-1, keepdims=True))
        p = e * (1.0 / jnp.sum(e, axis=-1, keepdims=True))
        o_ref[:, cols] = jnp.dot(p.astype(BF16), v_ref[:, cols].astype(BF16), preferred_element_type=F32)


def _cross_attn(x, norm_w, mem, norm_mem, wq, wk, wv, wo, tq=512):
    B, S, D = x.shape
    T = B * S
    M = mem.shape[1]
    x2 = x.reshape(T, D)
    q = _matmul(x2, wq.astype(BF16), norm_w=norm_w)
    kv = _matmul(mem.reshape(B * M, D), jnp.concatenate([wk, wv], axis=1).astype(BF16), norm_w=norm_mem)
    nq = S // tq
    o = pl.pallas_call(
        _cross_attn_kernel,
        grid=(B, nq),
        in_specs=[pl.BlockSpec((tq, D), lambda b, i: (b * nq + i, 0)),
                  pl.BlockSpec((M, D), lambda b, i: (b, 0)), pl.BlockSpec((M, D), lambda b, i: (b, 1))],
        out_specs=pl.BlockSpec((tq, D), lambda b, i: (b * nq + i, 0)),
        out_shape=jax.ShapeDtypeStruct((T, D), F32),
        compiler_params=_cparams(("parallel", "parallel")),
        name="cross_attn",
    )(q, kv, kv)
    return _matmul(o, wo.astype(BF16), residual=x2).reshape(B, S, D)


def _cross_attn_j(x, norm_w, mem, norm_mem, wq, wk, wv, wo):
    B, S, D = x.shape
    T = B * S
    M = mem.shape[1]
    x2 = x.reshape(T, D)
    q = _matmul(x2, wq.astype(BF16), norm_w=norm_w).reshape(B, S, CA_HEADS, CA_HD)
    m2 = mem.reshape(B * M, D)
    kv = _matmul(m2, jnp.concatenate([wk, wv], axis=1).astype(BF16), norm_w=norm_mem)
    k = kv[:, :D].reshape(B, M, CA_HEADS, CA_HD)
    v = kv[:, D:].reshape(B, M, CA_HEADS, CA_HD)
    s = jnp.einsum('bshd,bmhd->bhsm', q, k) * (CA_HD ** -0.5)
    p = jax.nn.softmax(s, axis=-1)
    o = jnp.einsum('bhsm,bmhd->bshd', p, v).reshape(T, D)
    return _matmul(o, wo.astype(BF16), residual=x2).reshape(B, S, D)


MOE_TM = 256
MOE_NEG = -1e30


def _rows_to_slabs(slab_ref, x, n):
    nch = x.shape[1] // LANES
    for c in range(nch):
        slab_ref[pl.ds(c, n, stride=nch), :] = x[:, c * LANES:(c + 1) * LANES]


def _slabs_to_rows(slab_ref, n, nch):
    return jnp.concatenate([slab_ref[pl.ds(c, n, stride=nch), :] for c in range(nch)], axis=1)


def _moe_route_kernel(x_ref, nw_ref, wr_ref, br_ref, xn_ref, ri_ref, rw_ref, cnt_ref, cnt_scr, *, tm):
    @pl.when(pl.program_id(0) == 0)
    def _():
        cnt_scr[...] = jnp.zeros_like(cnt_scr)

    x = x_ref[...]
    xn = x * lax.rsqrt(jnp.mean(x * x, axis=-1, keepdims=True) + EPS) * nw_ref[...]
    _rows_to_slabs(xn_ref, xn, tm)
    logits = jnp.dot(xn, wr_ref[...], preferred_element_type=F32, precision=lax.Precision.HIGHEST) + br_ref[...]
    lane = lax.broadcasted_iota(jnp.int32, (tm, LANES), 1)
    gmask = lane < MOE_GROUPS
    lg = jnp.where(gmask, logits, MOE_NEG)
    gmax = jnp.max(lg, axis=-1, keepdims=True)
    grp = jnp.min(jnp.where(lg == gmax, lane, LANES), axis=-1, keepdims=True)
    p_grp = 1.0 / jnp.sum(jnp.where(gmask, jnp.exp(lg - gmax), 0.0), axis=-1, keepdims=True)
    lo = MOE_GROUPS + grp * MOE_PER_GROUP
    le = jnp.where((lane >= lo) & (lane < lo + MOE_PER_GROUP), logits, MOE_NEG)
    t1 = jnp.max(le, axis=-1, keepdims=True)
    i1 = jnp.min(jnp.where(le == t1, lane, LANES), axis=-1, keepdims=True)
    le2 = jnp.where(lane == i1, MOE_NEG, le)
    t2 = jnp.max(le2, axis=-1, keepdims=True)
    i2 = jnp.min(jnp.where(le2 == t2, lane, LANES), axis=-1, keepdims=True)
    e21 = jnp.exp(t2 - t1)
    w1 = p_grp / (1.0 + e21)
    w2 = w1 * e21
    e1 = i1 - MOE_GROUPS
    e2 = i2 - MOE_GROUPS
    oh1 = jnp.where(lane == e1, 1.0, 0.0)
    oh2 = jnp.where(lane == e2, 1.0, 0.0)
    both = oh1 + oh2
    r_i = lax.broadcasted_iota(jnp.int32, (tm, tm), 0)
    c_i = lax.broadcasted_iota(jnp.int32, (tm, tm), 1)
    ltri = jnp.where(c_i < r_i, 1.0, 0.0).astype(BF16)
    before = jnp.dot(ltri, both.astype(BF16), preferred_element_type=F32) + cnt_scr[...]
    pos1 = jnp.sum(oh1 * before, axis=-1, keepdims=True).astype(jnp.int32)
    pos2 = jnp.sum(oh2 * before, axis=-1, keepdims=True).astype(jnp.int32)
    cnt_scr[...] = cnt_scr[...] + jnp.sum(both, axis=0, keepdims=True)
    ri_ref[...] = jnp.where(lane == 0, e1, jnp.where(lane == 1, e2, jnp.where(lane == 2, pos1,
                            jnp.where(lane == 3, pos2, 0))))
    rw_ref[...] = jnp.where(lane == 0, w1, jnp.where(lane == 1, w2, 0.0))
    cnt_ref[...] = cnt_scr[...]


def _moe_dest(tok, e1_ref, e2_ref, p1_ref, p2_ref, off_ref):
    return off_ref[e1_ref[tok]] + p1_ref[tok], off_ref[e2_ref[tok]] + p2_ref[tok]


def _moe_rowmap_kernel(e1_ref, e2_ref, p1_ref, p2_ref, off_ref, cnt_ref, rt_ref, *, n_tok, rows):
    def clear(i, carry):
        rt_ref[i] = 0
        return carry

    def clear_pad(e, carry):
        used = off_ref[e] + cnt_ref[e]
        end = jnp.where(e + 1 < MOE_EXPERTS, off_ref[jnp.minimum(e + 1, MOE_EXPERTS - 1)], rows)
        lax.fori_loop(used, end, clear, 0)
        return carry

    def place(t, carry):
        d1, d2 = _moe_dest(t, e1_ref, e2_ref, p1_ref, p2_ref, off_ref)
        rt_ref[d1] = t
        rt_ref[d2] = t
        return carry

    lax.fori_loop(0, MOE_EXPERTS, clear_pad, 0)
    lax.fori_loop(0, n_tok, place, 0, unroll=8)


def _moe_expert_kernel(te_ref, nu_ref, rt_ref, xn_hbm, wg_ref, wu_ref, wd_ref, y_ref, xbuf, sems):
    del te_ref
    i = pl.program_id(0)
    nu = nu_ref[0]
    slot = i % 2

    nch = xbuf.shape[1] // MOE_TM

    def row_copy(tile, r, s):
        src = pl.multiple_of(rt_ref[tile * MOE_TM + r] * nch, nch)
        return pltpu.make_async_copy(xn_hbm.at[pl.ds(src, nch)], xbuf.at[s, pl.ds(r * nch, nch)], sems.at[s])

    def drain(s):
        def body(r, carry):
            row_copy(0, 0, s).wait()
            return carry
        lax.fori_loop(0, MOE_TM, body, 0, unroll=8)

    @pl.when(i == 0)
    def _():
        def body(r, carry):
            row_copy(0, r, 0).start()
            return carry
        lax.fori_loop(0, MOE_TM, body, 0, unroll=8)

    @pl.when(i < nu)
    def _():
        drain(slot)
        nxt = jnp.minimum(i + 1, nu - 1)
        for r in range(MOE_TM):
            row_copy(nxt, r, 1 - slot).start()
        x = _slabs_to_rows(xbuf.at[slot], MOE_TM, nch).astype(BF16)
        gate = jnp.dot(x, wg_ref[...].astype(BF16), preferred_element_type=F32)
        up = jnp.dot(x, wu_ref[...].astype(BF16), preferred_element_type=F32)
        hid = gate * jax.nn.sigmoid(gate) * up
        _rows_to_slabs(y_ref, jnp.dot(hid.astype(BF16), wd_ref[...].astype(BF16), preferred_element_type=F32),
                       MOE_TM)

        @pl.when(i == nu - 1)
        def _():
            drain(1 - slot)

    @pl.when(i >= nu)
    def _():
        y_ref[...] = jnp.zeros_like(y_ref)


def _moe_combine_kernel(e1_ref, e2_ref, p1_ref, p2_ref, off_ref, x_ref, rw_ref, ys_hbm, o_ref, buf1, buf2, sems, *, tm):
    i = pl.program_id(0)
    slot = i % 2

    nch = buf1.shape[1] // tm

    def row_copy(src, t, buf, s):
        return pltpu.make_async_copy(ys_hbm.at[pl.ds(pl.multiple_of(src * nch, nch), nch)],
                                     buf.at[s, pl.ds(t * nch, nch)], sems.at[s])

    def issue_tile(tile, s):
        def body(t, carry):
            d1, d2 = _moe_dest(tile * tm + t, e1_ref, e2_ref, p1_ref, p2_ref, off_ref)
            row_copy(d1, t, buf1, s).start()
            row_copy(d2, t, buf2, s).start()
            return carry
        lax.fori_loop(0, tm, body, 0, unroll=4)

    def drain(t, carry):
        row_copy(0, 0, buf1, slot).wait()
        row_copy(0, 0, buf2, slot).wait()
        return carry

    @pl.when(i == 0)
    def _():
        issue_tile(0, 0)

    lax.fori_loop(0, tm, drain, 0, unroll=8)

    @pl.when(i + 1 < pl.num_programs(0))
    def _():
        issue_tile(i + 1, 1 - slot)

    w = rw_ref[...]
    o_ref[...] = (x_ref[...] + w[:, 0:1] * _slabs_to_rows(buf1.at[slot], tm, nch)
                  + w[:, 1:2] * _slabs_to_rows(buf2.at[slot], tm, nch))


def _hier_moe(x, norm_w, wg, bg, we, be, w_gate, w_up, w_down, layer):
    B, S, D = x.shape
    T = B * S
    x2 = x.reshape(T, D)
    FF = w_gate.shape[-1]
    nch = D // LANES
    tm_r = 512
    wr = jnp.pad(jnp.concatenate([wg, we], axis=1), ((0, 0), (0, LANES - MOE_GROUPS - MOE_EXPERTS)))
    br = jnp.pad(jnp.concatenate([bg, be]), (0, LANES - MOE_GROUPS - MOE_EXPERTS)).reshape(1, LANES)
    xn, ri, rw, cnt = pl.pallas_call(
        functools.partial(_moe_route_kernel, tm=tm_r),
        grid=(T // tm_r,),
        in_specs=[pl.BlockSpec((tm_r, D), lambda i: (i, 0)), pl.BlockSpec((1, D), lambda i: (0, 0)),
                  pl.BlockSpec((D, LANES), lambda i: (0, 0)), pl.BlockSpec((1, LANES), lambda i: (0, 0))],
        out_specs=[pl.BlockSpec((tm_r * nch, LANES), lambda i: (i, 0)), pl.BlockSpec((tm_r, LANES), lambda i: (i, 0)),
                   pl.BlockSpec((tm_r, LANES), lambda i: (i, 0)), pl.BlockSpec((1, LANES), lambda i: (0, 0))],
        out_shape=[jax.ShapeDtypeStruct((T * nch, LANES), F32), jax.ShapeDtypeStruct((T, LANES), jnp.int32),
                   jax.ShapeDtypeStruct((T, LANES), F32), jax.ShapeDtypeStruct((1, LANES), F32)],
        scratch_shapes=[pltpu.VMEM((1, LANES), F32)],
        compiler_params=_cparams(("arbitrary",)),
        name="moe_route",
    )(x2, norm_w.reshape(1, D), wr, br)

    counts = cnt[0, :MOE_EXPERTS].astype(jnp.int32)
    padded = (counts + MOE_TM - 1) // MOE_TM * MOE_TM
    ends = jnp.cumsum(padded)
    off = (ends - padded).astype(jnp.int32)
    n_tiles = (T * MOE_TOPK) // MOE_TM + MOE_EXPERTS
    rows = n_tiles * MOE_TM
    n_used = (ends[-1] // MOE_TM).astype(jnp.int32).reshape(1)
    tile_e = jnp.minimum(jnp.searchsorted(ends, jnp.arange(n_tiles, dtype=jnp.int32) * MOE_TM, side='right'),
                         MOE_EXPERTS - 1).astype(jnp.int32)
    e1, e2, p1, p2 = ri[:, 0], ri[:, 1], ri[:, 2], ri[:, 3]

    row_tok = pl.pallas_call(
        functools.partial(_moe_rowmap_kernel, n_tok=T, rows=rows),
        grid_spec=pltpu.PrefetchScalarGridSpec(
            num_scalar_prefetch=6, grid=(1,), in_specs=[],
            out_specs=pl.BlockSpec(memory_space=pltpu.SMEM)),
        out_shape=jax.ShapeDtypeStruct((rows,), jnp.int32),
        compiler_params=_cparams(("arbitrary",)),
        name="moe_rowmap",
    )(e1, e2, p1, p2, off, counts)

    def w_ix(i, te, nu, rt):
        return (layer, te[jnp.minimum(i, nu[0] - 1)], 0, 0)

    ys = pl.pallas_call(
        _moe_expert_kernel,
        grid_spec=pltpu.PrefetchScalarGridSpec(
            num_scalar_prefetch=3, grid=(n_tiles,),
            in_specs=[pl.BlockSpec(memory_space=pl.ANY),
                      pl.BlockSpec((None, None, D, FF), w_ix), pl.BlockSpec((None, None, D, FF), w_ix),
                      pl.BlockSpec((None, None, FF, D), w_ix)],
            out_specs=pl.BlockSpec((MOE_TM * nch, LANES), lambda i, te, nu, rt: (i, 0)),
            scratch_shapes=[pltpu.VMEM((2, MOE_TM * nch, LANES), F32), pltpu.SemaphoreType.DMA((2,))]),
        out_shape=jax.ShapeDtypeStruct((rows * nch, LANES), F32),
        compiler_params=_cparams(("arbitrary",)),
        name="moe_experts",
    )(tile_e, n_used, row_tok, xn, w_gate, w_up, w_down)

    tm_c = 256
    out = pl.pallas_call(
        functools.partial(_moe_combine_kernel, tm=tm_c),
        grid_spec=pltpu.PrefetchScalarGridSpec(
            num_scalar_prefetch=5, grid=(T // tm_c,),
            in_specs=[pl.BlockSpec((tm_c, D), lambda i, *_: (i, 0)), pl.BlockSpec((tm_c, LANES), lambda i, *_: (i, 0)),
                      pl.BlockSpec(memory_space=pl.ANY)],
            out_specs=pl.BlockSpec((tm_c, D), lambda i, *_: (i, 0)),
            scratch_shapes=[pltpu.VMEM((2, tm_c * nch, LANES), F32), pltpu.VMEM((2, tm_c * nch, LANES), F32),
                            pltpu.SemaphoreType.DMA((2,))]),
        out_shape=jax.ShapeDtypeStruct((T, D), F32),
        compiler_params=_cparams(("arbitrary",)),
        name="moe_combine",
    )(e1, e2, p1, p2, off, x2, rw, ys)
    return out.reshape(B, S, D)


def _hier_moe_j(x, norm_w, wg, bg, we, be, w_gate, w_up, w_down):
    B, S, D = x.shape
    h = _rmsnorm_j(x, norm_w)
    T = B * S
    A = T * MOE_TOPK
    xt = h.reshape(T, D)
    lg = xt @ wg + bg
    grp = jnp.argmax(lg, axis=-1)
    p_grp = jnp.take_along_axis(jax.nn.softmax(lg, axis=-1), grp[:, None], axis=-1)
    le = (xt @ we + be).reshape(T, MOE_GROUPS, MOE_PER_GROUP)
    le = jnp.take_along_axis(le, grp[:, None, None], axis=1)[:, 0]
    top_l, top_e = lax.top_k(le, MOE_TOPK)
    wts = (p_grp * jax.nn.softmax(top_l, axis=-1)).reshape(A)
    eid = (grp[:, None] * MOE_PER_GROUP + top_e).reshape(A)
    tok = jnp.repeat(jnp.arange(T, dtype=jnp.int32), MOE_TOPK)
    order = jnp.argsort(eid)
    e_s, tok_s, w_s = eid[order], tok[order], wts[order]
    counts = jax.ops.segment_sum(jnp.ones((A,), jnp.int32), eid, num_segments=MOE_EXPERTS)
    padded = (counts + MOE_ROWS - 1) // MOE_ROWS * MOE_ROWS
    ends = jnp.cumsum(padded)
    dest = (ends - padded)[e_s] + jnp.arange(A, dtype=jnp.int32) - (jnp.cumsum(counts) - counts)[e_s]
    n_chunks = -(-A // MOE_ROWS) + MOE_EXPERTS
    rows = n_chunks * MOE_ROWS
    row_tok = jnp.full((rows,), T, jnp.int32).at[dest].set(tok_s)
    row_w = jnp.zeros((rows,), F32).at[dest].set(w_s)
    chunk_e = jnp.minimum(jnp.searchsorted(ends, jnp.arange(n_chunks, dtype=jnp.int32) * MOE_ROWS, side='right'),
                          MOE_EXPERTS - 1)
    x_pad = jnp.concatenate([xt, jnp.zeros((1, D), xt.dtype)], axis=0)

    def expert_rows(args):
        t_idx, w_r, e = args
        xr = x_pad[t_idx]
        hid = jax.nn.silu(xr @ w_gate[e]) * (xr @ w_up[e])
        y = hid @ w_down[e]
        return y * w_r[:, None]

    out = lax.map(expert_rows, (row_tok.reshape(n_chunks, MOE_ROWS), row_w.reshape(n_chunks, MOE_ROWS), chunk_e))
    y = jax.ops.segment_sum(out.reshape(rows, D), row_tok, num_segments=T + 1)[:T]
    return x + y.reshape(B, S, D)


def kernel(x, mem, norm_mix, norm_cross, norm_mem, norm_ffn, norm_final,
           ab_w_in, ml_conv_w, ml_conv_b, ml_gate_b, rw_mu, rw_w0, rw_w_up, rw_a0, rw_a_up, rw_g_up,
           rw_k_k, rw_k_a, rw_r_k, rw_ln_w, rw_ln_b, ab_w_out,
           nsa_w_in, nsa_gate_b, cmp_pos, cmp_w1, cmp_w2, nsa_w_out,
           ca_wq, ca_wk, ca_wv, ca_wo,
           moe_wg, moe_bg, moe_we, moe_be, moe_w_gate, moe_w_up, moe_w_down):
    B, S, D = x.shape
    for l in range(DEPTH):
        j = l // 2
        if l % 2 == 0:
            x = _ab_mixer(x, norm_mix[l], ab_w_in[j], ml_conv_w[j], ml_conv_b[j], ml_gate_b[j], rw_mu[j], rw_w0[j],
                          rw_w_up[j], rw_a0[j], rw_a_up[j], rw_g_up[j], rw_k_k[j], rw_k_a[j], rw_r_k[j],
                          rw_ln_w[j], rw_ln_b[j], ab_w_out[j])
        else:
            x = _nsa(x, norm_mix[l], nsa_w_in[j], nsa_gate_b[j], cmp_pos[j], cmp_w1[j], cmp_w2[j], nsa_w_out[j])
        x = _cross_attn(x, norm_cross[l], mem, norm_mem[l], ca_wq[l], ca_wk[l], ca_wv[l], ca_wo[l])
        x = _hier_moe(x, norm_ffn[l], moe_wg[l], moe_bg[l], moe_we[l], moe_be[l],
                      moe_w_gate, moe_w_up, moe_w_down, l)
    return _rmsnorm_rows(x.reshape(B * S, D), norm_final).reshape(B, S, D)
```

```python
import functools
import math

import jax
import jax.numpy as jnp
import numpy as np
from jax import lax
from jax.experimental import pallas as pl
from jax.experimental.pallas import tpu as pltpu

F32 = jnp.float32
BF16 = jnp.bfloat16

D_MODEL = 2048
DEPTH = 2
EPS = 1e-6
ROPE_THETA = 500000.0

ML_HEADS = 4
ML_DV = D_MODEL // 2 // ML_HEADS
ML_DK = ML_DV // 2
ML_CHUNK = 64
ML_CONV = 4
RW_HEAD = 64
RW_HEADS = D_MODEL // 2 // RW_HEAD
RW_DIM = RW_HEADS * RW_HEAD
RW_LORA_W = 64
RW_LORA_A = 64
RW_LORA_G = 128
RW_LN_EPS = 64e-5
ML_SPLITS = (2 * ML_HEADS * ML_DK, ML_HEADS * ML_DV, ML_HEADS * ML_DV, 2 * ML_HEADS)
RW_SPLITS = (RW_DIM, RW_DIM, RW_DIM, RW_LORA_W, RW_LORA_A, RW_LORA_G)
ML_COLS = sum(ML_SPLITS)
RW_COLS = sum(RW_SPLITS)

NSA_HEADS = 16
NSA_KV = 4
NSA_REP = NSA_HEADS // NSA_KV
NSA_HD = D_MODEL // NSA_HEADS
ROPE_DIM = NSA_HD // 4
CMP_BLOCK = 32
CMP_STRIDE = 16
SEL_BLOCK = 64
SEL_TOPK = 16
WINDOW = 512

CA_HEADS = 4
CA_HD = D_MODEL // CA_HEADS

MOE_GROUPS = 8
MOE_PER_GROUP = 8
MOE_EXPERTS = MOE_GROUPS * MOE_PER_GROUP
MOE_TOPK = 2
MOE_FF = D_MODEL // 4

VMEM_LIMIT = 48 * 1024 * 1024
LANES = 128
SUBLANES = 8
NT_DIMS = (((1,), (1,)), ((), ()))


def _cparams(sem):
    return pltpu.CompilerParams(dimension_semantics=sem, vmem_limit_bytes=VMEM_LIMIT)


def _mm_kernel(*refs, has_norm, has_res):
    it = iter(refs)
    x_ref = next(it)
    w_ref = next(it)
    nw_ref = next(it) if has_norm else None
    r_ref = next(it) if has_res else None
    o_ref = next(it)
    xs_ref = next(it)

    @pl.when(pl.program_id(1) == 0)
    def _():
        x = x_ref[...]
        if has_norm:
            ms = jnp.mean(x * x, axis=-1, keepdims=True)
            x = x * lax.rsqrt(ms + EPS) * nw_ref[...]
        xs_ref[...] = x.astype(BF16)

    acc = jnp.dot(xs_ref[...], w_ref[...], preferred_element_type=F32)
    if has_res:
        acc = acc + r_ref[...]
    o_ref[...] = acc


def _matmul(x, w_bf16, *, norm_w=None, residual=None, tm=1024, tn=512):
    M, K = x.shape
    N = w_bf16.shape[1]
    tm = min(tm, M)
    tn = min(tn, N)
    assert M % tm == 0 and N % tn == 0, (M, N, tm, tn)
    has_norm = norm_w is not None
    has_res = residual is not None
    in_specs = [pl.BlockSpec((tm, K), lambda i, j: (i, 0)),
                pl.BlockSpec((K, tn), lambda i, j: (0, j))]
    args = [x, w_bf16]
    if has_norm:
        in_specs.append(pl.BlockSpec((1, K), lambda i, j: (0, 0)))
        args.append(norm_w.reshape(1, K))
    if has_res:
        in_specs.append(pl.BlockSpec((tm, tn), lambda i, j: (i, j)))
        args.append(residual)
    return pl.pallas_call(
        functools.partial(_mm_kernel, has_norm=has_norm, has_res=has_res),
        grid=(M // tm, N // tn),
        in_specs=in_specs,
        out_specs=pl.BlockSpec((tm, tn), lambda i, j: (i, j)),
        out_shape=jax.ShapeDtypeStruct((M, N), F32),
        scratch_shapes=[pltpu.VMEM((tm, K), BF16)],
        compiler_params=_cparams(("parallel", "arbitrary")),
        name="mm_norm" if has_norm else "mm",
    )(*args)


def _mm2_kernel(xa_ref, xb_ref, wa_ref, wb_ref, r_ref, o_ref, xas_ref, xbs_ref):
    @pl.when(pl.program_id(1) == 0)
    def _():
        xas_ref[...] = xa_ref[...].astype(BF16)
        xbs_ref[...] = xb_ref[...].astype(BF16)

    o_ref[...] = (jnp.dot(xas_ref[...], wa_ref[...], preferred_element_type=F32)
                  + jnp.dot(xbs_ref[...], wb_ref[...], preferred_element_type=F32) + r_ref[...])


def _matmul_pair(xa, wa_bf16, xb, wb_bf16, residual, tm=1024, tn=512):
    M, Ka = xa.shape
    Kb = xb.shape[1]
    N = wa_bf16.shape[1]
    tm = min(tm, M)
    tn = min(tn, N)
    assert M % tm == 0 and N % tn == 0, (M, N, tm, tn)
    return pl.pallas_call(
        _mm2_kernel,
        grid=(M // tm, N // tn),
        in_specs=[pl.BlockSpec((tm, Ka), lambda i, j: (i, 0)), pl.BlockSpec((tm, Kb), lambda i, j: (i, 0)),
                  pl.BlockSpec((Ka, tn), lambda i, j: (0, j)), pl.BlockSpec((Kb, tn), lambda i, j: (0, j)),
                  pl.BlockSpec((tm, tn), lambda i, j: (i, j))],
        out_specs=pl.BlockSpec((tm, tn), lambda i, j: (i, j)),
        out_shape=jax.ShapeDtypeStruct((M, N), F32),
        scratch_shapes=[pltpu.VMEM((tm, Ka), BF16), pltpu.VMEM((tm, Kb), BF16)],
        compiler_params=_cparams(("parallel", "arbitrary")),
        name="mm_pair",
    )(xa, xb, wa_bf16, wb_bf16, residual)


def _pad_cols(w, mult):
    n = w.shape[1]
    pad = (-n) % mult
    if pad:
        w = jnp.pad(w, ((0, 0), (0, pad)))
    return w


def _rmsnorm_kernel(x_ref, w_ref, o_ref):
    x = x_ref[...]
    ms = jnp.mean(x * x, axis=-1, keepdims=True)
    o_ref[...] = x * lax.rsqrt(ms + EPS) * w_ref[...]


def _rmsnorm_rows(x, w, tm=512):
    M, K = x.shape
    return pl.pallas_call(
        _rmsnorm_kernel,
        grid=(M // tm,),
        in_specs=[pl.BlockSpec((tm, K), lambda i: (i, 0)), pl.BlockSpec((1, K), lambda i: (0, 0))],
        out_specs=pl.BlockSpec((tm, K), lambda i: (i, 0)),
        out_shape=jax.ShapeDtypeStruct((M, K), F32),
        compiler_params=_cparams(("parallel",)),
        name="rmsnorm",
    )(x, w.reshape(1, K))


def _rwkv_scan_kernel(r_ref, w_ref, k_ref, v_ref, kk_ref, kb_ref, y_ref, s_ref, *, tb, nb, npairs):
    @pl.when(pl.program_id(1) == 0)
    def _():
        s_ref[...] = jnp.zeros_like(s_ref)

    row = lax.broadcasted_iota(jnp.int32, (LANES, LANES), 0)
    col = lax.broadcasted_iota(jnp.int32, (LANES, LANES), 1)
    hmat = jnp.where((row // RW_HEAD) == (col // RW_HEAD), 1.0, 0.0).astype(BF16)
    vrow = lax.broadcasted_iota(jnp.int32, (RW_HEAD, LANES), 0)
    vcol = lax.broadcasted_iota(jnp.int32, (RW_HEAD, LANES), 1)
    diag = jnp.where((vcol % RW_HEAD) == vrow, 1.0, 0.0).astype(BF16)
    row16 = lax.broadcasted_iota(jnp.int32, (2 * SUBLANES, LANES), 0)
    lane16 = lax.broadcasted_iota(jnp.int32, (2 * SUBLANES, LANES), 1)
    head16 = (row16 // SUBLANES) == (lane16 // RW_HEAD)
    row8 = lax.broadcasted_iota(jnp.int32, (SUBLANES, LANES), 0)
    lane8 = lax.broadcasted_iota(jnp.int32, (SUBLANES, LANES), 1)
    nq = npairs // 2
    lns = [pl.ds(p * LANES, LANES) for p in range(npairs)]

    def group_sum(parts):
        lhs = jnp.concatenate([q.astype(BF16) for q in parts], axis=0)
        out = jnp.dot(lhs, hmat, preferred_element_type=F32)
        return [out[i * RW_HEAD:(i + 1) * RW_HEAD] for i in range(len(parts))]

    def step(t8, carry):
        rows = pl.ds(pl.multiple_of(t8 * SUBLANES, SUBLANES), SUBLANES)

        def tiles(ref):
            return [[ref[b, rows, ln] for ln in lns] for b in range(nb)]

        kk8, w8, kb8, k8, v8, r8 = (tiles(kk_ref), tiles(w_ref), tiles(kb_ref), tiles(k_ref), tiles(v_ref),
                                    tiles(r_ref))
        s = [[s_ref[b, p] for p in range(npairs)] for b in range(nb)]
        vcols = [[lax.dot_general(diag, jnp.where(head16, jnp.concatenate([v8[b][p]] * 2, axis=0), 0.0).astype(BF16),
                                  NT_DIMS, preferred_element_type=F32).astype(BF16)
                  for p in range(npairs)] for b in range(nb)]
        vk = [[None] * npairs for _ in range(nb)]
        for b in range(nb):
            for p in range(npairs):
                ksel = jnp.where(head16, jnp.concatenate([k8[b][p]] * 2, axis=0), 0.0)
                rhs = jnp.concatenate([jnp.where((row16 % SUBLANES) == j, ksel, 0.0) for j in range(SUBLANES)],
                                      axis=1).astype(BF16)
                vk[b][p] = jnp.dot(vcols[b][p], rhs, preferred_element_type=F32)
        for j in range(SUBLANES):
            sl = slice(j, j + 1)
            for b in range(nb):
                sk = group_sum([s[b][i] * kk8[b][i][sl] for i in range(npairs)])
                s[b] = [s[b][i] * w8[b][i][sl] - sk[i] * kb8[b][i][sl] + vk[b][i][:, j * LANES:(j + 1) * LANES]
                        for i in range(npairs)]
            for b in range(nb):
                for q in range(nq):
                    p0, p1 = 2 * q, 2 * q + 1
                    a = jnp.where((row8 == 0) & (lane8 < RW_HEAD), r8[b][p0][sl],
                        jnp.where((row8 == 1) & (lane8 >= RW_HEAD), r8[b][p0][sl],
                        jnp.where((row8 == 2) & (lane8 < RW_HEAD), r8[b][p1][sl],
                        jnp.where((row8 == 3) & (lane8 >= RW_HEAD), r8[b][p1][sl], 0.0))))
                    st = jnp.concatenate([s[b][p0], s[b][p1]], axis=0).astype(BF16)
                    yq = lax.dot_general(a.astype(BF16), st, NT_DIMS, preferred_element_type=F32)
                    base = ((t8 * SUBLANES + j) * nq + q) * SUBLANES
                    y_ref[b, pl.ds(pl.multiple_of(base, SUBLANES), SUBLANES), :] = yq
        for b in range(nb):
            for p in range(npairs):
                s_ref[b, p] = s[b][p]
        return carry

    lax.fori_loop(0, tb // SUBLANES, step, 0)


def _rwkv_scan(r, w, k, v, kk, kb, tb=64, nb=4):
    B, S, C = r.shape
    npairs = C // LANES
    tb = min(tb, S)
    nb = min(nb, B)
    spec = pl.BlockSpec((nb, tb, C), lambda b, t: (b, t, 0))
    yrows = (npairs // 2) * SUBLANES
    return pl.pallas_call(
        functools.partial(_rwkv_scan_kernel, tb=tb, nb=nb, npairs=npairs),
        grid=(B // nb, S // tb),
        in_specs=[spec] * 6,
        out_specs=pl.BlockSpec((nb, tb * yrows, LANES), lambda b, t: (b, t, 0)),
        out_shape=jax.ShapeDtypeStruct((B, S * yrows, LANES), F32),
        scratch_shapes=[pltpu.VMEM((nb, npairs, RW_HEAD, LANES), F32)],
        compiler_params=_cparams(("parallel", "arbitrary")),
        name="rwkv_scan",
    )(r, w, k, v, kk, kb)


NSA_NEG = -1e30
NSA_FORCED = 1e30
NSA_REMOVED = -3e30
NSA_COL_Q, NSA_COL_KC, NSA_COL_VC, NSA_COL_KS, NSA_COL_VS, NSA_COL_KW, NSA_COL_VW, NSA_COL_GL = (
    0, 16, 20, 24, 28, 32, 36, 40)


def _rope_tables(S):
    half = ROPE_DIM // 2
    inv = 1.0 / (ROPE_THETA ** (jnp.arange(half, dtype=F32) / half))
    ang = jnp.arange(S, dtype=F32)[:, None] * inv[None, :]
    cos, sin = jnp.cos(ang), jnp.sin(ang)
    one = jnp.ones((S, NSA_HD - ROPE_DIM), F32)
    zero = jnp.zeros((S, NSA_HD - ROPE_DIM), F32)
    zh = jnp.zeros((S, half), F32)
    return (jnp.concatenate([cos, cos, one], axis=1), jnp.concatenate([zh, sin, zero], axis=1),
            jnp.concatenate([-sin, zh, zero], axis=1))


def _rope_rows(x, c, s1, s2):
    half = ROPE_DIM // 2
    return x * c + pltpu.roll(x, half, 1) * s1 + pltpu.roll(x, NSA_HD - half, 1) * s2


def _nsa_prep_q_kernel(x_ref, c_ref, s1_ref, s2_ref, q_ref, qr_ref):
    c, s1, s2 = c_ref[...], s1_ref[...], s2_ref[...]
    for h in range(NSA_HEADS):
        cols = slice(h * NSA_HD, (h + 1) * NSA_HD)
        x = x_ref[0, :, cols] * (NSA_HD ** -0.5)
        q_ref[0, :, cols] = x.astype(BF16)
        qr_ref[0, :, cols] = _rope_rows(x, c, s1, s2).astype(BF16)


def _nsa_prep_kv_kernel(ks_ref, kw_ref, vs_ref, vw_ref, c_ref, s1_ref, s2_ref, kso_ref, kwo_ref, vst_ref, vwt_ref):
    c, s1, s2 = c_ref[...], s1_ref[...], s2_ref[...]
    kso_ref[0, 0] = _rope_rows(ks_ref[0], c, s1, s2).astype(BF16)
    kwo_ref[0, 0] = _rope_rows(kw_ref[0], c, s1, s2).astype(BF16)
    vst_ref[0, 0] = vs_ref[0].T.astype(BF16)
    vwt_ref[0, 0] = vw_ref[0].T.astype(BF16)


def _nsa_compress_kernel(kc_ref, vc_ref, pe_ref, w1_ref, w2_ref, kcmp_ref, vcmpt_ref, *, ncp):
    for which, x_ref in enumerate((kc_ref, vc_ref)):
        za = jnp.zeros((ncp, NSA_HD), F32)
        zb = jnp.zeros((ncp, NSA_HD), F32)
        for p in range(CMP_STRIDE):
            xp = x_ref[pl.ds(p, ncp, stride=CMP_STRIDE), :]
            za = za + jnp.dot((xp + pe_ref[which, p:p + 1, :]).astype(BF16), w1_ref[which, p],
                              preferred_element_type=F32)
            zb = zb + jnp.dot((xp + pe_ref[which, CMP_STRIDE + p:CMP_STRIDE + p + 1, :]).astype(BF16),
                              w1_ref[which, CMP_STRIDE + p], preferred_element_type=F32)
        pre = za + pltpu.roll(zb, ncp - 1, 0)
        out = jnp.dot(jax.nn.gelu(pre).astype(BF16), w2_ref[which], preferred_element_type=F32)
        if which == 0:
            kcmp_ref[0, 0] = out.astype(BF16)
        else:
            vcmpt_ref[0, 0] = out.T.astype(BF16)


def _nsa_attn_kernel(qt_ref, qrt_ref, kcmp_ref, vcmpt_ref, ks_ref, kw_ref, vst_ref, vwt_ref, gl_ref, gb_ref, ovt_ref,
                     o_ref, sel_ref, ms_ref, ls_ref, accs_ref, *, tq, kt, ncp, nsel):
    R = NSA_REP
    g = pl.program_id(1)
    qi = pl.program_id(2)
    t0 = qi * tq
    q_rows = jnp.concatenate([qt_ref[0, :, r * NSA_HD:(r + 1) * NSA_HD] for r in range(R)], axis=0)
    qr_rows = jnp.concatenate([qrt_ref[0, :, r * NSA_HD:(r + 1) * NSA_HD] for r in range(R)], axis=0)
    qpos = t0 + lax.broadcasted_iota(jnp.int32, (1, tq), 1)

    def per_head(fn, a):
        return jnp.concatenate([fn(a[:, r * tq:(r + 1) * tq]) for r in range(R)], axis=1)

    s = lax.dot_general(kcmp_ref[0, 0], q_rows, NT_DIMS, preferred_element_type=F32)
    cend = lax.broadcasted_iota(jnp.int32, (ncp, 1), 0) * CMP_STRIDE + (CMP_BLOCK - 1)
    vis = cend <= qpos
    s = per_head(lambda a: jnp.where(vis, a, NSA_NEG), s)
    m = jnp.max(s, axis=0, keepdims=True)
    e = per_head(lambda a: jnp.where(vis, a, 0.0), jnp.exp(s - m))
    d = jnp.sum(e, axis=0, keepdims=True)
    p = e * (1.0 / jnp.where(d > 0, d, 1.0))
    ocmp_t = jnp.dot(vcmpt_ref[0, 0], p.astype(BF16), preferred_element_type=F32)

    psum = p[:, 0:tq]
    for r in range(1, R):
        psum = psum + p[:, r * tq:(r + 1) * tq]
    imp = jnp.dot(ovt_ref[...], psum, preferred_element_type=F32, precision=lax.Precision.HIGHEST)
    sidx = lax.broadcasted_iota(jnp.int32, (nsel, tq), 0)
    cur = qpos // SEL_BLOCK
    forced = (sidx == 0) | (sidx == cur) | (sidx == cur - 1)
    score = jnp.where(forced, NSA_FORCED, jnp.where(sidx <= cur, imp, NSA_NEG))
    sel = jnp.zeros((nsel, tq), F32)
    for _ in range(SEL_TOPK):
        mx = jnp.max(score, axis=0, keepdims=True)
        idx = jnp.min(jnp.where(score == mx, sidx, nsel), axis=0, keepdims=True)
        pick = (sidx == idx) & (mx > 0.5 * NSA_NEG)
        sel = jnp.where(pick, 1.0, sel)
        score = jnp.where(pick, NSA_REMOVED, score)
    sel_ref[...] = sel

    def online_update(s, valid, v_t, m_ref, l_ref, acc_ref):
        s = per_head(lambda a: jnp.where(valid, a, NSA_NEG), s)
        m_old = m_ref[...]
        m_new = jnp.maximum(m_old, jnp.max(s, axis=0, keepdims=True))
        alpha = jnp.exp(m_old - m_new)
        pexp = jnp.exp(s - m_new)
        l_ref[...] = alpha * l_ref[...] + jnp.sum(pexp, axis=0, keepdims=True)
        acc_ref[...] = alpha * acc_ref[...] + jnp.dot(v_t, pexp.astype(BF16), preferred_element_type=F32)
        m_ref[...] = m_new

    ms_ref[...] = jnp.full(ms_ref.shape, NSA_NEG, F32)
    ls_ref[...] = jnp.zeros(ls_ref.shape, F32)
    accs_ref[...] = jnp.zeros(accs_ref.shape, F32)

    nblk = kt // SEL_BLOCK

    def sel_body(c, carry):
        k0 = pl.multiple_of(c * kt, kt)
        s = lax.dot_general(ks_ref[0, 0, pl.ds(k0, kt), :], qr_rows, NT_DIMS, preferred_element_type=F32)
        selrows = sel_ref[pl.ds(pl.multiple_of(c * nblk, nblk), nblk), :]
        selexp = jnp.concatenate(
            [jnp.broadcast_to(selrows[b:b + 1], (SEL_BLOCK, tq)) for b in range(nblk)], axis=0)
        kpos = k0 + lax.broadcasted_iota(jnp.int32, (kt, 1), 0)
        valid = (selexp > 0.5) & (kpos <= qpos)
        online_update(s, valid, vst_ref[0, 0, :, pl.ds(k0, kt)], ms_ref, ls_ref, accs_ref)
        return carry

    lax.fori_loop(0, t0 // kt + 1, sel_body, 0)


    nwk = WINDOW + tq
    k0 = pl.multiple_of(jnp.maximum(t0 - WINDOW, 0), tq)
    s = lax.dot_general(kw_ref[0, 0, pl.ds(k0, nwk), :], qr_rows, NT_DIMS, preferred_element_type=F32)
    dist = qpos - (k0 + lax.broadcasted_iota(jnp.int32, (nwk, 1), 0))
    inwin = (dist >= 0) & (dist < WINDOW)
    s = per_head(lambda a: jnp.where(inwin, a, NSA_NEG), s)
    ew = jnp.exp(s - jnp.max(s, axis=0, keepdims=True))
    owin_t = (jnp.dot(vwt_ref[0, 0, :, pl.ds(k0, nwk)], ew.astype(BF16), preferred_element_type=F32)
              * (1.0 / jnp.sum(ew, axis=0, keepdims=True)))

    oslc_t = accs_ref[...] * (1.0 / ls_ref[...])

    gates_t = jax.nn.sigmoid(gl_ref[0] + gb_ref[...]).T
    rid = lax.broadcasted_iota(jnp.int32, (LANES, 1), 0)

    def gate_row(which, r):
        return jnp.sum(jnp.where(rid == which * NSA_HEADS + g * R + r, gates_t, 0.0), axis=0, keepdims=True)

    for r in range(R):
        cols = slice(r * tq, (r + 1) * tq)
        o_t = (gate_row(0, r) * ocmp_t[:, cols] + gate_row(1, r) * oslc_t[:, cols]
               + gate_row(2, r) * owin_t[:, cols])
        o_ref[0, :, r * NSA_HD:(r + 1) * NSA_HD] = o_t.T


def _nsa_attention(p3, gate_b, cmp_pos, cmp_w1, cmp_w2, *, tq=128, kt=512, tp=512):
    B, S, _ = p3.shape
    G, R, HD = NSA_KV, NSA_REP, NSA_HD
    assert CMP_BLOCK == 2 * CMP_STRIDE and S % kt == 0 and WINDOW % tq == 0 and tq == LANES
    ncp = S // CMP_STRIDE
    nsel = S // SEL_BLOCK
    n_cmp = (S - CMP_BLOCK) // CMP_STRIDE + 1
    c_tab, s1_tab, s2_tab = _rope_tables(S)
    tab_spec3 = pl.BlockSpec((tp, HD), lambda b, h, i: (i, 0))

    qw = NSA_HEADS * HD
    tab_spec2 = pl.BlockSpec((tp, HD), lambda b, i: (i, 0))
    q_s, q_r = pl.pallas_call(
        _nsa_prep_q_kernel,
        grid=(B, S // tp),
        in_specs=[pl.BlockSpec((1, tp, qw), lambda b, i: (b, i, 0)), tab_spec2, tab_spec2, tab_spec2],
        out_specs=[pl.BlockSpec((1, tp, qw), lambda b, i: (b, i, 0))] * 2,
        out_shape=[jax.ShapeDtypeStruct((B, S, qw), BF16)] * 2,
        compiler_params=_cparams(("parallel", "parallel")),
        name="nsa_prep_q",
    )(p3, c_tab, s1_tab, s2_tab)

    def col_spec(col0):
        return pl.BlockSpec((1, tp, HD), lambda b, g, i: (b, i, col0 + g))

    ks_rot, kw_rot, vs_t, vw_t = pl.pallas_call(
        _nsa_prep_kv_kernel,
        grid=(B, G, S // tp),
        in_specs=[col_spec(NSA_COL_KS), col_spec(NSA_COL_KW), col_spec(NSA_COL_VS), col_spec(NSA_COL_VW),
                  tab_spec3, tab_spec3, tab_spec3],
        out_specs=[pl.BlockSpec((1, 1, tp, HD), lambda b, g, i: (b, g, i, 0))] * 2
        + [pl.BlockSpec((1, 1, HD, tp), lambda b, g, i: (b, g, 0, i))] * 2,
        out_shape=[jax.ShapeDtypeStruct((B, G, S, HD), BF16)] * 2 + [jax.ShapeDtypeStruct((B, G, HD, S), BF16)] * 2,
        compiler_params=_cparams(("parallel", "parallel", "parallel")),
        name="nsa_prep_kv",
    )(p3, p3, p3, p3, c_tab, s1_tab, s2_tab)

    k_cmp, v_cmp_t = pl.pallas_call(
        functools.partial(_nsa_compress_kernel, ncp=ncp),
        grid=(B, G),
        in_specs=[pl.BlockSpec((None, S, HD), lambda b, g: (b, 0, NSA_COL_KC + g)),
                  pl.BlockSpec((None, S, HD), lambda b, g: (b, 0, NSA_COL_VC + g)),
                  pl.BlockSpec((2, CMP_BLOCK, HD), lambda b, g: (0, 0, 0)),
                  pl.BlockSpec((2, CMP_BLOCK, HD, HD), lambda b, g: (0, 0, 0, 0)),
                  pl.BlockSpec((2, HD, HD), lambda b, g: (0, 0, 0))],
        out_specs=[pl.BlockSpec((1, 1, ncp, HD), lambda b, g: (b, g, 0, 0)),
                   pl.BlockSpec((1, 1, HD, ncp), lambda b, g: (b, g, 0, 0))],
        out_shape=[jax.ShapeDtypeStruct((B, G, ncp, HD), BF16), jax.ShapeDtypeStruct((B, G, HD, ncp), BF16)],
        compiler_params=_cparams(("parallel", "parallel")),
        name="nsa_compress",
    )(p3, p3, cmp_pos, cmp_w1.astype(BF16), cmp_w2.astype(BF16))

    c0 = np.arange(ncp)[None, :] * CMP_STRIDE
    s0 = np.arange(nsel)[:, None] * SEL_BLOCK
    ov_t = np.clip(np.minimum(c0 + CMP_BLOCK, s0 + SEL_BLOCK) - np.maximum(c0, s0), 0, None) / CMP_BLOCK
    ov_t = ov_t * (np.arange(ncp)[None, :] < n_cmp)
    gb = jnp.pad(gate_b, (0, LANES - gate_b.shape[0])).reshape(1, LANES)
    ncols = R * tq

    def full_kv(shape):
        return pl.BlockSpec((1, 1) + shape, lambda b, g, i: (b, g, 0, 0))

    return pl.pallas_call(
        functools.partial(_nsa_attn_kernel, tq=tq, kt=kt, ncp=ncp, nsel=nsel),
        grid=(B, G, S // tq),
        in_specs=[pl.BlockSpec((1, tq, R * HD), lambda b, g, i: (b, i, g)),
                  pl.BlockSpec((1, tq, R * HD), lambda b, g, i: (b, i, g)),
                  full_kv((ncp, HD)), full_kv((HD, ncp)),
                  full_kv((S, HD)), full_kv((S, HD)), full_kv((HD, S)), full_kv((HD, S)),
                  pl.BlockSpec((1, tq, LANES), lambda b, g, i: (b, i, NSA_COL_GL)),
                  pl.BlockSpec((1, LANES), lambda b, g, i: (0, 0)),
                  pl.BlockSpec((nsel, ncp), lambda b, g, i: (0, 0))],
        out_specs=pl.BlockSpec((1, tq, R * HD), lambda b, g, i: (b, i, g)),
        out_shape=jax.ShapeDtypeStruct((B, S, NSA_HEADS * HD), F32),
        scratch_shapes=[pltpu.VMEM((nsel, tq), F32),
                        pltpu.VMEM((1, ncols), F32), pltpu.VMEM((1, ncols), F32), pltpu.VMEM((HD, ncols), F32)],
        compiler_params=_cparams(("parallel", "parallel", "arbitrary")),
        name="nsa_attn",
    )(q_s, q_r, k_cmp, v_cmp_t, ks_rot, kw_rot, vs_t, vw_t, p3, gb, jnp.asarray(ov_t, F32))


AB_COL_GATES = 50 * LANES
AB_COLS_PAD = 52 * LANES


def _ab_permute_cols(w_in):
    ml, rw = w_in[:, :ML_COLS], w_in[:, ML_COLS:]
    main, gif = ml[:, :ML_COLS - 2 * ML_HEADS], ml[:, ML_COLS - 2 * ML_HEADS:]
    w = jnp.concatenate([main, rw, gif], axis=1)
    return jnp.pad(w, ((0, 0), (0, AB_COLS_PAD - w.shape[1])))


def _mlstm_kernel(gb_ref, qk_ref, v_ref, o_ref, gcol_ref, grow_ref, cw_ref, cb_ref, y_ref,
                  halo_ref, c_ref, n_ref, m_ref, qc_ref, kc_ref, *, tb):
    H, DK, DV, L = ML_HEADS, ML_DK, ML_DV, ML_CHUNK
    HALO = SUBLANES

    @pl.when(pl.program_id(1) == 0)
    def _():
        halo_ref[...] = jnp.zeros_like(halo_ref)
        c_ref[...] = jnp.zeros_like(c_ref)
        n_ref[...] = jnp.zeros_like(n_ref)
        m_ref[...] = jnp.zeros_like(m_ref)

    x = qk_ref[0]
    xe = jnp.concatenate([halo_ref[...], x], axis=0)
    y = cb_ref[...]
    for j in range(ML_CONV):
        lo = HALO - (ML_CONV - 1) + j
        y = y + xe[lo:lo + tb] * cw_ref[j:j + 1, :]
    halo_ref[...] = x[tb - HALO:]
    y = y * jax.nn.sigmoid(y)
    qc_ref[...] = y[:, :H * DK] * (DK ** -0.5)
    kc_ref[...] = y[:, H * DK:]

    r_i = lax.broadcasted_iota(jnp.int32, (L, L), 0)
    c_i = lax.broadcasted_iota(jnp.int32, (L, L), 1)
    tri = c_i <= r_i

    def chunk(c, carry):
        rows = pl.ds(pl.multiple_of(c * L, L), L)
        for h in range(H):
            q = qc_ref[rows, h * DK:(h + 1) * DK]
            k = kc_ref[rows, h * DK:(h + 1) * DK]
            v = v_ref[0, rows, h * DV:(h + 1) * DV]
            gc = gcol_ref[0, h, rows, :]
            gr = grow_ref[0, h, c]
            i_col = gc[:, 0:1] + gb_ref[h]
            lf_col = jax.nn.log_sigmoid(gc[:, 1:2] + gb_ref[H + h])
            i_row = gr[0:1, :] + gb_ref[h]
            lf_row = jax.nn.log_sigmoid(gr[1:2, :] + gb_ref[H + h])
            b_col = jnp.sum(jnp.where(tri, lf_row, 0.0), axis=1, keepdims=True)
            b_row = jnp.sum(jnp.where(c_i >= r_i, lf_col, 0.0), axis=0, keepdims=True)
            dmat = jnp.where(tri, b_col - b_row + i_row, NSA_NEG)
            m_old = m_ref[h]
            inter = b_col + m_old
            m_row = jnp.maximum(inter, jnp.max(dmat, axis=1, keepdims=True))
            w_in = jnp.exp(dmat - m_row)
            w_st = jnp.exp(inter - m_row)
            qb = q.astype(BF16)
            s = lax.dot_general(qb, k.astype(BF16), (((1,), (1,)), ((), ())), preferred_element_type=F32) * w_in
            num = (w_st * jnp.dot(qb, c_ref[h].astype(BF16), preferred_element_type=F32)
                   + jnp.dot(s.astype(BF16), v.astype(BF16), preferred_element_type=F32))
            den = w_st * jnp.sum(q * n_ref[h], axis=1, keepdims=True) + jnp.sum(s, axis=1, keepdims=True)
            hid = num * (1.0 / jnp.maximum(jnp.abs(den), jnp.exp(-m_row)))
            y_ref[0, rows, h * DV:(h + 1) * DV] = jax.nn.sigmoid(o_ref[0, rows, h * DV:(h + 1) * DV]) * hid
            b_last = b_col[L - 1:L, :]
            g_key = b_last - b_col + i_col
            m_new = jnp.maximum(b_last + m_old, jnp.max(g_key, axis=0, keepdims=True))
            wk = jnp.exp(g_key - m_new)
            decay = jnp.exp(b_last + m_old - m_new)
            kw_t = (k * wk).T.astype(BF16)
            c_ref[h] = decay * c_ref[h] + jnp.dot(kw_t, v.astype(BF16), preferred_element_type=F32)
            n_ref[h] = decay * n_ref[h] + jnp.sum(wk * k, axis=0, keepdims=True)
            m_ref[h] = m_new
        return carry

    lax.fori_loop(0, tb // L, chunk, 0)


def _mlstm(p3, conv_w, conv_b, gate_b, tb=512):
    B, S, _ = p3.shape
    H, DK, DV, L = ML_HEADS, ML_DK, ML_DV, ML_CHUNK
    tb = min(tb, S)
    nc = S // L
    gif = p3[:, :, AB_COL_GATES:AB_COL_GATES + 2 * H].reshape(B, S, 2, H)
    gcol = gif.transpose(0, 3, 1, 2)
    grow = gif.reshape(B, nc, L, 2, H).transpose(0, 4, 1, 3, 2)
    qkw = 2 * H * DK
    return pl.pallas_call(
        functools.partial(_mlstm_kernel, tb=tb),
        grid_spec=pltpu.PrefetchScalarGridSpec(
            num_scalar_prefetch=1, grid=(B, S // tb),
            in_specs=[pl.BlockSpec((1, tb, qkw), lambda b, t, gb: (b, t, 0)),
                      pl.BlockSpec((1, tb, H * DV), lambda b, t, gb: (b, t, 1)),
                      pl.BlockSpec((1, tb, H * DV), lambda b, t, gb: (b, t, 2)),
                      pl.BlockSpec((1, H, tb, 2), lambda b, t, gb: (b, 0, t, 0)),
                      pl.BlockSpec((1, H, tb // L, 2, L), lambda b, t, gb: (b, 0, t, 0, 0)),
                      pl.BlockSpec((ML_CONV, qkw), lambda b, t, gb: (0, 0)),
                      pl.BlockSpec((1, qkw), lambda b, t, gb: (0, 0))],
            out_specs=pl.BlockSpec((1, tb, H * DV), lambda b, t, gb: (b, t, 0)),
            scratch_shapes=[pltpu.VMEM((SUBLANES, qkw), F32), pltpu.VMEM((H, DK, DV), F32),
                            pltpu.VMEM((H, 1, DK), F32), pltpu.VMEM((H, 1, 1), F32),
                            pltpu.VMEM((tb, H * DK), F32), pltpu.VMEM((tb, H * DK), F32)]),
        out_shape=jax.ShapeDtypeStruct((B, S, H * DV), F32),
        compiler_params=_cparams(("parallel", "arbitrary")),
        name="mlstm",
    )(gate_b, p3, p3, p3, gcol, grow, conv_w, conv_b.reshape(1, qkw))


def _head_sum_bcast(a, hmat):
    parts = [jnp.dot(a[:, i * LANES:(i + 1) * LANES], hmat, preferred_element_type=F32,
                     precision=lax.Precision.HIGHEST) for i in range(a.shape[1] // LANES)]
    return jnp.concatenate(parts, axis=1)


def _head_hmat():
    row = lax.broadcasted_iota(jnp.int32, (LANES, LANES), 0)
    col = lax.broadcasted_iota(jnp.int32, (LANES, LANES), 1)
    return jnp.where((row // RW_HEAD) == (col // RW_HEAD), 1.0, 0.0).astype(F32)


def _rwkv_prep_kernel(r_ref, k_ref, v_ref, lo_ref, mu_ref, w0_ref, wup_ref, a0_ref, aup_ref, gup_ref, kk_ref2, ka_ref,
                      rk_ref, ro_ref, wo_ref, ko_ref, vo_ref, kko_ref, kbo_ref, go_ref, bo_ref, prev_ref, *, tb):
    @pl.when(pl.program_id(1) == 0)
    def _():
        prev_ref[...] = jnp.zeros_like(prev_ref)

    C = RW_DIM
    x = jnp.concatenate([r_ref[0], k_ref[0], v_ref[0], lo_ref[0]], axis=1)
    prev = prev_ref[...]
    row0 = lax.broadcasted_iota(jnp.int32, (tb, 1), 0) == 0
    shifted = jnp.where(row0, prev[SUBLANES - 1:SUBLANES, :], pltpu.roll(x, 1, 0))
    prev_ref[...] = x[tb - SUBLANES:]
    xm = x + (shifted - x) * mu_ref[...]
    r, k, v, lo = xm[:, :C], xm[:, C:2 * C], xm[:, 2 * C:3 * C], xm[:, 3 * C:]
    xwa = lo[:, :LANES]
    xg = lo[:, LANES:]
    lw = jnp.dot(jnp.tanh(xwa).astype(BF16), wup_ref[...], preferred_element_type=F32)
    la = jnp.dot(xwa.astype(BF16), aup_ref[...], preferred_element_type=F32)
    g = jnp.dot(jax.nn.sigmoid(xg).astype(BF16), gup_ref[...], preferred_element_type=F32)
    w = jnp.exp(-math.exp(-0.5) * jax.nn.sigmoid(w0_ref[...] + lw))
    a = jax.nn.sigmoid(a0_ref[...] + la)
    hmat = _head_hmat()
    kk = k * kk_ref2[...]
    kk = kk * lax.rsqrt(_head_sum_bcast(kk * kk, hmat) + 1e-12)
    k2 = k * (1.0 + (a - 1.0) * ka_ref[...])
    ro_ref[0] = r
    wo_ref[0] = w
    ko_ref[0] = k2
    vo_ref[0] = v
    kko_ref[0] = kk
    kbo_ref[0] = kk * a
    go_ref[0] = g
    bo_ref[0] = _head_sum_bcast(r * k2 * rk_ref[...], hmat) * v


def _rwkv_post_kernel(y_ref, b_ref, g_ref, lnw_ref, lnb_ref, o_ref, *, tb):
    nq = RW_DIM // LANES // 2
    yrows = nq * SUBLANES
    lane = lax.broadcasted_iota(jnp.int32, (tb, LANES), 1)
    parts = []
    for q in range(nq):
        a0, a1, a2, a3 = [y_ref[pl.ds(q * SUBLANES + i, tb, stride=yrows), :] for i in range(4)]
        parts.append(jnp.where(lane < RW_HEAD, a0, pltpu.roll(a1, RW_HEAD, 1)))
        parts.append(jnp.where(lane < RW_HEAD, pltpu.roll(a2, RW_HEAD, 1), a3))
    y = jnp.concatenate(parts, axis=1)
    hmat = _head_hmat()
    mu = _head_sum_bcast(y, hmat) * (1.0 / RW_HEAD)
    yc = y - mu
    var = _head_sum_bcast(yc * yc, hmat) * (1.0 / RW_HEAD)
    yn = yc * lax.rsqrt(var + RW_LN_EPS) * lnw_ref[...] + lnb_ref[...]
    o_ref[0] = (yn + b_ref[0]) * g_ref[0]


def _rwkv_branch(p3, mu, w0, w_up, a0, a_up, g_up, k_k, k_a, r_k, ln_w, ln_b, tb=256):
    B, S, _ = p3.shape
    C = RW_DIM
    tb = min(tb, S)
    row = lambda t: t.reshape(1, -1)
    wup = jnp.concatenate([w_up, jnp.zeros_like(a_up)], axis=0).astype(BF16)
    aup = jnp.concatenate([jnp.zeros_like(w_up), a_up], axis=0).astype(BF16)
    seq = pl.BlockSpec((1, tb, C), lambda b, t: (b, t, 0))
    par = pl.BlockSpec((1, C), lambda b, t: (0, 0))
    r, w, k, v, kk, kb, g, bonus = pl.pallas_call(
        functools.partial(_rwkv_prep_kernel, tb=tb),
        grid=(B, S // tb),
        in_specs=[pl.BlockSpec((1, tb, C), lambda b, t: (b, t, 3)), pl.BlockSpec((1, tb, C), lambda b, t: (b, t, 4)),
                  pl.BlockSpec((1, tb, C), lambda b, t: (b, t, 5)),
                  pl.BlockSpec((1, tb, 2 * LANES), lambda b, t: (b, t, 24)),
                  pl.BlockSpec((1, RW_COLS), lambda b, t: (0, 0)), par,
                  pl.BlockSpec((LANES, C), lambda b, t: (0, 0)), par, pl.BlockSpec((LANES, C), lambda b, t: (0, 0)),
                  pl.BlockSpec((RW_LORA_G, C), lambda b, t: (0, 0)), par, par, par],
        out_specs=[seq] * 8,
        out_shape=[jax.ShapeDtypeStruct((B, S, C), F32)] * 8,
        scratch_shapes=[pltpu.VMEM((SUBLANES, RW_COLS), F32)],
        compiler_params=_cparams(("parallel", "arbitrary")),
        name="rwkv_prep",
    )(p3, p3, p3, p3, row(mu), row(w0), wup, row(a0), aup, g_up.astype(BF16), row(k_k), row(k_a), row(r_k))
    y_raw = _rwkv_scan(r, w, k, v, kk, kb)
    yrows = y_raw.shape[1] // S
    return pl.pallas_call(
        functools.partial(_rwkv_post_kernel, tb=tb),
        grid=(B, S // tb),
        in_specs=[pl.BlockSpec((None, tb * yrows, LANES), lambda b, t: (b, t, 0)), seq, seq, par, par],
        out_specs=seq,
        out_shape=jax.ShapeDtypeStruct((B, S, C), F32),
        compiler_params=_cparams(("parallel", "parallel")),
        name="rwkv_post",
    )(y_raw, bonus, g, row(ln_w), row(ln_b))


def _ab_mixer(x, norm_w, w_in, conv_w, conv_b, gate_b, mu, w0, w_up, a0, a_up, g_up, k_k, k_a, r_k, ln_w, ln_b,
              w_out):
    B, S, D = x.shape
    T = B * S
    x2 = x.reshape(T, D)
    p3 = _matmul(x2, _ab_permute_cols(w_in).astype(BF16), norm_w=norm_w).reshape(B, S, AB_COLS_PAD)
    y_m = _mlstm(p3, conv_w, conv_b, gate_b)
    y_r = _rwkv_branch(p3, mu, w0, w_up, a0, a_up, g_up, k_k, k_a, r_k, ln_w, ln_b)
    hm = ML_HEADS * ML_DV
    return _matmul_pair(y_m.reshape(T, hm), w_out[:hm].astype(BF16), y_r.reshape(T, RW_DIM),
                        w_out[hm:].astype(BF16), x2).reshape(B, S, D)


def _nsa(x, norm_w, w_in, gate_b, cmp_pos, cmp_w1, cmp_w2, w_out):
    B, S, D = x.shape
    T = B * S
    x2 = x.reshape(T, D)
    p = _matmul(x2, _pad_cols(w_in, 512).astype(BF16), norm_w=norm_w)
    o = _nsa_attention(p.reshape(B, S, -1), gate_b, cmp_pos, cmp_w1, cmp_w2)
    return _matmul(o.reshape(T, NSA_HEADS * NSA_HD), w_out.astype(BF16), residual=x2).reshape(B, S, D)


def _cross_attn_kernel(q_ref, k_ref, v_ref, o_ref):
    for h in range(CA_HEADS):
        cols = slice(h * CA_HD, (h + 1) * CA_HD)
        q = q_ref[:, cols].astype(BF16)
        k = k_ref[:, cols].astype(BF16)
        s = lax.dot_general(q, k, (((1,), (1,)), ((), ())), preferred_element_type=F32) * (CA_HD ** -0.5)
        e = jnp.exp(s - jnp.max(s, axis=-1, keepdims=True))
        p = e * (1.0 / jnp.sum(e, axis=-1, keepdims=True))
        o_ref[:, cols] = jnp.dot(p.astype(BF16), v_ref[:, cols].astype(BF16), preferred_element_type=F32)


def _cross_attn(x, norm_w, mem, norm_mem, wq, wk, wv, wo, tq=512):
    B, S, D = x.shape
    T = B * S
    M = mem.shape[1]
    x2 = x.reshape(T, D)
    q = _matmul(x2, wq.astype(BF16), norm_w=norm_w)
    kv = _matmul(mem.reshape(B * M, D), jnp.concatenate([wk, wv], axis=1).astype(BF16), norm_w=norm_mem)
    nq = S // tq
    o = pl.pallas_call(
        _cross_attn_kernel,
        grid=(B, nq),
        in_specs=[pl.BlockSpec((tq, D), lambda b, i: (b * nq + i, 0)),
                  pl.BlockSpec((M, D), lambda b, i: (b, 0)), pl.BlockSpec((M, D), lambda b, i: (b, 1))],
        out_specs=pl.BlockSpec((tq, D), lambda b, i: (b * nq + i, 0)),
        out_shape=jax.ShapeDtypeStruct((T, D), F32),
        compiler_params=_cparams(("parallel", "parallel")),
        name="cross_attn",
    )(q, kv, kv)
    return _matmul(o, wo.astype(BF16), residual=x2).reshape(B, S, D)


MOE_TM = 256
MOE_NEG = -1e30


def _moe_route_kernel(x_ref, nw_ref, wr_ref, br_ref, xn_ref, ri_ref, rw_ref, cnt_ref, cnt_scr, *, tm):
    @pl.when(pl.program_id(0) == 0)
    def _():
        cnt_scr[...] = jnp.zeros_like(cnt_scr)

    x = x_ref[...]
    xn = x * lax.rsqrt(jnp.mean(x * x, axis=-1, keepdims=True) + EPS) * nw_ref[...]
    xn_ref[...] = xn
    logits = jnp.dot(xn, wr_ref[...], preferred_element_type=F32, precision=lax.Precision.HIGHEST) + br_ref[...]
    lane = lax.broadcasted_iota(jnp.int32, (tm, LANES), 1)
    gmask = lane < MOE_GROUPS
    lg = jnp.where(gmask, logits, MOE_NEG)
    gmax = jnp.max(lg, axis=-1, keepdims=True)
    grp = jnp.min(jnp.where(lg == gmax, lane, LANES), axis=-1, keepdims=True)
    p_grp = 1.0 / jnp.sum(jnp.where(gmask, jnp.exp(lg - gmax), 0.0), axis=-1, keepdims=True)
    lo = MOE_GROUPS + grp * MOE_PER_GROUP
    le = jnp.where((lane >= lo) & (lane < lo + MOE_PER_GROUP), logits, MOE_NEG)
    t1 = jnp.max(le, axis=-1, keepdims=True)
    i1 = jnp.min(jnp.where(le == t1, lane, LANES), axis=-1, keepdims=True)
    le2 = jnp.where(lane == i1, MOE_NEG, le)
    t2 = jnp.max(le2, axis=-1, keepdims=True)
    i2 = jnp.min(jnp.where(le2 == t2, lane, LANES), axis=-1, keepdims=True)
    e21 = jnp.exp(t2 - t1)
    w1 = p_grp / (1.0 + e21)
    w2 = w1 * e21
    e1 = i1 - MOE_GROUPS
    e2 = i2 - MOE_GROUPS
    oh1 = jnp.where(lane == e1, 1.0, 0.0)
    oh2 = jnp.where(lane == e2, 1.0, 0.0)
    both = oh1 + oh2
    r_i = lax.broadcasted_iota(jnp.int32, (tm, tm), 0)
    c_i = lax.broadcasted_iota(jnp.int32, (tm, tm), 1)
    ltri = jnp.where(c_i < r_i, 1.0, 0.0).astype(BF16)
    before = jnp.dot(ltri, both.astype(BF16), preferred_element_type=F32) + cnt_scr[...]
    pos1 = jnp.sum(oh1 * before, axis=-1, keepdims=True).astype(jnp.int32)
    pos2 = jnp.sum(oh2 * before, axis=-1, keepdims=True).astype(jnp.int32)
    cnt_scr[...] = cnt_scr[...] + jnp.sum(both, axis=0, keepdims=True)
    ri_ref[...] = jnp.where(lane == 0, e1, jnp.where(lane == 1, e2, jnp.where(lane == 2, pos1,
                            jnp.where(lane == 3, pos2, 0))))
    rw_ref[...] = jnp.where(lane == 0, w1, jnp.where(lane == 1, w2, 0.0))
    cnt_ref[...] = cnt_scr[...]


def _moe_dest(tok, e1_ref, e2_ref, p1_ref, p2_ref, off_ref):
    return off_ref[e1_ref[tok]] + p1_ref[tok], off_ref[e2_ref[tok]] + p2_ref[tok]


def _moe_rowmap_kernel(e1_ref, e2_ref, p1_ref, p2_ref, off_ref, cnt_ref, rt_ref, *, n_tok, rows):
    def clear(i, carry):
        rt_ref[i] = 0
        return carry

    def clear_pad(e, carry):
        used = off_ref[e] + cnt_ref[e]
        end = jnp.where(e + 1 < MOE_EXPERTS, off_ref[jnp.minimum(e + 1, MOE_EXPERTS - 1)], rows)
        lax.fori_loop(used, end, clear, 0)
        return carry

    def place(t, carry):
        d1, d2 = _moe_dest(t, e1_ref, e2_ref, p1_ref, p2_ref, off_ref)
        rt_ref[d1] = t
        rt_ref[d2] = t
        return carry

    lax.fori_loop(0, MOE_EXPERTS, clear_pad, 0)
    lax.fori_loop(0, n_tok, place, 0, unroll=8)


def _moe_expert_kernel(te_ref, nu_ref, rt_ref, xn_hbm, wg_ref, wu_ref, wd_ref, y_ref, xbuf, sems):
    del te_ref
    i = pl.program_id(0)
    nu = nu_ref[0]
    slot = i % 2

    def row_copy(tile, r, s):
        return pltpu.make_async_copy(xn_hbm.at[pl.ds(rt_ref[tile * MOE_TM + r], 1)], xbuf.at[s, pl.ds(r, 1)],
                                     sems.at[s])

    def drain(s):
        def body(r, carry):
            row_copy(0, 0, s).wait()
            return carry
        lax.fori_loop(0, MOE_TM, body, 0, unroll=8)

    @pl.when(i == 0)
    def _():
        def body(r, carry):
            row_copy(0, r, 0).start()
            return carry
        lax.fori_loop(0, MOE_TM, body, 0, unroll=8)

    @pl.when(i < nu)
    def _():
        drain(slot)
        nxt = jnp.minimum(i + 1, nu - 1)
        for r in range(MOE_TM):
            row_copy(nxt, r, 1 - slot).start()
        x = xbuf[slot].astype(BF16)
        gate = jnp.dot(x, wg_ref[...].astype(BF16), preferred_element_type=F32)
        up = jnp.dot(x, wu_ref[...].astype(BF16), preferred_element_type=F32)
        hid = gate * jax.nn.sigmoid(gate) * up
        y_ref[...] = jnp.dot(hid.astype(BF16), wd_ref[...].astype(BF16), preferred_element_type=F32)

        @pl.when(i == nu - 1)
        def _():
            drain(1 - slot)

    @pl.when(i >= nu)
    def _():
        y_ref[...] = jnp.zeros_like(y_ref)


def _moe_combine_kernel(e1_ref, e2_ref, p1_ref, p2_ref, off_ref, x_ref, rw_ref, ys_hbm, o_ref, buf1, buf2, sems, *, tm):
    i = pl.program_id(0)
    slot = i % 2

    def row_copy(src, t, buf, s):
        return pltpu.make_async_copy(ys_hbm.at[pl.ds(src, 1)], buf.at[s, pl.ds(t, 1)], sems.at[s])

    def issue_tile(tile, s):
        def body(t, carry):
            d1, d2 = _moe_dest(tile * tm + t, e1_ref, e2_ref, p1_ref, p2_ref, off_ref)
            row_copy(d1, t, buf1, s).start(priority=0)
            row_copy(d2, t, buf2, s).start(priority=1)
            return carry
        lax.fori_loop(0, tm, body, 0, unroll=4)

    def drain(t, carry):
        row_copy(0, 0, buf1, slot).wait()
        row_copy(0, 0, buf2, slot).wait()
        return carry

    @pl.when(i == 0)
    def _():
        issue_tile(0, 0)

    lax.fori_loop(0, tm, drain, 0, unroll=8)

    @pl.when(i + 1 < pl.num_programs(0))
    def _():
        issue_tile(i + 1, 1 - slot)

    w = rw_ref[...]
    o_ref[...] = x_ref[...] + w[:, 0:1] * buf1[slot] + w[:, 1:2] * buf2[slot]


def _hier_moe(x, norm_w, wg, bg, we, be, w_gate, w_up, w_down, layer):
    B, S, D = x.shape
    T = B * S
    x2 = x.reshape(T, D)
    FF = w_gate.shape[-1]
    tm_r = 512
    wr = jnp.pad(jnp.concatenate([wg, we], axis=1), ((0, 0), (0, LANES - MOE_GROUPS - MOE_EXPERTS)))
    br = jnp.pad(jnp.concatenate([bg, be]), (0, LANES - MOE_GROUPS - MOE_EXPERTS)).reshape(1, LANES)
    xn, ri, rw, cnt = pl.pallas_call(
        functools.partial(_moe_route_kernel, tm=tm_r),
        grid=(T // tm_r,),
        in_specs=[pl.BlockSpec((tm_r, D), lambda i: (i, 0)), pl.BlockSpec((1, D), lambda i: (0, 0)),
                  pl.BlockSpec((D, LANES), lambda i: (0, 0)), pl.BlockSpec((1, LANES), lambda i: (0, 0))],
        out_specs=[pl.BlockSpec((tm_r, D), lambda i: (i, 0)), pl.BlockSpec((tm_r, LANES), lambda i: (i, 0)),
                   pl.BlockSpec((tm_r, LANES), lambda i: (i, 0)), pl.BlockSpec((1, LANES), lambda i: (0, 0))],
        out_shape=[jax.ShapeDtypeStruct((T, D), F32), jax.ShapeDtypeStruct((T, LANES), jnp.int32),
                   jax.ShapeDtypeStruct((T, LANES), F32), jax.ShapeDtypeStruct((1, LANES), F32)],
        scratch_shapes=[pltpu.VMEM((1, LANES), F32)],
        compiler_params=_cparams(("arbitrary",)),
        name="moe_route",
    )(x2, norm_w.reshape(1, D), wr, br)

    counts = cnt[0, :MOE_EXPERTS].astype(jnp.int32)
    padded = (counts + MOE_TM - 1) // MOE_TM * MOE_TM
    ends = jnp.cumsum(padded)
    off = (ends - padded).astype(jnp.int32)
    n_tiles = (T * MOE_TOPK) // MOE_TM + MOE_EXPERTS
    rows = n_tiles * MOE_TM
    n_used = (ends[-1] // MOE_TM).astype(jnp.int32).reshape(1)
    tile_e = jnp.minimum(jnp.searchsorted(ends, jnp.arange(n_tiles, dtype=jnp.int32) * MOE_TM, side='right'),
                         MOE_EXPERTS - 1).astype(jnp.int32)
    e1, e2, p1, p2 = ri[:, 0], ri[:, 1], ri[:, 2], ri[:, 3]

    row_tok = pl.pallas_call(
        functools.partial(_moe_rowmap_kernel, n_tok=T, rows=rows),
        grid_spec=pltpu.PrefetchScalarGridSpec(
            num_scalar_prefetch=6, grid=(1,), in_specs=[],
            out_specs=pl.BlockSpec(memory_space=pltpu.SMEM)),
        out_shape=jax.ShapeDtypeStruct((rows,), jnp.int32),
        compiler_params=_cparams(("arbitrary",)),
        name="moe_rowmap",
    )(e1, e2, p1, p2, off, counts)

    def w_ix(i, te, nu, rt):
        return (layer, te[jnp.minimum(i, nu[0] - 1)], 0, 0)

    ys = pl.pallas_call(
        _moe_expert_kernel,
        grid_spec=pltpu.PrefetchScalarGridSpec(
            num_scalar_prefetch=3, grid=(n_tiles,),
            in_specs=[pl.BlockSpec(memory_space=pl.ANY),
                      pl.BlockSpec((None, None, D, FF), w_ix), pl.BlockSpec((None, None, D, FF), w_ix),
                      pl.BlockSpec((None, None, FF, D), w_ix)],
            out_specs=pl.BlockSpec((MOE_TM, D), lambda i, te, nu, rt: (i, 0)),
            scratch_shapes=[pltpu.VMEM((2, MOE_TM, D), F32), pltpu.SemaphoreType.DMA((2,))]),
        out_shape=jax.ShapeDtypeStruct((rows, D), F32),
        compiler_params=_cparams(("arbitrary",)),
        name="moe_experts",
    )(tile_e, n_used, row_tok, xn, w_gate, w_up, w_down)

    tm_c = 256
    out = pl.pallas_call(
        functools.partial(_moe_combine_kernel, tm=tm_c),
        grid_spec=pltpu.PrefetchScalarGridSpec(
            num_scalar_prefetch=5, grid=(T // tm_c,),
            in_specs=[pl.BlockSpec((tm_c, D), lambda i, *_: (i, 0)), pl.BlockSpec((tm_c, LANES), lambda i, *_: (i, 0)),
                      pl.BlockSpec(memory_space=pl.ANY)],
            out_specs=pl.BlockSpec((tm_c, D), lambda i, *_: (i, 0)),
            scratch_shapes=[pltpu.VMEM((2, tm_c, D), F32), pltpu.VMEM((2, tm_c, D), F32),
                            pltpu.SemaphoreType.DMA((2,))]),
        out_shape=jax.ShapeDtypeStruct((T, D), F32),
        compiler_params=_cparams(("arbitrary",)),
        name="moe_combine",
    )(e1, e2, p1, p2, off, x2, rw, ys)
    return out.reshape(B, S, D)


def kernel(x, mem, norm_mix, norm_cross, norm_mem, norm_ffn, norm_final,
           ab_w_in, ml_conv_w, ml_conv_b, ml_gate_b, rw_mu, rw_w0, rw_w_up, rw_a0, rw_a_up, rw_g_up,
           rw_k_k, rw_k_a, rw_r_k, rw_ln_w, rw_ln_b, ab_w_out,
           nsa_w_in, nsa_gate_b, cmp_pos, cmp_w1, cmp_w2, nsa_w_out,
           ca_wq, ca_wk, ca_wv, ca_wo,
           moe_wg, moe_bg, moe_we, moe_be, moe_w_gate, moe_w_up, moe_w_down):
    B, S, D = x.shape
    for l in range(DEPTH):
        j = l // 2
        if l % 2 == 0:
            x = _ab_mixer(x, norm_mix[l], ab_w_in[j], ml_conv_w[j], ml_conv_b[j], ml_gate_b[j], rw_mu[j], rw_w0[j],
                          rw_w_up[j], rw_a0[j], rw_a_up[j], rw_g_up[j], rw_k_k[j], rw_k_a[j], rw_r_k[j],
                          rw_ln_w[j], rw_ln_b[j], ab_w_out[j])
        else:
            x = _nsa(x, norm_mix[l], nsa_w_in[j], nsa_gate_b[j], cmp_pos[j], cmp_w1[j], cmp_w2[j], nsa_w_out[j])
        x = _cross_attn(x, norm_cross[l], mem, norm_mem[l], ca_wq[l], ca_wk[l], ca_wv[l], ca_wo[l])
        x = _hier_moe(x, norm_ffn[l], moe_wg[l], moe_bg[l], moe_we[l], moe_be[l],
                      moe_w_gate, moe_w_up, moe_w_down, l)
    return _rmsnorm_rows(x.reshape(B * S, D), norm_final).reshape(B, S, D)
```

```python
import functools
import math

import jax
import jax.numpy as jnp
import numpy as np
from jax import lax
from jax.experimental import pallas as pl
from jax.experimental.pallas import tpu as pltpu

F32 = jnp.float32
BF16 = jnp.bfloat16

D_MODEL = 2048
DEPTH = 2
EPS = 1e-6
ROPE_THETA = 500000.0

ML_HEADS = 4
ML_DV = D_MODEL // 2 // ML_HEADS
ML_DK = ML_DV // 2
ML_CHUNK = 64
ML_CONV = 4
RW_HEAD = 64
RW_HEADS = D_MODEL // 2 // RW_HEAD
RW_DIM = RW_HEADS * RW_HEAD
RW_LORA_W = 64
RW_LORA_A = 64
RW_LORA_G = 128
RW_LN_EPS = 64e-5
ML_SPLITS = (2 * ML_HEADS * ML_DK, ML_HEADS * ML_DV, ML_HEADS * ML_DV, 2 * ML_HEADS)
RW_SPLITS = (RW_DIM, RW_DIM, RW_DIM, RW_LORA_W, RW_LORA_A, RW_LORA_G)
ML_COLS = sum(ML_SPLITS)
RW_COLS = sum(RW_SPLITS)

NSA_HEADS = 16
NSA_KV = 4
NSA_REP = NSA_HEADS // NSA_KV
NSA_HD = D_MODEL // NSA_HEADS
ROPE_DIM = NSA_HD // 4
CMP_BLOCK = 32
CMP_STRIDE = 16
SEL_BLOCK = 64
SEL_TOPK = 16
WINDOW = 512

CA_HEADS = 4
CA_HD = D_MODEL // CA_HEADS

MOE_GROUPS = 8
MOE_PER_GROUP = 8
MOE_EXPERTS = MOE_GROUPS * MOE_PER_GROUP
MOE_TOPK = 2
MOE_FF = D_MODEL // 4

VMEM_LIMIT = 48 * 1024 * 1024
LANES = 128
SUBLANES = 8
NT_DIMS = (((1,), (1,)), ((), ()))


def _cparams(sem):
    return pltpu.CompilerParams(dimension_semantics=sem, vmem_limit_bytes=VMEM_LIMIT)


def _mm_kernel(*refs, has_norm, has_res):
    it = iter(refs)
    x_ref = next(it)
    w_ref = next(it)
    nw_ref = next(it) if has_norm else None
    r_ref = next(it) if has_res else None
    o_ref = next(it)
    xs_ref = next(it)

    @pl.when(pl.program_id(1) == 0)
    def _():
        x = x_ref[...]
        if has_norm:
            ms = jnp.mean(x * x, axis=-1, keepdims=True)
            x = x * lax.rsqrt(ms + EPS) * nw_ref[...]
        xs_ref[...] = x.astype(BF16)

    acc = jnp.dot(xs_ref[...], w_ref[...], preferred_element_type=F32)
    if has_res:
        acc = acc + r_ref[...]
    o_ref[...] = acc


def _matmul(x, w_bf16, *, norm_w=None, residual=None, tm=1024, tn=512):
    M, K = x.shape
    N = w_bf16.shape[1]
    tm = min(tm, M)
    tn = min(tn, N)
    assert M % tm == 0 and N % tn == 0, (M, N, tm, tn)
    has_norm = norm_w is not None
    has_res = residual is not None
    in_specs = [pl.BlockSpec((tm, K), lambda i, j: (i, 0)),
                pl.BlockSpec((K, tn), lambda i, j: (0, j))]
    args = [x, w_bf16]
    if has_norm:
        in_specs.append(pl.BlockSpec((1, K), lambda i, j: (0, 0)))
        args.append(norm_w.reshape(1, K))
    if has_res:
        in_specs.append(pl.BlockSpec((tm, tn), lambda i, j: (i, j)))
        args.append(residual)
    return pl.pallas_call(
        functools.partial(_mm_kernel, has_norm=has_norm, has_res=has_res),
        grid=(M // tm, N // tn),
        in_specs=in_specs,
        out_specs=pl.BlockSpec((tm, tn), lambda i, j: (i, j)),
        out_shape=jax.ShapeDtypeStruct((M, N), F32),
        scratch_shapes=[pltpu.VMEM((tm, K), BF16)],
        compiler_params=_cparams(("parallel", "arbitrary")),
        name="mm_norm" if has_norm else "mm",
    )(*args)


def _mm2_kernel(xa_ref, xb_ref, wa_ref, wb_ref, r_ref, o_ref, xas_ref, xbs_ref):
    @pl.when(pl.program_id(1) == 0)
    def _():
        xas_ref[...] = xa_ref[...].astype(BF16)
        xbs_ref[...] = xb_ref[...].astype(BF16)

    o_ref[...] = (jnp.dot(xas_ref[...], wa_ref[...], preferred_element_type=F32)
                  + jnp.dot(xbs_ref[...], wb_ref[...], preferred_element_type=F32) + r_ref[...])


def _matmul_pair(xa, wa_bf16, xb, wb_bf16, residual, tm=1024, tn=512):
    M, Ka = xa.shape
    Kb = xb.shape[1]
    N = wa_bf16.shape[1]
    tm = min(tm, M)
    tn = min(tn, N)
    assert M % tm == 0 and N % tn == 0, (M, N, tm, tn)
    return pl.pallas_call(
        _mm2_kernel,
        grid=(M // tm, N // tn),
        in_specs=[pl.BlockSpec((tm, Ka), lambda i, j: (i, 0)), pl.BlockSpec((tm, Kb), lambda i, j: (i, 0)),
                  pl.BlockSpec((Ka, tn), lambda i, j: (0, j)), pl.BlockSpec((Kb, tn), lambda i, j: (0, j)),
                  pl.BlockSpec((tm, tn), lambda i, j: (i, j))],
        out_specs=pl.BlockSpec((tm, tn), lambda i, j: (i, j)),
        out_shape=jax.ShapeDtypeStruct((M, N), F32),
        scratch_shapes=[pltpu.VMEM((tm, Ka), BF16), pltpu.VMEM((tm, Kb), BF16)],
        compiler_params=_cparams(("parallel", "arbitrary")),
        name="mm_pair",
    )(xa, xb, wa_bf16, wb_bf16, residual)


def _pad_cols(w, mult):
    n = w.shape[1]
    pad = (-n) % mult
    if pad:
        w = jnp.pad(w, ((0, 0), (0, pad)))
    return w


def _rmsnorm_kernel(x_ref, w_ref, o_ref):
    x = x_ref[...]
    ms = jnp.mean(x * x, axis=-1, keepdims=True)
    o_ref[...] = x * lax.rsqrt(ms + EPS) * w_ref[...]


def _rmsnorm_rows(x, w, tm=512):
    M, K = x.shape
    return pl.pallas_call(
        _rmsnorm_kernel,
        grid=(M // tm,),
        in_specs=[pl.BlockSpec((tm, K), lambda i: (i, 0)), pl.BlockSpec((1, K), lambda i: (0, 0))],
        out_specs=pl.BlockSpec((tm, K), lambda i: (i, 0)),
        out_shape=jax.ShapeDtypeStruct((M, K), F32),
        compiler_params=_cparams(("parallel",)),
        name="rmsnorm",
    )(x, w.reshape(1, K))


def _rwkv_scan_kernel(r_ref, w_ref, k_ref, v_ref, kk_ref, kb_ref, y_ref, s_ref, *, tb, nb, npairs):
    @pl.when(pl.program_id(1) == 0)
    def _():
        s_ref[...] = jnp.zeros_like(s_ref)

    row = lax.broadcasted_iota(jnp.int32, (LANES, LANES), 0)
    col = lax.broadcasted_iota(jnp.int32, (LANES, LANES), 1)
    hmat = jnp.where((row // RW_HEAD) == (col // RW_HEAD), 1.0, 0.0).astype(BF16)
    vrow = lax.broadcasted_iota(jnp.int32, (RW_HEAD, LANES), 0)
    vcol = lax.broadcasted_iota(jnp.int32, (RW_HEAD, LANES), 1)
    diag = jnp.where((vcol % RW_HEAD) == vrow, 1.0, 0.0).astype(BF16)
    row16 = lax.broadcasted_iota(jnp.int32, (2 * SUBLANES, LANES), 0)
    lane16 = lax.broadcasted_iota(jnp.int32, (2 * SUBLANES, LANES), 1)
    head16 = (row16 // SUBLANES) == (lane16 // RW_HEAD)
    row8 = lax.broadcasted_iota(jnp.int32, (SUBLANES, LANES), 0)
    lane8 = lax.broadcasted_iota(jnp.int32, (SUBLANES, LANES), 1)
    nq = npairs // 2
    lns = [pl.ds(p * LANES, LANES) for p in range(npairs)]

    def group_sum(parts):
        lhs = jnp.concatenate([q.astype(BF16) for q in parts], axis=0)
        out = jnp.dot(lhs, hmat, preferred_element_type=F32)
        return [out[i * RW_HEAD:(i + 1) * RW_HEAD] for i in range(len(parts))]

    def step(t8, carry):
        rows = pl.ds(pl.multiple_of(t8 * SUBLANES, SUBLANES), SUBLANES)

        def tiles(ref):
            return [[ref[b, rows, ln] for ln in lns] for b in range(nb)]

        kk8, w8, kb8, k8, v8, r8 = (tiles(kk_ref), tiles(w_ref), tiles(kb_ref), tiles(k_ref), tiles(v_ref),
                                    tiles(r_ref))
        s = [[s_ref[b, p] for p in range(npairs)] for b in range(nb)]
        vcols = [[lax.dot_general(diag, jnp.where(head16, jnp.concatenate([v8[b][p]] * 2, axis=0), 0.0).astype(BF16),
                                  NT_DIMS, preferred_element_type=F32).astype(BF16)
                  for p in range(npairs)] for b in range(nb)]
        vk = [[None] * npairs for _ in range(nb)]
        for b in range(nb):
            for p in range(npairs):
                ksel = jnp.where(head16, jnp.concatenate([k8[b][p]] * 2, axis=0), 0.0)
                rhs = jnp.concatenate([jnp.where((row16 % SUBLANES) == j, ksel, 0.0) for j in range(SUBLANES)],
                                      axis=1).astype(BF16)
                vk[b][p] = jnp.dot(vcols[b][p], rhs, preferred_element_type=F32)
        for j in range(SUBLANES):
            sl = slice(j, j + 1)
            for b in range(nb):
                sk = group_sum([s[b][i] * kk8[b][i][sl] for i in range(npairs)])
                s[b] = [s[b][i] * w8[b][i][sl] - sk[i] * kb8[b][i][sl] + vk[b][i][:, j * LANES:(j + 1) * LANES]
                        for i in range(npairs)]
            for b in range(nb):
                for q in range(nq):
                    p0, p1 = 2 * q, 2 * q + 1
                    a = jnp.where((row8 == 0) & (lane8 < RW_HEAD), r8[b][p0][sl],
                        jnp.where((row8 == 1) & (lane8 >= RW_HEAD), r8[b][p0][sl],
                        jnp.where((row8 == 2) & (lane8 < RW_HEAD), r8[b][p1][sl],
                        jnp.where((row8 == 3) & (lane8 >= RW_HEAD), r8[b][p1][sl], 0.0))))
                    st = jnp.concatenate([s[b][p0], s[b][p1]], axis=0).astype(BF16)
                    yq = lax.dot_general(a.astype(BF16), st, NT_DIMS, preferred_element_type=F32)
                    base = ((t8 * SUBLANES + j) * nq + q) * SUBLANES
                    y_ref[b, pl.ds(pl.multiple_of(base, SUBLANES), SUBLANES), :] = yq
        for b in range(nb):
            for p in range(npairs):
                s_ref[b, p] = s[b][p]
        return carry

    lax.fori_loop(0, tb // SUBLANES, step, 0)


def _rwkv_scan(r, w, k, v, kk, kb, tb=64, nb=4):
    B, S, C = r.shape
    npairs = C // LANES
    tb = min(tb, S)
    nb = min(nb, B)
    spec = pl.BlockSpec((nb, tb, C), lambda b, t: (b, t, 0))
    yrows = (npairs // 2) * SUBLANES
    return pl.pallas_call(
        functools.partial(_rwkv_scan_kernel, tb=tb, nb=nb, npairs=npairs),
        grid=(B // nb, S // tb),
        in_specs=[spec] * 6,
        out_specs=pl.BlockSpec((nb, tb * yrows, LANES), lambda b, t: (b, t, 0)),
        out_shape=jax.ShapeDtypeStruct((B, S * yrows, LANES), F32),
        scratch_shapes=[pltpu.VMEM((nb, npairs, RW_HEAD, LANES), F32)],
        compiler_params=_cparams(("parallel", "arbitrary")),
        name="rwkv_scan",
    )(r, w, k, v, kk, kb)


NSA_NEG = -1e30
NSA_FORCED = 1e30
NSA_REMOVED = -3e30
NSA_COL_Q, NSA_COL_KC, NSA_COL_VC, NSA_COL_KS, NSA_COL_VS, NSA_COL_KW, NSA_COL_VW, NSA_COL_GL = (
    0, 16, 20, 24, 28, 32, 36, 40)


def _rope_tables(S):
    half = ROPE_DIM // 2
    inv = 1.0 / (ROPE_THETA ** (jnp.arange(half, dtype=F32) / half))
    ang = jnp.arange(S, dtype=F32)[:, None] * inv[None, :]
    cos, sin = jnp.cos(ang), jnp.sin(ang)
    one = jnp.ones((S, NSA_HD - ROPE_DIM), F32)
    zero = jnp.zeros((S, NSA_HD - ROPE_DIM), F32)
    zh = jnp.zeros((S, half), F32)
    return (jnp.concatenate([cos, cos, one], axis=1), jnp.concatenate([zh, sin, zero], axis=1),
            jnp.concatenate([-sin, zh, zero], axis=1))


def _rope_rows(x, c, s1, s2):
    half = ROPE_DIM // 2
    return x * c + pltpu.roll(x, half, 1) * s1 + pltpu.roll(x, NSA_HD - half, 1) * s2


def _nsa_prep_q_kernel(x_ref, c_ref, s1_ref, s2_ref, q_ref, qr_ref):
    c, s1, s2 = c_ref[...], s1_ref[...], s2_ref[...]
    for h in range(NSA_HEADS):
        cols = slice(h * NSA_HD, (h + 1) * NSA_HD)
        x = x_ref[0, :, cols] * (NSA_HD ** -0.5)
        q_ref[0, :, cols] = x.astype(BF16)
        qr_ref[0, :, cols] = _rope_rows(x, c, s1, s2).astype(BF16)


def _nsa_prep_kv_kernel(ks_ref, kw_ref, vs_ref, vw_ref, c_ref, s1_ref, s2_ref, kso_ref, kwo_ref, vst_ref, vwt_ref):
    c, s1, s2 = c_ref[...], s1_ref[...], s2_ref[...]
    kso_ref[0, 0] = _rope_rows(ks_ref[0], c, s1, s2).astype(BF16)
    kwo_ref[0, 0] = _rope_rows(kw_ref[0], c, s1, s2).astype(BF16)
    vst_ref[0, 0] = vs_ref[0].T.astype(BF16)
    vwt_ref[0, 0] = vw_ref[0].T.astype(BF16)


def _nsa_compress_kernel(kc_ref, vc_ref, pe_ref, w1_ref, w2_ref, kcmp_ref, vcmpt_ref, *, ncp):
    for which, x_ref in enumerate((kc_ref, vc_ref)):
        za = jnp.zeros((ncp, NSA_HD), F32)
        zb = jnp.zeros((ncp, NSA_HD), F32)
        for p in range(CMP_STRIDE):
            xp = x_ref[pl.ds(p, ncp, stride=CMP_STRIDE), :]
            za = za + jnp.dot((xp + pe_ref[which, p:p + 1, :]).astype(BF16), w1_ref[which, p],
                              preferred_element_type=F32)
            zb = zb + jnp.dot((xp + pe_ref[which, CMP_STRIDE + p:CMP_STRIDE + p + 1, :]).astype(BF16),
                              w1_ref[which, CMP_STRIDE + p], preferred_element_type=F32)
        pre = za + pltpu.roll(zb, ncp - 1, 0)
        out = jnp.dot(jax.nn.gelu(pre).astype(BF16), w2_ref[which], preferred_element_type=F32)
        if which == 0:
            kcmp_ref[0, 0] = out.astype(BF16)
        else:
            vcmpt_ref[0, 0] = out.T.astype(BF16)


def _nsa_attn_kernel(qt_ref, qrt_ref, kcmp_ref, vcmpt_ref, ks_ref, kw_ref, vst_ref, vwt_ref, gl_ref, gb_ref, ovt_ref,
                     o_ref, sel_ref, ms_ref, ls_ref, accs_ref, *, tq, kt, ncp, nsel):
    R = NSA_REP
    g = pl.program_id(1)
    qi = pl.program_id(2)
    t0 = qi * tq
    q_rows = jnp.concatenate([qt_ref[0, :, r * NSA_HD:(r + 1) * NSA_HD] for r in range(R)], axis=0)
    qr_rows = jnp.concatenate([qrt_ref[0, :, r * NSA_HD:(r + 1) * NSA_HD] for r in range(R)], axis=0)
    qpos = t0 + lax.broadcasted_iota(jnp.int32, (1, tq), 1)

    def per_head(fn, a):
        return jnp.concatenate([fn(a[:, r * tq:(r + 1) * tq]) for r in range(R)], axis=1)

    s = lax.dot_general(kcmp_ref[0, 0], q_rows, NT_DIMS, preferred_element_type=F32)
    cend = lax.broadcasted_iota(jnp.int32, (ncp, 1), 0) * CMP_STRIDE + (CMP_BLOCK - 1)
    vis = cend <= qpos
    s = per_head(lambda a: jnp.where(vis, a, NSA_NEG), s)
    m = jnp.max(s, axis=0, keepdims=True)
    e = per_head(lambda a: jnp.where(vis, a, 0.0), jnp.exp(s - m))
    d = jnp.sum(e, axis=0, keepdims=True)
    p = e * (1.0 / jnp.where(d > 0, d, 1.0))
    ocmp_t = jnp.dot(vcmpt_ref[0, 0], p.astype(BF16), preferred_element_type=F32)

    psum = p[:, 0:tq]
    for r in range(1, R):
        psum = psum + p[:, r * tq:(r + 1) * tq]
    imp = jnp.dot(ovt_ref[...], psum, preferred_element_type=F32, precision=lax.Precision.HIGHEST)
    sidx = lax.broadcasted_iota(jnp.int32, (nsel, tq), 0)
    cur = qpos // SEL_BLOCK
    forced = (sidx == 0) | (sidx == cur) | (sidx == cur - 1)
    score = jnp.where(forced, NSA_FORCED, jnp.where(sidx <= cur, imp, NSA_NEG))
    sel = jnp.zeros((nsel, tq), F32)
    for _ in range(SEL_TOPK):
        mx = jnp.max(score, axis=0, keepdims=True)
        idx = jnp.min(jnp.where(score == mx, sidx, nsel), axis=0, keepdims=True)
        pick = (sidx == idx) & (mx > 0.5 * NSA_NEG)
        sel = jnp.where(pick, 1.0, sel)
        score = jnp.where(pick, NSA_REMOVED, score)
    sel_ref[...] = sel

    def online_update(s, valid, v_t, m_ref, l_ref, acc_ref):
        s = per_head(lambda a: jnp.where(valid, a, NSA_NEG), s)
        m_old = m_ref[...]
        m_new = jnp.maximum(m_old, jnp.max(s, axis=0, keepdims=True))
        alpha = jnp.exp(m_old - m_new)
        pexp = jnp.exp(s - m_new)
        l_ref[...] = alpha * l_ref[...] + jnp.sum(pexp, axis=0, keepdims=True)
        acc_ref[...] = alpha * acc_ref[...] + jnp.dot(v_t, pexp.astype(BF16), preferred_element_type=F32)
        m_ref[...] = m_new

    ms_ref[...] = jnp.full(ms_ref.shape, NSA_NEG, F32)
    ls_ref[...] = jnp.zeros(ls_ref.shape, F32)
    accs_ref[...] = jnp.zeros(accs_ref.shape, F32)

    nblk = kt // SEL_BLOCK

    def sel_body(c, carry):
        k0 = pl.multiple_of(c * kt, kt)
        s = lax.dot_general(ks_ref[0, 0, pl.ds(k0, kt), :], qr_rows, NT_DIMS, preferred_element_type=F32)
        selrows = sel_ref[pl.ds(pl.multiple_of(c * nblk, nblk), nblk), :]
        selexp = jnp.concatenate(
            [jnp.broadcast_to(selrows[b:b + 1], (SEL_BLOCK, tq)) for b in range(nblk)], axis=0)
        kpos = k0 + lax.broadcasted_iota(jnp.int32, (kt, 1), 0)
        valid = (selexp > 0.5) & (kpos <= qpos)
        online_update(s, valid, vst_ref[0, 0, :, pl.ds(k0, kt)], ms_ref, ls_ref, accs_ref)
        return carry

    lax.fori_loop(0, t0 // kt + 1, sel_body, 0)


    nwk = WINDOW + tq
    k0 = pl.multiple_of(jnp.maximum(t0 - WINDOW, 0), tq)
    s = lax.dot_general(kw_ref[0, 0, pl.ds(k0, nwk), :], qr_rows, NT_DIMS, preferred_element_type=F32)
    dist = qpos - (k0 + lax.broadcasted_iota(jnp.int32, (nwk, 1), 0))
    inwin = (dist >= 0) & (dist < WINDOW)
    s = per_head(lambda a: jnp.where(inwin, a, NSA_NEG), s)
    ew = jnp.exp(s - jnp.max(s, axis=0, keepdims=True))
    owin_t = (jnp.dot(vwt_ref[0, 0, :, pl.ds(k0, nwk)], ew.astype(BF16), preferred_element_type=F32)
              * (1.0 / jnp.sum(ew, axis=0, keepdims=True)))

    oslc_t = accs_ref[...] * (1.0 / ls_ref[...])

    gates_t = jax.nn.sigmoid(gl_ref[0] + gb_ref[...]).T
    rid = lax.broadcasted_iota(jnp.int32, (LANES, 1), 0)

    def gate_row(which, r):
        return jnp.sum(jnp.where(rid == which * NSA_HEADS + g * R + r, gates_t, 0.0), axis=0, keepdims=True)

    for r in range(R):
        cols = slice(r * tq, (r + 1) * tq)
        o_t = (gate_row(0, r) * ocmp_t[:, cols] + gate_row(1, r) * oslc_t[:, cols]
               + gate_row(2, r) * owin_t[:, cols])
        o_ref[0, :, r * NSA_HD:(r + 1) * NSA_HD] = o_t.T


def _nsa_attention(p3, gate_b, cmp_pos, cmp_w1, cmp_w2, *, tq=128, kt=512, tp=512):
    B, S, _ = p3.shape
    G, R, HD = NSA_KV, NSA_REP, NSA_HD
    assert CMP_BLOCK == 2 * CMP_STRIDE and S % kt == 0 and WINDOW % tq == 0 and tq == LANES
    ncp = S // CMP_STRIDE
    nsel = S // SEL_BLOCK
    n_cmp = (S - CMP_BLOCK) // CMP_STRIDE + 1
    c_tab, s1_tab, s2_tab = _rope_tables(S)
    tab_spec3 = pl.BlockSpec((tp, HD), lambda b, h, i: (i, 0))

    qw = NSA_HEADS * HD
    tab_spec2 = pl.BlockSpec((tp, HD), lambda b, i: (i, 0))
    q_s, q_r = pl.pallas_call(
        _nsa_prep_q_kernel,
        grid=(B, S // tp),
        in_specs=[pl.BlockSpec((1, tp, qw), lambda b, i: (b, i, 0)), tab_spec2, tab_spec2, tab_spec2],
        out_specs=[pl.BlockSpec((1, tp, qw), lambda b, i: (b, i, 0))] * 2,
        out_shape=[jax.ShapeDtypeStruct((B, S, qw), BF16)] * 2,
        compiler_params=_cparams(("parallel", "parallel")),
        name="nsa_prep_q",
    )(p3, c_tab, s1_tab, s2_tab)

    def col_spec(col0):
        return pl.BlockSpec((1, tp, HD), lambda b, g, i: (b, i, col0 + g))

    ks_rot, kw_rot, vs_t, vw_t = pl.pallas_call(
        _nsa_prep_kv_kernel,
        grid=(B, G, S // tp),
        in_specs=[col_spec(NSA_COL_KS), col_spec(NSA_COL_KW), col_spec(NSA_COL_VS), col_spec(NSA_COL_VW),
                  tab_spec3, tab_spec3, tab_spec3],
        out_specs=[pl.BlockSpec((1, 1, tp, HD), lambda b, g, i: (b, g, i, 0))] * 2
        + [pl.BlockSpec((1, 1, HD, tp), lambda b, g, i: (b, g, 0, i))] * 2,
        out_shape=[jax.ShapeDtypeStruct((B, G, S, HD), BF16)] * 2 + [jax.ShapeDtypeStruct((B, G, HD, S), BF16)] * 2,
        compiler_params=_cparams(("parallel", "parallel", "parallel")),
        name="nsa_prep_kv",
    )(p3, p3, p3, p3, c_tab, s1_tab, s2_tab)

    k_cmp, v_cmp_t = pl.pallas_call(
        functools.partial(_nsa_compress_kernel, ncp=ncp),
        grid=(B, G),
        in_specs=[pl.BlockSpec((None, S, HD), lambda b, g: (b, 0, NSA_COL_KC + g)),
                  pl.BlockSpec((None, S, HD), lambda b, g: (b, 0, NSA_COL_VC + g)),
                  pl.BlockSpec((2, CMP_BLOCK, HD), lambda b, g: (0, 0, 0)),
                  pl.BlockSpec((2, CMP_BLOCK, HD, HD), lambda b, g: (0, 0, 0, 0)),
                  pl.BlockSpec((2, HD, HD), lambda b, g: (0, 0, 0))],
        out_specs=[pl.BlockSpec((1, 1, ncp, HD), lambda b, g: (b, g, 0, 0)),
                   pl.BlockSpec((1, 1, HD, ncp), lambda b, g: (b, g, 0, 0))],
        out_shape=[jax.ShapeDtypeStruct((B, G, ncp, HD), BF16), jax.ShapeDtypeStruct((B, G, HD, ncp), BF16)],
        compiler_params=_cparams(("parallel", "parallel")),
        name="nsa_compress",
    )(p3, p3, cmp_pos, cmp_w1.astype(BF16), cmp_w2.astype(BF16))

    c0 = np.arange(ncp)[None, :] * CMP_STRIDE
    s0 = np.arange(nsel)[:, None] * SEL_BLOCK
    ov_t = np.clip(np.minimum(c0 + CMP_BLOCK, s0 + SEL_BLOCK) - np.maximum(c0, s0), 0, None) / CMP_BLOCK
    ov_t = ov_t * (np.arange(ncp)[None, :] < n_cmp)
    gb = jnp.pad(gate_b, (0, LANES - gate_b.shape[0])).reshape(1, LANES)
    ncols = R * tq

    def full_kv(shape):
        return pl.BlockSpec((1, 1) + shape, lambda b, g, i: (b, g, 0, 0))

    return pl.pallas_call(
        functools.partial(_nsa_attn_kernel, tq=tq, kt=kt, ncp=ncp, nsel=nsel),
        grid=(B, G, S // tq),
        in_specs=[pl.BlockSpec((1, tq, R * HD), lambda b, g, i: (b, i, g)),
                  pl.BlockSpec((1, tq, R * HD), lambda b, g, i: (b, i, g)),
                  full_kv((ncp, HD)), full_kv((HD, ncp)),
                  full_kv((S, HD)), full_kv((S, HD)), full_kv((HD, S)), full_kv((HD, S)),
                  pl.BlockSpec((1, tq, LANES), lambda b, g, i: (b, i, NSA_COL_GL)),
                  pl.BlockSpec((1, LANES), lambda b, g, i: (0, 0)),
                  pl.BlockSpec((nsel, ncp), lambda b, g, i: (0, 0))],
        out_specs=pl.BlockSpec((1, tq, R * HD), lambda b, g, i: (b, i, g)),
        out_shape=jax.ShapeDtypeStruct((B, S, NSA_HEADS * HD), F32),
        scratch_shapes=[pltpu.VMEM((nsel, tq), F32),
                        pltpu.VMEM((1, ncols), F32), pltpu.VMEM((1, ncols), F32), pltpu.VMEM((HD, ncols), F32)],
        compiler_params=_cparams(("parallel", "parallel", "arbitrary")),
        name="nsa_attn",
    )(q_s, q_r, k_cmp, v_cmp_t, ks_rot, kw_rot, vs_t, vw_t, p3, gb, jnp.asarray(ov_t, F32))


AB_COL_GATES = 50 * LANES
AB_COLS_PAD = 52 * LANES


def _ab_permute_cols(w_in):
    ml, rw = w_in[:, :ML_COLS], w_in[:, ML_COLS:]
    main, gif = ml[:, :ML_COLS - 2 * ML_HEADS], ml[:, ML_COLS - 2 * ML_HEADS:]
    w = jnp.concatenate([main, rw, gif], axis=1)
    return jnp.pad(w, ((0, 0), (0, AB_COLS_PAD - w.shape[1])))


def _mlstm_kernel(gb_ref, qk_ref, v_ref, o_ref, gcol_ref, grow_ref, cw_ref, cb_ref, y_ref,
                  halo_ref, c_ref, n_ref, m_ref, qc_ref, kc_ref, *, tb):
    H, DK, DV, L = ML_HEADS, ML_DK, ML_DV, ML_CHUNK
    HALO = SUBLANES

    @pl.when(pl.program_id(1) == 0)
    def _():
        halo_ref[...] = jnp.zeros_like(halo_ref)
        c_ref[...] = jnp.zeros_like(c_ref)
        n_ref[...] = jnp.zeros_like(n_ref)
        m_ref[...] = jnp.zeros_like(m_ref)

    x = qk_ref[0]
    xe = jnp.concatenate([halo_ref[...], x], axis=0)
    y = cb_ref[...]
    for j in range(ML_CONV):
        lo = HALO - (ML_CONV - 1) + j
        y = y + xe[lo:lo + tb] * cw_ref[j:j + 1, :]
    halo_ref[...] = x[tb - HALO:]
    y = y * jax.nn.sigmoid(y)
    qc_ref[...] = y[:, :H * DK] * (DK ** -0.5)
    kc_ref[...] = y[:, H * DK:]

    r_i = lax.broadcasted_iota(jnp.int32, (L, L), 0)
    c_i = lax.broadcasted_iota(jnp.int32, (L, L), 1)
    tri = c_i <= r_i

    def chunk(c, carry):
        rows = pl.ds(pl.multiple_of(c * L, L), L)
        for h in range(H):
            q = qc_ref[rows, h * DK:(h + 1) * DK]
            k = kc_ref[rows, h * DK:(h + 1) * DK]
            v = v_ref[0, rows, h * DV:(h + 1) * DV]
            gc = gcol_ref[0, h, rows, :]
            gr = grow_ref[0, h, c]
            i_col = gc[:, 0:1] + gb_ref[h]
            lf_col = jax.nn.log_sigmoid(gc[:, 1:2] + gb_ref[H + h])
            i_row = gr[0:1, :] + gb_ref[h]
            lf_row = jax.nn.log_sigmoid(gr[1:2, :] + gb_ref[H + h])
            b_col = jnp.sum(jnp.where(tri, lf_row, 0.0), axis=1, keepdims=True)
            b_row = jnp.sum(jnp.where(c_i >= r_i, lf_col, 0.0), axis=0, keepdims=True)
            dmat = jnp.where(tri, b_col - b_row + i_row, NSA_NEG)
            m_old = m_ref[h]
            inter = b_col + m_old
            m_row = jnp.maximum(inter, jnp.max(dmat, axis=1, keepdims=True))
            w_in = jnp.exp(dmat - m_row)
            w_st = jnp.exp(inter - m_row)
            qb = q.astype(BF16)
            s = lax.dot_general(qb, k.astype(BF16), (((1,), (1,)), ((), ())), preferred_element_type=F32) * w_in
            num = (w_st * jnp.dot(qb, c_ref[h].astype(BF16), preferred_element_type=F32)
                   + jnp.dot(s.astype(BF16), v.astype(BF16), preferred_element_type=F32))
            den = w_st * jnp.sum(q * n_ref[h], axis=1, keepdims=True) + jnp.sum(s, axis=1, keepdims=True)
            hid = num * (1.0 / jnp.maximum(jnp.abs(den), jnp.exp(-m_row)))
            y_ref[0, rows, h * DV:(h + 1) * DV] = jax.nn.sigmoid(o_ref[0, rows, h * DV:(h + 1) * DV]) * hid
            b_last = b_col[L - 1:L, :]
            g_key = b_last - b_col + i_col
            m_new = jnp.maximum(b_last + m_old, jnp.max(g_key, axis=0, keepdims=True))
            wk = jnp.exp(g_key - m_new)
            decay = jnp.exp(b_last + m_old - m_new)
            kw_t = (k * wk).T.astype(BF16)
            c_ref[h] = decay * c_ref[h] + jnp.dot(kw_t, v.astype(BF16), preferred_element_type=F32)
            n_ref[h] = decay * n_ref[h] + jnp.sum(wk * k, axis=0, keepdims=True)
            m_ref[h] = m_new
        return carry

    lax.fori_loop(0, tb // L, chunk, 0)


def _mlstm(p3, conv_w, conv_b, gate_b, tb=512):
    B, S, _ = p3.shape
    H, DK, DV, L = ML_HEADS, ML_DK, ML_DV, ML_CHUNK
    tb = min(tb, S)
    nc = S // L
    gif = p3[:, :, AB_COL_GATES:AB_COL_GATES + 2 * H].reshape(B, S, 2, H)
    gcol = gif.transpose(0, 3, 1, 2)
    grow = gif.reshape(B, nc, L, 2, H).transpose(0, 4, 1, 3, 2)
    qkw = 2 * H * DK
    return pl.pallas_call(
        functools.partial(_mlstm_kernel, tb=tb),
        grid_spec=pltpu.PrefetchScalarGridSpec(
            num_scalar_prefetch=1, grid=(B, S // tb),
            in_specs=[pl.BlockSpec((1, tb, qkw), lambda b, t, gb: (b, t, 0)),
                      pl.BlockSpec((1, tb, H * DV), lambda b, t, gb: (b, t, 1)),
                      pl.BlockSpec((1, tb, H * DV), lambda b, t, gb: (b, t, 2)),
                      pl.BlockSpec((1, H, tb, 2), lambda b, t, gb: (b, 0, t, 0)),
                      pl.BlockSpec((1, H, tb // L, 2, L), lambda b, t, gb: (b, 0, t, 0, 0)),
                      pl.BlockSpec((ML_CONV, qkw), lambda b, t, gb: (0, 0)),
                      pl.BlockSpec((1, qkw), lambda b, t, gb: (0, 0))],
            out_specs=pl.BlockSpec((1, tb, H * DV), lambda b, t, gb: (b, t, 0)),
            scratch_shapes=[pltpu.VMEM((SUBLANES, qkw), F32), pltpu.VMEM((H, DK, DV), F32),
                            pltpu.VMEM((H, 1, DK), F32), pltpu.VMEM((H, 1, 1), F32),
                            pltpu.VMEM((tb, H * DK), F32), pltpu.VMEM((tb, H * DK), F32)]),
        out_shape=jax.ShapeDtypeStruct((B, S, H * DV), F32),
        compiler_params=_cparams(("parallel", "arbitrary")),
        name="mlstm",
    )(gate_b, p3, p3, p3, gcol, grow, conv_w, conv_b.reshape(1, qkw))


def _head_sum_bcast(a, hmat):
    parts = [jnp.dot(a[:, i * LANES:(i + 1) * LANES], hmat, preferred_element_type=F32,
                     precision=lax.Precision.HIGHEST) for i in range(a.shape[1] // LANES)]
    return jnp.concatenate(parts, axis=1)


def _head_hmat():
    row = lax.broadcasted_iota(jnp.int32, (LANES, LANES), 0)
    col = lax.broadcasted_iota(jnp.int32, (LANES, LANES), 1)
    return jnp.where((row // RW_HEAD) == (col // RW_HEAD), 1.0, 0.0).astype(F32)


def _rwkv_prep_kernel(r_ref, k_ref, v_ref, lo_ref, mu_ref, w0_ref, wup_ref, a0_ref, aup_ref, gup_ref, kk_ref2, ka_ref,
                      rk_ref, ro_ref, wo_ref, ko_ref, vo_ref, kko_ref, kbo_ref, go_ref, bo_ref, prev_ref, *, tb):
    @pl.when(pl.program_id(1) == 0)
    def _():
        prev_ref[...] = jnp.zeros_like(prev_ref)

    C = RW_DIM
    x = jnp.concatenate([r_ref[0], k_ref[0], v_ref[0], lo_ref[0]], axis=1)
    prev = prev_ref[...]
    row0 = lax.broadcasted_iota(jnp.int32, (tb, 1), 0) == 0
    shifted = jnp.where(row0, prev[SUBLANES - 1:SUBLANES, :], pltpu.roll(x, 1, 0))
    prev_ref[...] = x[tb - SUBLANES:]
    xm = x + (shifted - x) * mu_ref[...]
    r, k, v, lo = xm[:, :C], xm[:, C:2 * C], xm[:, 2 * C:3 * C], xm[:, 3 * C:]
    xwa = lo[:, :LANES]
    xg = lo[:, LANES:]
    lw = jnp.dot(jnp.tanh(xwa).astype(BF16), wup_ref[...], preferred_element_type=F32)
    la = jnp.dot(xwa.astype(BF16), aup_ref[...], preferred_element_type=F32)
    g = jnp.dot(jax.nn.sigmoid(xg).astype(BF16), gup_ref[...], preferred_element_type=F32)
    w = jnp.exp(-math.exp(-0.5) * jax.nn.sigmoid(w0_ref[...] + lw))
    a = jax.nn.sigmoid(a0_ref[...] + la)
    hmat = _head_hmat()
    kk = k * kk_ref2[...]
    kk = kk * lax.rsqrt(_head_sum_bcast(kk * kk, hmat) + 1e-12)
    k2 = k * (1.0 + (a - 1.0) * ka_ref[...])
    ro_ref[0] = r
    wo_ref[0] = w
    ko_ref[0] = k2
    vo_ref[0] = v
    kko_ref[0] = kk
    kbo_ref[0] = kk * a
    go_ref[0] = g
    bo_ref[0] = _head_sum_bcast(r * k2 * rk_ref[...], hmat) * v


def _rwkv_post_kernel(y_ref, b_ref, g_ref, lnw_ref, lnb_ref, o_ref, *, tb):
    nq = RW_DIM // LANES // 2
    yrows = nq * SUBLANES
    lane = lax.broadcasted_iota(jnp.int32, (tb, LANES), 1)
    parts = []
    for q in range(nq):
        a0, a1, a2, a3 = [y_ref[pl.ds(q * SUBLANES + i, tb, stride=yrows), :] for i in range(4)]
        parts.append(jnp.where(lane < RW_HEAD, a0, pltpu.roll(a1, RW_HEAD, 1)))
        parts.append(jnp.where(lane < RW_HEAD, pltpu.roll(a2, RW_HEAD, 1), a3))
    y = jnp.concatenate(parts, axis=1)
    hmat = _head_hmat()
    mu = _head_sum_bcast(y, hmat) * (1.0 / RW_HEAD)
    yc = y - mu
    var = _head_sum_bcast(yc * yc, hmat) * (1.0 / RW_HEAD)
    yn = yc * lax.rsqrt(var + RW_LN_EPS) * lnw_ref[...] + lnb_ref[...]
    o_ref[0] = (yn + b_ref[0]) * g_ref[0]


def _rwkv_branch(p3, mu, w0, w_up, a0, a_up, g_up, k_k, k_a, r_k, ln_w, ln_b, tb=256):
    B, S, _ = p3.shape
    C = RW_DIM
    tb = min(tb, S)
    row = lambda t: t.reshape(1, -1)
    wup = jnp.concatenate([w_up, jnp.zeros_like(a_up)], axis=0).astype(BF16)
    aup = jnp.concatenate([jnp.zeros_like(w_up), a_up], axis=0).astype(BF16)
    seq = pl.BlockSpec((1, tb, C), lambda b, t: (b, t, 0))
    par = pl.BlockSpec((1, C), lambda b, t: (0, 0))
    r, w, k, v, kk, kb, g, bonus = pl.pallas_call(
        functools.partial(_rwkv_prep_kernel, tb=tb),
        grid=(B, S // tb),
        in_specs=[pl.BlockSpec((1, tb, C), lambda b, t: (b, t, 3)), pl.BlockSpec((1, tb, C), lambda b, t: (b, t, 4)),
                  pl.BlockSpec((1, tb, C), lambda b, t: (b, t, 5)),
                  pl.BlockSpec((1, tb, 2 * LANES), lambda b, t: (b, t, 24)),
                  pl.BlockSpec((1, RW_COLS), lambda b, t: (0, 0)), par,
                  pl.BlockSpec((LANES, C), lambda b, t: (0, 0)), par, pl.BlockSpec((LANES, C), lambda b, t: (0, 0)),
                  pl.BlockSpec((RW_LORA_G, C), lambda b, t: (0, 0)), par, par, par],
        out_specs=[seq] * 8,
        out_shape=[jax.ShapeDtypeStruct((B, S, C), F32)] * 8,
        scratch_shapes=[pltpu.VMEM((SUBLANES, RW_COLS), F32)],
        compiler_params=_cparams(("parallel", "arbitrary")),
        name="rwkv_prep",
    )(p3, p3, p3, p3, row(mu), row(w0), wup, row(a0), aup, g_up.astype(BF16), row(k_k), row(k_a), row(r_k))
    y_raw = _rwkv_scan(r, w, k, v, kk, kb)
    yrows = y_raw.shape[1] // S
    return pl.pallas_call(
        functools.partial(_rwkv_post_kernel, tb=tb),
        grid=(B, S // tb),
        in_specs=[pl.BlockSpec((None, tb * yrows, LANES), lambda b, t: (b, t, 0)), seq, seq, par, par],
        out_specs=seq,
        out_shape=jax.ShapeDtypeStruct((B, S, C), F32),
        compiler_params=_cparams(("parallel", "parallel")),
        name="rwkv_post",
    )(y_raw, bonus, g, row(ln_w), row(ln_b))


def _ab_mixer(x, norm_w, w_in, conv_w, conv_b, gate_b, mu, w0, w_up, a0, a_up, g_up, k_k, k_a, r_k, ln_w, ln_b,
              w_out):
    B, S, D = x.shape
    T = B * S
    x2 = x.reshape(T, D)
    p3 = _matmul(x2, _ab_permute_cols(w_in).astype(BF16), norm_w=norm_w).reshape(B, S, AB_COLS_PAD)
    y_m = _mlstm(p3, conv_w, conv_b, gate_b)
    y_r = _rwkv_branch(p3, mu, w0, w_up, a0, a_up, g_up, k_k, k_a, r_k, ln_w, ln_b)
    hm = ML_HEADS * ML_DV
    return _matmul_pair(y_m.reshape(T, hm), w_out[:hm].astype(BF16), y_r.reshape(T, RW_DIM),
                        w_out[hm:].astype(BF16), x2).reshape(B, S, D)


def _nsa(x, norm_w, w_in, gate_b, cmp_pos, cmp_w1, cmp_w2, w_out):
    B, S, D = x.shape
    T = B * S
    x2 = x.reshape(T, D)
    p = _matmul(x2, _pad_cols(w_in, 512).astype(BF16), norm_w=norm_w)
    o = _nsa_attention(p.reshape(B, S, -1), gate_b, cmp_pos, cmp_w1, cmp_w2)
    return _matmul(o.reshape(T, NSA_HEADS * NSA_HD), w_out.astype(BF16), residual=x2).reshape(B, S, D)


def _cross_attn_kernel(q_ref, k_ref, v_ref, o_ref):
    for h in range(CA_HEADS):
        cols = slice(h * CA_HD, (h + 1) * CA_HD)
        q = q_ref[:, cols].astype(BF16)
        k = k_ref[:, cols].astype(BF16)
        s = lax.dot_general(q, k, (((1,), (1,)), ((), ())), preferred_element_type=F32) * (CA_HD ** -0.5)
        e = jnp.exp(s - jnp.max(s, axis=-1, keepdims=True))
        p = e * (1.0 / jnp.sum(e, axis=-1, keepdims=True))
        o_ref[:, cols] = jnp.dot(p.astype(BF16), v_ref[:, cols].astype(BF16), preferred_element_type=F32)


def _cross_attn(x, norm_w, mem, norm_mem, wq, wk, wv, wo, tq=512):
    B, S, D = x.shape
    T = B * S
    M = mem.shape[1]
    x2 = x.reshape(T, D)
    q = _matmul(x2, wq.astype(BF16), norm_w=norm_w)
    kv = _matmul(mem.reshape(B * M, D), jnp.concatenate([wk, wv], axis=1).astype(BF16), norm_w=norm_mem)
    nq = S // tq
    o = pl.pallas_call(
        _cross_attn_kernel,
        grid=(B, nq),
        in_specs=[pl.BlockSpec((tq, D), lambda b, i: (b * nq + i, 0)),
                  pl.BlockSpec((M, D), lambda b, i: (b, 0)), pl.BlockSpec((M, D), lambda b, i: (b, 1))],
        out_specs=pl.BlockSpec((tq, D), lambda b, i: (b * nq + i, 0)),
        out_shape=jax.ShapeDtypeStruct((T, D), F32),
        compiler_params=_cparams(("parallel", "parallel")),
        name="cross_attn",
    )(q, kv, kv)
    return _matmul(o, wo.astype(BF16), residual=x2).reshape(B, S, D)


MOE_TM = 256
MOE_NEG = -1e30


def _moe_route_kernel(x_ref, nw_ref, wr_ref, br_ref, xn_ref, ri_ref, rw_ref, cnt_ref, cnt_scr, *, tm):
    @pl.when(pl.program_id(0) == 0)
    def _():
        cnt_scr[...] = jnp.zeros_like(cnt_scr)

    x = x_ref[...]
    xn = x * lax.rsqrt(jnp.mean(x * x, axis=-1, keepdims=True) + EPS) * nw_ref[...]
    xn_ref[...] = xn
    logits = jnp.dot(xn, wr_ref[...], preferred_element_type=F32, precision=lax.Precision.HIGHEST) + br_ref[...]
    lane = lax.broadcasted_iota(jnp.int32, (tm, LANES), 1)
    gmask = lane < MOE_GROUPS
    lg = jnp.where(gmask, logits, MOE_NEG)
    gmax = jnp.max(lg, axis=-1, keepdims=True)
    grp = jnp.min(jnp.where(lg == gmax, lane, LANES), axis=-1, keepdims=True)
    p_grp = 1.0 / jnp.sum(jnp.where(gmask, jnp.exp(lg - gmax), 0.0), axis=-1, keepdims=True)
    lo = MOE_GROUPS + grp * MOE_PER_GROUP
    le = jnp.where((lane >= lo) & (lane < lo + MOE_PER_GROUP), logits, MOE_NEG)
    t1 = jnp.max(le, axis=-1, keepdims=True)
    i1 = jnp.min(jnp.where(le == t1, lane, LANES), axis=-1, keepdims=True)
    le2 = jnp.where(lane == i1, MOE_NEG, le)
    t2 = jnp.max(le2, axis=-1, keepdims=True)
    i2 = jnp.min(jnp.where(le2 == t2, lane, LANES), axis=-1, keepdims=True)
    e21 = jnp.exp(t2 - t1)
    w1 = p_grp / (1.0 + e21)
    w2 = w1 * e21
    e1 = i1 - MOE_GROUPS
    e2 = i2 - MOE_GROUPS
    oh1 = jnp.where(lane == e1, 1.0, 0.0)
    oh2 = jnp.where(lane == e2, 1.0, 0.0)
    both = oh1 + oh2
    r_i = lax.broadcasted_iota(jnp.int32, (tm, tm), 0)
    c_i = lax.broadcasted_iota(jnp.int32, (tm, tm), 1)
    ltri = jnp.where(c_i < r_i, 1.0, 0.0).astype(BF16)
    before = jnp.dot(ltri, both.astype(BF16), preferred_element_type=F32) + cnt_scr[...]
    pos1 = jnp.sum(oh1 * before, axis=-1, keepdims=True).astype(jnp.int32)
    pos2 = jnp.sum(oh2 * before, axis=-1, keepdims=True).astype(jnp.int32)
    cnt_scr[...] = cnt_scr[...] + jnp.sum(both, axis=0, keepdims=True)
    ri_ref[...] = jnp.where(lane == 0, e1, jnp.where(lane == 1, e2, jnp.where(lane == 2, pos1,
                            jnp.where(lane == 3, pos2, 0))))
    rw_ref[...] = jnp.where(lane == 0, w1, jnp.where(lane == 1, w2, 0.0))
    cnt_ref[...] = cnt_scr[...]


def _moe_dest(tok, e1_ref, e2_ref, p1_ref, p2_ref, off_ref):
    return off_ref[e1_ref[tok]] + p1_ref[tok], off_ref[e2_ref[tok]] + p2_ref[tok]


def _moe_rowmap_kernel(e1_ref, e2_ref, p1_ref, p2_ref, off_ref, cnt_ref, rt_ref, *, n_tok, rows):
    def clear(i, carry):
        rt_ref[i] = 0
        return carry

    def clear_pad(e, carry):
        used = off_ref[e] + cnt_ref[e]
        end = jnp.where(e + 1 < MOE_EXPERTS, off_ref[jnp.minimum(e + 1, MOE_EXPERTS - 1)], rows)
        lax.fori_loop(used, end, clear, 0)
        return carry

    def place(t, carry):
        d1, d2 = _moe_dest(t, e1_ref, e2_ref, p1_ref, p2_ref, off_ref)
        rt_ref[d1] = t
        rt_ref[d2] = t
        return carry

    lax.fori_loop(0, MOE_EXPERTS, clear_pad, 0)
    lax.fori_loop(0, n_tok, place, 0, unroll=8)


def _moe_expert_kernel(te_ref, nu_ref, rt_ref, nxt_ref, wsl_ref, xn_hbm, wg_hbm, wu_hbm, wd_hbm, y_ref,
                       xbuf, sems, wg_buf, wu_buf, wd_buf, wsems, *, layer):
    i = pl.program_id(0)
    nu = nu_ref[0]
    slot = i % 2
    e = te_ref[i]
    wslot = wsl_ref[e]

    def weight_copies(src_e, s):
        return [pltpu.make_async_copy(hbm.at[layer, src_e], buf.at[s], wsems.at[s])
                for hbm, buf in ((wg_hbm, wg_buf), (wu_hbm, wu_buf), (wd_hbm, wd_buf))]

    @pl.when(i == 0)
    def _():
        for cp in weight_copies(e, wslot):
            cp.start()

    @pl.when((i < nu) & ((i == 0) | (te_ref[jnp.maximum(i - 1, 0)] != e)))
    def _():
        for cp in weight_copies(e, wslot):
            cp.wait()
        nxt_e = nxt_ref[e]

        @pl.when(nxt_e >= 0)
        def _():
            for cp in weight_copies(nxt_e, 1 - wslot):
                cp.start()

    def row_copy(tile, r, s):
        return pltpu.make_async_copy(xn_hbm.at[pl.ds(rt_ref[tile * MOE_TM + r], 1)], xbuf.at[s, pl.ds(r, 1)],
                                     sems.at[s])

    def drain(s):
        def body(r, carry):
            row_copy(0, 0, s).wait()
            return carry
        lax.fori_loop(0, MOE_TM, body, 0, unroll=8)

    @pl.when(i == 0)
    def _():
        def body(r, carry):
            row_copy(0, r, 0).start()
            return carry
        lax.fori_loop(0, MOE_TM, body, 0, unroll=8)

    @pl.when(i < nu)
    def _():
        drain(slot)
        nxt = jnp.minimum(i + 1, nu - 1)
        for r in range(MOE_TM):
            row_copy(nxt, r, 1 - slot).start()
        x = xbuf[slot].astype(BF16)
        gate = jnp.dot(x, wg_buf[wslot].astype(BF16), preferred_element_type=F32)
        up = jnp.dot(x, wu_buf[wslot].astype(BF16), preferred_element_type=F32)
        hid = gate * jax.nn.sigmoid(gate) * up
        y_ref[...] = jnp.dot(hid.astype(BF16), wd_buf[wslot].astype(BF16), preferred_element_type=F32)

        @pl.when(i == nu - 1)
        def _():
            drain(1 - slot)

    @pl.when(i >= nu)
    def _():
        y_ref[...] = jnp.zeros_like(y_ref)


def _moe_combine_kernel(e1_ref, e2_ref, p1_ref, p2_ref, off_ref, x_ref, rw_ref, ys_hbm, o_ref, buf1, buf2, sems, *, tm):
    i = pl.program_id(0)
    slot = i % 2

    def row_copy(src, t, buf, s):
        return pltpu.make_async_copy(ys_hbm.at[pl.ds(src, 1)], buf.at[s, pl.ds(t, 1)], sems.at[s])

    def issue_tile(tile, s):
        def body(t, carry):
            d1, d2 = _moe_dest(tile * tm + t, e1_ref, e2_ref, p1_ref, p2_ref, off_ref)
            row_copy(d1, t, buf1, s).start(priority=0)
            row_copy(d2, t, buf2, s).start(priority=1)
            return carry
        lax.fori_loop(0, tm, body, 0, unroll=4)

    def drain(t, carry):
        row_copy(0, 0, buf1, slot).wait()
        row_copy(0, 0, buf2, slot).wait()
        return carry

    @pl.when(i == 0)
    def _():
        issue_tile(0, 0)

    lax.fori_loop(0, tm, drain, 0, unroll=8)

    @pl.when(i + 1 < pl.num_programs(0))
    def _():
        issue_tile(i + 1, 1 - slot)

    w = rw_ref[...]
    o_ref[...] = x_ref[...] + w[:, 0:1] * buf1[slot] + w[:, 1:2] * buf2[slot]


def _hier_moe(x, norm_w, wg, bg, we, be, w_gate, w_up, w_down, layer):
    B, S, D = x.shape
    T = B * S
    x2 = x.reshape(T, D)
    FF = w_gate.shape[-1]
    tm_r = 512
    wr = jnp.pad(jnp.concatenate([wg, we], axis=1), ((0, 0), (0, LANES - MOE_GROUPS - MOE_EXPERTS)))
    br = jnp.pad(jnp.concatenate([bg, be]), (0, LANES - MOE_GROUPS - MOE_EXPERTS)).reshape(1, LANES)
    xn, ri, rw, cnt = pl.pallas_call(
        functools.partial(_moe_route_kernel, tm=tm_r),
        grid=(T // tm_r,),
        in_specs=[pl.BlockSpec((tm_r, D), lambda i: (i, 0)), pl.BlockSpec((1, D), lambda i: (0, 0)),
                  pl.BlockSpec((D, LANES), lambda i: (0, 0)), pl.BlockSpec((1, LANES), lambda i: (0, 0))],
        out_specs=[pl.BlockSpec((tm_r, D), lambda i: (i, 0)), pl.BlockSpec((tm_r, LANES), lambda i: (i, 0)),
                   pl.BlockSpec((tm_r, LANES), lambda i: (i, 0)), pl.BlockSpec((1, LANES), lambda i: (0, 0))],
        out_shape=[jax.ShapeDtypeStruct((T, D), F32), jax.ShapeDtypeStruct((T, LANES), jnp.int32),
                   jax.ShapeDtypeStruct((T, LANES), F32), jax.ShapeDtypeStruct((1, LANES), F32)],
        scratch_shapes=[pltpu.VMEM((1, LANES), F32)],
        compiler_params=_cparams(("arbitrary",)),
        name="moe_route",
    )(x2, norm_w.reshape(1, D), wr, br)

    counts = cnt[0, :MOE_EXPERTS].astype(jnp.int32)
    padded = (counts + MOE_TM - 1) // MOE_TM * MOE_TM
    ends = jnp.cumsum(padded)
    off = (ends - padded).astype(jnp.int32)
    n_tiles = (T * MOE_TOPK) // MOE_TM + MOE_EXPERTS
    rows = n_tiles * MOE_TM
    n_used = (ends[-1] // MOE_TM).astype(jnp.int32).reshape(1)
    tile_e = jnp.minimum(jnp.searchsorted(ends, jnp.arange(n_tiles, dtype=jnp.int32) * MOE_TM, side='right'),
                         MOE_EXPERTS - 1).astype(jnp.int32)
    e1, e2, p1, p2 = ri[:, 0], ri[:, 1], ri[:, 2], ri[:, 3]

    row_tok = pl.pallas_call(
        functools.partial(_moe_rowmap_kernel, n_tok=T, rows=rows),
        grid_spec=pltpu.PrefetchScalarGridSpec(
            num_scalar_prefetch=6, grid=(1,), in_specs=[],
            out_specs=pl.BlockSpec(memory_space=pltpu.SMEM)),
        out_shape=jax.ShapeDtypeStruct((rows,), jnp.int32),
        compiler_params=_cparams(("arbitrary",)),
        name="moe_rowmap",
    )(e1, e2, p1, p2, off, counts)

    has_rows = counts > 0
    eids = jnp.arange(MOE_EXPERTS, dtype=jnp.int32)
    later = jnp.where(has_rows, eids, MOE_EXPERTS)
    next_from = lax.cummin(later, axis=0, reverse=True)
    next_owner = jnp.concatenate([next_from[1:], jnp.full((1,), MOE_EXPERTS, jnp.int32)])
    next_owner = jnp.where(next_owner >= MOE_EXPERTS, -1, next_owner).astype(jnp.int32)
    w_slot = ((jnp.cumsum(has_rows.astype(jnp.int32)) - 1) % 2).astype(jnp.int32)

    ys = pl.pallas_call(
        functools.partial(_moe_expert_kernel, layer=layer),
        grid_spec=pltpu.PrefetchScalarGridSpec(
            num_scalar_prefetch=5, grid=(n_tiles,),
            in_specs=[pl.BlockSpec(memory_space=pl.ANY)] * 4,
            out_specs=pl.BlockSpec((MOE_TM, D), lambda i, *_: (i, 0)),
            scratch_shapes=[pltpu.VMEM((2, MOE_TM, D), F32), pltpu.SemaphoreType.DMA((2,)),
                            pltpu.VMEM((2, D, FF), F32), pltpu.VMEM((2, D, FF), F32), pltpu.VMEM((2, FF, D), F32),
                            pltpu.SemaphoreType.DMA((2,))]),
        out_shape=jax.ShapeDtypeStruct((rows, D), F32),
        compiler_params=_cparams(("arbitrary",)),
        name="moe_experts",
    )(tile_e, n_used, row_tok, next_owner, w_slot, xn, w_gate, w_up, w_down)

    tm_c = 256
    out = pl.pallas_call(
        functools.partial(_moe_combine_kernel, tm=tm_c),
        grid_spec=pltpu.PrefetchScalarGridSpec(
            num_scalar_prefetch=5, grid=(T // tm_c,),
            in_specs=[pl.BlockSpec((tm_c, D), lambda i, *_: (i, 0)), pl.BlockSpec((tm_c, LANES), lambda i, *_: (i, 0)),
                      pl.BlockSpec(memory_space=pl.ANY)],
            out_specs=pl.BlockSpec((tm_c, D), lambda i, *_: (i, 0)),
            scratch_shapes=[pltpu.VMEM((2, tm_c, D), F32), pltpu.VMEM((2, tm_c, D), F32),
                            pltpu.SemaphoreType.DMA((2,))]),
        out_shape=jax.ShapeDtypeStruct((T, D), F32),
        compiler_params=_cparams(("arbitrary",)),
        name="moe_combine",
    )(e1, e2, p1, p2, off, x2, rw, ys)
    return out.reshape(B, S, D)


def kernel(x, mem, norm_mix, norm_cross, norm_mem, norm_ffn, norm_final,
           ab_w_in, ml_conv_w, ml_conv_b, ml_gate_b, rw_mu, rw_w0, rw_w_up, rw_a0, rw_a_up, rw_g_up,
           rw_k_k, rw_k_a, rw_r_k, rw_ln_w, rw_ln_b, ab_w_out,
           nsa_w_in, nsa_gate_b, cmp_pos, cmp_w1, cmp_w2, nsa_w_out,
           ca_wq, ca_wk, ca_wv, ca_wo,
           moe_wg, moe_bg, moe_we, moe_be, moe_w_gate, moe_w_up, moe_w_down):
    B, S, D = x.shape
    for l in range(DEPTH):
        j = l // 2
        if l % 2 == 0:
            x = _ab_mixer(x, norm_mix[l], ab_w_in[j], ml_conv_w[j], ml_conv_b[j], ml_gate_b[j], rw_mu[j], rw_w0[j],
                          rw_w_up[j], rw_a0[j], rw_a_up[j], rw_g_up[j], rw_k_k[j], rw_k_a[j], rw_r_k[j],
                          rw_ln_w[j], rw_ln_b[j], ab_w_out[j])
        else:
            x = _nsa(x, norm_mix[l], nsa_w_in[j], nsa_gate_b[j], cmp_pos[j], cmp_w1[j], cmp_w2[j], nsa_w_out[j])
        x = _cross_attn(x, norm_cross[l], mem, norm_mem[l], ca_wq[l], ca_wk[l], ca_wv[l], ca_wo[l])
        x = _hier_moe(x, norm_ffn[l], moe_wg[l], moe_bg[l], moe_we[l], moe_be[l],
                      moe_w_gate, moe_w_up, moe_w_down, l)
    return _rmsnorm_rows(x.reshape(B * S, D), norm_final).reshape(B, S, D)
```

```python
import functools
import math

import jax
import jax.numpy as jnp
import numpy as np
from jax import lax
from jax.experimental import pallas as pl
from jax.experimental.pallas import tpu as pltpu

F32 = jnp.float32
BF16 = jnp.bfloat16

D_MODEL = 2048
DEPTH = 2
EPS = 1e-6
ROPE_THETA = 500000.0

ML_HEADS = 4
ML_DV = D_MODEL // 2 // ML_HEADS
ML_DK = ML_DV // 2
ML_CHUNK = 64
ML_CONV = 4
RW_HEAD = 64
RW_HEADS = D_MODEL // 2 // RW_HEAD
RW_DIM = RW_HEADS * RW_HEAD
RW_LORA_W = 64
RW_LORA_A = 64
RW_LORA_G = 128
RW_LN_EPS = 64e-5
ML_SPLITS = (2 * ML_HEADS * ML_DK, ML_HEADS * ML_DV, ML_HEADS * ML_DV, 2 * ML_HEADS)
RW_SPLITS = (RW_DIM, RW_DIM, RW_DIM, RW_LORA_W, RW_LORA_A, RW_LORA_G)
ML_COLS = sum(ML_SPLITS)
RW_COLS = sum(RW_SPLITS)

NSA_HEADS = 16
NSA_KV = 4
NSA_REP = NSA_HEADS // NSA_KV
NSA_HD = D_MODEL // NSA_HEADS
ROPE_DIM = NSA_HD // 4
CMP_BLOCK = 32
CMP_STRIDE = 16
SEL_BLOCK = 64
SEL_TOPK = 16
WINDOW = 512

CA_HEADS = 4
CA_HD = D_MODEL // CA_HEADS

MOE_GROUPS = 8
MOE_PER_GROUP = 8
MOE_EXPERTS = MOE_GROUPS * MOE_PER_GROUP
MOE_TOPK = 2
MOE_FF = D_MODEL // 4

VMEM_LIMIT = 48 * 1024 * 1024
LANES = 128
SUBLANES = 8
NT_DIMS = (((1,), (1,)), ((), ()))


def _cparams(sem):
    return pltpu.CompilerParams(dimension_semantics=sem, vmem_limit_bytes=VMEM_LIMIT)


def _mm_kernel(*refs, has_norm, has_res):
    it = iter(refs)
    x_ref = next(it)
    w_ref = next(it)
    nw_ref = next(it) if has_norm else None
    r_ref = next(it) if has_res else None
    o_ref = next(it)
    xs_ref = next(it)

    @pl.when(pl.program_id(1) == 0)
    def _():
        x = x_ref[...]
        if has_norm:
            ms = jnp.mean(x * x, axis=-1, keepdims=True)
            x = x * lax.rsqrt(ms + EPS) * nw_ref[...]
        xs_ref[...] = x.astype(BF16)

    acc = jnp.dot(xs_ref[...], w_ref[...], preferred_element_type=F32)
    if has_res:
        acc = acc + r_ref[...]
    o_ref[...] = acc


def _matmul(x, w_bf16, *, norm_w=None, residual=None, tm=1024, tn=512):
    M, K = x.shape
    N = w_bf16.shape[1]
    tm = min(tm, M)
    tn = min(tn, N)
    assert M % tm == 0 and N % tn == 0, (M, N, tm, tn)
    has_norm = norm_w is not None
    has_res = residual is not None
    in_specs = [pl.BlockSpec((tm, K), lambda i, j: (i, 0)),
                pl.BlockSpec((K, tn), lambda i, j: (0, j))]
    args = [x, w_bf16]
    if has_norm:
        in_specs.append(pl.BlockSpec((1, K), lambda i, j: (0, 0)))
        args.append(norm_w.reshape(1, K))
    if has_res:
        in_specs.append(pl.BlockSpec((tm, tn), lambda i, j: (i, j)))
        args.append(residual)
    return pl.pallas_call(
        functools.partial(_mm_kernel, has_norm=has_norm, has_res=has_res),
        grid=(M // tm, N // tn),
        in_specs=in_specs,
        out_specs=pl.BlockSpec((tm, tn), lambda i, j: (i, j)),
        out_shape=jax.ShapeDtypeStruct((M, N), F32),
        scratch_shapes=[pltpu.VMEM((tm, K), BF16)],
        compiler_params=_cparams(("parallel", "arbitrary")),
        name="mm_norm" if has_norm else "mm",
    )(*args)


def _mm2_kernel(xa_ref, xb_ref, wa_ref, wb_ref, r_ref, o_ref, xas_ref, xbs_ref):
    @pl.when(pl.program_id(1) == 0)
    def _():
        xas_ref[...] = xa_ref[...].astype(BF16)
        xbs_ref[...] = xb_ref[...].astype(BF16)

    o_ref[...] = (jnp.dot(xas_ref[...], wa_ref[...], preferred_element_type=F32)
                  + jnp.dot(xbs_ref[...], wb_ref[...], preferred_element_type=F32) + r_ref[...])


def _matmul_pair(xa, wa_bf16, xb, wb_bf16, residual, tm=1024, tn=512):
    M, Ka = xa.shape
    Kb = xb.shape[1]
    N = wa_bf16.shape[1]
    tm = min(tm, M)
    tn = min(tn, N)
    assert M % tm == 0 and N % tn == 0, (M, N, tm, tn)
    return pl.pallas_call(
        _mm2_kernel,
        grid=(M // tm, N // tn),
        in_specs=[pl.BlockSpec((tm, Ka), lambda i, j: (i, 0)), pl.BlockSpec((tm, Kb), lambda i, j: (i, 0)),
                  pl.BlockSpec((Ka, tn), lambda i, j: (0, j)), pl.BlockSpec((Kb, tn), lambda i, j: (0, j)),
                  pl.BlockSpec((tm, tn), lambda i, j: (i, j))],
        out_specs=pl.BlockSpec((tm, tn), lambda i, j: (i, j)),
        out_shape=jax.ShapeDtypeStruct((M, N), F32),
        scratch_shapes=[pltpu.VMEM((tm, Ka), BF16), pltpu.VMEM((tm, Kb), BF16)],
        compiler_params=_cparams(("parallel", "arbitrary")),
        name="mm_pair",
    )(xa, xb, wa_bf16, wb_bf16, residual)


def _pad_cols(w, mult):
    n = w.shape[1]
    pad = (-n) % mult
    if pad:
        w = jnp.pad(w, ((0, 0), (0, pad)))
    return w


def _rmsnorm_kernel(x_ref, w_ref, o_ref):
    x = x_ref[...]
    ms = jnp.mean(x * x, axis=-1, keepdims=True)
    o_ref[...] = x * lax.rsqrt(ms + EPS) * w_ref[...]


def _rmsnorm_rows(x, w, tm=512):
    M, K = x.shape
    return pl.pallas_call(
        _rmsnorm_kernel,
        grid=(M // tm,),
        in_specs=[pl.BlockSpec((tm, K), lambda i: (i, 0)), pl.BlockSpec((1, K), lambda i: (0, 0))],
        out_specs=pl.BlockSpec((tm, K), lambda i: (i, 0)),
        out_shape=jax.ShapeDtypeStruct((M, K), F32),
        compiler_params=_cparams(("parallel",)),
        name="rmsnorm",
    )(x, w.reshape(1, K))


def _rwkv_scan_kernel(r_ref, w_ref, k_ref, v_ref, kk_ref, kb_ref, y_ref, s_ref, *, tb, nb, npairs):
    @pl.when(pl.program_id(1) == 0)
    def _():
        s_ref[...] = jnp.zeros_like(s_ref)

    row = lax.broadcasted_iota(jnp.int32, (LANES, LANES), 0)
    col = lax.broadcasted_iota(jnp.int32, (LANES, LANES), 1)
    hmat = jnp.where((row // RW_HEAD) == (col // RW_HEAD), 1.0, 0.0).astype(BF16)
    vrow = lax.broadcasted_iota(jnp.int32, (RW_HEAD, LANES), 0)
    vcol = lax.broadcasted_iota(jnp.int32, (RW_HEAD, LANES), 1)
    diag = jnp.where((vcol % RW_HEAD) == vrow, 1.0, 0.0).astype(BF16)
    row16 = lax.broadcasted_iota(jnp.int32, (2 * SUBLANES, LANES), 0)
    lane16 = lax.broadcasted_iota(jnp.int32, (2 * SUBLANES, LANES), 1)
    head16 = (row16 // SUBLANES) == (lane16 // RW_HEAD)
    row8 = lax.broadcasted_iota(jnp.int32, (SUBLANES, LANES), 0)
    lane8 = lax.broadcasted_iota(jnp.int32, (SUBLANES, LANES), 1)
    nq = npairs // 2
    lns = [pl.ds(p * LANES, LANES) for p in range(npairs)]

    def group_sum(parts):
        lhs = jnp.concatenate([q.astype(BF16) for q in parts], axis=0)
        out = jnp.dot(lhs, hmat, preferred_element_type=F32)
        return [out[i * RW_HEAD:(i + 1) * RW_HEAD] for i in range(len(parts))]

    def step(t8, carry):
        rows = pl.ds(pl.multiple_of(t8 * SUBLANES, SUBLANES), SUBLANES)

        def tiles(ref):
            return [[ref[b, rows, ln] for ln in lns] for b in range(nb)]

        kk8, w8, kb8, k8, v8, r8 = (tiles(kk_ref), tiles(w_ref), tiles(kb_ref), tiles(k_ref), tiles(v_ref),
                                    tiles(r_ref))
        s = [[s_ref[b, p] for p in range(npairs)] for b in range(nb)]
        vcols = [[lax.dot_general(diag, jnp.where(head16, jnp.concatenate([v8[b][p]] * 2, axis=0), 0.0).astype(BF16),
                                  NT_DIMS, preferred_element_type=F32).astype(BF16)
                  for p in range(npairs)] for b in range(nb)]
        vk = [[None] * npairs for _ in range(nb)]
        for b in range(nb):
            for p in range(npairs):
                ksel = jnp.where(head16, jnp.concatenate([k8[b][p]] * 2, axis=0), 0.0)
                rhs = jnp.concatenate([jnp.where((row16 % SUBLANES) == j, ksel, 0.0) for j in range(SUBLANES)],
                                      axis=1).astype(BF16)
                vk[b][p] = jnp.dot(vcols[b][p], rhs, preferred_element_type=F32)
        for j in range(SUBLANES):
            sl = slice(j, j + 1)
            for b in range(nb):
                sk = group_sum([s[b][i] * kk8[b][i][sl] for i in range(npairs)])
                s[b] = [s[b][i] * w8[b][i][sl] - sk[i] * kb8[b][i][sl] + vk[b][i][:, j * LANES:(j + 1) * LANES]
                        for i in range(npairs)]
            for b in range(nb):
                for q in range(nq):
                    p0, p1 = 2 * q, 2 * q + 1
                    a = jnp.where((row8 == 0) & (lane8 < RW_HEAD), r8[b][p0][sl],
                        jnp.where((row8 == 1) & (lane8 >= RW_HEAD), r8[b][p0][sl],
                        jnp.where((row8 == 2) & (lane8 < RW_HEAD), r8[b][p1][sl],
                        jnp.where((row8 == 3) & (lane8 >= RW_HEAD), r8[b][p1][sl], 0.0))))
                    st = jnp.concatenate([s[b][p0], s[b][p1]], axis=0).astype(BF16)
                    yq = lax.dot_general(a.astype(BF16), st, NT_DIMS, preferred_element_type=F32)
                    base = ((t8 * SUBLANES + j) * nq + q) * SUBLANES
                    y_ref[b, pl.ds(pl.multiple_of(base, SUBLANES), SUBLANES), :] = yq
        for b in range(nb):
            for p in range(npairs):
                s_ref[b, p] = s[b][p]
        return carry

    lax.fori_loop(0, tb // SUBLANES, step, 0)


def _rwkv_scan(r, w, k, v, kk, kb, tb=64, nb=4):
    B, S, C = r.shape
    npairs = C // LANES
    tb = min(tb, S)
    nb = min(nb, B)
    spec = pl.BlockSpec((nb, tb, C), lambda b, t: (b, t, 0))
    yrows = (npairs // 2) * SUBLANES
    return pl.pallas_call(
        functools.partial(_rwkv_scan_kernel, tb=tb, nb=nb, npairs=npairs),
        grid=(B // nb, S // tb),
        in_specs=[spec] * 6,
        out_specs=pl.BlockSpec((nb, tb * yrows, LANES), lambda b, t: (b, t, 0)),
        out_shape=jax.ShapeDtypeStruct((B, S * yrows, LANES), F32),
        scratch_shapes=[pltpu.VMEM((nb, npairs, RW_HEAD, LANES), F32)],
        compiler_params=_cparams(("parallel", "arbitrary")),
        name="rwkv_scan",
    )(r, w, k, v, kk, kb)


NSA_NEG = -1e30
NSA_FORCED = 1e30
NSA_REMOVED = -3e30
NSA_COL_Q, NSA_COL_KC, NSA_COL_VC, NSA_COL_KS, NSA_COL_VS, NSA_COL_KW, NSA_COL_VW, NSA_COL_GL = (
    0, 16, 20, 24, 28, 32, 36, 40)


def _rope_tables(S):
    half = ROPE_DIM // 2
    inv = 1.0 / (ROPE_THETA ** (jnp.arange(half, dtype=F32) / half))
    ang = jnp.arange(S, dtype=F32)[:, None] * inv[None, :]
    cos, sin = jnp.cos(ang), jnp.sin(ang)
    one = jnp.ones((S, NSA_HD - ROPE_DIM), F32)
    zero = jnp.zeros((S, NSA_HD - ROPE_DIM), F32)
    zh = jnp.zeros((S, half), F32)
    return (jnp.concatenate([cos, cos, one], axis=1), jnp.concatenate([zh, sin, zero], axis=1),
            jnp.concatenate([-sin, zh, zero], axis=1))


def _rope_rows(x, c, s1, s2):
    half = ROPE_DIM // 2
    return x * c + pltpu.roll(x, half, 1) * s1 + pltpu.roll(x, NSA_HD - half, 1) * s2


def _nsa_prep_q_kernel(x_ref, c_ref, s1_ref, s2_ref, q_ref, qr_ref):
    c, s1, s2 = c_ref[...], s1_ref[...], s2_ref[...]
    for h in range(NSA_HEADS):
        cols = slice(h * NSA_HD, (h + 1) * NSA_HD)
        x = x_ref[0, :, cols] * (NSA_HD ** -0.5)
        q_ref[0, :, cols] = x.astype(BF16)
        qr_ref[0, :, cols] = _rope_rows(x, c, s1, s2).astype(BF16)


def _nsa_prep_kv_kernel(ks_ref, kw_ref, vs_ref, vw_ref, c_ref, s1_ref, s2_ref, kso_ref, kwo_ref, vst_ref, vwt_ref):
    c, s1, s2 = c_ref[...], s1_ref[...], s2_ref[...]
    kso_ref[0, 0] = _rope_rows(ks_ref[0], c, s1, s2).astype(BF16)
    kwo_ref[0, 0] = _rope_rows(kw_ref[0], c, s1, s2).astype(BF16)
    vst_ref[0, 0] = vs_ref[0].T.astype(BF16)
    vwt_ref[0, 0] = vw_ref[0].T.astype(BF16)


def _nsa_compress_kernel(kc_ref, vc_ref, pe_ref, w1_ref, w2_ref, kcmp_ref, vcmpt_ref, *, ncp):
    for which, x_ref in enumerate((kc_ref, vc_ref)):
        za = jnp.zeros((ncp, NSA_HD), F32)
        zb = jnp.zeros((ncp, NSA_HD), F32)
        for p in range(CMP_STRIDE):
            xp = x_ref[pl.ds(p, ncp, stride=CMP_STRIDE), :]
            za = za + jnp.dot((xp + pe_ref[which, p:p + 1, :]).astype(BF16), w1_ref[which, p],
                              preferred_element_type=F32)
            zb = zb + jnp.dot((xp + pe_ref[which, CMP_STRIDE + p:CMP_STRIDE + p + 1, :]).astype(BF16),
                              w1_ref[which, CMP_STRIDE + p], preferred_element_type=F32)
        pre = za + pltpu.roll(zb, ncp - 1, 0)
        out = jnp.dot(jax.nn.gelu(pre).astype(BF16), w2_ref[which], preferred_element_type=F32)
        if which == 0:
            kcmp_ref[0, 0] = out.astype(BF16)
        else:
            vcmpt_ref[0, 0] = out.T.astype(BF16)


def _nsa_attn_kernel(qt_ref, qrt_ref, kcmp_ref, vcmpt_ref, ks_ref, kw_ref, vst_ref, vwt_ref, gl_ref, gb_ref, ovt_ref,
                     o_ref, sel_ref, ms_ref, ls_ref, accs_ref, *, tq, kt, ncp, nsel):
    R = NSA_REP
    g = pl.program_id(1)
    qi = pl.program_id(2)
    t0 = qi * tq
    q_rows = jnp.concatenate([qt_ref[0, :, r * NSA_HD:(r + 1) * NSA_HD] for r in range(R)], axis=0)
    qr_rows = jnp.concatenate([qrt_ref[0, :, r * NSA_HD:(r + 1) * NSA_HD] for r in range(R)], axis=0)
    qpos = t0 + lax.broadcasted_iota(jnp.int32, (1, tq), 1)

    def per_head(fn, a):
        return jnp.concatenate([fn(a[:, r * tq:(r + 1) * tq]) for r in range(R)], axis=1)

    s = lax.dot_general(kcmp_ref[0, 0], q_rows, NT_DIMS, preferred_element_type=F32)
    cend = lax.broadcasted_iota(jnp.int32, (ncp, 1), 0) * CMP_STRIDE + (CMP_BLOCK - 1)
    vis = cend <= qpos
    s = per_head(lambda a: jnp.where(vis, a, NSA_NEG), s)
    m = jnp.max(s, axis=0, keepdims=True)
    e = per_head(lambda a: jnp.where(vis, a, 0.0), jnp.exp(s - m))
    d = jnp.sum(e, axis=0, keepdims=True)
    p = e * (1.0 / jnp.where(d > 0, d, 1.0))
    ocmp_t = jnp.dot(vcmpt_ref[0, 0], p.astype(BF16), preferred_element_type=F32)

    psum = p[:, 0:tq]
    for r in range(1, R):
        psum = psum + p[:, r * tq:(r + 1) * tq]
    imp = jnp.dot(ovt_ref[...], psum, preferred_element_type=F32, precision=lax.Precision.HIGHEST)
    sidx = lax.broadcasted_iota(jnp.int32, (nsel, tq), 0)
    cur = qpos // SEL_BLOCK
    forced = (sidx == 0) | (sidx == cur) | (sidx == cur - 1)
    score = jnp.where(forced, NSA_FORCED, jnp.where(sidx <= cur, imp, NSA_NEG))
    sel = jnp.zeros((nsel, tq), F32)
    for _ in range(SEL_TOPK):
        mx = jnp.max(score, axis=0, keepdims=True)
        idx = jnp.min(jnp.where(score == mx, sidx, nsel), axis=0, keepdims=True)
        pick = (sidx == idx) & (mx > 0.5 * NSA_NEG)
        sel = jnp.where(pick, 1.0, sel)
        score = jnp.where(pick, NSA_REMOVED, score)
    sel_ref[...] = sel

    def online_update(s, valid, v_t, m_ref, l_ref, acc_ref):
        s = per_head(lambda a: jnp.where(valid, a, NSA_NEG), s)
        m_old = m_ref[...]
        m_new = jnp.maximum(m_old, jnp.max(s, axis=0, keepdims=True))
        alpha = jnp.exp(m_old - m_new)
        pexp = jnp.exp(s - m_new)
        l_ref[...] = alpha * l_ref[...] + jnp.sum(pexp, axis=0, keepdims=True)
        acc_ref[...] = alpha * acc_ref[...] + jnp.dot(v_t, pexp.astype(BF16), preferred_element_type=F32)
        m_ref[...] = m_new

    ms_ref[...] = jnp.full(ms_ref.shape, NSA_NEG, F32)
    ls_ref[...] = jnp.zeros(ls_ref.shape, F32)
    accs_ref[...] = jnp.zeros(accs_ref.shape, F32)

    nblk = kt // SEL_BLOCK

    def sel_body(c, carry):
        k0 = pl.multiple_of(c * kt, kt)
        s = lax.dot_general(ks_ref[0, 0, pl.ds(k0, kt), :], qr_rows, NT_DIMS, preferred_element_type=F32)
        selrows = sel_ref[pl.ds(pl.multiple_of(c * nblk, nblk), nblk), :]
        selexp = jnp.concatenate(
            [jnp.broadcast_to(selrows[b:b + 1], (SEL_BLOCK, tq)) for b in range(nblk)], axis=0)
        kpos = k0 + lax.broadcasted_iota(jnp.int32, (kt, 1), 0)
        valid = (selexp > 0.5) & (kpos <= qpos)
        online_update(s, valid, vst_ref[0, 0, :, pl.ds(k0, kt)], ms_ref, ls_ref, accs_ref)
        return carry

    lax.fori_loop(0, t0 // kt + 1, sel_body, 0)


    nwk = WINDOW + tq
    k0 = pl.multiple_of(jnp.maximum(t0 - WINDOW, 0), tq)
    s = lax.dot_general(kw_ref[0, 0, pl.ds(k0, nwk), :], qr_rows, NT_DIMS, preferred_element_type=F32)
    dist = qpos - (k0 + lax.broadcasted_iota(jnp.int32, (nwk, 1), 0))
    inwin = (dist >= 0) & (dist < WINDOW)
    s = per_head(lambda a: jnp.where(inwin, a, NSA_NEG), s)
    ew = jnp.exp(s - jnp.max(s, axis=0, keepdims=True))
    owin_t = (jnp.dot(vwt_ref[0, 0, :, pl.ds(k0, nwk)], ew.astype(BF16), preferred_element_type=F32)
              * (1.0 / jnp.sum(ew, axis=0, keepdims=True)))

    oslc_t = accs_ref[...] * (1.0 / ls_ref[...])

    gates_t = jax.nn.sigmoid(gl_ref[0] + gb_ref[...]).T
    rid = lax.broadcasted_iota(jnp.int32, (LANES, 1), 0)

    def gate_row(which, r):
        return jnp.sum(jnp.where(rid == which * NSA_HEADS + g * R + r, gates_t, 0.0), axis=0, keepdims=True)

    for r in range(R):
        cols = slice(r * tq, (r + 1) * tq)
        o_t = (gate_row(0, r) * ocmp_t[:, cols] + gate_row(1, r) * oslc_t[:, cols]
               + gate_row(2, r) * owin_t[:, cols])
        o_ref[0, :, r * NSA_HD:(r + 1) * NSA_HD] = o_t.T


def _nsa_attention(p3, gate_b, cmp_pos, cmp_w1, cmp_w2, *, tq=128, kt=512, tp=512):
    B, S, _ = p3.shape
    G, R, HD = NSA_KV, NSA_REP, NSA_HD
    assert CMP_BLOCK == 2 * CMP_STRIDE and S % kt == 0 and WINDOW % tq == 0 and tq == LANES
    ncp = S // CMP_STRIDE
    nsel = S // SEL_BLOCK
    n_cmp = (S - CMP_BLOCK) // CMP_STRIDE + 1
    c_tab, s1_tab, s2_tab = _rope_tables(S)
    tab_spec3 = pl.BlockSpec((tp, HD), lambda b, h, i: (i, 0))

    qw = NSA_HEADS * HD
    tab_spec2 = pl.BlockSpec((tp, HD), lambda b, i: (i, 0))
    q_s, q_r = pl.pallas_call(
        _nsa_prep_q_kernel,
        grid=(B, S // tp),
        in_specs=[pl.BlockSpec((1, tp, qw), lambda b, i: (b, i, 0)), tab_spec2, tab_spec2, tab_spec2],
        out_specs=[pl.BlockSpec((1, tp, qw), lambda b, i: (b, i, 0))] * 2,
        out_shape=[jax.ShapeDtypeStruct((B, S, qw), BF16)] * 2,
        compiler_params=_cparams(("parallel", "parallel")),
        name="nsa_prep_q",
    )(p3, c_tab, s1_tab, s2_tab)

    def col_spec(col0):
        return pl.BlockSpec((1, tp, HD), lambda b, g, i: (b, i, col0 + g))

    ks_rot, kw_rot, vs_t, vw_t = pl.pallas_call(
        _nsa_prep_kv_kernel,
        grid=(B, G, S // tp),
        in_specs=[col_spec(NSA_COL_KS), col_spec(NSA_COL_KW), col_spec(NSA_COL_VS), col_spec(NSA_COL_VW),
                  tab_spec3, tab_spec3, tab_spec3],
        out_specs=[pl.BlockSpec((1, 1, tp, HD), lambda b, g, i: (b, g, i, 0))] * 2
        + [pl.BlockSpec((1, 1, HD, tp), lambda b, g, i: (b, g, 0, i))] * 2,
        out_shape=[jax.ShapeDtypeStruct((B, G, S, HD), BF16)] * 2 + [jax.ShapeDtypeStruct((B, G, HD, S), BF16)] * 2,
        compiler_params=_cparams(("parallel", "parallel", "parallel")),
        name="nsa_prep_kv",
    )(p3, p3, p3, p3, c_tab, s1_tab, s2_tab)

    k_cmp, v_cmp_t = pl.pallas_call(
        functools.partial(_nsa_compress_kernel, ncp=ncp),
        grid=(B, G),
        in_specs=[pl.BlockSpec((None, S, HD), lambda b, g: (b, 0, NSA_COL_KC + g)),
                  pl.BlockSpec((None, S, HD), lambda b, g: (b, 0, NSA_COL_VC + g)),
                  pl.BlockSpec((2, CMP_BLOCK, HD), lambda b, g: (0, 0, 0)),
                  pl.BlockSpec((2, CMP_BLOCK, HD, HD), lambda b, g: (0, 0, 0, 0)),
                  pl.BlockSpec((2, HD, HD), lambda b, g: (0, 0, 0))],
        out_specs=[pl.BlockSpec((1, 1, ncp, HD), lambda b, g: (b, g, 0, 0)),
                   pl.BlockSpec((1, 1, HD, ncp), lambda b, g: (b, g, 0, 0))],
        out_shape=[jax.ShapeDtypeStruct((B, G, ncp, HD), BF16), jax.ShapeDtypeStruct((B, G, HD, ncp), BF16)],
        compiler_params=_cparams(("parallel", "parallel")),
        name="nsa_compress",
    )(p3, p3, cmp_pos, cmp_w1.astype(BF16), cmp_w2.astype(BF16))

    c0 = np.arange(ncp)[None, :] * CMP_STRIDE
    s0 = np.arange(nsel)[:, None] * SEL_BLOCK
    ov_t = np.clip(np.minimum(c0 + CMP_BLOCK, s0 + SEL_BLOCK) - np.maximum(c0, s0), 0, None) / CMP_BLOCK
    ov_t = ov_t * (np.arange(ncp)[None, :] < n_cmp)
    gb = jnp.pad(gate_b, (0, LANES - gate_b.shape[0])).reshape(1, LANES)
    ncols = R * tq

    def full_kv(shape):
        return pl.BlockSpec((1, 1) + shape, lambda b, g, i: (b, g, 0, 0))

    return pl.pallas_call(
        functools.partial(_nsa_attn_kernel, tq=tq, kt=kt, ncp=ncp, nsel=nsel),
        grid=(B, G, S // tq),
        in_specs=[pl.BlockSpec((1, tq, R * HD), lambda b, g, i: (b, i, g)),
                  pl.BlockSpec((1, tq, R * HD), lambda b, g, i: (b, i, g)),
                  full_kv((ncp, HD)), full_kv((HD, ncp)),
                  full_kv((S, HD)), full_kv((S, HD)), full_kv((HD, S)), full_kv((HD, S)),
                  pl.BlockSpec((1, tq, LANES), lambda b, g, i: (b, i, NSA_COL_GL)),
                  pl.BlockSpec((1, LANES), lambda b, g, i: (0, 0)),
                  pl.BlockSpec((nsel, ncp), lambda b, g, i: (0, 0))],
        out_specs=pl.BlockSpec((1, tq, R * HD), lambda b, g, i: (b, i, g)),
        out_shape=jax.ShapeDtypeStruct((B, S, NSA_HEADS * HD), F32),
        scratch_shapes=[pltpu.VMEM((nsel, tq), F32),
                        pltpu.VMEM((1, ncols), F32), pltpu.VMEM((1, ncols), F32), pltpu.VMEM((HD, ncols), F32)],
        compiler_params=_cparams(("parallel", "parallel", "arbitrary")),
        name="nsa_attn",
    )(q_s, q_r, k_cmp, v_cmp_t, ks_rot, kw_rot, vs_t, vw_t, p3, gb, jnp.asarray(ov_t, F32))


AB_COL_GATES = 50 * LANES
AB_COLS_PAD = 52 * LANES


def _ab_permute_cols(w_in):
    ml, rw = w_in[:, :ML_COLS], w_in[:, ML_COLS:]
    main, gif = ml[:, :ML_COLS - 2 * ML_HEADS], ml[:, ML_COLS - 2 * ML_HEADS:]
    w = jnp.concatenate([main, rw, gif], axis=1)
    return jnp.pad(w, ((0, 0), (0, AB_COLS_PAD - w.shape[1])))


def _mlstm_kernel(gb_ref, qk_ref, v_ref, o_ref, gcol_ref, grow_ref, cw_ref, cb_ref, y_ref,
                  halo_ref, c_ref, n_ref, m_ref, qc_ref, kc_ref, *, tb, nb):
    H, DK, DV, L = ML_HEADS, ML_DK, ML_DV, ML_CHUNK
    HALO = SUBLANES

    @pl.when(pl.program_id(1) == 0)
    def _():
        halo_ref[...] = jnp.zeros_like(halo_ref)
        c_ref[...] = jnp.zeros_like(c_ref)
        n_ref[...] = jnp.zeros_like(n_ref)
        m_ref[...] = jnp.zeros_like(m_ref)

    for b in range(nb):
        x = qk_ref[b]
        xe = jnp.concatenate([halo_ref[b], x], axis=0)
        y = cb_ref[...]
        for j in range(ML_CONV):
            lo = HALO - (ML_CONV - 1) + j
            y = y + xe[lo:lo + tb] * cw_ref[j:j + 1, :]
        halo_ref[b] = x[tb - HALO:]
        y = y * jax.nn.sigmoid(y)
        qc_ref[b] = y[:, :H * DK] * (DK ** -0.5)
        kc_ref[b] = y[:, H * DK:]

    r_i = lax.broadcasted_iota(jnp.int32, (L, L), 0)
    c_i = lax.broadcasted_iota(jnp.int32, (L, L), 1)
    tri = c_i <= r_i

    def chunk(c, carry):
        rows = pl.ds(pl.multiple_of(c * L, L), L)
        chains = [(b, h) for b in range(nb) for h in range(H)]
        n = len(chains)
        q = [qc_ref[b, rows, h * DK:(h + 1) * DK] for b, h in chains]
        k = [kc_ref[b, rows, h * DK:(h + 1) * DK] for b, h in chains]
        v = [v_ref[b, rows, h * DV:(h + 1) * DV].astype(BF16) for b, h in chains]
        gc = [gcol_ref[b, h, rows, :] for b, h in chains]
        gr = [grow_ref[b, h, c] for b, h in chains]
        i_col = [gc[j][:, 0:1] + gb_ref[h] for j, (b, h) in enumerate(chains)]
        lf_col = [jax.nn.log_sigmoid(gc[j][:, 1:2] + gb_ref[H + h]) for j, (b, h) in enumerate(chains)]
        i_row = [gr[j][0:1, :] + gb_ref[h] for j, (b, h) in enumerate(chains)]
        lf_row = [jax.nn.log_sigmoid(gr[j][1:2, :] + gb_ref[H + h]) for j, (b, h) in enumerate(chains)]
        b_col = [jnp.sum(jnp.where(tri, lf_row[j], 0.0), axis=1, keepdims=True) for j in range(n)]
        b_row = [jnp.sum(jnp.where(c_i >= r_i, lf_col[j], 0.0), axis=0, keepdims=True) for j in range(n)]
        dmat = [jnp.where(tri, b_col[j] - b_row[j] + i_row[j], NSA_NEG) for j in range(n)]
        m_old = [m_ref[b * H + h] for b, h in chains]
        inter = [b_col[j] + m_old[j] for j in range(n)]
        m_row = [jnp.maximum(inter[j], jnp.max(dmat[j], axis=1, keepdims=True)) for j in range(n)]
        w_in = [jnp.exp(dmat[j] - m_row[j]) for j in range(n)]
        w_st = [jnp.exp(inter[j] - m_row[j]) for j in range(n)]
        qb = [q[j].astype(BF16) for j in range(n)]
        s = [lax.dot_general(qb[j], k[j].astype(BF16), NT_DIMS, preferred_element_type=F32) * w_in[j]
             for j in range(n)]
        qc_state = [jnp.dot(qb[j], c_ref[b * H + h].astype(BF16), preferred_element_type=F32)
                    for j, (b, h) in enumerate(chains)]
        sv = [jnp.dot(s[j].astype(BF16), v[j], preferred_element_type=F32) for j in range(n)]
        for j, (b, h) in enumerate(chains):
            den = (w_st[j] * jnp.sum(q[j] * n_ref[b * H + h], axis=1, keepdims=True)
                   + jnp.sum(s[j], axis=1, keepdims=True))
            hid = (w_st[j] * qc_state[j] + sv[j]) * (1.0 / jnp.maximum(jnp.abs(den), jnp.exp(-m_row[j])))
            y_ref[b, rows, h * DV:(h + 1) * DV] = jax.nn.sigmoid(o_ref[b, rows, h * DV:(h + 1) * DV]) * hid
        b_last = [b_col[j][L - 1:L, :] for j in range(n)]
        g_key = [b_last[j] - b_col[j] + i_col[j] for j in range(n)]
        m_new = [jnp.maximum(b_last[j] + m_old[j], jnp.max(g_key[j], axis=0, keepdims=True)) for j in range(n)]
        wk = [jnp.exp(g_key[j] - m_new[j]) for j in range(n)]
        kv = [jnp.dot((k[j] * wk[j]).T.astype(BF16), v[j], preferred_element_type=F32) for j in range(n)]
        for j, (b, h) in enumerate(chains):
            bh = b * H + h
            decay = jnp.exp(b_last[j] + m_old[j] - m_new[j])
            c_ref[bh] = decay * c_ref[bh] + kv[j]
            n_ref[bh] = decay * n_ref[bh] + jnp.sum(wk[j] * k[j], axis=0, keepdims=True)
            m_ref[bh] = m_new[j]
        return carry

    lax.fori_loop(0, tb // L, chunk, 0)


def _mlstm(p3, conv_w, conv_b, gate_b, tb=256, nb=2):
    B, S, _ = p3.shape
    H, DK, DV, L = ML_HEADS, ML_DK, ML_DV, ML_CHUNK
    tb = min(tb, S)
    nb = min(nb, B)
    nc = S // L
    gif = p3[:, :, AB_COL_GATES:AB_COL_GATES + 2 * H].reshape(B, S, 2, H)
    gcol = gif.transpose(0, 3, 1, 2)
    grow = gif.reshape(B, nc, L, 2, H).transpose(0, 4, 1, 3, 2)
    qkw = 2 * H * DK
    return pl.pallas_call(
        functools.partial(_mlstm_kernel, tb=tb, nb=nb),
        grid_spec=pltpu.PrefetchScalarGridSpec(
            num_scalar_prefetch=1, grid=(B // nb, S // tb),
            in_specs=[pl.BlockSpec((nb, tb, qkw), lambda b, t, gb: (b, t, 0)),
                      pl.BlockSpec((nb, tb, H * DV), lambda b, t, gb: (b, t, 1)),
                      pl.BlockSpec((nb, tb, H * DV), lambda b, t, gb: (b, t, 2)),
                      pl.BlockSpec((nb, H, tb, 2), lambda b, t, gb: (b, 0, t, 0)),
                      pl.BlockSpec((nb, H, tb // L, 2, L), lambda b, t, gb: (b, 0, t, 0, 0)),
                      pl.BlockSpec((ML_CONV, qkw), lambda b, t, gb: (0, 0)),
                      pl.BlockSpec((1, qkw), lambda b, t, gb: (0, 0))],
            out_specs=pl.BlockSpec((nb, tb, H * DV), lambda b, t, gb: (b, t, 0)),
            scratch_shapes=[pltpu.VMEM((nb, SUBLANES, qkw), F32), pltpu.VMEM((nb * H, DK, DV), F32),
                            pltpu.VMEM((nb * H, 1, DK), F32), pltpu.VMEM((nb * H, 1, 1), F32),
                            pltpu.VMEM((nb, tb, H * DK), F32), pltpu.VMEM((nb, tb, H * DK), F32)]),
        out_shape=jax.ShapeDtypeStruct((B, S, H * DV), F32),
        compiler_params=_cparams(("parallel", "arbitrary")),
        name="mlstm",
    )(gate_b, p3, p3, p3, gcol, grow, conv_w, conv_b.reshape(1, qkw))


def _head_sum_bcast(a, hmat):
    parts = [jnp.dot(a[:, i * LANES:(i + 1) * LANES], hmat, preferred_element_type=F32,
                     precision=lax.Precision.HIGHEST) for i in range(a.shape[1] // LANES)]
    return jnp.concatenate(parts, axis=1)


def _head_hmat():
    row = lax.broadcasted_iota(jnp.int32, (LANES, LANES), 0)
    col = lax.broadcasted_iota(jnp.int32, (LANES, LANES), 1)
    return jnp.where((row // RW_HEAD) == (col // RW_HEAD), 1.0, 0.0).astype(F32)


def _rwkv_prep_kernel(r_ref, k_ref, v_ref, lo_ref, mu_ref, w0_ref, wup_ref, a0_ref, aup_ref, gup_ref, kk_ref2, ka_ref,
                      rk_ref, ro_ref, wo_ref, ko_ref, vo_ref, kko_ref, kbo_ref, go_ref, bo_ref, prev_ref, *, tb):
    @pl.when(pl.program_id(1) == 0)
    def _():
        prev_ref[...] = jnp.zeros_like(prev_ref)

    C = RW_DIM
    x = jnp.concatenate([r_ref[0], k_ref[0], v_ref[0], lo_ref[0]], axis=1)
    prev = prev_ref[...]
    row0 = lax.broadcasted_iota(jnp.int32, (tb, 1), 0) == 0
    shifted = jnp.where(row0, prev[SUBLANES - 1:SUBLANES, :], pltpu.roll(x, 1, 0))
    prev_ref[...] = x[tb - SUBLANES:]
    xm = x + (shifted - x) * mu_ref[...]
    r, k, v, lo = xm[:, :C], xm[:, C:2 * C], xm[:, 2 * C:3 * C], xm[:, 3 * C:]
    xwa = lo[:, :LANES]
    xg = lo[:, LANES:]
    lw = jnp.dot(jnp.tanh(xwa).astype(BF16), wup_ref[...], preferred_element_type=F32)
    la = jnp.dot(xwa.astype(BF16), aup_ref[...], preferred_element_type=F32)
    g = jnp.dot(jax.nn.sigmoid(xg).astype(BF16), gup_ref[...], preferred_element_type=F32)
    w = jnp.exp(-math.exp(-0.5) * jax.nn.sigmoid(w0_ref[...] + lw))
    a = jax.nn.sigmoid(a0_ref[...] + la)
    hmat = _head_hmat()
    kk = k * kk_ref2[...]
    kk = kk * lax.rsqrt(_head_sum_bcast(kk * kk, hmat) + 1e-12)
    k2 = k * (1.0 + (a - 1.0) * ka_ref[...])
    ro_ref[0] = r
    wo_ref[0] = w
    ko_ref[0] = k2
    vo_ref[0] = v
    kko_ref[0] = kk
    kbo_ref[0] = kk * a
    go_ref[0] = g
    bo_ref[0] = _head_sum_bcast(r * k2 * rk_ref[...], hmat) * v


def _rwkv_post_kernel(y_ref, b_ref, g_ref, lnw_ref, lnb_ref, o_ref, *, tb):
    nq = RW_DIM // LANES // 2
    yrows = nq * SUBLANES
    lane = lax.broadcasted_iota(jnp.int32, (tb, LANES), 1)
    parts = []
    for q in range(nq):
        a0, a1, a2, a3 = [y_ref[pl.ds(q * SUBLANES + i, tb, stride=yrows), :] for i in range(4)]
        parts.append(jnp.where(lane < RW_HEAD, a0, pltpu.roll(a1, RW_HEAD, 1)))
        parts.append(jnp.where(lane < RW_HEAD, pltpu.roll(a2, RW_HEAD, 1), a3))
    y = jnp.concatenate(parts, axis=1)
    hmat = _head_hmat()
    mu = _head_sum_bcast(y, hmat) * (1.0 / RW_HEAD)
    yc = y - mu
    var = _head_sum_bcast(yc * yc, hmat) * (1.0 / RW_HEAD)
    yn = yc * lax.rsqrt(var + RW_LN_EPS) * lnw_ref[...] + lnb_ref[...]
    o_ref[0] = (yn + b_ref[0]) * g_ref[0]


def _rwkv_branch(p3, mu, w0, w_up, a0, a_up, g_up, k_k, k_a, r_k, ln_w, ln_b, tb=256):
    B, S, _ = p3.shape
    C = RW_DIM
    tb = min(tb, S)
    row = lambda t: t.reshape(1, -1)
    wup = jnp.concatenate([w_up, jnp.zeros_like(a_up)], axis=0).astype(BF16)
    aup = jnp.concatenate([jnp.zeros_like(w_up), a_up], axis=0).astype(BF16)
    seq = pl.BlockSpec((1, tb, C), lambda b, t: (b, t, 0))
    par = pl.BlockSpec((1, C), lambda b, t: (0, 0))
    r, w, k, v, kk, kb, g, bonus = pl.pallas_call(
        functools.partial(_rwkv_prep_kernel, tb=tb),
        grid=(B, S // tb),
        in_specs=[pl.BlockSpec((1, tb, C), lambda b, t: (b, t, 3)), pl.BlockSpec((1, tb, C), lambda b, t: (b, t, 4)),
                  pl.BlockSpec((1, tb, C), lambda b, t: (b, t, 5)),
                  pl.BlockSpec((1, tb, 2 * LANES), lambda b, t: (b, t, 24)),
                  pl.BlockSpec((1, RW_COLS), lambda b, t: (0, 0)), par,
                  pl.BlockSpec((LANES, C), lambda b, t: (0, 0)), par, pl.BlockSpec((LANES, C), lambda b, t: (0, 0)),
                  pl.BlockSpec((RW_LORA_G, C), lambda b, t: (0, 0)), par, par, par],
        out_specs=[seq] * 8,
        out_shape=[jax.ShapeDtypeStruct((B, S, C), F32)] * 8,
        scratch_shapes=[pltpu.VMEM((SUBLANES, RW_COLS), F32)],
        compiler_params=_cparams(("parallel", "arbitrary")),
        name="rwkv_prep",
    )(p3, p3, p3, p3, row(mu), row(w0), wup, row(a0), aup, g_up.astype(BF16), row(k_k), row(k_a), row(r_k))
    y_raw = _rwkv_scan(r, w, k, v, kk, kb)
    yrows = y_raw.shape[1] // S
    return pl.pallas_call(
        functools.partial(_rwkv_post_kernel, tb=tb),
        grid=(B, S // tb),
        in_specs=[pl.BlockSpec((None, tb * yrows, LANES), lambda b, t: (b, t, 0)), seq, seq, par, par],
        out_specs=seq,
        out_shape=jax.ShapeDtypeStruct((B, S, C), F32),
        compiler_params=_cparams(("parallel", "parallel")),
        name="rwkv_post",
    )(y_raw, bonus, g, row(ln_w), row(ln_b))


def _ab_mixer(x, norm_w, w_in, conv_w, conv_b, gate_b, mu, w0, w_up, a0, a_up, g_up, k_k, k_a, r_k, ln_w, ln_b,
              w_out):
    B, S, D = x.shape
    T = B * S
    x2 = x.reshape(T, D)
    p3 = _matmul(x2, _ab_permute_cols(w_in).astype(BF16), norm_w=norm_w).reshape(B, S, AB_COLS_PAD)
    y_m = _mlstm(p3, conv_w, conv_b, gate_b)
    y_r = _rwkv_branch(p3, mu, w0, w_up, a0, a_up, g_up, k_k, k_a, r_k, ln_w, ln_b)
    hm = ML_HEADS * ML_DV
    return _matmul_pair(y_m.reshape(T, hm), w_out[:hm].astype(BF16), y_r.reshape(T, RW_DIM),
                        w_out[hm:].astype(BF16), x2).reshape(B, S, D)


def _nsa(x, norm_w, w_in, gate_b, cmp_pos, cmp_w1, cmp_w2, w_out):
    B, S, D = x.shape
    T = B * S
    x2 = x.reshape(T, D)
    p = _matmul(x2, _pad_cols(w_in, 512).astype(BF16), norm_w=norm_w)
    o = _nsa_attention(p.reshape(B, S, -1), gate_b, cmp_pos, cmp_w1, cmp_w2)
    return _matmul(o.reshape(T, NSA_HEADS * NSA_HD), w_out.astype(BF16), residual=x2).reshape(B, S, D)


def _cross_attn_kernel(q_ref, k_ref, v_ref, o_ref):
    for h in range(CA_HEADS):
        cols = slice(h * CA_HD, (h + 1) * CA_HD)
        q = q_ref[:, cols].astype(BF16)
        k = k_ref[:, cols].astype(BF16)
        s = lax.dot_general(q, k, (((1,), (1,)), ((), ())), preferred_element_type=F32) * (CA_HD ** -0.5)
        e = jnp.exp(s - jnp.max(s, axis=-1, keepdims=True))
        p = e * (1.0 / jnp.sum(e, axis=-1, keepdims=True))
        o_ref[:, cols] = jnp.dot(p.astype(BF16), v_ref[:, cols].astype(BF16), preferred_element_type=F32)


def _cross_attn(x, norm_w, mem, norm_mem, wq, wk, wv, wo, tq=512):
    B, S, D = x.shape
    T = B * S
    M = mem.shape[1]
    x2 = x.reshape(T, D)
    q = _matmul(x2, wq.astype(BF16), norm_w=norm_w)
    kv = _matmul(mem.reshape(B * M, D), jnp.concatenate([wk, wv], axis=1).astype(BF16), norm_w=norm_mem)
    nq = S // tq
    o = pl.pallas_call(
        _cross_attn_kernel,
        grid=(B, nq),
        in_specs=[pl.BlockSpec((tq, D), lambda b, i: (b * nq + i, 0)),
                  pl.BlockSpec((M, D), lambda b, i: (b, 0)), pl.BlockSpec((M, D), lambda b, i: (b, 1))],
        out_specs=pl.BlockSpec((tq, D), lambda b, i: (b * nq + i, 0)),
        out_shape=jax.ShapeDtypeStruct((T, D), F32),
        compiler_params=_cparams(("parallel", "parallel")),
        name="cross_attn",
    )(q, kv, kv)
    return _matmul(o, wo.astype(BF16), residual=x2).reshape(B, S, D)


MOE_TM = 256
MOE_NEG = -1e30


def _moe_route_kernel(x_ref, nw_ref, wr_ref, br_ref, xn_ref, ri_ref, rw_ref, cnt_ref, cnt_scr, *, tm):
    @pl.when(pl.program_id(0) == 0)
    def _():
        cnt_scr[...] = jnp.zeros_like(cnt_scr)

    x = x_ref[...]
    xn = x * lax.rsqrt(jnp.mean(x * x, axis=-1, keepdims=True) + EPS) * nw_ref[...]
    xn_ref[...] = xn
    logits = jnp.dot(xn, wr_ref[...], preferred_element_type=F32, precision=lax.Precision.HIGHEST) + br_ref[...]
    lane = lax.broadcasted_iota(jnp.int32, (tm, LANES), 1)
    gmask = lane < MOE_GROUPS
    lg = jnp.where(gmask, logits, MOE_NEG)
    gmax = jnp.max(lg, axis=-1, keepdims=True)
    grp = jnp.min(jnp.where(lg == gmax, lane, LANES), axis=-1, keepdims=True)
    p_grp = 1.0 / jnp.sum(jnp.where(gmask, jnp.exp(lg - gmax), 0.0), axis=-1, keepdims=True)
    lo = MOE_GROUPS + grp * MOE_PER_GROUP
    le = jnp.where((lane >= lo) & (lane < lo + MOE_PER_GROUP), logits, MOE_NEG)
    t1 = jnp.max(le, axis=-1, keepdims=True)
    i1 = jnp.min(jnp.where(le == t1, lane, LANES), axis=-1, keepdims=True)
    le2 = jnp.where(lane == i1, MOE_NEG, le)
    t2 = jnp.max(le2, axis=-1, keepdims=True)
    i2 = jnp.min(jnp.where(le2 == t2, lane, LANES), axis=-1, keepdims=True)
    e21 = jnp.exp(t2 - t1)
    w1 = p_grp / (1.0 + e21)
    w2 = w1 * e21
    e1 = i1 - MOE_GROUPS
    e2 = i2 - MOE_GROUPS
    oh1 = jnp.where(lane == e1, 1.0, 0.0)
    oh2 = jnp.where(lane == e2, 1.0, 0.0)
    both = oh1 + oh2
    r_i = lax.broadcasted_iota(jnp.int32, (tm, tm), 0)
    c_i = lax.broadcasted_iota(jnp.int32, (tm, tm), 1)
    ltri = jnp.where(c_i < r_i, 1.0, 0.0).astype(BF16)
    before = jnp.dot(ltri, both.astype(BF16), preferred_element_type=F32) + cnt_scr[...]
    pos1 = jnp.sum(oh1 * before, axis=-1, keepdims=True).astype(jnp.int32)
    pos2 = jnp.sum(oh2 * before, axis=-1, keepdims=True).astype(jnp.int32)
    cnt_scr[...] = cnt_scr[...] + jnp.sum(both, axis=0, keepdims=True)
    ri_ref[...] = jnp.where(lane == 0, e1, jnp.where(lane == 1, e2, jnp.where(lane == 2, pos1,
                            jnp.where(lane == 3, pos2, 0))))
    rw_ref[...] = jnp.where(lane == 0, w1, jnp.where(lane == 1, w2, 0.0))
    cnt_ref[...] = cnt_scr[...]


def _moe_dest(tok, e1_ref, e2_ref, p1_ref, p2_ref, off_ref):
    return off_ref[e1_ref[tok]] + p1_ref[tok], off_ref[e2_ref[tok]] + p2_ref[tok]


def _moe_rowmap_kernel(e1_ref, e2_ref, p1_ref, p2_ref, off_ref, cnt_ref, rt_ref, *, n_tok, rows):
    def clear(i, carry):
        rt_ref[i] = 0
        return carry

    def clear_pad(e, carry):
        used = off_ref[e] + cnt_ref[e]
        end = jnp.where(e + 1 < MOE_EXPERTS, off_ref[jnp.minimum(e + 1, MOE_EXPERTS - 1)], rows)
        lax.fori_loop(used, end, clear, 0)
        return carry

    def place(t, carry):
        d1, d2 = _moe_dest(t, e1_ref, e2_ref, p1_ref, p2_ref, off_ref)
        rt_ref[d1] = t
        rt_ref[d2] = t
        return carry

    lax.fori_loop(0, MOE_EXPERTS, clear_pad, 0)
    lax.fori_loop(0, n_tok, place, 0, unroll=8)


def _moe_expert_kernel(te_ref, nu_ref, rt_ref, xn_hbm, wg_ref, wu_ref, wd_ref, y_ref, xbuf, sems):
    del te_ref
    i = pl.program_id(0)
    nu = nu_ref[0]
    slot = i % 2

    def row_copy(tile, r, s):
        return pltpu.make_async_copy(xn_hbm.at[pl.ds(rt_ref[tile * MOE_TM + r], 1)], xbuf.at[s, pl.ds(r, 1)],
                                     sems.at[s])

    def drain(s):
        def body(r, carry):
            row_copy(0, 0, s).wait()
            return carry
        lax.fori_loop(0, MOE_TM, body, 0, unroll=8)

    @pl.when(i == 0)
    def _():
        def body(r, carry):
            row_copy(0, r, 0).start()
            return carry
        lax.fori_loop(0, MOE_TM, body, 0, unroll=8)

    @pl.when(i < nu)
    def _():
        drain(slot)
        nxt = jnp.minimum(i + 1, nu - 1)
        for r in range(MOE_TM):
            row_copy(nxt, r, 1 - slot).start()
        x = xbuf[slot].astype(BF16)
        gate = jnp.dot(x, wg_ref[...].astype(BF16), preferred_element_type=F32)
        up = jnp.dot(x, wu_ref[...].astype(BF16), preferred_element_type=F32)
        hid = gate * jax.nn.sigmoid(gate) * up
        y_ref[...] = jnp.dot(hid.astype(BF16), wd_ref[...].astype(BF16), preferred_element_type=F32)

        @pl.when(i == nu - 1)
        def _():
            drain(1 - slot)

    @pl.when(i >= nu)
    def _():
        y_ref[...] = jnp.zeros_like(y_ref)


def _moe_combine_kernel(e1_ref, e2_ref, p1_ref, p2_ref, off_ref, x_ref, rw_ref, ys_hbm, o_ref, buf1, buf2, sems, *, tm):
    i = pl.program_id(0)
    slot = i % 2

    def row_copy(src, t, buf, s):
        return pltpu.make_async_copy(ys_hbm.at[pl.ds(src, 1)], buf.at[s, pl.ds(t, 1)], sems.at[s])

    def issue_tile(tile, s):
        def body(t, carry):
            d1, d2 = _moe_dest(tile * tm + t, e1_ref, e2_ref, p1_ref, p2_ref, off_ref)
            row_copy(d1, t, buf1, s).start(priority=0)
            row_copy(d2, t, buf2, s).start(priority=1)
            return carry
        lax.fori_loop(0, tm, body, 0, unroll=4)

    def drain(t, carry):
        row_copy(0, 0, buf1, slot).wait()
        row_copy(0, 0, buf2, slot).wait()
        return carry

    @pl.when(i == 0)
    def _():
        issue_tile(0, 0)

    lax.fori_loop(0, tm, drain, 0, unroll=8)

    @pl.when(i + 1 < pl.num_programs(0))
    def _():
        issue_tile(i + 1, 1 - slot)

    w = rw_ref[...]
    o_ref[...] = x_ref[...] + w[:, 0:1] * buf1[slot] + w[:, 1:2] * buf2[slot]


def _hier_moe(x, norm_w, wg, bg, we, be, w_gate, w_up, w_down, layer):
    B, S, D = x.shape
    T = B * S
    x2 = x.reshape(T, D)
    FF = w_gate.shape[-1]
    tm_r = 512
    wr = jnp.pad(jnp.concatenate([wg, we], axis=1), ((0, 0), (0, LANES - MOE_GROUPS - MOE_EXPERTS)))
    br = jnp.pad(jnp.concatenate([bg, be]), (0, LANES - MOE_GROUPS - MOE_EXPERTS)).reshape(1, LANES)
    xn, ri, rw, cnt = pl.pallas_call(
        functools.partial(_moe_route_kernel, tm=tm_r),
        grid=(T // tm_r,),
        in_specs=[pl.BlockSpec((tm_r, D), lambda i: (i, 0)), pl.BlockSpec((1, D), lambda i: (0, 0)),
                  pl.BlockSpec((D, LANES), lambda i: (0, 0)), pl.BlockSpec((1, LANES), lambda i: (0, 0))],
        out_specs=[pl.BlockSpec((tm_r, D), lambda i: (i, 0)), pl.BlockSpec((tm_r, LANES), lambda i: (i, 0)),
                   pl.BlockSpec((tm_r, LANES), lambda i: (i, 0)), pl.BlockSpec((1, LANES), lambda i: (0, 0))],
        out_shape=[jax.ShapeDtypeStruct((T, D), F32), jax.ShapeDtypeStruct((T, LANES), jnp.int32),
                   jax.ShapeDtypeStruct((T, LANES), F32), jax.ShapeDtypeStruct((1, LANES), F32)],
        scratch_shapes=[pltpu.VMEM((1, LANES), F32)],
        compiler_params=_cparams(("arbitrary",)),
        name="moe_route",
    )(x2, norm_w.reshape(1, D), wr, br)

    counts = cnt[0, :MOE_EXPERTS].astype(jnp.int32)
    padded = (counts + MOE_TM - 1) // MOE_TM * MOE_TM
    ends = jnp.cumsum(padded)
    off = (ends - padded).astype(jnp.int32)
    n_tiles = (T * MOE_TOPK) // MOE_TM + MOE_EXPERTS
    rows = n_tiles * MOE_TM
    n_used = (ends[-1] // MOE_TM).astype(jnp.int32).reshape(1)
    tile_e = jnp.minimum(jnp.searchsorted(ends, jnp.arange(n_tiles, dtype=jnp.int32) * MOE_TM, side='right'),
                         MOE_EXPERTS - 1).astype(jnp.int32)
    e1, e2, p1, p2 = ri[:, 0], ri[:, 1], ri[:, 2], ri[:, 3]

    row_tok = pl.pallas_call(
        functools.partial(_moe_rowmap_kernel, n_tok=T, rows=rows),
        grid_spec=pltpu.PrefetchScalarGridSpec(
            num_scalar_prefetch=6, grid=(1,), in_specs=[],
            out_specs=pl.BlockSpec(memory_space=pltpu.SMEM)),
        out_shape=jax.ShapeDtypeStruct((rows,), jnp.int32),
        compiler_params=_cparams(("arbitrary",)),
        name="moe_rowmap",
    )(e1, e2, p1, p2, off, counts)

    def w_ix(i, te, nu, rt):
        return (layer, te[jnp.minimum(i, nu[0] - 1)], 0, 0)

    ys = pl.pallas_call(
        _moe_expert_kernel,
        grid_spec=pltpu.PrefetchScalarGridSpec(
            num_scalar_prefetch=3, grid=(n_tiles,),
            in_specs=[pl.BlockSpec(memory_space=pl.ANY),
                      pl.BlockSpec((None, None, D, FF), w_ix), pl.BlockSpec((None, None, D, FF), w_ix),
                      pl.BlockSpec((None, None, FF, D), w_ix)],
            out_specs=pl.BlockSpec((MOE_TM, D), lambda i, te, nu, rt: (i, 0)),
            scratch_shapes=[pltpu.VMEM((2, MOE_TM, D), F32), pltpu.SemaphoreType.DMA((2,))]),
        out_shape=jax.ShapeDtypeStruct((rows, D), F32),
        compiler_params=_cparams(("arbitrary",)),
        name="moe_experts",
    )(tile_e, n_used, row_tok, xn, w_gate, w_up, w_down)

    tm_c = 256
    out = pl.pallas_call(
        functools.partial(_moe_combine_kernel, tm=tm_c),
        grid_spec=pltpu.PrefetchScalarGridSpec(
            num_scalar_prefetch=5, grid=(T // tm_c,),
            in_specs=[pl.BlockSpec((tm_c, D), lambda i, *_: (i, 0)), pl.BlockSpec((tm_c, LANES), lambda i, *_: (i, 0)),
                      pl.BlockSpec(memory_space=pl.ANY)],
            out_specs=pl.BlockSpec((tm_c, D), lambda i, *_: (i, 0)),
            scratch_shapes=[pltpu.VMEM((2, tm_c, D), F32), pltpu.VMEM((2, tm_c, D), F32),
                            pltpu.SemaphoreType.DMA((2,))]),
        out_shape=jax.ShapeDtypeStruct((T, D), F32),
        compiler_params=_cparams(("arbitrary",)),
        name="moe_combine",
    )(e1, e2, p1, p2, off, x2, rw, ys)
    return out.reshape(B, S, D)


def kernel(x, mem, norm_mix, norm_cross, norm_mem, norm_ffn, norm_final,
           ab_w_in, ml_conv_w, ml_conv_b, ml_gate_b, rw_mu, rw_w0, rw_w_up, rw_a0, rw_a_up, rw_g_up,
           rw_k_k, rw_k_a, rw_r_k, rw_ln_w, rw_ln_b, ab_w_out,
           nsa_w_in, nsa_gate_b, cmp_pos, cmp_w1, cmp_w2, nsa_w_out,
           ca_wq, ca_wk, ca_wv, ca_wo,
           moe_wg, moe_bg, moe_we, moe_be, moe_w_gate, moe_w_up, moe_w_down):
    B, S, D = x.shape
    for l in range(DEPTH):
        j = l // 2
        if l % 2 == 0:
            x = _ab_mixer(x, norm_mix[l], ab_w_in[j], ml_conv_w[j], ml_conv_b[j], ml_gate_b[j], rw_mu[j], rw_w0[j],
                          rw_w_up[j], rw_a0[j], rw_a_up[j], rw_g_up[j], rw_k_k[j], rw_k_a[j], rw_r_k[j],
                          rw_ln_w[j], rw_ln_b[j], ab_w_out[j])
        else:
            x = _nsa(x, norm_mix[l], nsa_w_in[j], nsa_gate_b[j], cmp_pos[j], cmp_w1[j], cmp_w2[j], nsa_w_out[j])
        x = _cross_attn(x, norm_cross[l], mem, norm_mem[l], ca_wq[l], ca_wk[l], ca_wv[l], ca_wo[l])
        x = _hier_moe(x, norm_ffn[l], moe_wg[l], moe_bg[l], moe_we[l], moe_be[l],
                      moe_w_gate, moe_w_up, moe_w_down, l)
    return _rmsnorm_rows(x.reshape(B * S, D), norm_final).reshape(B, S, D)
```

```python
import functools
import math

import jax
import jax.numpy as jnp
import numpy as np
from jax import lax
from jax.experimental import pallas as pl
from jax.experimental.pallas import tpu as pltpu

F32 = jnp.float32
BF16 = jnp.bfloat16

D_MODEL = 2048
DEPTH = 2
EPS = 1e-6
ROPE_THETA = 500000.0

ML_HEADS = 4
ML_DV = D_MODEL // 2 // ML_HEADS
ML_DK = ML_DV // 2
ML_CHUNK = 64
ML_CONV = 4
RW_HEAD = 64
RW_HEADS = D_MODEL // 2 // RW_HEAD
RW_DIM = RW_HEADS * RW_HEAD
RW_LORA_W = 64
RW_LORA_A = 64
RW_LORA_G = 128
RW_LN_EPS = 64e-5
ML_SPLITS = (2 * ML_HEADS * ML_DK, ML_HEADS * ML_DV, ML_HEADS * ML_DV, 2 * ML_HEADS)
RW_SPLITS = (RW_DIM, RW_DIM, RW_DIM, RW_LORA_W, RW_LORA_A, RW_LORA_G)
ML_COLS = sum(ML_SPLITS)
RW_COLS = sum(RW_SPLITS)

NSA_HEADS = 16
NSA_KV = 4
NSA_REP = NSA_HEADS // NSA_KV
NSA_HD = D_MODEL // NSA_HEADS
ROPE_DIM = NSA_HD // 4
CMP_BLOCK = 32
CMP_STRIDE = 16
SEL_BLOCK = 64
SEL_TOPK = 16
WINDOW = 512

CA_HEADS = 4
CA_HD = D_MODEL // CA_HEADS

MOE_GROUPS = 8
MOE_PER_GROUP = 8
MOE_EXPERTS = MOE_GROUPS * MOE_PER_GROUP
MOE_TOPK = 2
MOE_FF = D_MODEL // 4

VMEM_LIMIT = 48 * 1024 * 1024
LANES = 128
SUBLANES = 8
NT_DIMS = (((1,), (1,)), ((), ()))


def _cparams(sem):
    return pltpu.CompilerParams(dimension_semantics=sem, vmem_limit_bytes=VMEM_LIMIT)


def _mm_kernel(*refs, has_norm, has_res):
    it = iter(refs)
    x_ref = next(it)
    w_ref = next(it)
    nw_ref = next(it) if has_norm else None
    r_ref = next(it) if has_res else None
    o_ref = next(it)
    xs_ref = next(it)

    @pl.when(pl.program_id(1) == 0)
    def _():
        x = x_ref[...]
        if has_norm:
            ms = jnp.mean(x * x, axis=-1, keepdims=True)
            x = x * lax.rsqrt(ms + EPS) * nw_ref[...]
        xs_ref[...] = x.astype(BF16)

    acc = jnp.dot(xs_ref[...], w_ref[...], preferred_element_type=F32)
    if has_res:
        acc = acc + r_ref[...]
    o_ref[...] = acc


def _matmul(x, w_bf16, *, norm_w=None, residual=None, tm=1024, tn=512):
    M, K = x.shape
    N = w_bf16.shape[1]
    tm = min(tm, M)
    tn = min(tn, N)
    assert M % tm == 0 and N % tn == 0, (M, N, tm, tn)
    has_norm = norm_w is not None
    has_res = residual is not None
    in_specs = [pl.BlockSpec((tm, K), lambda i, j: (i, 0)),
                pl.BlockSpec((K, tn), lambda i, j: (0, j))]
    args = [x, w_bf16]
    if has_norm:
        in_specs.append(pl.BlockSpec((1, K), lambda i, j: (0, 0)))
        args.append(norm_w.reshape(1, K))
    if has_res:
        in_specs.append(pl.BlockSpec((tm, tn), lambda i, j: (i, j)))
        args.append(residual)
    return pl.pallas_call(
        functools.partial(_mm_kernel, has_norm=has_norm, has_res=has_res),
        grid=(M // tm, N // tn),
        in_specs=in_specs,
        out_specs=pl.BlockSpec((tm, tn), lambda i, j: (i, j)),
        out_shape=jax.ShapeDtypeStruct((M, N), F32),
        scratch_shapes=[pltpu.VMEM((tm, K), BF16)],
        compiler_params=_cparams(("parallel", "arbitrary")),
        name="mm_norm" if has_norm else "mm",
    )(*args)


def _mm2_kernel(xa_ref, xb_ref, wa_ref, wb_ref, r_ref, o_ref, xas_ref, xbs_ref):
    @pl.when(pl.program_id(1) == 0)
    def _():
        xas_ref[...] = xa_ref[...].astype(BF16)
        xbs_ref[...] = xb_ref[...].astype(BF16)

    o_ref[...] = (jnp.dot(xas_ref[...], wa_ref[...], preferred_element_type=F32)
                  + jnp.dot(xbs_ref[...], wb_ref[...], preferred_element_type=F32) + r_ref[...])


def _matmul_pair(xa, wa_bf16, xb, wb_bf16, residual, tm=1024, tn=512):
    M, Ka = xa.shape
    Kb = xb.shape[1]
    N = wa_bf16.shape[1]
    tm = min(tm, M)
    tn = min(tn, N)
    assert M % tm == 0 and N % tn == 0, (M, N, tm, tn)
    return pl.pallas_call(
        _mm2_kernel,
        grid=(M // tm, N // tn),
        in_specs=[pl.BlockSpec((tm, Ka), lambda i, j: (i, 0)), pl.BlockSpec((tm, Kb), lambda i, j: (i, 0)),
                  pl.BlockSpec((Ka, tn), lambda i, j: (0, j)), pl.BlockSpec((Kb, tn), lambda i, j: (0, j)),
                  pl.BlockSpec((tm, tn), lambda i, j: (i, j))],
        out_specs=pl.BlockSpec((tm, tn), lambda i, j: (i, j)),
        out_shape=jax.ShapeDtypeStruct((M, N), F32),
        scratch_shapes=[pltpu.VMEM((tm, Ka), BF16), pltpu.VMEM((tm, Kb), BF16)],
        compiler_params=_cparams(("parallel", "arbitrary")),
        name="mm_pair",
    )(xa, xb, wa_bf16, wb_bf16, residual)


def _pad_cols(w, mult):
    n = w.shape[1]
    pad = (-n) % mult
    if pad:
        w = jnp.pad(w, ((0, 0), (0, pad)))
    return w


def _rmsnorm_kernel(x_ref, w_ref, o_ref):
    x = x_ref[...]
    ms = jnp.mean(x * x, axis=-1, keepdims=True)
    o_ref[...] = x * lax.rsqrt(ms + EPS) * w_ref[...]


def _rmsnorm_rows(x, w, tm=512):
    M, K = x.shape
    return pl.pallas_call(
        _rmsnorm_kernel,
        grid=(M // tm,),
        in_specs=[pl.BlockSpec((tm, K), lambda i: (i, 0)), pl.BlockSpec((1, K), lambda i: (0, 0))],
        out_specs=pl.BlockSpec((tm, K), lambda i: (i, 0)),
        out_shape=jax.ShapeDtypeStruct((M, K), F32),
        compiler_params=_cparams(("parallel",)),
        name="rmsnorm",
    )(x, w.reshape(1, K))


def _rwkv_scan_kernel(r_ref, w_ref, k_ref, v_ref, kk_ref, kb_ref, y_ref, s_ref, *, tb, nb, npairs):
    @pl.when(pl.program_id(1) == 0)
    def _():
        s_ref[...] = jnp.zeros_like(s_ref)

    row = lax.broadcasted_iota(jnp.int32, (LANES, LANES), 0)
    col = lax.broadcasted_iota(jnp.int32, (LANES, LANES), 1)
    hmat = jnp.where((row // RW_HEAD) == (col // RW_HEAD), 1.0, 0.0).astype(BF16)
    vrow = lax.broadcasted_iota(jnp.int32, (RW_HEAD, LANES), 0)
    vcol = lax.broadcasted_iota(jnp.int32, (RW_HEAD, LANES), 1)
    diag = jnp.where((vcol % RW_HEAD) == vrow, 1.0, 0.0).astype(BF16)
    row16 = lax.broadcasted_iota(jnp.int32, (2 * SUBLANES, LANES), 0)
    lane16 = lax.broadcasted_iota(jnp.int32, (2 * SUBLANES, LANES), 1)
    head16 = (row16 // SUBLANES) == (lane16 // RW_HEAD)
    row8 = lax.broadcasted_iota(jnp.int32, (SUBLANES, LANES), 0)
    lane8 = lax.broadcasted_iota(jnp.int32, (SUBLANES, LANES), 1)
    nq = npairs // 2
    lns = [pl.ds(p * LANES, LANES) for p in range(npairs)]

    def group_sum(parts):
        lhs = jnp.concatenate([q.astype(BF16) for q in parts], axis=0)
        out = jnp.dot(lhs, hmat, preferred_element_type=F32)
        return [out[i * RW_HEAD:(i + 1) * RW_HEAD] for i in range(len(parts))]

    def step(t8, carry):
        rows = pl.ds(pl.multiple_of(t8 * SUBLANES, SUBLANES), SUBLANES)

        def tiles(ref):
            return [[ref[b, rows, ln] for ln in lns] for b in range(nb)]

        kk8, w8, kb8, k8, v8, r8 = (tiles(kk_ref), tiles(w_ref), tiles(kb_ref), tiles(k_ref), tiles(v_ref),
                                    tiles(r_ref))
        s = [[s_ref[b, p] for p in range(npairs)] for b in range(nb)]
        vcols = [[lax.dot_general(diag, jnp.where(head16, jnp.concatenate([v8[b][p]] * 2, axis=0), 0.0).astype(BF16),
                                  NT_DIMS, preferred_element_type=F32).astype(BF16)
                  for p in range(npairs)] for b in range(nb)]
        vk = [[None] * npairs for _ in range(nb)]
        for b in range(nb):
            for p in range(npairs):
                ksel = jnp.where(head16, jnp.concatenate([k8[b][p]] * 2, axis=0), 0.0)
                rhs = jnp.concatenate([jnp.where((row16 % SUBLANES) == j, ksel, 0.0) for j in range(SUBLANES)],
                                      axis=1).astype(BF16)
                vk[b][p] = jnp.dot(vcols[b][p], rhs, preferred_element_type=F32)
        for j in range(SUBLANES):
            sl = slice(j, j + 1)
            for b in range(nb):
                sk = group_sum([s[b][i] * kk8[b][i][sl] for i in range(npairs)])
                s[b] = [s[b][i] * w8[b][i][sl] - sk[i] * kb8[b][i][sl] + vk[b][i][:, j * LANES:(j + 1) * LANES]
                        for i in range(npairs)]
            for b in range(nb):
                for q in range(nq):
                    p0, p1 = 2 * q, 2 * q + 1
                    a = jnp.where((row8 == 0) & (lane8 < RW_HEAD), r8[b][p0][sl],
                        jnp.where((row8 == 1) & (lane8 >= RW_HEAD), r8[b][p0][sl],
                        jnp.where((row8 == 2) & (lane8 < RW_HEAD), r8[b][p1][sl],
                        jnp.where((row8 == 3) & (lane8 >= RW_HEAD), r8[b][p1][sl], 0.0))))
                    st = jnp.concatenate([s[b][p0], s[b][p1]], axis=0).astype(BF16)
                    yq = lax.dot_general(a.astype(BF16), st, NT_DIMS, preferred_element_type=F32)
                    base = ((t8 * SUBLANES + j) * nq + q) * SUBLANES
                    y_ref[b, pl.ds(pl.multiple_of(base, SUBLANES), SUBLANES), :] = yq
        for b in range(nb):
            for p in range(npairs):
                s_ref[b, p] = s[b][p]
        return carry

    lax.fori_loop(0, tb // SUBLANES, step, 0)


def _rwkv_scan(r, w, k, v, kk, kb, tb=64, nb=4):
    B, S, C = r.shape
    npairs = C // LANES
    tb = min(tb, S)
    nb = min(nb, B)
    spec = pl.BlockSpec((nb, tb, C), lambda b, t: (b, t, 0))
    yrows = (npairs // 2) * SUBLANES
    return pl.pallas_call(
        functools.partial(_rwkv_scan_kernel, tb=tb, nb=nb, npairs=npairs),
        grid=(B // nb, S // tb),
        in_specs=[spec] * 6,
        out_specs=pl.BlockSpec((nb, tb * yrows, LANES), lambda b, t: (b, t, 0)),
        out_shape=jax.ShapeDtypeStruct((B, S * yrows, LANES), F32),
        scratch_shapes=[pltpu.VMEM((nb, npairs, RW_HEAD, LANES), F32)],
        compiler_params=_cparams(("parallel", "arbitrary")),
        name="rwkv_scan",
    )(r, w, k, v, kk, kb)


NSA_NEG = -1e30
NSA_FORCED = 1e30
NSA_REMOVED = -3e30
NSA_COL_Q, NSA_COL_KC, NSA_COL_VC, NSA_COL_KS, NSA_COL_VS, NSA_COL_KW, NSA_COL_VW, NSA_COL_GL = (
    0, 16, 20, 24, 28, 32, 36, 40)


def _rope_tables(S):
    half = ROPE_DIM // 2
    inv = 1.0 / (ROPE_THETA ** (jnp.arange(half, dtype=F32) / half))
    ang = jnp.arange(S, dtype=F32)[:, None] * inv[None, :]
    cos, sin = jnp.cos(ang), jnp.sin(ang)
    one = jnp.ones((S, NSA_HD - ROPE_DIM), F32)
    zero = jnp.zeros((S, NSA_HD - ROPE_DIM), F32)
    zh = jnp.zeros((S, half), F32)
    return (jnp.concatenate([cos, cos, one], axis=1), jnp.concatenate([zh, sin, zero], axis=1),
            jnp.concatenate([-sin, zh, zero], axis=1))


def _rope_rows(x, c, s1, s2):
    half = ROPE_DIM // 2
    return x * c + pltpu.roll(x, half, 1) * s1 + pltpu.roll(x, NSA_HD - half, 1) * s2


def _nsa_prep_q_kernel(x_ref, c_ref, s1_ref, s2_ref, q_ref, qr_ref):
    c, s1, s2 = c_ref[...], s1_ref[...], s2_ref[...]
    for h in range(NSA_HEADS):
        cols = slice(h * NSA_HD, (h + 1) * NSA_HD)
        x = x_ref[0, :, cols] * (NSA_HD ** -0.5)
        q_ref[0, :, cols] = x.astype(BF16)
        qr_ref[0, :, cols] = _rope_rows(x, c, s1, s2).astype(BF16)


def _nsa_prep_kv_kernel(ks_ref, kw_ref, vs_ref, vw_ref, c_ref, s1_ref, s2_ref, kso_ref, kwo_ref, vst_ref, vwt_ref):
    c, s1, s2 = c_ref[...], s1_ref[...], s2_ref[...]
    kso_ref[0, 0] = _rope_rows(ks_ref[0], c, s1, s2).astype(BF16)
    kwo_ref[0, 0] = _rope_rows(kw_ref[0], c, s1, s2).astype(BF16)
    vst_ref[0, 0] = vs_ref[0].T.astype(BF16)
    vwt_ref[0, 0] = vw_ref[0].T.astype(BF16)


def _nsa_compress_kernel(kc_ref, vc_ref, pe_ref, w1_ref, w2_ref, kcmp_ref, vcmpt_ref, *, ncp):
    for which, x_ref in enumerate((kc_ref, vc_ref)):
        za = jnp.zeros((ncp, NSA_HD), F32)
        zb = jnp.zeros((ncp, NSA_HD), F32)
        for p in range(CMP_STRIDE):
            xp = x_ref[pl.ds(p, ncp, stride=CMP_STRIDE), :]
            za = za + jnp.dot((xp + pe_ref[which, p:p + 1, :]).astype(BF16), w1_ref[which, p],
                              preferred_element_type=F32)
            zb = zb + jnp.dot((xp + pe_ref[which, CMP_STRIDE + p:CMP_STRIDE + p + 1, :]).astype(BF16),
                              w1_ref[which, CMP_STRIDE + p], preferred_element_type=F32)
        pre = za + pltpu.roll(zb, ncp - 1, 0)
        out = jnp.dot(jax.nn.gelu(pre).astype(BF16), w2_ref[which], preferred_element_type=F32)
        if which == 0:
            kcmp_ref[0, 0] = out.astype(BF16)
        else:
            vcmpt_ref[0, 0] = out.T.astype(BF16)


def _nsa_attn_kernel(qt_ref, qrt_ref, kcmp_ref, vcmpt_ref, ks_ref, kw_ref, vst_ref, vwt_ref, gl_ref, gb_ref, ovt_ref,
                     o_ref, sel_ref, ms_ref, ls_ref, accs_ref, *, tq, kt, ncp, nsel):
    R = NSA_REP
    g = pl.program_id(1)
    qi = pl.program_id(2)
    t0 = qi * tq
    q_rows = jnp.concatenate([qt_ref[0, :, r * NSA_HD:(r + 1) * NSA_HD] for r in range(R)], axis=0)
    qr_rows = jnp.concatenate([qrt_ref[0, :, r * NSA_HD:(r + 1) * NSA_HD] for r in range(R)], axis=0)
    qpos = t0 + lax.broadcasted_iota(jnp.int32, (1, tq), 1)

    def per_head(fn, a):
        return jnp.concatenate([fn(a[:, r * tq:(r + 1) * tq]) for r in range(R)], axis=1)

    nwk = WINDOW + tq
    kw0 = pl.multiple_of(jnp.maximum(t0 - WINDOW, 0), tq)
    dist = qpos - (kw0 + lax.broadcasted_iota(jnp.int32, (nwk, 1), 0))
    inwin = (dist >= 0) & (dist < WINDOW)

    def window_head(r):
        sw = lax.dot_general(kw_ref[0, 0, pl.ds(kw0, nwk), :], qr_rows[r * tq:(r + 1) * tq], NT_DIMS,
                             preferred_element_type=F32)
        sw = jnp.where(inwin, sw, NSA_NEG)
        ew = jnp.exp(sw - jnp.max(sw, axis=0, keepdims=True))
        return (jnp.dot(vwt_ref[0, 0, :, pl.ds(kw0, nwk)], ew.astype(BF16), preferred_element_type=F32)
                * (1.0 / jnp.sum(ew, axis=0, keepdims=True)))

    s = lax.dot_general(kcmp_ref[0, 0], q_rows, NT_DIMS, preferred_element_type=F32)
    cend = lax.broadcasted_iota(jnp.int32, (ncp, 1), 0) * CMP_STRIDE + (CMP_BLOCK - 1)
    vis = cend <= qpos
    s = per_head(lambda a: jnp.where(vis, a, NSA_NEG), s)
    m = jnp.max(s, axis=0, keepdims=True)
    e = per_head(lambda a: jnp.where(vis, a, 0.0), jnp.exp(s - m))
    d = jnp.sum(e, axis=0, keepdims=True)
    p = e * (1.0 / jnp.where(d > 0, d, 1.0))
    ocmp_t = jnp.dot(vcmpt_ref[0, 0], p.astype(BF16), preferred_element_type=F32)

    psum = p[:, 0:tq]
    for r in range(1, R):
        psum = psum + p[:, r * tq:(r + 1) * tq]
    imp = jnp.dot(ovt_ref[...], psum, preferred_element_type=F32, precision=lax.Precision.HIGHEST)
    sidx = lax.broadcasted_iota(jnp.int32, (nsel, tq), 0)
    cur = qpos // SEL_BLOCK
    forced = (sidx == 0) | (sidx == cur) | (sidx == cur - 1)
    score = jnp.where(forced, NSA_FORCED, jnp.where(sidx <= cur, imp, NSA_NEG))
    sel = jnp.zeros((nsel, tq), F32)
    owin_heads = []
    for it in range(SEL_TOPK):
        mx = jnp.max(score, axis=0, keepdims=True)
        idx = jnp.min(jnp.where(score == mx, sidx, nsel), axis=0, keepdims=True)
        pick = (sidx == idx) & (mx > 0.5 * NSA_NEG)
        sel = jnp.where(pick, 1.0, sel)
        score = jnp.where(pick, NSA_REMOVED, score)
        if it % (SEL_TOPK // R) == 0:
            owin_heads.append(window_head(it // (SEL_TOPK // R)))
    owin_t = jnp.concatenate(owin_heads, axis=1)
    sel_ref[...] = sel

    def online_update(s, valid, v_t, m_ref, l_ref, acc_ref):
        s = per_head(lambda a: jnp.where(valid, a, NSA_NEG), s)
        m_old = m_ref[...]
        m_new = jnp.maximum(m_old, jnp.max(s, axis=0, keepdims=True))
        alpha = jnp.exp(m_old - m_new)
        pexp = jnp.exp(s - m_new)
        l_ref[...] = alpha * l_ref[...] + jnp.sum(pexp, axis=0, keepdims=True)
        acc_ref[...] = alpha * acc_ref[...] + jnp.dot(v_t, pexp.astype(BF16), preferred_element_type=F32)
        m_ref[...] = m_new

    ms_ref[...] = jnp.full(ms_ref.shape, NSA_NEG, F32)
    ls_ref[...] = jnp.zeros(ls_ref.shape, F32)
    accs_ref[...] = jnp.zeros(accs_ref.shape, F32)

    nblk = kt // SEL_BLOCK

    def sel_body(c, carry):
        k0 = pl.multiple_of(c * kt, kt)
        s = lax.dot_general(ks_ref[0, 0, pl.ds(k0, kt), :], qr_rows, NT_DIMS, preferred_element_type=F32)
        selrows = sel_ref[pl.ds(pl.multiple_of(c * nblk, nblk), nblk), :]
        selexp = jnp.concatenate(
            [jnp.broadcast_to(selrows[b:b + 1], (SEL_BLOCK, tq)) for b in range(nblk)], axis=0)
        kpos = k0 + lax.broadcasted_iota(jnp.int32, (kt, 1), 0)
        valid = (selexp > 0.5) & (kpos <= qpos)
        online_update(s, valid, vst_ref[0, 0, :, pl.ds(k0, kt)], ms_ref, ls_ref, accs_ref)
        return carry

    lax.fori_loop(0, t0 // kt + 1, sel_body, 0)
    oslc_t = accs_ref[...] * (1.0 / ls_ref[...])

    gates_t = jax.nn.sigmoid(gl_ref[0] + gb_ref[...]).T
    rid = lax.broadcasted_iota(jnp.int32, (LANES, 1), 0)

    def gate_row(which, r):
        return jnp.sum(jnp.where(rid == which * NSA_HEADS + g * R + r, gates_t, 0.0), axis=0, keepdims=True)

    for r in range(R):
        cols = slice(r * tq, (r + 1) * tq)
        o_t = (gate_row(0, r) * ocmp_t[:, cols] + gate_row(1, r) * oslc_t[:, cols]
               + gate_row(2, r) * owin_t[:, cols])
        o_ref[0, :, r * NSA_HD:(r + 1) * NSA_HD] = o_t.T


def _nsa_attention(p3, gate_b, cmp_pos, cmp_w1, cmp_w2, *, tq=128, kt=512, tp=512):
    B, S, _ = p3.shape
    G, R, HD = NSA_KV, NSA_REP, NSA_HD
    assert CMP_BLOCK == 2 * CMP_STRIDE and S % kt == 0 and WINDOW % tq == 0 and tq == LANES
    ncp = S // CMP_STRIDE
    nsel = S // SEL_BLOCK
    n_cmp = (S - CMP_BLOCK) // CMP_STRIDE + 1
    c_tab, s1_tab, s2_tab = _rope_tables(S)
    tab_spec3 = pl.BlockSpec((tp, HD), lambda b, h, i: (i, 0))

    qw = NSA_HEADS * HD
    tab_spec2 = pl.BlockSpec((tp, HD), lambda b, i: (i, 0))
    q_s, q_r = pl.pallas_call(
        _nsa_prep_q_kernel,
        grid=(B, S // tp),
        in_specs=[pl.BlockSpec((1, tp, qw), lambda b, i: (b, i, 0)), tab_spec2, tab_spec2, tab_spec2],
        out_specs=[pl.BlockSpec((1, tp, qw), lambda b, i: (b, i, 0))] * 2,
        out_shape=[jax.ShapeDtypeStruct((B, S, qw), BF16)] * 2,
        compiler_params=_cparams(("parallel", "parallel")),
        name="nsa_prep_q",
    )(p3, c_tab, s1_tab, s2_tab)

    def col_spec(col0):
        return pl.BlockSpec((1, tp, HD), lambda b, g, i: (b, i, col0 + g))

    ks_rot, kw_rot, vs_t, vw_t = pl.pallas_call(
        _nsa_prep_kv_kernel,
        grid=(B, G, S // tp),
        in_specs=[col_spec(NSA_COL_KS), col_spec(NSA_COL_KW), col_spec(NSA_COL_VS), col_spec(NSA_COL_VW),
                  tab_spec3, tab_spec3, tab_spec3],
        out_specs=[pl.BlockSpec((1, 1, tp, HD), lambda b, g, i: (b, g, i, 0))] * 2
        + [pl.BlockSpec((1, 1, HD, tp), lambda b, g, i: (b, g, 0, i))] * 2,
        out_shape=[jax.ShapeDtypeStruct((B, G, S, HD), BF16)] * 2 + [jax.ShapeDtypeStruct((B, G, HD, S), BF16)] * 2,
        compiler_params=_cparams(("parallel", "parallel", "parallel")),
        name="nsa_prep_kv",
    )(p3, p3, p3, p3, c_tab, s1_tab, s2_tab)

    k_cmp, v_cmp_t = pl.pallas_call(
        functools.partial(_nsa_compress_kernel, ncp=ncp),
        grid=(B, G),
        in_specs=[pl.BlockSpec((None, S, HD), lambda b, g: (b, 0, NSA_COL_KC + g)),
                  pl.BlockSpec((None, S, HD), lambda b, g: (b, 0, NSA_COL_VC + g)),
                  pl.BlockSpec((2, CMP_BLOCK, HD), lambda b, g: (0, 0, 0)),
                  pl.BlockSpec((2, CMP_BLOCK, HD, HD), lambda b, g: (0, 0, 0, 0)),
                  pl.BlockSpec((2, HD, HD), lambda b, g: (0, 0, 0))],
        out_specs=[pl.BlockSpec((1, 1, ncp, HD), lambda b, g: (b, g, 0, 0)),
                   pl.BlockSpec((1, 1, HD, ncp), lambda b, g: (b, g, 0, 0))],
        out_shape=[jax.ShapeDtypeStruct((B, G, ncp, HD), BF16), jax.ShapeDtypeStruct((B, G, HD, ncp), BF16)],
        compiler_params=_cparams(("parallel", "parallel")),
        name="nsa_compress",
    )(p3, p3, cmp_pos, cmp_w1.astype(BF16), cmp_w2.astype(BF16))

    c0 = np.arange(ncp)[None, :] * CMP_STRIDE
    s0 = np.arange(nsel)[:, None] * SEL_BLOCK
    ov_t = np.clip(np.minimum(c0 + CMP_BLOCK, s0 + SEL_BLOCK) - np.maximum(c0, s0), 0, None) / CMP_BLOCK
    ov_t = ov_t * (np.arange(ncp)[None, :] < n_cmp)
    gb = jnp.pad(gate_b, (0, LANES - gate_b.shape[0])).reshape(1, LANES)
    ncols = R * tq

    def full_kv(shape):
        return pl.BlockSpec((1, 1) + shape, lambda b, g, i: (b, g, 0, 0))

    return pl.pallas_call(
        functools.partial(_nsa_attn_kernel, tq=tq, kt=kt, ncp=ncp, nsel=nsel),
        grid=(B, G, S // tq),
        in_specs=[pl.BlockSpec((1, tq, R * HD), lambda b, g, i: (b, i, g)),
                  pl.BlockSpec((1, tq, R * HD), lambda b, g, i: (b, i, g)),
                  full_kv((ncp, HD)), full_kv((HD, ncp)),
                  full_kv((S, HD)), full_kv((S, HD)), full_kv((HD, S)), full_kv((HD, S)),
                  pl.BlockSpec((1, tq, LANES), lambda b, g, i: (b, i, NSA_COL_GL)),
                  pl.BlockSpec((1, LANES), lambda b, g, i: (0, 0)),
                  pl.BlockSpec((nsel, ncp), lambda b, g, i: (0, 0))],
        out_specs=pl.BlockSpec((1, tq, R * HD), lambda b, g, i: (b, i, g)),
        out_shape=jax.ShapeDtypeStruct((B, S, NSA_HEADS * HD), F32),
        scratch_shapes=[pltpu.VMEM((nsel, tq), F32),
                        pltpu.VMEM((1, ncols), F32), pltpu.VMEM((1, ncols), F32), pltpu.VMEM((HD, ncols), F32)],
        compiler_params=_cparams(("parallel", "parallel", "arbitrary")),
        name="nsa_attn",
    )(q_s, q_r, k_cmp, v_cmp_t, ks_rot, kw_rot, vs_t, vw_t, p3, gb, jnp.asarray(ov_t, F32))


AB_COL_GATES = 50 * LANES
AB_COLS_PAD = 52 * LANES


def _ab_permute_cols(w_in):
    ml, rw = w_in[:, :ML_COLS], w_in[:, ML_COLS:]
    main, gif = ml[:, :ML_COLS - 2 * ML_HEADS], ml[:, ML_COLS - 2 * ML_HEADS:]
    w = jnp.concatenate([main, rw, gif], axis=1)
    return jnp.pad(w, ((0, 0), (0, AB_COLS_PAD - w.shape[1])))


def _mlstm_kernel(gb_ref, qk_ref, v_ref, o_ref, gcol_ref, grow_ref, cw_ref, cb_ref, y_ref,
                  halo_ref, c_ref, n_ref, m_ref, qc_ref, kc_ref, *, tb, nb):
    H, DK, DV, L = ML_HEADS, ML_DK, ML_DV, ML_CHUNK
    HALO = SUBLANES

    @pl.when(pl.program_id(1) == 0)
    def _():
        halo_ref[...] = jnp.zeros_like(halo_ref)
        c_ref[...] = jnp.zeros_like(c_ref)
        n_ref[...] = jnp.zeros_like(n_ref)
        m_ref[...] = jnp.zeros_like(m_ref)

    for b in range(nb):
        x = qk_ref[b]
        xe = jnp.concatenate([halo_ref[b], x], axis=0)
        y = cb_ref[...]
        for j in range(ML_CONV):
            lo = HALO - (ML_CONV - 1) + j
            y = y + xe[lo:lo + tb] * cw_ref[j:j + 1, :]
        halo_ref[b] = x[tb - HALO:]
        y = y * jax.nn.sigmoid(y)
        qc_ref[b] = y[:, :H * DK] * (DK ** -0.5)
        kc_ref[b] = y[:, H * DK:]

    r_i = lax.broadcasted_iota(jnp.int32, (L, L), 0)
    c_i = lax.broadcasted_iota(jnp.int32, (L, L), 1)
    tri = c_i <= r_i

    def chunk(c, carry):
        rows = pl.ds(pl.multiple_of(c * L, L), L)
        chains = [(b, h) for b in range(nb) for h in range(H)]
        n = len(chains)
        q = [qc_ref[b, rows, h * DK:(h + 1) * DK] for b, h in chains]
        k = [kc_ref[b, rows, h * DK:(h + 1) * DK] for b, h in chains]
        v = [v_ref[b, rows, h * DV:(h + 1) * DV].astype(BF16) for b, h in chains]
        gc = [gcol_ref[b, h, rows, :] for b, h in chains]
        gr = [grow_ref[b, h, c] for b, h in chains]
        i_col = [gc[j][:, 0:1] + gb_ref[h] for j, (b, h) in enumerate(chains)]
        lf_col = [jax.nn.log_sigmoid(gc[j][:, 1:2] + gb_ref[H + h]) for j, (b, h) in enumerate(chains)]
        i_row = [gr[j][0:1, :] + gb_ref[h] for j, (b, h) in enumerate(chains)]
        lf_row = [jax.nn.log_sigmoid(gr[j][1:2, :] + gb_ref[H + h]) for j, (b, h) in enumerate(chains)]
        b_col = [jnp.sum(jnp.where(tri, lf_row[j], 0.0), axis=1, keepdims=True) for j in range(n)]
        b_row = [jnp.sum(jnp.where(c_i >= r_i, lf_col[j], 0.0), axis=0, keepdims=True) for j in range(n)]
        dmat = [jnp.where(tri, b_col[j] - b_row[j] + i_row[j], NSA_NEG) for j in range(n)]
        m_old = [m_ref[b * H + h] for b, h in chains]
        inter = [b_col[j] + m_old[j] for j in range(n)]
        m_row = [jnp.maximum(inter[j], jnp.max(dmat[j], axis=1, keepdims=True)) for j in range(n)]
        w_in = [jnp.exp(dmat[j] - m_row[j]) for j in range(n)]
        w_st = [jnp.exp(inter[j] - m_row[j]) for j in range(n)]
        qb = [q[j].astype(BF16) for j in range(n)]
        s = [lax.dot_general(qb[j], k[j].astype(BF16), NT_DIMS, preferred_element_type=F32) * w_in[j]
             for j in range(n)]
        qc_state = [jnp.dot(qb[j], c_ref[b * H + h].astype(BF16), preferred_element_type=F32)
                    for j, (b, h) in enumerate(chains)]
        sv = [jnp.dot(s[j].astype(BF16), v[j], preferred_element_type=F32) for j in range(n)]
        for j, (b, h) in enumerate(chains):
            den = (w_st[j] * jnp.sum(q[j] * n_ref[b * H + h], axis=1, keepdims=True)
                   + jnp.sum(s[j], axis=1, keepdims=True))
            hid = (w_st[j] * qc_state[j] + sv[j]) * (1.0 / jnp.maximum(jnp.abs(den), jnp.exp(-m_row[j])))
            y_ref[b, rows, h * DV:(h + 1) * DV] = jax.nn.sigmoid(o_ref[b, rows, h * DV:(h + 1) * DV]) * hid
        b_last = [b_col[j][L - 1:L, :] for j in range(n)]
        g_key = [b_last[j] - b_col[j] + i_col[j] for j in range(n)]
        m_new = [jnp.maximum(b_last[j] + m_old[j], jnp.max(g_key[j], axis=0, keepdims=True)) for j in range(n)]
        wk = [jnp.exp(g_key[j] - m_new[j]) for j in range(n)]
        kv = [jnp.dot((k[j] * wk[j]).T.astype(BF16), v[j], preferred_element_type=F32) for j in range(n)]
        for j, (b, h) in enumerate(chains):
            bh = b * H + h
            decay = jnp.exp(b_last[j] + m_old[j] - m_new[j])
            c_ref[bh] = decay * c_ref[bh] + kv[j]
            n_ref[bh] = decay * n_ref[bh] + jnp.sum(wk[j] * k[j], axis=0, keepdims=True)
            m_ref[bh] = m_new[j]
        return carry

    lax.fori_loop(0, tb // L, chunk, 0)


def _mlstm(p3, conv_w, conv_b, gate_b, tb=256, nb=2):
    B, S, _ = p3.shape
    H, DK, DV, L = ML_HEADS, ML_DK, ML_DV, ML_CHUNK
    tb = min(tb, S)
    nb = min(nb, B)
    nc = S // L
    gif = p3[:, :, AB_COL_GATES:AB_COL_GATES + 2 * H].reshape(B, S, 2, H)
    gcol = gif.transpose(0, 3, 1, 2)
    grow = gif.reshape(B, nc, L, 2, H).transpose(0, 4, 1, 3, 2)
    qkw = 2 * H * DK
    return pl.pallas_call(
        functools.partial(_mlstm_kernel, tb=tb, nb=nb),
        grid_spec=pltpu.PrefetchScalarGridSpec(
            num_scalar_prefetch=1, grid=(B // nb, S // tb),
            in_specs=[pl.BlockSpec((nb, tb, qkw), lambda b, t, gb: (b, t, 0)),
                      pl.BlockSpec((nb, tb, H * DV), lambda b, t, gb: (b, t, 1)),
                      pl.BlockSpec((nb, tb, H * DV), lambda b, t, gb: (b, t, 2)),
                      pl.BlockSpec((nb, H, tb, 2), lambda b, t, gb: (b, 0, t, 0)),
                      pl.BlockSpec((nb, H, tb // L, 2, L), lambda b, t, gb: (b, 0, t, 0, 0)),
                      pl.BlockSpec((ML_CONV, qkw), lambda b, t, gb: (0, 0)),
                      pl.BlockSpec((1, qkw), lambda b, t, gb: (0, 0))],
            out_specs=pl.BlockSpec((nb, tb, H * DV), lambda b, t, gb: (b, t, 0)),
            scratch_shapes=[pltpu.VMEM((nb, SUBLANES, qkw), F32), pltpu.VMEM((nb * H, DK, DV), F32),
                            pltpu.VMEM((nb * H, 1, DK), F32), pltpu.VMEM((nb * H, 1, 1), F32),
                            pltpu.VMEM((nb, tb, H * DK), F32), pltpu.VMEM((nb, tb, H * DK), F32)]),
        out_shape=jax.ShapeDtypeStruct((B, S, H * DV), F32),
        compiler_params=_cparams(("parallel", "arbitrary")),
        name="mlstm",
    )(gate_b, p3, p3, p3, gcol, grow, conv_w, conv_b.reshape(1, qkw))


def _head_sum_bcast(a, hmat):
    parts = [jnp.dot(a[:, i * LANES:(i + 1) * LANES], hmat, preferred_element_type=F32,
                     precision=lax.Precision.HIGHEST) for i in range(a.shape[1] // LANES)]
    return jnp.concatenate(parts, axis=1)


def _head_hmat():
    row = lax.broadcasted_iota(jnp.int32, (LANES, LANES), 0)
    col = lax.broadcasted_iota(jnp.int32, (LANES, LANES), 1)
    return jnp.where((row // RW_HEAD) == (col // RW_HEAD), 1.0, 0.0).astype(F32)


def _rwkv_prep_kernel(r_ref, k_ref, v_ref, lo_ref, mu_ref, w0_ref, wup_ref, a0_ref, aup_ref, gup_ref, kk_ref2, ka_ref,
                      rk_ref, ro_ref, wo_ref, ko_ref, vo_ref, kko_ref, kbo_ref, go_ref, bo_ref, prev_ref, *, tb):
    @pl.when(pl.program_id(1) == 0)
    def _():
        prev_ref[...] = jnp.zeros_like(prev_ref)

    C = RW_DIM
    x = jnp.concatenate([r_ref[0], k_ref[0], v_ref[0], lo_ref[0]], axis=1)
    prev = prev_ref[...]
    row0 = lax.broadcasted_iota(jnp.int32, (tb, 1), 0) == 0
    shifted = jnp.where(row0, prev[SUBLANES - 1:SUBLANES, :], pltpu.roll(x, 1, 0))
    prev_ref[...] = x[tb - SUBLANES:]
    xm = x + (shifted - x) * mu_ref[...]
    r, k, v, lo = xm[:, :C], xm[:, C:2 * C], xm[:, 2 * C:3 * C], xm[:, 3 * C:]
    xwa = lo[:, :LANES]
    xg = lo[:, LANES:]
    lw = jnp.dot(jnp.tanh(xwa).astype(BF16), wup_ref[...], preferred_element_type=F32)
    la = jnp.dot(xwa.astype(BF16), aup_ref[...], preferred_element_type=F32)
    g = jnp.dot(jax.nn.sigmoid(xg).astype(BF16), gup_ref[...], preferred_element_type=F32)
    w = jnp.exp(-math.exp(-0.5) * jax.nn.sigmoid(w0_ref[...] + lw))
    a = jax.nn.sigmoid(a0_ref[...] + la)
    hmat = _head_hmat()
    kk = k * kk_ref2[...]
    kk = kk * lax.rsqrt(_head_sum_bcast(kk * kk, hmat) + 1e-12)
    k2 = k * (1.0 + (a - 1.0) * ka_ref[...])
    ro_ref[0] = r
    wo_ref[0] = w
    ko_ref[0] = k2
    vo_ref[0] = v
    kko_ref[0] = kk
    kbo_ref[0] = kk * a
    go_ref[0] = g
    bo_ref[0] = _head_sum_bcast(r * k2 * rk_ref[...], hmat) * v


def _rwkv_post_kernel(y_ref, b_ref, g_ref, lnw_ref, lnb_ref, o_ref, *, tb):
    nq = RW_DIM // LANES // 2
    yrows = nq * SUBLANES
    lane = lax.broadcasted_iota(jnp.int32, (tb, LANES), 1)
    parts = []
    for q in range(nq):
        a0, a1, a2, a3 = [y_ref[pl.ds(q * SUBLANES + i, tb, stride=yrows), :] for i in range(4)]
        parts.append(jnp.where(lane < RW_HEAD, a0, pltpu.roll(a1, RW_HEAD, 1)))
        parts.append(jnp.where(lane < RW_HEAD, pltpu.roll(a2, RW_HEAD, 1), a3))
    y = jnp.concatenate(parts, axis=1)
    hmat = _head_hmat()
    mu = _head_sum_bcast(y, hmat) * (1.0 / RW_HEAD)
    yc = y - mu
    var = _head_sum_bcast(yc * yc, hmat) * (1.0 / RW_HEAD)
    yn = yc * lax.rsqrt(var + RW_LN_EPS) * lnw_ref[...] + lnb_ref[...]
    o_ref[0] = (yn + b_ref[0]) * g_ref[0]


def _rwkv_branch(p3, mu, w0, w_up, a0, a_up, g_up, k_k, k_a, r_k, ln_w, ln_b, tb=256):
    B, S, _ = p3.shape
    C = RW_DIM
    tb = min(tb, S)
    row = lambda t: t.reshape(1, -1)
    wup = jnp.concatenate([w_up, jnp.zeros_like(a_up)], axis=0).astype(BF16)
    aup = jnp.concatenate([jnp.zeros_like(w_up), a_up], axis=0).astype(BF16)
    seq = pl.BlockSpec((1, tb, C), lambda b, t: (b, t, 0))
    par = pl.BlockSpec((1, C), lambda b, t: (0, 0))
    r, w, k, v, kk, kb, g, bonus = pl.pallas_call(
        functools.partial(_rwkv_prep_kernel, tb=tb),
        grid=(B, S // tb),
        in_specs=[pl.BlockSpec((1, tb, C), lambda b, t: (b, t, 3)), pl.BlockSpec((1, tb, C), lambda b, t: (b, t, 4)),
                  pl.BlockSpec((1, tb, C), lambda b, t: (b, t, 5)),
                  pl.BlockSpec((1, tb, 2 * LANES), lambda b, t: (b, t, 24)),
                  pl.BlockSpec((1, RW_COLS), lambda b, t: (0, 0)), par,
                  pl.BlockSpec((LANES, C), lambda b, t: (0, 0)), par, pl.BlockSpec((LANES, C), lambda b, t: (0, 0)),
                  pl.BlockSpec((RW_LORA_G, C), lambda b, t: (0, 0)), par, par, par],
        out_specs=[seq] * 8,
        out_shape=[jax.ShapeDtypeStruct((B, S, C), F32)] * 8,
        scratch_shapes=[pltpu.VMEM((SUBLANES, RW_COLS), F32)],
        compiler_params=_cparams(("parallel", "arbitrary")),
        name="rwkv_prep",
    )(p3, p3, p3, p3, row(mu), row(w0), wup, row(a0), aup, g_up.astype(BF16), row(k_k), row(k_a), row(r_k))
    y_raw = _rwkv_scan(r, w, k, v, kk, kb)
    yrows = y_raw.shape[1] // S
    return pl.pallas_call(
        functools.partial(_rwkv_post_kernel, tb=tb),
        grid=(B, S // tb),
        in_specs=[pl.BlockSpec((None, tb * yrows, LANES), lambda b, t: (b, t, 0)), seq, seq, par, par],
        out_specs=seq,
        out_shape=jax.ShapeDtypeStruct((B, S, C), F32),
        compiler_params=_cparams(("parallel", "parallel")),
        name="rwkv_post",
    )(y_raw, bonus, g, row(ln_w), row(ln_b))


def _ab_mixer(x, norm_w, w_in, conv_w, conv_b, gate_b, mu, w0, w_up, a0, a_up, g_up, k_k, k_a, r_k, ln_w, ln_b,
              w_out):
    B, S, D = x.shape
    T = B * S
    x2 = x.reshape(T, D)
    p3 = _matmul(x2, _ab_permute_cols(w_in).astype(BF16), norm_w=norm_w).reshape(B, S, AB_COLS_PAD)
    y_m = _mlstm(p3, conv_w, conv_b, gate_b)
    y_r = _rwkv_branch(p3, mu, w0, w_up, a0, a_up, g_up, k_k, k_a, r_k, ln_w, ln_b)
    hm = ML_HEADS * ML_DV
    return _matmul_pair(y_m.reshape(T, hm), w_out[:hm].astype(BF16), y_r.reshape(T, RW_DIM),
                        w_out[hm:].astype(BF16), x2).reshape(B, S, D)


def _nsa(x, norm_w, w_in, gate_b, cmp_pos, cmp_w1, cmp_w2, w_out):
    B, S, D = x.shape
    T = B * S
    x2 = x.reshape(T, D)
    p = _matmul(x2, _pad_cols(w_in, 512).astype(BF16), norm_w=norm_w)
    o = _nsa_attention(p.reshape(B, S, -1), gate_b, cmp_pos, cmp_w1, cmp_w2)
    return _matmul(o.reshape(T, NSA_HEADS * NSA_HD), w_out.astype(BF16), residual=x2).reshape(B, S, D)


def _cross_attn_kernel(q_ref, k_ref, v_ref, o_ref):
    heads = [slice(h * CA_HD, (h + 1) * CA_HD) for h in range(CA_HEADS)]
    s = [lax.dot_general(q_ref[:, c].astype(BF16), k_ref[:, c].astype(BF16), NT_DIMS,
                         preferred_element_type=F32) * (CA_HD ** -0.5) for c in heads]
    e = [jnp.exp(a - jnp.max(a, axis=-1, keepdims=True)) for a in s]
    p = [a * (1.0 / jnp.sum(a, axis=-1, keepdims=True)) for a in e]
    for c, a in zip(heads, p):
        o_ref[:, c] = jnp.dot(a.astype(BF16), v_ref[:, c].astype(BF16), preferred_element_type=F32)


def _cross_attn(x, norm_w, mem, norm_mem, wq, wk, wv, wo, tq=512):
    B, S, D = x.shape
    T = B * S
    M = mem.shape[1]
    x2 = x.reshape(T, D)
    q = _matmul(x2, wq.astype(BF16), norm_w=norm_w)
    kv = _matmul(mem.reshape(B * M, D), jnp.concatenate([wk, wv], axis=1).astype(BF16), norm_w=norm_mem)
    nq = S // tq
    o = pl.pallas_call(
        _cross_attn_kernel,
        grid=(B, nq),
        in_specs=[pl.BlockSpec((tq, D), lambda b, i: (b * nq + i, 0)),
                  pl.BlockSpec((M, D), lambda b, i: (b, 0)), pl.BlockSpec((M, D), lambda b, i: (b, 1))],
        out_specs=pl.BlockSpec((tq, D), lambda b, i: (b * nq + i, 0)),
        out_shape=jax.ShapeDtypeStruct((T, D), F32),
        compiler_params=_cparams(("parallel", "parallel")),
        name="cross_attn",
    )(q, kv, kv)
    return _matmul(o, wo.astype(BF16), residual=x2).reshape(B, S, D)


MOE_TM = 256
MOE_NEG = -1e30


def _moe_route_kernel(x_ref, nw_ref, wr_ref, br_ref, xn_ref, ri_ref, rw_ref, cnt_ref, cnt_scr, *, tm):
    @pl.when(pl.program_id(0) == 0)
    def _():
        cnt_scr[...] = jnp.zeros_like(cnt_scr)

    x = x_ref[...]
    xn = x * lax.rsqrt(jnp.mean(x * x, axis=-1, keepdims=True) + EPS) * nw_ref[...]
    xn_ref[...] = xn
    logits = jnp.dot(xn, wr_ref[...], preferred_element_type=F32, precision=lax.Precision.HIGHEST) + br_ref[...]
    lane = lax.broadcasted_iota(jnp.int32, (tm, LANES), 1)
    gmask = lane < MOE_GROUPS
    lg = jnp.where(gmask, logits, MOE_NEG)
    gmax = jnp.max(lg, axis=-1, keepdims=True)
    grp = jnp.min(jnp.where(lg == gmax, lane, LANES), axis=-1, keepdims=True)
    p_grp = 1.0 / jnp.sum(jnp.where(gmask, jnp.exp(lg - gmax), 0.0), axis=-1, keepdims=True)
    lo = MOE_GROUPS + grp * MOE_PER_GROUP
    le = jnp.where((lane >= lo) & (lane < lo + MOE_PER_GROUP), logits, MOE_NEG)
    t1 = jnp.max(le, axis=-1, keepdims=True)
    i1 = jnp.min(jnp.where(le == t1, lane, LANES), axis=-1, keepdims=True)
    le2 = jnp.where(lane == i1, MOE_NEG, le)
    t2 = jnp.max(le2, axis=-1, keepdims=True)
    i2 = jnp.min(jnp.where(le2 == t2, lane, LANES), axis=-1, keepdims=True)
    e21 = jnp.exp(t2 - t1)
    w1 = p_grp / (1.0 + e21)
    w2 = w1 * e21
    e1 = i1 - MOE_GROUPS
    e2 = i2 - MOE_GROUPS
    oh1 = jnp.where(lane == e1, 1.0, 0.0)
    oh2 = jnp.where(lane == e2, 1.0, 0.0)
    both = oh1 + oh2
    r_i = lax.broadcasted_iota(jnp.int32, (tm, tm), 0)
    c_i = lax.broadcasted_iota(jnp.int32, (tm, tm), 1)
    ltri = jnp.where(c_i < r_i, 1.0, 0.0).astype(BF16)
    before = jnp.dot(ltri, both.astype(BF16), preferred_element_type=F32) + cnt_scr[...]
    pos1 = jnp.sum(oh1 * before, axis=-1, keepdims=True).astype(jnp.int32)
    pos2 = jnp.sum(oh2 * before, axis=-1, keepdims=True).astype(jnp.int32)
    cnt_scr[...] = cnt_scr[...] + jnp.sum(both, axis=0, keepdims=True)
    ri_ref[...] = jnp.where(lane == 0, e1, jnp.where(lane == 1, e2, jnp.where(lane == 2, pos1,
                            jnp.where(lane == 3, pos2, 0))))
    rw_ref[...] = jnp.where(lane == 0, w1, jnp.where(lane == 1, w2, 0.0))
    cnt_ref[...] = cnt_scr[...]


def _moe_dest(tok, e1_ref, e2_ref, p1_ref, p2_ref, off_ref):
    return off_ref[e1_ref[tok]] + p1_ref[tok], off_ref[e2_ref[tok]] + p2_ref[tok]


def _moe_rowmap_kernel(e1_ref, e2_ref, p1_ref, p2_ref, off_ref, cnt_ref, rt_ref, *, n_tok, rows):
    def clear(i, carry):
        rt_ref[i] = 0
        return carry

    def clear_pad(e, carry):
        used = off_ref[e] + cnt_ref[e]
        end = jnp.where(e + 1 < MOE_EXPERTS, off_ref[jnp.minimum(e + 1, MOE_EXPERTS - 1)], rows)
        lax.fori_loop(used, end, clear, 0)
        return carry

    def place(t, carry):
        d1, d2 = _moe_dest(t, e1_ref, e2_ref, p1_ref, p2_ref, off_ref)
        rt_ref[d1] = t
        rt_ref[d2] = t
        return carry

    lax.fori_loop(0, MOE_EXPERTS, clear_pad, 0)
    lax.fori_loop(0, n_tok, place, 0, unroll=8)


def _moe_expert_kernel(te_ref, nu_ref, rt_ref, xn_hbm, wg_ref, wu_ref, wd_ref, y_ref, xbuf, sems):
    del te_ref
    i = pl.program_id(0)
    nu = nu_ref[0]
    slot = i % 2

    def row_copy(tile, r, s):
        return pltpu.make_async_copy(xn_hbm.at[pl.ds(rt_ref[tile * MOE_TM + r], 1)], xbuf.at[s, pl.ds(r, 1)],
                                     sems.at[s])

    def drain(s):
        def body(r, carry):
            row_copy(0, 0, s).wait()
            return carry
        lax.fori_loop(0, MOE_TM, body, 0, unroll=8)

    @pl.when(i == 0)
    def _():
        def body(r, carry):
            row_copy(0, r, 0).start()
            return carry
        lax.fori_loop(0, MOE_TM, body, 0, unroll=8)

    @pl.when(i < nu)
    def _():
        drain(slot)
        nxt = jnp.minimum(i + 1, nu - 1)
        for r in range(MOE_TM):
            row_copy(nxt, r, 1 - slot).start()
        x = xbuf[slot].astype(BF16)
        gate = jnp.dot(x, wg_ref[...].astype(BF16), preferred_element_type=F32)
        up = jnp.dot(x, wu_ref[...].astype(BF16), preferred_element_type=F32)
        hid = gate * jax.nn.sigmoid(gate) * up
        y_ref[...] = jnp.dot(hid.astype(BF16), wd_ref[...].astype(BF16), preferred_element_type=F32)

        @pl.when(i == nu - 1)
        def _():
            drain(1 - slot)

    @pl.when(i >= nu)
    def _():
        y_ref[...] = jnp.zeros_like(y_ref)


def _moe_combine_kernel(e1_ref, e2_ref, p1_ref, p2_ref, off_ref, x_ref, rw_ref, ys_hbm, o_ref, buf1, buf2, sems, *, tm):
    i = pl.program_id(0)
    slot = i % 2

    def row_copy(src, t, buf, s):
        return pltpu.make_async_copy(ys_hbm.at[pl.ds(src, 1)], buf.at[s, pl.ds(t, 1)], sems.at[s])

    def issue_tile(tile, s):
        def body(t, carry):
            d1, d2 = _moe_dest(tile * tm + t, e1_ref, e2_ref, p1_ref, p2_ref, off_ref)
            row_copy(d1, t, buf1, s).start(priority=0)
            row_copy(d2, t, buf2, s).start(priority=1)
            return carry
        lax.fori_loop(0, tm, body, 0, unroll=4)

    def drain(t, carry):
        row_copy(0, 0, buf1, slot).wait()
        row_copy(0, 0, buf2, slot).wait()
        return carry

    @pl.when(i == 0)
    def _():
        issue_tile(0, 0)

    lax.fori_loop(0, tm, drain, 0, unroll=8)

    @pl.when(i + 1 < pl.num_programs(0))
    def _():
        issue_tile(i + 1, 1 - slot)

    w = rw_ref[...]
    o_ref[...] = x_ref[...] + w[:, 0:1] * buf1[slot] + w[:, 1:2] * buf2[slot]


def _hier_moe(x, norm_w, wg, bg, we, be, w_gate, w_up, w_down, layer):
    B, S, D = x.shape
    T = B * S
    x2 = x.reshape(T, D)
    FF = w_gate.shape[-1]
    tm_r = 512
    wr = jnp.pad(jnp.concatenate([wg, we], axis=1), ((0, 0), (0, LANES - MOE_GROUPS - MOE_EXPERTS)))
    br = jnp.pad(jnp.concatenate([bg, be]), (0, LANES - MOE_GROUPS - MOE_EXPERTS)).reshape(1, LANES)
    xn, ri, rw, cnt = pl.pallas_call(
        functools.partial(_moe_route_kernel, tm=tm_r),
        grid=(T // tm_r,),
        in_specs=[pl.BlockSpec((tm_r, D), lambda i: (i, 0)), pl.BlockSpec((1, D), lambda i: (0, 0)),
                  pl.BlockSpec((D, LANES), lambda i: (0, 0)), pl.BlockSpec((1, LANES), lambda i: (0, 0))],
        out_specs=[pl.BlockSpec((tm_r, D), lambda i: (i, 0)), pl.BlockSpec((tm_r, LANES), lambda i: (i, 0)),
                   pl.BlockSpec((tm_r, LANES), lambda i: (i, 0)), pl.BlockSpec((1, LANES), lambda i: (0, 0))],
        out_shape=[jax.ShapeDtypeStruct((T, D), F32), jax.ShapeDtypeStruct((T, LANES), jnp.int32),
                   jax.ShapeDtypeStruct((T, LANES), F32), jax.ShapeDtypeStruct((1, LANES), F32)],
        scratch_shapes=[pltpu.VMEM((1, LANES), F32)],
        compiler_params=_cparams(("arbitrary",)),
        name="moe_route",
    )(x2, norm_w.reshape(1, D), wr, br)

    counts = cnt[0, :MOE_EXPERTS].astype(jnp.int32)
    padded = (counts + MOE_TM - 1) // MOE_TM * MOE_TM
    ends = jnp.cumsum(padded)
    off = (ends - padded).astype(jnp.int32)
    n_tiles = (T * MOE_TOPK) // MOE_TM + MOE_EXPERTS
    rows = n_tiles * MOE_TM
    n_used = (ends[-1] // MOE_TM).astype(jnp.int32).reshape(1)
    tile_e = jnp.minimum(jnp.searchsorted(ends, jnp.arange(n_tiles, dtype=jnp.int32) * MOE_TM, side='right'),
                         MOE_EXPERTS - 1).astype(jnp.int32)
    e1, e2, p1, p2 = ri[:, 0], ri[:, 1], ri[:, 2], ri[:, 3]

    row_tok = pl.pallas_call(
        functools.partial(_moe_rowmap_kernel, n_tok=T, rows=rows),
        grid_spec=pltpu.PrefetchScalarGridSpec(
            num_scalar_prefetch=6, grid=(1,), in_specs=[],
            out_specs=pl.BlockSpec(memory_space=pltpu.SMEM)),
        out_shape=jax.ShapeDtypeStruct((rows,), jnp.int32),
        compiler_params=_cparams(("arbitrary",)),
        name="moe_rowmap",
    )(e1, e2, p1, p2, off, counts)

    def w_ix(i, te, nu, rt):
        return (layer, te[jnp.minimum(i, nu[0] - 1)], 0, 0)

    ys = pl.pallas_call(
        _moe_expert_kernel,
        grid_spec=pltpu.PrefetchScalarGridSpec(
            num_scalar_prefetch=3, grid=(n_tiles,),
            in_specs=[pl.BlockSpec(memory_space=pl.ANY),
                      pl.BlockSpec((None, None, D, FF), w_ix), pl.BlockSpec((None, None, D, FF), w_ix),
                      pl.BlockSpec((None, None, FF, D), w_ix)],
            out_specs=pl.BlockSpec((MOE_TM, D), lambda i, te, nu, rt: (i, 0)),
            scratch_shapes=[pltpu.VMEM((2, MOE_TM, D), F32), pltpu.SemaphoreType.DMA((2,))]),
        out_shape=jax.ShapeDtypeStruct((rows, D), F32),
        compiler_params=_cparams(("arbitrary",)),
        name="moe_experts",
    )(tile_e, n_used, row_tok, xn, w_gate, w_up, w_down)

    tm_c = 256
    out = pl.pallas_call(
        functools.partial(_moe_combine_kernel, tm=tm_c),
        grid_spec=pltpu.PrefetchScalarGridSpec(
            num_scalar_prefetch=5, grid=(T // tm_c,),
            in_specs=[pl.BlockSpec((tm_c, D), lambda i, *_: (i, 0)), pl.BlockSpec((tm_c, LANES), lambda i, *_: (i, 0)),
                      pl.BlockSpec(memory_space=pl.ANY)],
            out_specs=pl.BlockSpec((tm_c, D), lambda i, *_: (i, 0)),
            scratch_shapes=[pltpu.VMEM((2, tm_c, D), F32), pltpu.VMEM((2, tm_c, D), F32),
                            pltpu.SemaphoreType.DMA((2,))]),
        out_shape=jax.ShapeDtypeStruct((T, D), F32),
        compiler_params=_cparams(("arbitrary",)),
        name="moe_combine",
    )(e1, e2, p1, p2, off, x2, rw, ys)
    return out.reshape(B, S, D)


def kernel(x, mem, norm_mix, norm_cross, norm_mem, norm_ffn, norm_final,
           ab_w_in, ml_conv_w, ml_conv_b, ml_gate_b, rw_mu, rw_w0, rw_w_up, rw_a0, rw_a_up, rw_g_up,
           rw_k_k, rw_k_a, rw_r_k, rw_ln_w, rw_ln_b, ab_w_out,
           nsa_w_in, nsa_gate_b, cmp_pos, cmp_w1, cmp_w2, nsa_w_out,
           ca_wq, ca_wk, ca_wv, ca_wo,
           moe_wg, moe_bg, moe_we, moe_be, moe_w_gate, moe_w_up, moe_w_down):
    B, S, D = x.shape
    for l in range(DEPTH):
        j = l // 2
        if l % 2 == 0:
            x = _ab_mixer(x, norm_mix[l], ab_w_in[j], ml_conv_w[j], ml_conv_b[j], ml_gate_b[j], rw_mu[j], rw_w0[j],
                          rw_w_up[j], rw_a0[j], rw_a_up[j], rw_g_up[j], rw_k_k[j], rw_k_a[j], rw_r_k[j],
                          rw_ln_w[j], rw_ln_b[j], ab_w_out[j])
        else:
            x = _nsa(x, norm_mix[l], nsa_w_in[j], nsa_gate_b[j], cmp_pos[j], cmp_w1[j], cmp_w2[j], nsa_w_out[j])
        x = _cross_attn(x, norm_cross[l], mem, norm_mem[l], ca_wq[l], ca_wk[l], ca_wv[l], ca_wo[l])
        x = _hier_moe(x, norm_ffn[l], moe_wg[l], moe_bg[l], moe_we[l], moe_be[l],
                      moe_w_gate, moe_w_up, moe_w_down, l)
    return _rmsnorm_rows(x.reshape(B * S, D), norm_final).reshape(B, S, D)
```

```python
import functools
import math

import jax
import jax.numpy as jnp
import numpy as np
from jax import lax
from jax.experimental import pallas as pl
from jax.experimental.pallas import tpu as pltpu

F32 = jnp.float32
BF16 = jnp.bfloat16

D_MODEL = 2048
DEPTH = 2
EPS = 1e-6
ROPE_THETA = 500000.0

ML_HEADS = 4
ML_DV = D_MODEL // 2 // ML_HEADS
ML_DK = ML_DV // 2
ML_CHUNK = 64
ML_CONV = 4
RW_HEAD = 64
RW_HEADS = D_MODEL // 2 // RW_HEAD
RW_DIM = RW_HEADS * RW_HEAD
RW_LORA_W = 64
RW_LORA_A = 64
RW_LORA_G = 128
RW_LN_EPS = 64e-5
ML_SPLITS = (2 * ML_HEADS * ML_DK, ML_HEADS * ML_DV, ML_HEADS * ML_DV, 2 * ML_HEADS)
RW_SPLITS = (RW_DIM, RW_DIM, RW_DIM, RW_LORA_W, RW_LORA_A, RW_LORA_G)
ML_COLS = sum(ML_SPLITS)
RW_COLS = sum(RW_SPLITS)

NSA_HEADS = 16
NSA_KV = 4
NSA_REP = NSA_HEADS // NSA_KV
NSA_HD = D_MODEL // NSA_HEADS
ROPE_DIM = NSA_HD // 4
CMP_BLOCK = 32
CMP_STRIDE = 16
SEL_BLOCK = 64
SEL_TOPK = 16
WINDOW = 512

CA_HEADS = 4
CA_HD = D_MODEL // CA_HEADS

MOE_GROUPS = 8
MOE_PER_GROUP = 8
MOE_EXPERTS = MOE_GROUPS * MOE_PER_GROUP
MOE_TOPK = 2
MOE_FF = D_MODEL // 4

VMEM_LIMIT = 48 * 1024 * 1024
LANES = 128
SUBLANES = 8
NT_DIMS = (((1,), (1,)), ((), ()))


def _cparams(sem):
    return pltpu.CompilerParams(dimension_semantics=sem, vmem_limit_bytes=VMEM_LIMIT)


def _mm_kernel(*refs, has_norm, has_res):
    it = iter(refs)
    x_ref = next(it)
    w_ref = next(it)
    nw_ref = next(it) if has_norm else None
    r_ref = next(it) if has_res else None
    o_ref = next(it)
    xs_ref = next(it)

    @pl.when(pl.program_id(1) == 0)
    def _():
        x = x_ref[...]
        if has_norm:
            ms = jnp.mean(x * x, axis=-1, keepdims=True)
            x = x * lax.rsqrt(ms + EPS) * nw_ref[...]
        xs_ref[...] = x.astype(BF16)

    acc = jnp.dot(xs_ref[...], w_ref[...], preferred_element_type=F32)
    if has_res:
        acc = acc + r_ref[...]
    o_ref[...] = acc


def _matmul(x, w_bf16, *, norm_w=None, residual=None, tm=1024, tn=512):
    M, K = x.shape
    N = w_bf16.shape[1]
    tm = min(tm, M)
    tn = min(tn, N)
    assert M % tm == 0 and N % tn == 0, (M, N, tm, tn)
    has_norm = norm_w is not None
    has_res = residual is not None
    in_specs = [pl.BlockSpec((tm, K), lambda i, j: (i, 0)),
                pl.BlockSpec((K, tn), lambda i, j: (0, j))]
    args = [x, w_bf16]
    if has_norm:
        in_specs.append(pl.BlockSpec((1, K), lambda i, j: (0, 0)))
        args.append(norm_w.reshape(1, K))
    if has_res:
        in_specs.append(pl.BlockSpec((tm, tn), lambda i, j: (i, j)))
        args.append(residual)
    return pl.pallas_call(
        functools.partial(_mm_kernel, has_norm=has_norm, has_res=has_res),
        grid=(M // tm, N // tn),
        in_specs=in_specs,
        out_specs=pl.BlockSpec((tm, tn), lambda i, j: (i, j)),
        out_shape=jax.ShapeDtypeStruct((M, N), F32),
        scratch_shapes=[pltpu.VMEM((tm, K), BF16)],
        compiler_params=_cparams(("parallel", "arbitrary")),
        name="mm_norm" if has_norm else "mm",
    )(*args)


def _mm2_kernel(xa_ref, xb_ref, wa_ref, wb_ref, r_ref, o_ref, xas_ref, xbs_ref):
    @pl.when(pl.program_id(1) == 0)
    def _():
        xas_ref[...] = xa_ref[...].astype(BF16)
        xbs_ref[...] = xb_ref[...].astype(BF16)

    o_ref[...] = (jnp.dot(xas_ref[...], wa_ref[...], preferred_element_type=F32)
                  + jnp.dot(xbs_ref[...], wb_ref[...], preferred_element_type=F32) + r_ref[...])


def _matmul_pair(xa, wa_bf16, xb, wb_bf16, residual, tm=1024, tn=512):
    M, Ka = xa.shape
    Kb = xb.shape[1]
    N = wa_bf16.shape[1]
    tm = min(tm, M)
    tn = min(tn, N)
    assert M % tm == 0 and N % tn == 0, (M, N, tm, tn)
    return pl.pallas_call(
        _mm2_kernel,
        grid=(M // tm, N // tn),
        in_specs=[pl.BlockSpec((tm, Ka), lambda i, j: (i, 0)), pl.BlockSpec((tm, Kb), lambda i, j: (i, 0)),
                  pl.BlockSpec((Ka, tn), lambda i, j: (0, j)), pl.BlockSpec((Kb, tn), lambda i, j: (0, j)),
                  pl.BlockSpec((tm, tn), lambda i, j: (i, j))],
        out_specs=pl.BlockSpec((tm, tn), lambda i, j: (i, j)),
        out_shape=jax.ShapeDtypeStruct((M, N), F32),
        scratch_shapes=[pltpu.VMEM((tm, Ka), BF16), pltpu.VMEM((tm, Kb), BF16)],
        compiler_params=_cparams(("parallel", "arbitrary")),
        name="mm_pair",
    )(xa, xb, wa_bf16, wb_bf16, residual)


def _pad_cols(w, mult):
    n = w.shape[1]
    pad = (-n) % mult
    if pad:
        w = jnp.pad(w, ((0, 0), (0, pad)))
    return w


def _rmsnorm_kernel(x_ref, w_ref, o_ref):
    x = x_ref[...]
    ms = jnp.mean(x * x, axis=-1, keepdims=True)
    o_ref[...] = x * lax.rsqrt(ms + EPS) * w_ref[...]


def _rmsnorm_rows(x, w, tm=512):
    M, K = x.shape
    return pl.pallas_call(
        _rmsnorm_kernel,
        grid=(M // tm,),
        in_specs=[pl.BlockSpec((tm, K), lambda i: (i, 0)), pl.BlockSpec((1, K), lambda i: (0, 0))],
        out_specs=pl.BlockSpec((tm, K), lambda i: (i, 0)),
        out_shape=jax.ShapeDtypeStruct((M, K), F32),
        compiler_params=_cparams(("parallel",)),
        name="rmsnorm",
    )(x, w.reshape(1, K))


def _rwkv_scan_kernel(r_ref, w_ref, k_ref, v_ref, kk_ref, kb_ref, y_ref, s_ref, *, tb, nb, npairs):
    @pl.when(pl.program_id(1) == 0)
    def _():
        s_ref[...] = jnp.zeros_like(s_ref)

    row = lax.broadcasted_iota(jnp.int32, (LANES, LANES), 0)
    col = lax.broadcasted_iota(jnp.int32, (LANES, LANES), 1)
    hmat = jnp.where((row // RW_HEAD) == (col // RW_HEAD), 1.0, 0.0).astype(BF16)
    vrow = lax.broadcasted_iota(jnp.int32, (RW_HEAD, LANES), 0)
    vcol = lax.broadcasted_iota(jnp.int32, (RW_HEAD, LANES), 1)
    diag = jnp.where((vcol % RW_HEAD) == vrow, 1.0, 0.0).astype(BF16)
    row16 = lax.broadcasted_iota(jnp.int32, (2 * SUBLANES, LANES), 0)
    lane16 = lax.broadcasted_iota(jnp.int32, (2 * SUBLANES, LANES), 1)
    head16 = (row16 // SUBLANES) == (lane16 // RW_HEAD)
    row8 = lax.broadcasted_iota(jnp.int32, (SUBLANES, LANES), 0)
    lane8 = lax.broadcasted_iota(jnp.int32, (SUBLANES, LANES), 1)
    nq = npairs // 2
    lns = [pl.ds(p * LANES, LANES) for p in range(npairs)]

    def group_sum(parts):
        lhs = jnp.concatenate([q.astype(BF16) for q in parts], axis=0)
        out = jnp.dot(lhs, hmat, preferred_element_type=F32)
        return [out[i * RW_HEAD:(i + 1) * RW_HEAD] for i in range(len(parts))]

    def step(t8, carry):
        rows = pl.ds(pl.multiple_of(t8 * SUBLANES, SUBLANES), SUBLANES)

        def tiles(ref):
            return [[ref[b, rows, ln] for ln in lns] for b in range(nb)]

        kk8, w8, kb8, k8, v8, r8 = (tiles(kk_ref), tiles(w_ref), tiles(kb_ref), tiles(k_ref), tiles(v_ref),
                                    tiles(r_ref))
        s = [[s_ref[b, p] for p in range(npairs)] for b in range(nb)]
        vcols = [[lax.dot_general(diag, jnp.where(head16, jnp.concatenate([v8[b][p]] * 2, axis=0), 0.0).astype(BF16),
                                  NT_DIMS, preferred_element_type=F32).astype(BF16)
                  for p in range(npairs)] for b in range(nb)]
        vk = [[None] * npairs for _ in range(nb)]
        for b in range(nb):
            for p in range(npairs):
                ksel = jnp.where(head16, jnp.concatenate([k8[b][p]] * 2, axis=0), 0.0)
                rhs = jnp.concatenate([jnp.where((row16 % SUBLANES) == j, ksel, 0.0) for j in range(SUBLANES)],
                                      axis=1).astype(BF16)
                vk[b][p] = jnp.dot(vcols[b][p], rhs, preferred_element_type=F32)
        for j in range(SUBLANES):
            sl = slice(j, j + 1)
            for b in range(nb):
                sk = group_sum([s[b][i] * kk8[b][i][sl] for i in range(npairs)])
                s[b] = [s[b][i] * w8[b][i][sl] - sk[i] * kb8[b][i][sl] + vk[b][i][:, j * LANES:(j + 1) * LANES]
                        for i in range(npairs)]
            for b in range(nb):
                for q in range(nq):
                    p0, p1 = 2 * q, 2 * q + 1
                    a = jnp.where((row8 == 0) & (lane8 < RW_HEAD), r8[b][p0][sl],
                        jnp.where((row8 == 1) & (lane8 >= RW_HEAD), r8[b][p0][sl],
                        jnp.where((row8 == 2) & (lane8 < RW_HEAD), r8[b][p1][sl],
                        jnp.where((row8 == 3) & (lane8 >= RW_HEAD), r8[b][p1][sl], 0.0))))
                    st = jnp.concatenate([s[b][p0], s[b][p1]], axis=0).astype(BF16)
                    yq = lax.dot_general(a.astype(BF16), st, NT_DIMS, preferred_element_type=F32)
                    base = ((t8 * SUBLANES + j) * nq + q) * SUBLANES
                    y_ref[b, pl.ds(pl.multiple_of(base, SUBLANES), SUBLANES), :] = yq
        for b in range(nb):
            for p in range(npairs):
                s_ref[b, p] = s[b][p]
        return carry

    lax.fori_loop(0, tb // SUBLANES, step, 0)


def _rwkv_scan(r, w, k, v, kk, kb, tb=64, nb=4):
    B, S, C = r.shape
    npairs = C // LANES
    tb = min(tb, S)
    nb = min(nb, B)
    spec = pl.BlockSpec((nb, tb, C), lambda b, t: (b, t, 0))
    yrows = (npairs // 2) * SUBLANES
    return pl.pallas_call(
        functools.partial(_rwkv_scan_kernel, tb=tb, nb=nb, npairs=npairs),
        grid=(B // nb, S // tb),
        in_specs=[spec] * 6,
        out_specs=pl.BlockSpec((nb, tb * yrows, LANES), lambda b, t: (b, t, 0)),
        out_shape=jax.ShapeDtypeStruct((B, S * yrows, LANES), F32),
        scratch_shapes=[pltpu.VMEM((nb, npairs, RW_HEAD, LANES), F32)],
        compiler_params=_cparams(("parallel", "arbitrary")),
        name="rwkv_scan",
    )(r, w, k, v, kk, kb)


NSA_NEG = -1e30
NSA_FORCED = 1e30
NSA_REMOVED = -3e30
NSA_COL_Q, NSA_COL_KC, NSA_COL_VC, NSA_COL_KS, NSA_COL_VS, NSA_COL_KW, NSA_COL_VW, NSA_COL_GL = (
    0, 16, 20, 24, 28, 32, 36, 40)


def _rope_tables(S):
    half = ROPE_DIM // 2
    inv = 1.0 / (ROPE_THETA ** (jnp.arange(half, dtype=F32) / half))
    ang = jnp.arange(S, dtype=F32)[:, None] * inv[None, :]
    cos, sin = jnp.cos(ang), jnp.sin(ang)
    one = jnp.ones((S, NSA_HD - ROPE_DIM), F32)
    zero = jnp.zeros((S, NSA_HD - ROPE_DIM), F32)
    zh = jnp.zeros((S, half), F32)
    return (jnp.concatenate([cos, cos, one], axis=1), jnp.concatenate([zh, sin, zero], axis=1),
            jnp.concatenate([-sin, zh, zero], axis=1))


def _rope_rows(x, c, s1, s2):
    half = ROPE_DIM // 2
    return x * c + pltpu.roll(x, half, 1) * s1 + pltpu.roll(x, NSA_HD - half, 1) * s2


def _nsa_prep_q_kernel(x_ref, c_ref, s1_ref, s2_ref, q_ref, qr_ref):
    c, s1, s2 = c_ref[...], s1_ref[...], s2_ref[...]
    for h in range(NSA_HEADS):
        cols = slice(h * NSA_HD, (h + 1) * NSA_HD)
        x = x_ref[0, :, cols] * (NSA_HD ** -0.5)
        q_ref[0, :, cols] = x.astype(BF16)
        qr_ref[0, :, cols] = _rope_rows(x, c, s1, s2).astype(BF16)


def _nsa_prep_kv_kernel(ks_ref, kw_ref, vs_ref, vw_ref, c_ref, s1_ref, s2_ref, kso_ref, kwo_ref, vst_ref, vwt_ref):
    c, s1, s2 = c_ref[...], s1_ref[...], s2_ref[...]
    kso_ref[0, 0] = _rope_rows(ks_ref[0], c, s1, s2).astype(BF16)
    kwo_ref[0, 0] = _rope_rows(kw_ref[0], c, s1, s2).astype(BF16)
    vst_ref[0, 0] = vs_ref[0].T.astype(BF16)
    vwt_ref[0, 0] = vw_ref[0].T.astype(BF16)


def _nsa_compress_kernel(kc_ref, vc_ref, pe_ref, w1_ref, w2_ref, kcmp_ref, vcmpt_ref, *, ncp):
    for which, x_ref in enumerate((kc_ref, vc_ref)):
        za = jnp.zeros((ncp, NSA_HD), F32)
        zb = jnp.zeros((ncp, NSA_HD), F32)
        for p in range(CMP_STRIDE):
            xp = x_ref[pl.ds(p, ncp, stride=CMP_STRIDE), :]
            za = za + jnp.dot((xp + pe_ref[which, p:p + 1, :]).astype(BF16), w1_ref[which, p],
                              preferred_element_type=F32)
            zb = zb + jnp.dot((xp + pe_ref[which, CMP_STRIDE + p:CMP_STRIDE + p + 1, :]).astype(BF16),
                              w1_ref[which, CMP_STRIDE + p], preferred_element_type=F32)
        pre = za + pltpu.roll(zb, ncp - 1, 0)
        out = jnp.dot(jax.nn.gelu(pre).astype(BF16), w2_ref[which], preferred_element_type=F32)
        if which == 0:
            kcmp_ref[0, 0] = out.astype(BF16)
        else:
            vcmpt_ref[0, 0] = out.T.astype(BF16)


def _nsa_attn_kernel(qt_ref, qrt_ref, kcmp_ref, vcmpt_ref, ks_ref, kw_ref, vst_ref, vwt_ref, gl_ref, gb_ref, ovt_ref,
                     o_ref, sel_ref, ms_ref, ls_ref, accs_ref, *, tq, kt, ncp, nsel):
    R = NSA_REP
    g = pl.program_id(1)
    qi = pl.program_id(2)
    t0 = qi * tq
    q_rows = jnp.concatenate([qt_ref[0, :, r * NSA_HD:(r + 1) * NSA_HD] for r in range(R)], axis=0)
    qr_rows = jnp.concatenate([qrt_ref[0, :, r * NSA_HD:(r + 1) * NSA_HD] for r in range(R)], axis=0)
    qpos = t0 + lax.broadcasted_iota(jnp.int32, (1, tq), 1)

    def per_head(fn, a):
        return jnp.concatenate([fn(a[:, r * tq:(r + 1) * tq]) for r in range(R)], axis=1)

    nwk = WINDOW + tq
    kw0 = pl.multiple_of(jnp.maximum(t0 - WINDOW, 0), tq)
    dist = qpos - (kw0 + lax.broadcasted_iota(jnp.int32, (nwk, 1), 0))
    inwin = (dist >= 0) & (dist < WINDOW)

    def window_head(r):
        sw = lax.dot_general(kw_ref[0, 0, pl.ds(kw0, nwk), :], qr_rows[r * tq:(r + 1) * tq], NT_DIMS,
                             preferred_element_type=F32)
        sw = jnp.where(inwin, sw, NSA_NEG)
        ew = jnp.exp(sw - jnp.max(sw, axis=0, keepdims=True))
        return (jnp.dot(vwt_ref[0, 0, :, pl.ds(kw0, nwk)], ew.astype(BF16), preferred_element_type=F32)
                * (1.0 / jnp.sum(ew, axis=0, keepdims=True)))

    s = lax.dot_general(kcmp_ref[0, 0], q_rows, NT_DIMS, preferred_element_type=F32)
    cend = lax.broadcasted_iota(jnp.int32, (ncp, 1), 0) * CMP_STRIDE + (CMP_BLOCK - 1)
    vis = cend <= qpos
    s = per_head(lambda a: jnp.where(vis, a, NSA_NEG), s)
    m = jnp.max(s, axis=0, keepdims=True)
    e = per_head(lambda a: jnp.where(vis, a, 0.0), jnp.exp(s - m))
    d = jnp.sum(e, axis=0, keepdims=True)
    p = e * (1.0 / jnp.where(d > 0, d, 1.0))
    ocmp_t = jnp.dot(vcmpt_ref[0, 0], p.astype(BF16), preferred_element_type=F32)

    psum = p[:, 0:tq]
    for r in range(1, R):
        psum = psum + p[:, r * tq:(r + 1) * tq]
    imp = jnp.dot(ovt_ref[...], psum, preferred_element_type=F32, precision=lax.Precision.HIGHEST)
    sidx = lax.broadcasted_iota(jnp.int32, (nsel, tq), 0)
    cur = qpos // SEL_BLOCK
    forced = (sidx == 0) | (sidx == cur) | (sidx == cur - 1)
    score = jnp.where(forced, NSA_FORCED, jnp.where(sidx <= cur, imp, NSA_NEG))
    sel = jnp.zeros((nsel, tq), F32)
    owin_heads = []
    for it in range(SEL_TOPK):
        mx = jnp.max(score, axis=0, keepdims=True)
        idx = jnp.min(jnp.where(score == mx, sidx, nsel), axis=0, keepdims=True)
        pick = (sidx == idx) & (mx > 0.5 * NSA_NEG)
        sel = jnp.where(pick, 1.0, sel)
        score = jnp.where(pick, NSA_REMOVED, score)
        if it % (SEL_TOPK // R) == 0:
            owin_heads.append(window_head(it // (SEL_TOPK // R)))
    owin_t = jnp.concatenate(owin_heads, axis=1)
    sel_ref[...] = sel

    def online_update(s, valid, v_t, m_ref, l_ref, acc_ref):
        s = per_head(lambda a: jnp.where(valid, a, NSA_NEG), s)
        m_old = m_ref[...]
        m_new = jnp.maximum(m_old, jnp.max(s, axis=0, keepdims=True))
        alpha = jnp.exp(m_old - m_new)
        pexp = jnp.exp(s - m_new)
        l_ref[...] = alpha * l_ref[...] + jnp.sum(pexp, axis=0, keepdims=True)
        acc_ref[...] = alpha * acc_ref[...] + jnp.dot(v_t, pexp.astype(BF16), preferred_element_type=F32)
        m_ref[...] = m_new

    ms_ref[...] = jnp.full(ms_ref.shape, NSA_NEG, F32)
    ls_ref[...] = jnp.zeros(ls_ref.shape, F32)
    accs_ref[...] = jnp.zeros(accs_ref.shape, F32)

    nblk = kt // SEL_BLOCK

    def sel_body(c, carry):
        k0 = pl.multiple_of(c * kt, kt)
        s = lax.dot_general(ks_ref[0, 0, pl.ds(k0, kt), :], qr_rows, NT_DIMS, preferred_element_type=F32)
        selrows = sel_ref[pl.ds(pl.multiple_of(c * nblk, nblk), nblk), :]
        selexp = jnp.concatenate(
            [jnp.broadcast_to(selrows[b:b + 1], (SEL_BLOCK, tq)) for b in range(nblk)], axis=0)
        kpos = k0 + lax.broadcasted_iota(jnp.int32, (kt, 1), 0)
        valid = (selexp > 0.5) & (kpos <= qpos)
        online_update(s, valid, vst_ref[0, 0, :, pl.ds(k0, kt)], ms_ref, ls_ref, accs_ref)
        return carry

    lax.fori_loop(0, t0 // kt + 1, sel_body, 0)
    oslc_t = accs_ref[...] * (1.0 / ls_ref[...])

    gates_t = jax.nn.sigmoid(gl_ref[0] + gb_ref[...]).T
    rid = lax.broadcasted_iota(jnp.int32, (LANES, 1), 0)

    def gate_row(which, r):
        return jnp.sum(jnp.where(rid == which * NSA_HEADS + g * R + r, gates_t, 0.0), axis=0, keepdims=True)

    for r in range(R):
        cols = slice(r * tq, (r + 1) * tq)
        o_t = (gate_row(0, r) * ocmp_t[:, cols] + gate_row(1, r) * oslc_t[:, cols]
               + gate_row(2, r) * owin_t[:, cols])
        o_ref[0, :, r * NSA_HD:(r + 1) * NSA_HD] = o_t.T


def _nsa_attention(p3, gate_b, cmp_pos, cmp_w1, cmp_w2, *, tq=256, kt=512, tp=512):
    B, S, _ = p3.shape
    G, R, HD = NSA_KV, NSA_REP, NSA_HD
    assert CMP_BLOCK == 2 * CMP_STRIDE and S % kt == 0 and WINDOW % tq == 0 and kt % tq == 0 and tq % LANES == 0
    ncp = S // CMP_STRIDE
    nsel = S // SEL_BLOCK
    n_cmp = (S - CMP_BLOCK) // CMP_STRIDE + 1
    c_tab, s1_tab, s2_tab = _rope_tables(S)
    tab_spec3 = pl.BlockSpec((tp, HD), lambda b, h, i: (i, 0))

    qw = NSA_HEADS * HD
    tab_spec2 = pl.BlockSpec((tp, HD), lambda b, i: (i, 0))
    q_s, q_r = pl.pallas_call(
        _nsa_prep_q_kernel,
        grid=(B, S // tp),
        in_specs=[pl.BlockSpec((1, tp, qw), lambda b, i: (b, i, 0)), tab_spec2, tab_spec2, tab_spec2],
        out_specs=[pl.BlockSpec((1, tp, qw), lambda b, i: (b, i, 0))] * 2,
        out_shape=[jax.ShapeDtypeStruct((B, S, qw), BF16)] * 2,
        compiler_params=_cparams(("parallel", "parallel")),
        name="nsa_prep_q",
    )(p3, c_tab, s1_tab, s2_tab)

    def col_spec(col0):
        return pl.BlockSpec((1, tp, HD), lambda b, g, i: (b, i, col0 + g))

    ks_rot, kw_rot, vs_t, vw_t = pl.pallas_call(
        _nsa_prep_kv_kernel,
        grid=(B, G, S // tp),
        in_specs=[col_spec(NSA_COL_KS), col_spec(NSA_COL_KW), col_spec(NSA_COL_VS), col_spec(NSA_COL_VW),
                  tab_spec3, tab_spec3, tab_spec3],
        out_specs=[pl.BlockSpec((1, 1, tp, HD), lambda b, g, i: (b, g, i, 0))] * 2
        + [pl.BlockSpec((1, 1, HD, tp), lambda b, g, i: (b, g, 0, i))] * 2,
        out_shape=[jax.ShapeDtypeStruct((B, G, S, HD), BF16)] * 2 + [jax.ShapeDtypeStruct((B, G, HD, S), BF16)] * 2,
        compiler_params=_cparams(("parallel", "parallel", "parallel")),
        name="nsa_prep_kv",
    )(p3, p3, p3, p3, c_tab, s1_tab, s2_tab)

    k_cmp, v_cmp_t = pl.pallas_call(
        functools.partial(_nsa_compress_kernel, ncp=ncp),
        grid=(B, G),
        in_specs=[pl.BlockSpec((None, S, HD), lambda b, g: (b, 0, NSA_COL_KC + g)),
                  pl.BlockSpec((None, S, HD), lambda b, g: (b, 0, NSA_COL_VC + g)),
                  pl.BlockSpec((2, CMP_BLOCK, HD), lambda b, g: (0, 0, 0)),
                  pl.BlockSpec((2, CMP_BLOCK, HD, HD), lambda b, g: (0, 0, 0, 0)),
                  pl.BlockSpec((2, HD, HD), lambda b, g: (0, 0, 0))],
        out_specs=[pl.BlockSpec((1, 1, ncp, HD), lambda b, g: (b, g, 0, 0)),
                   pl.BlockSpec((1, 1, HD, ncp), lambda b, g: (b, g, 0, 0))],
        out_shape=[jax.ShapeDtypeStruct((B, G, ncp, HD), BF16), jax.ShapeDtypeStruct((B, G, HD, ncp), BF16)],
        compiler_params=_cparams(("parallel", "parallel")),
        name="nsa_compress",
    )(p3, p3, cmp_pos, cmp_w1.astype(BF16), cmp_w2.astype(BF16))

    c0 = np.arange(ncp)[None, :] * CMP_STRIDE
    s0 = np.arange(nsel)[:, None] * SEL_BLOCK
    ov_t = np.clip(np.minimum(c0 + CMP_BLOCK, s0 + SEL_BLOCK) - np.maximum(c0, s0), 0, None) / CMP_BLOCK
    ov_t = ov_t * (np.arange(ncp)[None, :] < n_cmp)
    gb = jnp.pad(gate_b, (0, LANES - gate_b.shape[0])).reshape(1, LANES)
    ncols = R * tq

    def full_kv(shape):
        return pl.BlockSpec((1, 1) + shape, lambda b, g, i: (b, g, 0, 0))

    return pl.pallas_call(
        functools.partial(_nsa_attn_kernel, tq=tq, kt=kt, ncp=ncp, nsel=nsel),
        grid=(B, G, S // tq),
        in_specs=[pl.BlockSpec((1, tq, R * HD), lambda b, g, i: (b, i, g)),
                  pl.BlockSpec((1, tq, R * HD), lambda b, g, i: (b, i, g)),
                  full_kv((ncp, HD)), full_kv((HD, ncp)),
                  full_kv((S, HD)), full_kv((S, HD)), full_kv((HD, S)), full_kv((HD, S)),
                  pl.BlockSpec((1, tq, LANES), lambda b, g, i: (b, i, NSA_COL_GL)),
                  pl.BlockSpec((1, LANES), lambda b, g, i: (0, 0)),
                  pl.BlockSpec((nsel, ncp), lambda b, g, i: (0, 0))],
        out_specs=pl.BlockSpec((1, tq, R * HD), lambda b, g, i: (b, i, g)),
        out_shape=jax.ShapeDtypeStruct((B, S, NSA_HEADS * HD), F32),
        scratch_shapes=[pltpu.VMEM((nsel, tq), F32),
                        pltpu.VMEM((1, ncols), F32), pltpu.VMEM((1, ncols), F32), pltpu.VMEM((HD, ncols), F32)],
        compiler_params=_cparams(("parallel", "parallel", "arbitrary")),
        name="nsa_attn",
    )(q_s, q_r, k_cmp, v_cmp_t, ks_rot, kw_rot, vs_t, vw_t, p3, gb, jnp.asarray(ov_t, F32))


AB_COL_GATES = 50 * LANES
AB_COLS_PAD = 52 * LANES


def _ab_permute_cols(w_in):
    ml, rw = w_in[:, :ML_COLS], w_in[:, ML_COLS:]
    main, gif = ml[:, :ML_COLS - 2 * ML_HEADS], ml[:, ML_COLS - 2 * ML_HEADS:]
    w = jnp.concatenate([main, rw, gif], axis=1)
    return jnp.pad(w, ((0, 0), (0, AB_COLS_PAD - w.shape[1])))


def _mlstm_kernel(gb_ref, qk_ref, v_ref, o_ref, gcol_ref, grow_ref, cw_ref, cb_ref, y_ref,
                  halo_ref, c_ref, n_ref, m_ref, qc_ref, kc_ref, *, tb, nb):
    H, DK, DV, L = ML_HEADS, ML_DK, ML_DV, ML_CHUNK
    HALO = SUBLANES

    @pl.when(pl.program_id(1) == 0)
    def _():
        halo_ref[...] = jnp.zeros_like(halo_ref)
        c_ref[...] = jnp.zeros_like(c_ref)
        n_ref[...] = jnp.zeros_like(n_ref)
        m_ref[...] = jnp.zeros_like(m_ref)

    for b in range(nb):
        x = qk_ref[b]
        xe = jnp.concatenate([halo_ref[b], x], axis=0)
        y = cb_ref[...]
        for j in range(ML_CONV):
            lo = HALO - (ML_CONV - 1) + j
            y = y + xe[lo:lo + tb] * cw_ref[j:j + 1, :]
        halo_ref[b] = x[tb - HALO:]
        y = y * jax.nn.sigmoid(y)
        qc_ref[b] = y[:, :H * DK] * (DK ** -0.5)
        kc_ref[b] = y[:, H * DK:]

    r_i = lax.broadcasted_iota(jnp.int32, (L, L), 0)
    c_i = lax.broadcasted_iota(jnp.int32, (L, L), 1)
    tri = c_i <= r_i

    def chunk(c, carry):
        rows = pl.ds(pl.multiple_of(c * L, L), L)
        chains = [(b, h) for b in range(nb) for h in range(H)]
        n = len(chains)
        q = [qc_ref[b, rows, h * DK:(h + 1) * DK] for b, h in chains]
        k = [kc_ref[b, rows, h * DK:(h + 1) * DK] for b, h in chains]
        v = [v_ref[b, rows, h * DV:(h + 1) * DV].astype(BF16) for b, h in chains]
        gc = [gcol_ref[b, h, rows, :] for b, h in chains]
        gr = [grow_ref[b, h, c] for b, h in chains]
        i_col = [gc[j][:, 0:1] + gb_ref[h] for j, (b, h) in enumerate(chains)]
        lf_col = [jax.nn.log_sigmoid(gc[j][:, 1:2] + gb_ref[H + h]) for j, (b, h) in enumerate(chains)]
        i_row = [gr[j][0:1, :] + gb_ref[h] for j, (b, h) in enumerate(chains)]
        lf_row = [jax.nn.log_sigmoid(gr[j][1:2, :] + gb_ref[H + h]) for j, (b, h) in enumerate(chains)]
        b_col = [jnp.sum(jnp.where(tri, lf_row[j], 0.0), axis=1, keepdims=True) for j in range(n)]
        b_row = [jnp.sum(jnp.where(c_i >= r_i, lf_col[j], 0.0), axis=0, keepdims=True) for j in range(n)]
        dmat = [jnp.where(tri, b_col[j] - b_row[j] + i_row[j], NSA_NEG) for j in range(n)]
        m_old = [m_ref[b * H + h] for b, h in chains]
        inter = [b_col[j] + m_old[j] for j in range(n)]
        m_row = [jnp.maximum(inter[j], jnp.max(dmat[j], axis=1, keepdims=True)) for j in range(n)]
        w_in = [jnp.exp(dmat[j] - m_row[j]) for j in range(n)]
        w_st = [jnp.exp(inter[j] - m_row[j]) for j in range(n)]
        qb = [q[j].astype(BF16) for j in range(n)]
        s = [lax.dot_general(qb[j], k[j].astype(BF16), NT_DIMS, preferred_element_type=F32) * w_in[j]
             for j in range(n)]
        qc_state = [jnp.dot(qb[j], c_ref[b * H + h].astype(BF16), preferred_element_type=F32)
                    for j, (b, h) in enumerate(chains)]
        sv = [jnp.dot(s[j].astype(BF16), v[j], preferred_element_type=F32) for j in range(n)]
        for j, (b, h) in enumerate(chains):
            den = (w_st[j] * jnp.sum(q[j] * n_ref[b * H + h], axis=1, keepdims=True)
                   + jnp.sum(s[j], axis=1, keepdims=True))
            hid = (w_st[j] * qc_state[j] + sv[j]) * (1.0 / jnp.maximum(jnp.abs(den), jnp.exp(-m_row[j])))
            y_ref[b, rows, h * DV:(h + 1) * DV] = jax.nn.sigmoid(o_ref[b, rows, h * DV:(h + 1) * DV]) * hid
        b_last = [b_col[j][L - 1:L, :] for j in range(n)]
        g_key = [b_last[j] - b_col[j] + i_col[j] for j in range(n)]
        m_new = [jnp.maximum(b_last[j] + m_old[j], jnp.max(g_key[j], axis=0, keepdims=True)) for j in range(n)]
        wk = [jnp.exp(g_key[j] - m_new[j]) for j in range(n)]
        kv = [jnp.dot((k[j] * wk[j]).T.astype(BF16), v[j], preferred_element_type=F32) for j in range(n)]
        for j, (b, h) in enumerate(chains):
            bh = b * H + h
            decay = jnp.exp(b_last[j] + m_old[j] - m_new[j])
            c_ref[bh] = decay * c_ref[bh] + kv[j]
            n_ref[bh] = decay * n_ref[bh] + jnp.sum(wk[j] * k[j], axis=0, keepdims=True)
            m_ref[bh] = m_new[j]
        return carry

    lax.fori_loop(0, tb // L, chunk, 0)


def _mlstm(p3, conv_w, conv_b, gate_b, tb=256, nb=2):
    B, S, _ = p3.shape
    H, DK, DV, L = ML_HEADS, ML_DK, ML_DV, ML_CHUNK
    tb = min(tb, S)
    nb = min(nb, B)
    nc = S // L
    gif = p3[:, :, AB_COL_GATES:AB_COL_GATES + 2 * H].reshape(B, S, 2, H)
    gcol = gif.transpose(0, 3, 1, 2)
    grow = gif.reshape(B, nc, L, 2, H).transpose(0, 4, 1, 3, 2)
    qkw = 2 * H * DK
    return pl.pallas_call(
        functools.partial(_mlstm_kernel, tb=tb, nb=nb),
        grid_spec=pltpu.PrefetchScalarGridSpec(
            num_scalar_prefetch=1, grid=(B // nb, S // tb),
            in_specs=[pl.BlockSpec((nb, tb, qkw), lambda b, t, gb: (b, t, 0)),
                      pl.BlockSpec((nb, tb, H * DV), lambda b, t, gb: (b, t, 1)),
                      pl.BlockSpec((nb, tb, H * DV), lambda b, t, gb: (b, t, 2)),
                      pl.BlockSpec((nb, H, tb, 2), lambda b, t, gb: (b, 0, t, 0)),
                      pl.BlockSpec((nb, H, tb // L, 2, L), lambda b, t, gb: (b, 0, t, 0, 0)),
                      pl.BlockSpec((ML_CONV, qkw), lambda b, t, gb: (0, 0)),
                      pl.BlockSpec((1, qkw), lambda b, t, gb: (0, 0))],
            out_specs=pl.BlockSpec((nb, tb, H * DV), lambda b, t, gb: (b, t, 0)),
            scratch_shapes=[pltpu.VMEM((nb, SUBLANES, qkw), F32), pltpu.VMEM((nb * H, DK, DV), F32),
                            pltpu.VMEM((nb * H, 1, DK), F32), pltpu.VMEM((nb * H, 1, 1), F32),
                            pltpu.VMEM((nb, tb, H * DK), F32), pltpu.VMEM((nb, tb, H * DK), F32)]),
        out_shape=jax.ShapeDtypeStruct((B, S, H * DV), F32),
        compiler_params=_cparams(("parallel", "arbitrary")),
        name="mlstm",
    )(gate_b, p3, p3, p3, gcol, grow, conv_w, conv_b.reshape(1, qkw))


def _head_sum_bcast(a, hmat):
    parts = [jnp.dot(a[:, i * LANES:(i + 1) * LANES], hmat, preferred_element_type=F32,
                     precision=lax.Precision.HIGHEST) for i in range(a.shape[1] // LANES)]
    return jnp.concatenate(parts, axis=1)


def _head_hmat():
    row = lax.broadcasted_iota(jnp.int32, (LANES, LANES), 0)
    col = lax.broadcasted_iota(jnp.int32, (LANES, LANES), 1)
    return jnp.where((row // RW_HEAD) == (col // RW_HEAD), 1.0, 0.0).astype(F32)


def _rwkv_prep_kernel(r_ref, k_ref, v_ref, lo_ref, mu_ref, w0_ref, wup_ref, a0_ref, aup_ref, gup_ref, kk_ref2, ka_ref,
                      rk_ref, ro_ref, wo_ref, ko_ref, vo_ref, kko_ref, kbo_ref, go_ref, bo_ref, prev_ref, *, tb):
    @pl.when(pl.program_id(1) == 0)
    def _():
        prev_ref[...] = jnp.zeros_like(prev_ref)

    C = RW_DIM
    x = jnp.concatenate([r_ref[0], k_ref[0], v_ref[0], lo_ref[0]], axis=1)
    prev = prev_ref[...]
    row0 = lax.broadcasted_iota(jnp.int32, (tb, 1), 0) == 0
    shifted = jnp.where(row0, prev[SUBLANES - 1:SUBLANES, :], pltpu.roll(x, 1, 0))
    prev_ref[...] = x[tb - SUBLANES:]
    xm = x + (shifted - x) * mu_ref[...]
    r, k, v, lo = xm[:, :C], xm[:, C:2 * C], xm[:, 2 * C:3 * C], xm[:, 3 * C:]
    xwa = lo[:, :LANES]
    xg = lo[:, LANES:]
    lw = jnp.dot(jnp.tanh(xwa).astype(BF16), wup_ref[...], preferred_element_type=F32)
    la = jnp.dot(xwa.astype(BF16), aup_ref[...], preferred_element_type=F32)
    g = jnp.dot(jax.nn.sigmoid(xg).astype(BF16), gup_ref[...], preferred_element_type=F32)
    w = jnp.exp(-math.exp(-0.5) * jax.nn.sigmoid(w0_ref[...] + lw))
    a = jax.nn.sigmoid(a0_ref[...] + la)
    hmat = _head_hmat()
    kk = k * kk_ref2[...]
    kk = kk * lax.rsqrt(_head_sum_bcast(kk * kk, hmat) + 1e-12)
    k2 = k * (1.0 + (a - 1.0) * ka_ref[...])
    ro_ref[0] = r
    wo_ref[0] = w
    ko_ref[0] = k2
    vo_ref[0] = v
    kko_ref[0] = kk
    kbo_ref[0] = kk * a
    go_ref[0] = g
    bo_ref[0] = _head_sum_bcast(r * k2 * rk_ref[...], hmat) * v


def _rwkv_post_kernel(y_ref, b_ref, g_ref, lnw_ref, lnb_ref, o_ref, *, tb):
    nq = RW_DIM // LANES // 2
    yrows = nq * SUBLANES
    lane = lax.broadcasted_iota(jnp.int32, (tb, LANES), 1)
    parts = []
    for q in range(nq):
        a0, a1, a2, a3 = [y_ref[pl.ds(q * SUBLANES + i, tb, stride=yrows), :] for i in range(4)]
        parts.append(jnp.where(lane < RW_HEAD, a0, pltpu.roll(a1, RW_HEAD, 1)))
        parts.append(jnp.where(lane < RW_HEAD, pltpu.roll(a2, RW_HEAD, 1), a3))
    y = jnp.concatenate(parts, axis=1)
    hmat = _head_hmat()
    mu = _head_sum_bcast(y, hmat) * (1.0 / RW_HEAD)
    yc = y - mu
    var = _head_sum_bcast(yc * yc, hmat) * (1.0 / RW_HEAD)
    yn = yc * lax.rsqrt(var + RW_LN_EPS) * lnw_ref[...] + lnb_ref[...]
    o_ref[0] = (yn + b_ref[0]) * g_ref[0]


def _rwkv_branch(p3, mu, w0, w_up, a0, a_up, g_up, k_k, k_a, r_k, ln_w, ln_b, tb=256):
    B, S, _ = p3.shape
    C = RW_DIM
    tb = min(tb, S)
    row = lambda t: t.reshape(1, -1)
    wup = jnp.concatenate([w_up, jnp.zeros_like(a_up)], axis=0).astype(BF16)
    aup = jnp.concatenate([jnp.zeros_like(w_up), a_up], axis=0).astype(BF16)
    seq = pl.BlockSpec((1, tb, C), lambda b, t: (b, t, 0))
    par = pl.BlockSpec((1, C), lambda b, t: (0, 0))
    r, w, k, v, kk, kb, g, bonus = pl.pallas_call(
        functools.partial(_rwkv_prep_kernel, tb=tb),
        grid=(B, S // tb),
        in_specs=[pl.BlockSpec((1, tb, C), lambda b, t: (b, t, 3)), pl.BlockSpec((1, tb, C), lambda b, t: (b, t, 4)),
                  pl.BlockSpec((1, tb, C), lambda b, t: (b, t, 5)),
                  pl.BlockSpec((1, tb, 2 * LANES), lambda b, t: (b, t, 24)),
                  pl.BlockSpec((1, RW_COLS), lambda b, t: (0, 0)), par,
                  pl.BlockSpec((LANES, C), lambda b, t: (0, 0)), par, pl.BlockSpec((LANES, C), lambda b, t: (0, 0)),
                  pl.BlockSpec((RW_LORA_G, C), lambda b, t: (0, 0)), par, par, par],
        out_specs=[seq] * 8,
        out_shape=[jax.ShapeDtypeStruct((B, S, C), F32)] * 8,
        scratch_shapes=[pltpu.VMEM((SUBLANES, RW_COLS), F32)],
        compiler_params=_cparams(("parallel", "arbitrary")),
        name="rwkv_prep",
    )(p3, p3, p3, p3, row(mu), row(w0), wup, row(a0), aup, g_up.astype(BF16), row(k_k), row(k_a), row(r_k))
    y_raw = _rwkv_scan(r, w, k, v, kk, kb)
    yrows = y_raw.shape[1] // S
    return pl.pallas_call(
        functools.partial(_rwkv_post_kernel, tb=tb),
        grid=(B, S // tb),
        in_specs=[pl.BlockSpec((None, tb * yrows, LANES), lambda b, t: (b, t, 0)), seq, seq, par, par],
        out_specs=seq,
        out_shape=jax.ShapeDtypeStruct((B, S, C), F32),
        compiler_params=_cparams(("parallel", "parallel")),
        name="rwkv_post",
    )(y_raw, bonus, g, row(ln_w), row(ln_b))


def _ab_mixer(x, norm_w, w_in, conv_w, conv_b, gate_b, mu, w0, w_up, a0, a_up, g_up, k_k, k_a, r_k, ln_w, ln_b,
              w_out):
    B, S, D = x.shape
    T = B * S
    x2 = x.reshape(T, D)
    p3 = _matmul(x2, _ab_permute_cols(w_in).astype(BF16), norm_w=norm_w).reshape(B, S, AB_COLS_PAD)
    y_m = _mlstm(p3, conv_w, conv_b, gate_b)
    y_r = _rwkv_branch(p3, mu, w0, w_up, a0, a_up, g_up, k_k, k_a, r_k, ln_w, ln_b)
    hm = ML_HEADS * ML_DV
    return _matmul_pair(y_m.reshape(T, hm), w_out[:hm].astype(BF16), y_r.reshape(T, RW_DIM),
                        w_out[hm:].astype(BF16), x2).reshape(B, S, D)


def _nsa(x, norm_w, w_in, gate_b, cmp_pos, cmp_w1, cmp_w2, w_out):
    B, S, D = x.shape
    T = B * S
    x2 = x.reshape(T, D)
    p = _matmul(x2, _pad_cols(w_in, 512).astype(BF16), norm_w=norm_w)
    o = _nsa_attention(p.reshape(B, S, -1), gate_b, cmp_pos, cmp_w1, cmp_w2)
    return _matmul(o.reshape(T, NSA_HEADS * NSA_HD), w_out.astype(BF16), residual=x2).reshape(B, S, D)


def _cross_attn_kernel(q_ref, k_ref, v_ref, o_ref):
    heads = [slice(h * CA_HD, (h + 1) * CA_HD) for h in range(CA_HEADS)]
    s = [lax.dot_general(q_ref[:, c].astype(BF16), k_ref[:, c].astype(BF16), NT_DIMS,
                         preferred_element_type=F32) * (CA_HD ** -0.5) for c in heads]
    e = [jnp.exp(a - jnp.max(a, axis=-1, keepdims=True)) for a in s]
    p = [a * (1.0 / jnp.sum(a, axis=-1, keepdims=True)) for a in e]
    for c, a in zip(heads, p):
        o_ref[:, c] = jnp.dot(a.astype(BF16), v_ref[:, c].astype(BF16), preferred_element_type=F32)


def _cross_attn(x, norm_w, mem, norm_mem, wq, wk, wv, wo, tq=512):
    B, S, D = x.shape
    T = B * S
    M = mem.shape[1]
    x2 = x.reshape(T, D)
    q = _matmul(x2, wq.astype(BF16), norm_w=norm_w)
    kv = _matmul(mem.reshape(B * M, D), jnp.concatenate([wk, wv], axis=1).astype(BF16), norm_w=norm_mem)
    nq = S // tq
    o = pl.pallas_call(
        _cross_attn_kernel,
        grid=(B, nq),
        in_specs=[pl.BlockSpec((tq, D), lambda b, i: (b * nq + i, 0)),
                  pl.BlockSpec((M, D), lambda b, i: (b, 0)), pl.BlockSpec((M, D), lambda b, i: (b, 1))],
        out_specs=pl.BlockSpec((tq, D), lambda b, i: (b * nq + i, 0)),
        out_shape=jax.ShapeDtypeStruct((T, D), F32),
        compiler_params=_cparams(("parallel", "parallel")),
        name="cross_attn",
    )(q, kv, kv)
    return _matmul(o, wo.astype(BF16), residual=x2).reshape(B, S, D)


MOE_TM = 256
MOE_NEG = -1e30


def _moe_route_kernel(x_ref, nw_ref, wr_ref, br_ref, xn_ref, ri_ref, rw_ref, cnt_ref, cnt_scr, *, tm):
    @pl.when(pl.program_id(0) == 0)
    def _():
        cnt_scr[...] = jnp.zeros_like(cnt_scr)

    x = x_ref[...]
    xn = x * lax.rsqrt(jnp.mean(x * x, axis=-1, keepdims=True) + EPS) * nw_ref[...]
    xn_ref[...] = xn
    logits = jnp.dot(xn, wr_ref[...], preferred_element_type=F32, precision=lax.Precision.HIGHEST) + br_ref[...]
    lane = lax.broadcasted_iota(jnp.int32, (tm, LANES), 1)
    gmask = lane < MOE_GROUPS
    lg = jnp.where(gmask, logits, MOE_NEG)
    gmax = jnp.max(lg, axis=-1, keepdims=True)
    grp = jnp.min(jnp.where(lg == gmax, lane, LANES), axis=-1, keepdims=True)
    p_grp = 1.0 / jnp.sum(jnp.where(gmask, jnp.exp(lg - gmax), 0.0), axis=-1, keepdims=True)
    lo = MOE_GROUPS + grp * MOE_PER_GROUP
    le = jnp.where((lane >= lo) & (lane < lo + MOE_PER_GROUP), logits, MOE_NEG)
    t1 = jnp.max(le, axis=-1, keepdims=True)
    i1 = jnp.min(jnp.where(le == t1, lane, LANES), axis=-1, keepdims=True)
    le2 = jnp.where(lane == i1, MOE_NEG, le)
    t2 = jnp.max(le2, axis=-1, keepdims=True)
    i2 = jnp.min(jnp.where(le2 == t2, lane, LANES), axis=-1, keepdims=True)
    e21 = jnp.exp(t2 - t1)
    w1 = p_grp / (1.0 + e21)
    w2 = w1 * e21
    e1 = i1 - MOE_GROUPS
    e2 = i2 - MOE_GROUPS
    oh1 = jnp.where(lane == e1, 1.0, 0.0)
    oh2 = jnp.where(lane == e2, 1.0, 0.0)
    both = oh1 + oh2
    r_i = lax.broadcasted_iota(jnp.int32, (tm, tm), 0)
    c_i = lax.broadcasted_iota(jnp.int32, (tm, tm), 1)
    ltri = jnp.where(c_i < r_i, 1.0, 0.0).astype(BF16)
    before = jnp.dot(ltri, both.astype(BF16), preferred_element_type=F32) + cnt_scr[...]
    pos1 = jnp.sum(oh1 * before, axis=-1, keepdims=True).astype(jnp.int32)
    pos2 = jnp.sum(oh2 * before, axis=-1, keepdims=True).astype(jnp.int32)
    cnt_scr[...] = cnt_scr[...] + jnp.sum(both, axis=0, keepdims=True)
    ri_ref[...] = jnp.where(lane == 0, e1, jnp.where(lane == 1, e2, jnp.where(lane == 2, pos1,
                            jnp.where(lane == 3, pos2, 0))))
    rw_ref[...] = jnp.where(lane == 0, w1, jnp.where(lane == 1, w2, 0.0))
    cnt_ref[...] = cnt_scr[...]


def _moe_dest(tok, e1_ref, e2_ref, p1_ref, p2_ref, off_ref):
    return off_ref[e1_ref[tok]] + p1_ref[tok], off_ref[e2_ref[tok]] + p2_ref[tok]


def _moe_rowmap_kernel(e1_ref, e2_ref, p1_ref, p2_ref, off_ref, cnt_ref, rt_ref, *, n_tok, rows):
    def clear(i, carry):
        rt_ref[i] = 0
        return carry

    def clear_pad(e, carry):
        used = off_ref[e] + cnt_ref[e]
        end = jnp.where(e + 1 < MOE_EXPERTS, off_ref[jnp.minimum(e + 1, MOE_EXPERTS - 1)], rows)
        lax.fori_loop(used, end, clear, 0)
        return carry

    def place(t, carry):
        d1, d2 = _moe_dest(t, e1_ref, e2_ref, p1_ref, p2_ref, off_ref)
        rt_ref[d1] = t
        rt_ref[d2] = t
        return carry

    lax.fori_loop(0, MOE_EXPERTS, clear_pad, 0)
    lax.fori_loop(0, n_tok, place, 0, unroll=8)


def _moe_expert_kernel(te_ref, nu_ref, rt_ref, xn_hbm, wg_ref, wu_ref, wd_ref, y_ref, xbuf, sems):
    del te_ref
    i = pl.program_id(0)
    nu = nu_ref[0]
    slot = i % 2

    def row_copy(tile, r, s):
        return pltpu.make_async_copy(xn_hbm.at[pl.ds(rt_ref[tile * MOE_TM + r], 1)], xbuf.at[s, pl.ds(r, 1)],
                                     sems.at[s])

    def drain(s):
        def body(r, carry):
            row_copy(0, 0, s).wait()
            return carry
        lax.fori_loop(0, MOE_TM, body, 0, unroll=8)

    @pl.when(i == 0)
    def _():
        def body(r, carry):
            row_copy(0, r, 0).start()
            return carry
        lax.fori_loop(0, MOE_TM, body, 0, unroll=8)

    @pl.when(i < nu)
    def _():
        drain(slot)
        nxt = jnp.minimum(i + 1, nu - 1)
        for r in range(MOE_TM):
            row_copy(nxt, r, 1 - slot).start()
        x = xbuf[slot].astype(BF16)
        gate = jnp.dot(x, wg_ref[...].astype(BF16), preferred_element_type=F32)
        up = jnp.dot(x, wu_ref[...].astype(BF16), preferred_element_type=F32)
        hid = gate * jax.nn.sigmoid(gate) * up
        y_ref[...] = jnp.dot(hid.astype(BF16), wd_ref[...].astype(BF16), preferred_element_type=F32)

        @pl.when(i == nu - 1)
        def _():
            drain(1 - slot)

    @pl.when(i >= nu)
    def _():
        y_ref[...] = jnp.zeros_like(y_ref)


def _moe_combine_kernel(e1_ref, e2_ref, p1_ref, p2_ref, off_ref, x_ref, rw_ref, ys_hbm, o_ref, buf1, buf2, sems, *, tm):
    i = pl.program_id(0)
    slot = i % 2

    def row_copy(src, t, buf, s):
        return pltpu.make_async_copy(ys_hbm.at[pl.ds(src, 1)], buf.at[s, pl.ds(t, 1)], sems.at[s])

    def issue_tile(tile, s):
        def body(t, carry):
            d1, d2 = _moe_dest(tile * tm + t, e1_ref, e2_ref, p1_ref, p2_ref, off_ref)
            row_copy(d1, t, buf1, s).start(priority=0)
            row_copy(d2, t, buf2, s).start(priority=1)
            return carry
        lax.fori_loop(0, tm, body, 0, unroll=4)

    def drain(t, carry):
        row_copy(0, 0, buf1, slot).wait()
        row_copy(0, 0, buf2, slot).wait()
        return carry

    @pl.when(i == 0)
    def _():
        issue_tile(0, 0)

    lax.fori_loop(0, tm, drain, 0, unroll=8)

    @pl.when(i + 1 < pl.num_programs(0))
    def _():
        issue_tile(i + 1, 1 - slot)

    w = rw_ref[...]
    o_ref[...] = x_ref[...] + w[:, 0:1] * buf1[slot] + w[:, 1:2] * buf2[slot]


def _hier_moe(x, norm_w, wg, bg, we, be, w_gate, w_up, w_down, layer):
    B, S, D = x.shape
    T = B * S
    x2 = x.reshape(T, D)
    FF = w_gate.shape[-1]
    tm_r = 512
    wr = jnp.pad(jnp.concatenate([wg, we], axis=1), ((0, 0), (0, LANES - MOE_GROUPS - MOE_EXPERTS)))
    br = jnp.pad(jnp.concatenate([bg, be]), (0, LANES - MOE_GROUPS - MOE_EXPERTS)).reshape(1, LANES)
    xn, ri, rw, cnt = pl.pallas_call(
        functools.partial(_moe_route_kernel, tm=tm_r),
        grid=(T // tm_r,),
        in_specs=[pl.BlockSpec((tm_r, D), lambda i: (i, 0)), pl.BlockSpec((1, D), lambda i: (0, 0)),
                  pl.BlockSpec((D, LANES), lambda i: (0, 0)), pl.BlockSpec((1, LANES), lambda i: (0, 0))],
        out_specs=[pl.BlockSpec((tm_r, D), lambda i: (i, 0)), pl.BlockSpec((tm_r, LANES), lambda i: (i, 0)),
                   pl.BlockSpec((tm_r, LANES), lambda i: (i, 0)), pl.BlockSpec((1, LANES), lambda i: (0, 0))],
        out_shape=[jax.ShapeDtypeStruct((T, D), F32), jax.ShapeDtypeStruct((T, LANES), jnp.int32),
                   jax.ShapeDtypeStruct((T, LANES), F32), jax.ShapeDtypeStruct((1, LANES), F32)],
        scratch_shapes=[pltpu.VMEM((1, LANES), F32)],
        compiler_params=_cparams(("arbitrary",)),
        name="moe_route",
    )(x2, norm_w.reshape(1, D), wr, br)

    counts = cnt[0, :MOE_EXPERTS].astype(jnp.int32)
    padded = (counts + MOE_TM - 1) // MOE_TM * MOE_TM
    ends = jnp.cumsum(padded)
    off = (ends - padded).astype(jnp.int32)
    n_tiles = (T * MOE_TOPK) // MOE_TM + MOE_EXPERTS
    rows = n_tiles * MOE_TM
    n_used = (ends[-1] // MOE_TM).astype(jnp.int32).reshape(1)
    tile_e = jnp.minimum(jnp.searchsorted(ends, jnp.arange(n_tiles, dtype=jnp.int32) * MOE_TM, side='right'),
                         MOE_EXPERTS - 1).astype(jnp.int32)
    e1, e2, p1, p2 = ri[:, 0], ri[:, 1], ri[:, 2], ri[:, 3]

    row_tok = pl.pallas_call(
        functools.partial(_moe_rowmap_kernel, n_tok=T, rows=rows),
        grid_spec=pltpu.PrefetchScalarGridSpec(
            num_scalar_prefetch=6, grid=(1,), in_specs=[],
            out_specs=pl.BlockSpec(memory_space=pltpu.SMEM)),
        out_shape=jax.ShapeDtypeStruct((rows,), jnp.int32),
        compiler_params=_cparams(("arbitrary",)),
        name="moe_rowmap",
    )(e1, e2, p1, p2, off, counts)

    def w_ix(i, te, nu, rt):
        return (layer, te[jnp.minimum(i, nu[0] - 1)], 0, 0)

    ys = pl.pallas_call(
        _moe_expert_kernel,
        grid_spec=pltpu.PrefetchScalarGridSpec(
            num_scalar_prefetch=3, grid=(n_tiles,),
            in_specs=[pl.BlockSpec(memory_space=pl.ANY),
                      pl.BlockSpec((None, None, D, FF), w_ix), pl.BlockSpec((None, None, D, FF), w_ix),
                      pl.BlockSpec((None, None, FF, D), w_ix)],
            out_specs=pl.BlockSpec((MOE_TM, D), lambda i, te, nu, rt: (i, 0)),
            scratch_shapes=[pltpu.VMEM((2, MOE_TM, D), F32), pltpu.SemaphoreType.DMA((2,))]),
        out_shape=jax.ShapeDtypeStruct((rows, D), F32),
        compiler_params=_cparams(("arbitrary",)),
        name="moe_experts",
    )(tile_e, n_used, row_tok, xn, w_gate, w_up, w_down)

    tm_c = 256
    out = pl.pallas_call(
        functools.partial(_moe_combine_kernel, tm=tm_c),
        grid_spec=pltpu.PrefetchScalarGridSpec(
            num_scalar_prefetch=5, grid=(T // tm_c,),
            in_specs=[pl.BlockSpec((tm_c, D), lambda i, *_: (i, 0)), pl.BlockSpec((tm_c, LANES), lambda i, *_: (i, 0)),
                      pl.BlockSpec(memory_space=pl.ANY)],
            out_specs=pl.BlockSpec((tm_c, D), lambda i, *_: (i, 0)),
            scratch_shapes=[pltpu.VMEM((2, tm_c, D), F32), pltpu.VMEM((2, tm_c, D), F32),
                            pltpu.SemaphoreType.DMA((2,))]),
        out_shape=jax.ShapeDtypeStruct((T, D), F32),
        compiler_params=_cparams(("arbitrary",)),
        name="moe_combine",
    )(e1, e2, p1, p2, off, x2, rw, ys)
    return out.reshape(B, S, D)


def kernel(x, mem, norm_mix, norm_cross, norm_mem, norm_ffn, norm_final,
           ab_w_in, ml_conv_w, ml_conv_b, ml_gate_b, rw_mu, rw_w0, rw_w_up, rw_a0, rw_a_up, rw_g_up,
           rw_k_k, rw_k_a, rw_r_k, rw_ln_w, rw_ln_b, ab_w_out,
           nsa_w_in, nsa_gate_b, cmp_pos, cmp_w1, cmp_w2, nsa_w_out,
           ca_wq, ca_wk, ca_wv, ca_wo,
           moe_wg, moe_bg, moe_we, moe_be, moe_w_gate, moe_w_up, moe_w_down):
    B, S, D = x.shape
    for l in range(DEPTH):
        j = l // 2
        if l % 2 == 0:
            x = _ab_mixer(x, norm_mix[l], ab_w_in[j], ml_conv_w[j], ml_conv_b[j], ml_gate_b[j], rw_mu[j], rw_w0[j],
                          rw_w_up[j], rw_a0[j], rw_a_up[j], rw_g_up[j], rw_k_k[j], rw_k_a[j], rw_r_k[j],
                          rw_ln_w[j], rw_ln_b[j], ab_w_out[j])
        else:
            x = _nsa(x, norm_mix[l], nsa_w_in[j], nsa_gate_b[j], cmp_pos[j], cmp_w1[j], cmp_w2[j], nsa_w_out[j])
        x = _cross_attn(x, norm_cross[l], mem, norm_mem[l], ca_wq[l], ca_wk[l], ca_wv[l], ca_wo[l])
        x = _hier_moe(x, norm_ffn[l], moe_wg[l], moe_bg[l], moe_we[l], moe_be[l],
                      moe_w_gate, moe_w_up, moe_w_down, l)
    return _rmsnorm_rows(x.reshape(B * S, D), norm_final).reshape(B, S, D)
```

```python
import functools
import math

import jax
import jax.numpy as jnp
import numpy as np
from jax import lax
from jax.experimental import pallas as pl
from jax.experimental.pallas import tpu as pltpu

F32 = jnp.float32
BF16 = jnp.bfloat16

D_MODEL = 2048
DEPTH = 2
EPS = 1e-6
ROPE_THETA = 500000.0

ML_HEADS = 4
ML_DV = D_MODEL // 2 // ML_HEADS
ML_DK = ML_DV // 2
ML_CHUNK = 64
ML_CONV = 4
RW_HEAD = 64
RW_HEADS = D_MODEL // 2 // RW_HEAD
RW_DIM = RW_HEADS * RW_HEAD
RW_LORA_W = 64
RW_LORA_A = 64
RW_LORA_G = 128
RW_LN_EPS = 64e-5
ML_SPLITS = (2 * ML_HEADS * ML_DK, ML_HEADS * ML_DV, ML_HEADS * ML_DV, 2 * ML_HEADS)
RW_SPLITS = (RW_DIM, RW_DIM, RW_DIM, RW_LORA_W, RW_LORA_A, RW_LORA_G)
ML_COLS = sum(ML_SPLITS)
RW_COLS = sum(RW_SPLITS)

NSA_HEADS = 16
NSA_KV = 4
NSA_REP = NSA_HEADS // NSA_KV
NSA_HD = D_MODEL // NSA_HEADS
ROPE_DIM = NSA_HD // 4
CMP_BLOCK = 32
CMP_STRIDE = 16
SEL_BLOCK = 64
SEL_TOPK = 16
WINDOW = 512

CA_HEADS = 4
CA_HD = D_MODEL // CA_HEADS

MOE_GROUPS = 8
MOE_PER_GROUP = 8
MOE_EXPERTS = MOE_GROUPS * MOE_PER_GROUP
MOE_TOPK = 2
MOE_FF = D_MODEL // 4

VMEM_LIMIT = 48 * 1024 * 1024
LANES = 128
SUBLANES = 8
NT_DIMS = (((1,), (1,)), ((), ()))


def _cparams(sem):
    return pltpu.CompilerParams(dimension_semantics=sem, vmem_limit_bytes=VMEM_LIMIT)


def _mm_kernel(*refs, has_norm, has_res):
    it = iter(refs)
    x_ref = next(it)
    w_ref = next(it)
    nw_ref = next(it) if has_norm else None
    r_ref = next(it) if has_res else None
    o_ref = next(it)
    xs_ref = next(it)

    @pl.when(pl.program_id(1) == 0)
    def _():
        x = x_ref[...]
        if has_norm:
            ms = jnp.mean(x * x, axis=-1, keepdims=True)
            x = x * lax.rsqrt(ms + EPS) * nw_ref[...]
        xs_ref[...] = x.astype(BF16)

    acc = jnp.dot(xs_ref[...], w_ref[...], preferred_element_type=F32)
    if has_res:
        acc = acc + r_ref[...]
    o_ref[...] = acc


def _matmul(x, w_bf16, *, norm_w=None, residual=None, tm=1024, tn=512):
    M, K = x.shape
    N = w_bf16.shape[1]
    tm = min(tm, M)
    tn = min(tn, N)
    assert M % tm == 0 and N % tn == 0, (M, N, tm, tn)
    has_norm = norm_w is not None
    has_res = residual is not None
    in_specs = [pl.BlockSpec((tm, K), lambda i, j: (i, 0)),
                pl.BlockSpec((K, tn), lambda i, j: (0, j))]
    args = [x, w_bf16]
    if has_norm:
        in_specs.append(pl.BlockSpec((1, K), lambda i, j: (0, 0)))
        args.append(norm_w.reshape(1, K))
    if has_res:
        in_specs.append(pl.BlockSpec((tm, tn), lambda i, j: (i, j)))
        args.append(residual)
    return pl.pallas_call(
        functools.partial(_mm_kernel, has_norm=has_norm, has_res=has_res),
        grid=(M // tm, N // tn),
        in_specs=in_specs,
        out_specs=pl.BlockSpec((tm, tn), lambda i, j: (i, j)),
        out_shape=jax.ShapeDtypeStruct((M, N), F32),
        scratch_shapes=[pltpu.VMEM((tm, K), BF16)],
        compiler_params=_cparams(("parallel", "arbitrary")),
        name="mm_norm" if has_norm else "mm",
    )(*args)


def _mm2_kernel(xa_ref, xb_ref, wa_ref, wb_ref, r_ref, o_ref, xas_ref, xbs_ref):
    @pl.when(pl.program_id(1) == 0)
    def _():
        xas_ref[...] = xa_ref[...].astype(BF16)
        xbs_ref[...] = xb_ref[...].astype(BF16)

    o_ref[...] = (jnp.dot(xas_ref[...], wa_ref[...], preferred_element_type=F32)
                  + jnp.dot(xbs_ref[...], wb_ref[...], preferred_element_type=F32) + r_ref[...])


def _matmul_pair(xa, wa_bf16, xb, wb_bf16, residual, tm=1024, tn=512):
    M, Ka = xa.shape
    Kb = xb.shape[1]
    N = wa_bf16.shape[1]
    tm = min(tm, M)
    tn = min(tn, N)
    assert M % tm == 0 and N % tn == 0, (M, N, tm, tn)
    return pl.pallas_call(
        _mm2_kernel,
        grid=(M // tm, N // tn),
        in_specs=[pl.BlockSpec((tm, Ka), lambda i, j: (i, 0)), pl.BlockSpec((tm, Kb), lambda i, j: (i, 0)),
                  pl.BlockSpec((Ka, tn), lambda i, j: (0, j)), pl.BlockSpec((Kb, tn), lambda i, j: (0, j)),
                  pl.BlockSpec((tm, tn), lambda i, j: (i, j))],
        out_specs=pl.BlockSpec((tm, tn), lambda i, j: (i, j)),
        out_shape=jax.ShapeDtypeStruct((M, N), F32),
        scratch_shapes=[pltpu.VMEM((tm, Ka), BF16), pltpu.VMEM((tm, Kb), BF16)],
        compiler_params=_cparams(("parallel", "arbitrary")),
        name="mm_pair",
    )(xa, xb, wa_bf16, wb_bf16, residual)


def _pad_cols(w, mult):
    n = w.shape[1]
    pad = (-n) % mult
    if pad:
        w = jnp.pad(w, ((0, 0), (0, pad)))
    return w


def _rmsnorm_kernel(x_ref, w_ref, o_ref):
    x = x_ref[...]
    ms = jnp.mean(x * x, axis=-1, keepdims=True)
    o_ref[...] = x * lax.rsqrt(ms + EPS) * w_ref[...]


def _rmsnorm_rows(x, w, tm=512):
    M, K = x.shape
    return pl.pallas_call(
        _rmsnorm_kernel,
        grid=(M // tm,),
        in_specs=[pl.BlockSpec((tm, K), lambda i: (i, 0)), pl.BlockSpec((1, K), lambda i: (0, 0))],
        out_specs=pl.BlockSpec((tm, K), lambda i: (i, 0)),
        out_shape=jax.ShapeDtypeStruct((M, K), F32),
        compiler_params=_cparams(("parallel",)),
        name="rmsnorm",
    )(x, w.reshape(1, K))


def _rwkv_scan_kernel(r_ref, w_ref, k_ref, v_ref, kk_ref, kb_ref, y_ref, s_ref, *, tb, nb, npairs):
    @pl.when(pl.program_id(1) == 0)
    def _():
        s_ref[...] = jnp.zeros_like(s_ref)

    row = lax.broadcasted_iota(jnp.int32, (LANES, LANES), 0)
    col = lax.broadcasted_iota(jnp.int32, (LANES, LANES), 1)
    hmat = jnp.where((row // RW_HEAD) == (col // RW_HEAD), 1.0, 0.0).astype(BF16)
    vrow = lax.broadcasted_iota(jnp.int32, (RW_HEAD, LANES), 0)
    vcol = lax.broadcasted_iota(jnp.int32, (RW_HEAD, LANES), 1)
    diag = jnp.where((vcol % RW_HEAD) == vrow, 1.0, 0.0).astype(BF16)
    row16 = lax.broadcasted_iota(jnp.int32, (2 * SUBLANES, LANES), 0)
    lane16 = lax.broadcasted_iota(jnp.int32, (2 * SUBLANES, LANES), 1)
    head16 = (row16 // SUBLANES) == (lane16 // RW_HEAD)
    row8 = lax.broadcasted_iota(jnp.int32, (SUBLANES, LANES), 0)
    lane8 = lax.broadcasted_iota(jnp.int32, (SUBLANES, LANES), 1)
    nq = npairs // 2
    lns = [pl.ds(p * LANES, LANES) for p in range(npairs)]

    def group_sum(parts):
        lhs = jnp.concatenate([q.astype(BF16) for q in parts], axis=0)
        out = jnp.dot(lhs, hmat, preferred_element_type=F32)
        return [out[i * RW_HEAD:(i + 1) * RW_HEAD] for i in range(len(parts))]

    def step(t8, carry):
        rows = pl.ds(pl.multiple_of(t8 * SUBLANES, SUBLANES), SUBLANES)

        def tiles(ref):
            return [[ref[b, rows, ln] for ln in lns] for b in range(nb)]

        kk8, w8, kb8, k8, v8, r8 = (tiles(kk_ref), tiles(w_ref), tiles(kb_ref), tiles(k_ref), tiles(v_ref),
                                    tiles(r_ref))
        s = [[s_ref[b, p] for p in range(npairs)] for b in range(nb)]
        vcols = [[lax.dot_general(diag, jnp.where(head16, jnp.concatenate([v8[b][p]] * 2, axis=0), 0.0).astype(BF16),
                                  NT_DIMS, preferred_element_type=F32).astype(BF16)
                  for p in range(npairs)] for b in range(nb)]
        vk = [[None] * npairs for _ in range(nb)]
        for b in range(nb):
            for p in range(npairs):
                ksel = jnp.where(head16, jnp.concatenate([k8[b][p]] * 2, axis=0), 0.0)
                rhs = jnp.concatenate([jnp.where((row16 % SUBLANES) == j, ksel, 0.0) for j in range(SUBLANES)],
                                      axis=1).astype(BF16)
                vk[b][p] = jnp.dot(vcols[b][p], rhs, preferred_element_type=F32)
        for j in range(SUBLANES):
            sl = slice(j, j + 1)
            for b in range(nb):
                sk = group_sum([s[b][i] * kk8[b][i][sl] for i in range(npairs)])
                s[b] = [s[b][i] * w8[b][i][sl] - sk[i] * kb8[b][i][sl] + vk[b][i][:, j * LANES:(j + 1) * LANES]
                        for i in range(npairs)]
            for b in range(nb):
                for q in range(nq):
                    p0, p1 = 2 * q, 2 * q + 1
                    a = jnp.where((row8 == 0) & (lane8 < RW_HEAD), r8[b][p0][sl],
                        jnp.where((row8 == 1) & (lane8 >= RW_HEAD), r8[b][p0][sl],
                        jnp.where((row8 == 2) & (lane8 < RW_HEAD), r8[b][p1][sl],
                        jnp.where((row8 == 3) & (lane8 >= RW_HEAD), r8[b][p1][sl], 0.0))))
                    st = jnp.concatenate([s[b][p0], s[b][p1]], axis=0).astype(BF16)
                    yq = lax.dot_general(a.astype(BF16), st, NT_DIMS, preferred_element_type=F32)
                    base = ((t8 * SUBLANES + j) * nq + q) * SUBLANES
                    y_ref[b, pl.ds(pl.multiple_of(base, SUBLANES), SUBLANES), :] = yq
        for b in range(nb):
            for p in range(npairs):
                s_ref[b, p] = s[b][p]
        return carry

    lax.fori_loop(0, tb // SUBLANES, step, 0)


def _rwkv_scan(r, w, k, v, kk, kb, tb=64, nb=4):
    B, S, C = r.shape
    npairs = C // LANES
    tb = min(tb, S)
    nb = min(nb, B)
    spec = pl.BlockSpec((nb, tb, C), lambda b, t: (b, t, 0))
    yrows = (npairs // 2) * SUBLANES
    return pl.pallas_call(
        functools.partial(_rwkv_scan_kernel, tb=tb, nb=nb, npairs=npairs),
        grid=(B // nb, S // tb),
        in_specs=[spec] * 6,
        out_specs=pl.BlockSpec((nb, tb * yrows, LANES), lambda b, t: (b, t, 0)),
        out_shape=jax.ShapeDtypeStruct((B, S * yrows, LANES), F32),
        scratch_shapes=[pltpu.VMEM((nb, npairs, RW_HEAD, LANES), F32)],
        compiler_params=_cparams(("parallel", "arbitrary")),
        name="rwkv_scan",
    )(r, w, k, v, kk, kb)


NSA_NEG = -1e30
NSA_FORCED = 1e30
NSA_REMOVED = -3e30
NSA_COL_Q, NSA_COL_KC, NSA_COL_VC, NSA_COL_KS, NSA_COL_VS, NSA_COL_KW, NSA_COL_VW, NSA_COL_GL = (
    0, 16, 20, 24, 28, 32, 36, 40)


def _rope_tables(S):
    half = ROPE_DIM // 2
    inv = 1.0 / (ROPE_THETA ** (jnp.arange(half, dtype=F32) / half))
    ang = jnp.arange(S, dtype=F32)[:, None] * inv[None, :]
    cos, sin = jnp.cos(ang), jnp.sin(ang)
    one = jnp.ones((S, NSA_HD - ROPE_DIM), F32)
    zero = jnp.zeros((S, NSA_HD - ROPE_DIM), F32)
    zh = jnp.zeros((S, half), F32)
    return (jnp.concatenate([cos, cos, one], axis=1), jnp.concatenate([zh, sin, zero], axis=1),
            jnp.concatenate([-sin, zh, zero], axis=1))


def _rope_rows(x, c, s1, s2):
    half = ROPE_DIM // 2
    return x * c + pltpu.roll(x, half, 1) * s1 + pltpu.roll(x, NSA_HD - half, 1) * s2


def _nsa_prep_q_kernel(x_ref, c_ref, s1_ref, s2_ref, q_ref, qr_ref):
    c, s1, s2 = c_ref[...], s1_ref[...], s2_ref[...]
    for h in range(NSA_HEADS):
        cols = slice(h * NSA_HD, (h + 1) * NSA_HD)
        x = x_ref[0, :, cols] * (NSA_HD ** -0.5)
        q_ref[0, :, cols] = x.astype(BF16)
        qr_ref[0, :, cols] = _rope_rows(x, c, s1, s2).astype(BF16)


def _nsa_prep_kv_kernel(ks_ref, kw_ref, vs_ref, vw_ref, c_ref, s1_ref, s2_ref, kso_ref, kwo_ref, vst_ref, vwt_ref):
    c, s1, s2 = c_ref[...], s1_ref[...], s2_ref[...]
    kso_ref[0, 0] = _rope_rows(ks_ref[0], c, s1, s2).astype(BF16)
    kwo_ref[0, 0] = _rope_rows(kw_ref[0], c, s1, s2).astype(BF16)
    vst_ref[0, 0] = vs_ref[0].T.astype(BF16)
    vwt_ref[0, 0] = vw_ref[0].T.astype(BF16)


def _nsa_compress_kernel(kc_ref, vc_ref, pe_ref, w1_ref, w2_ref, kcmp_ref, vcmpt_ref, *, ncp):
    for which, x_ref in enumerate((kc_ref, vc_ref)):
        za = jnp.zeros((ncp, NSA_HD), F32)
        zb = jnp.zeros((ncp, NSA_HD), F32)
        for p in range(CMP_STRIDE):
            xp = x_ref[pl.ds(p, ncp, stride=CMP_STRIDE), :]
            za = za + jnp.dot((xp + pe_ref[which, p:p + 1, :]).astype(BF16), w1_ref[which, p],
                              preferred_element_type=F32)
            zb = zb + jnp.dot((xp + pe_ref[which, CMP_STRIDE + p:CMP_STRIDE + p + 1, :]).astype(BF16),
                              w1_ref[which, CMP_STRIDE + p], preferred_element_type=F32)
        pre = za + pltpu.roll(zb, ncp - 1, 0)
        out = jnp.dot(jax.nn.gelu(pre).astype(BF16), w2_ref[which], preferred_element_type=F32)
        if which == 0:
            kcmp_ref[0, 0] = out.astype(BF16)
        else:
            vcmpt_ref[0, 0] = out.T.astype(BF16)


def _nsa_attn_kernel(qt_ref, qrt_ref, kcmp_ref, vcmpt_ref, ks_ref, kw_ref, vst_ref, vwt_ref, gl_ref, gb_ref, ovt_ref,
                     o_ref, sel_ref, ms_ref, ls_ref, accs_ref, *, tq, kt, ncp, nsel):
    R = NSA_REP
    g = pl.program_id(1)
    qi = pl.program_id(2)
    t0 = qi * tq
    q_rows = jnp.concatenate([qt_ref[0, :, r * NSA_HD:(r + 1) * NSA_HD] for r in range(R)], axis=0)
    qr_rows = jnp.concatenate([qrt_ref[0, :, r * NSA_HD:(r + 1) * NSA_HD] for r in range(R)], axis=0)
    qpos = t0 + lax.broadcasted_iota(jnp.int32, (1, tq), 1)

    def per_head(fn, a):
        return jnp.concatenate([fn(a[:, r * tq:(r + 1) * tq]) for r in range(R)], axis=1)

    nwk = WINDOW + tq
    kw0 = pl.multiple_of(jnp.maximum(t0 - WINDOW, 0), tq)
    dist = qpos - (kw0 + lax.broadcasted_iota(jnp.int32, (nwk, 1), 0))
    inwin = (dist >= 0) & (dist < WINDOW)

    def window_head(r):
        sw = lax.dot_general(kw_ref[0, 0, pl.ds(kw0, nwk), :], qr_rows[r * tq:(r + 1) * tq], NT_DIMS,
                             preferred_element_type=F32)
        sw = jnp.where(inwin, sw, NSA_NEG)
        ew = jnp.exp(sw - jnp.max(sw, axis=0, keepdims=True))
        return (jnp.dot(vwt_ref[0, 0, :, pl.ds(kw0, nwk)], ew.astype(BF16), preferred_element_type=F32)
                * (1.0 / jnp.sum(ew, axis=0, keepdims=True)))

    s = lax.dot_general(kcmp_ref[0, 0], q_rows, NT_DIMS, preferred_element_type=F32)
    cend = lax.broadcasted_iota(jnp.int32, (ncp, 1), 0) * CMP_STRIDE + (CMP_BLOCK - 1)
    vis = cend <= qpos
    s = per_head(lambda a: jnp.where(vis, a, NSA_NEG), s)
    m = jnp.max(s, axis=0, keepdims=True)
    e = per_head(lambda a: jnp.where(vis, a, 0.0), jnp.exp(s - m))
    d = jnp.sum(e, axis=0, keepdims=True)
    p = e * (1.0 / jnp.where(d > 0, d, 1.0))
    ocmp_t = jnp.dot(vcmpt_ref[0, 0], p.astype(BF16), preferred_element_type=F32)

    psum = p[:, 0:tq]
    for r in range(1, R):
        psum = psum + p[:, r * tq:(r + 1) * tq]
    imp = jnp.dot(ovt_ref[...], psum, preferred_element_type=F32, precision=lax.Precision.HIGHEST)
    sidx = lax.broadcasted_iota(jnp.int32, (nsel, tq), 0)
    cur = qpos // SEL_BLOCK
    forced = (sidx == 0) | (sidx == cur) | (sidx == cur - 1)
    score = jnp.where(forced, NSA_FORCED, jnp.where(sidx <= cur, imp, NSA_NEG))
    sel = jnp.zeros((nsel, tq), F32)
    owin_heads = []
    for it in range(SEL_TOPK):
        mx = jnp.max(score, axis=0, keepdims=True)
        idx = jnp.min(jnp.where(score == mx, sidx, nsel), axis=0, keepdims=True)
        pick = (sidx == idx) & (mx > 0.5 * NSA_NEG)
        sel = jnp.where(pick, 1.0, sel)
        score = jnp.where(pick, NSA_REMOVED, score)
        if it % (SEL_TOPK // R) == 0:
            owin_heads.append(window_head(it // (SEL_TOPK // R)))
    owin_t = jnp.concatenate(owin_heads, axis=1)
    sel_ref[...] = sel

    def online_update(s, valid, v_t, m_ref, l_ref, acc_ref):
        s = per_head(lambda a: jnp.where(valid, a, NSA_NEG), s)
        m_old = m_ref[...]
        m_new = jnp.maximum(m_old, jnp.max(s, axis=0, keepdims=True))
        alpha = jnp.exp(m_old - m_new)
        pexp = jnp.exp(s - m_new)
        l_ref[...] = alpha * l_ref[...] + jnp.sum(pexp, axis=0, keepdims=True)
        acc_ref[...] = alpha * acc_ref[...] + jnp.dot(v_t, pexp.astype(BF16), preferred_element_type=F32)
        m_ref[...] = m_new

    ms_ref[...] = jnp.full(ms_ref.shape, NSA_NEG, F32)
    ls_ref[...] = jnp.zeros(ls_ref.shape, F32)
    accs_ref[...] = jnp.zeros(accs_ref.shape, F32)

    nblk = kt // SEL_BLOCK

    def sel_body(c, carry):
        k0 = pl.multiple_of(c * kt, kt)
        s = lax.dot_general(ks_ref[0, 0, pl.ds(k0, kt), :], qr_rows, NT_DIMS, preferred_element_type=F32)
        selrows = sel_ref[pl.ds(pl.multiple_of(c * nblk, nblk), nblk), :]
        selexp = jnp.concatenate(
            [jnp.broadcast_to(selrows[b:b + 1], (SEL_BLOCK, tq)) for b in range(nblk)], axis=0)
        kpos = k0 + lax.broadcasted_iota(jnp.int32, (kt, 1), 0)
        valid = (selexp > 0.5) & (kpos <= qpos)
        online_update(s, valid, vst_ref[0, 0, :, pl.ds(k0, kt)], ms_ref, ls_ref, accs_ref)
        return carry

    lax.fori_loop(0, t0 // kt + 1, sel_body, 0)
    oslc_t = accs_ref[...] * (1.0 / ls_ref[...])

    gates_t = jax.nn.sigmoid(gl_ref[0] + gb_ref[...]).T
    rid = lax.broadcasted_iota(jnp.int32, (LANES, 1), 0)

    def gate_row(which, r):
        return jnp.sum(jnp.where(rid == which * NSA_HEADS + g * R + r, gates_t, 0.0), axis=0, keepdims=True)

    for r in range(R):
        cols = slice(r * tq, (r + 1) * tq)
        o_t = (gate_row(0, r) * ocmp_t[:, cols] + gate_row(1, r) * oslc_t[:, cols]
               + gate_row(2, r) * owin_t[:, cols])
        o_ref[0, :, r * NSA_HD:(r + 1) * NSA_HD] = o_t.T


def _nsa_attention(p3, gate_b, cmp_pos, cmp_w1, cmp_w2, *, tq=512, kt=512, tp=512):
    B, S, _ = p3.shape
    G, R, HD = NSA_KV, NSA_REP, NSA_HD
    assert CMP_BLOCK == 2 * CMP_STRIDE and S % kt == 0 and WINDOW % tq == 0 and kt % tq == 0 and tq % LANES == 0
    ncp = S // CMP_STRIDE
    nsel = S // SEL_BLOCK
    n_cmp = (S - CMP_BLOCK) // CMP_STRIDE + 1
    c_tab, s1_tab, s2_tab = _rope_tables(S)
    tab_spec3 = pl.BlockSpec((tp, HD), lambda b, h, i: (i, 0))

    qw = NSA_HEADS * HD
    tab_spec2 = pl.BlockSpec((tp, HD), lambda b, i: (i, 0))
    q_s, q_r = pl.pallas_call(
        _nsa_prep_q_kernel,
        grid=(B, S // tp),
        in_specs=[pl.BlockSpec((1, tp, qw), lambda b, i: (b, i, 0)), tab_spec2, tab_spec2, tab_spec2],
        out_specs=[pl.BlockSpec((1, tp, qw), lambda b, i: (b, i, 0))] * 2,
        out_shape=[jax.ShapeDtypeStruct((B, S, qw), BF16)] * 2,
        compiler_params=_cparams(("parallel", "parallel")),
        name="nsa_prep_q",
    )(p3, c_tab, s1_tab, s2_tab)

    def col_spec(col0):
        return pl.BlockSpec((1, tp, HD), lambda b, g, i: (b, i, col0 + g))

    ks_rot, kw_rot, vs_t, vw_t = pl.pallas_call(
        _nsa_prep_kv_kernel,
        grid=(B, G, S // tp),
        in_specs=[col_spec(NSA_COL_KS), col_spec(NSA_COL_KW), col_spec(NSA_COL_VS), col_spec(NSA_COL_VW),
                  tab_spec3, tab_spec3, tab_spec3],
        out_specs=[pl.BlockSpec((1, 1, tp, HD), lambda b, g, i: (b, g, i, 0))] * 2
        + [pl.BlockSpec((1, 1, HD, tp), lambda b, g, i: (b, g, 0, i))] * 2,
        out_shape=[jax.ShapeDtypeStruct((B, G, S, HD), BF16)] * 2 + [jax.ShapeDtypeStruct((B, G, HD, S), BF16)] * 2,
        compiler_params=_cparams(("parallel", "parallel", "parallel")),
        name="nsa_prep_kv",
    )(p3, p3, p3, p3, c_tab, s1_tab, s2_tab)

    k_cmp, v_cmp_t = pl.pallas_call(
        functools.partial(_nsa_compress_kernel, ncp=ncp),
        grid=(B, G),
        in_specs=[pl.BlockSpec((None, S, HD), lambda b, g: (b, 0, NSA_COL_KC + g)),
                  pl.BlockSpec((None, S, HD), lambda b, g: (b, 0, NSA_COL_VC + g)),
                  pl.BlockSpec((2, CMP_BLOCK, HD), lambda b, g: (0, 0, 0)),
                  pl.BlockSpec((2, CMP_BLOCK, HD, HD), lambda b, g: (0, 0, 0, 0)),
                  pl.BlockSpec((2, HD, HD), lambda b, g: (0, 0, 0))],
        out_specs=[pl.BlockSpec((1, 1, ncp, HD), lambda b, g: (b, g, 0, 0)),
                   pl.BlockSpec((1, 1, HD, ncp), lambda b, g: (b, g, 0, 0))],
        out_shape=[jax.ShapeDtypeStruct((B, G, ncp, HD), BF16), jax.ShapeDtypeStruct((B, G, HD, ncp), BF16)],
        compiler_params=_cparams(("parallel", "parallel")),
        name="nsa_compress",
    )(p3, p3, cmp_pos, cmp_w1.astype(BF16), cmp_w2.astype(BF16))

    c0 = np.arange(ncp)[None, :] * CMP_STRIDE
    s0 = np.arange(nsel)[:, None] * SEL_BLOCK
    ov_t = np.clip(np.minimum(c0 + CMP_BLOCK, s0 + SEL_BLOCK) - np.maximum(c0, s0), 0, None) / CMP_BLOCK
    ov_t = ov_t * (np.arange(ncp)[None, :] < n_cmp)
    gb = jnp.pad(gate_b, (0, LANES - gate_b.shape[0])).reshape(1, LANES)
    ncols = R * tq

    def full_kv(shape):
        return pl.BlockSpec((1, 1) + shape, lambda b, g, i: (b, g, 0, 0))

    return pl.pallas_call(
        functools.partial(_nsa_attn_kernel, tq=tq, kt=kt, ncp=ncp, nsel=nsel),
        grid=(B, G, S // tq),
        in_specs=[pl.BlockSpec((1, tq, R * HD), lambda b, g, i: (b, i, g)),
                  pl.BlockSpec((1, tq, R * HD), lambda b, g, i: (b, i, g)),
                  full_kv((ncp, HD)), full_kv((HD, ncp)),
                  full_kv((S, HD)), full_kv((S, HD)), full_kv((HD, S)), full_kv((HD, S)),
                  pl.BlockSpec((1, tq, LANES), lambda b, g, i: (b, i, NSA_COL_GL)),
                  pl.BlockSpec((1, LANES), lambda b, g, i: (0, 0)),
                  pl.BlockSpec((nsel, ncp), lambda b, g, i: (0, 0))],
        out_specs=pl.BlockSpec((1, tq, R * HD), lambda b, g, i: (b, i, g)),
        out_shape=jax.ShapeDtypeStruct((B, S, NSA_HEADS * HD), F32),
        scratch_shapes=[pltpu.VMEM((nsel, tq), F32),
                        pltpu.VMEM((1, ncols), F32), pltpu.VMEM((1, ncols), F32), pltpu.VMEM((HD, ncols), F32)],
        compiler_params=_cparams(("parallel", "parallel", "arbitrary")),
        name="nsa_attn",
    )(q_s, q_r, k_cmp, v_cmp_t, ks_rot, kw_rot, vs_t, vw_t, p3, gb, jnp.asarray(ov_t, F32))


AB_COL_GATES = 50 * LANES
AB_COLS_PAD = 52 * LANES


def _ab_permute_cols(w_in):
    ml, rw = w_in[:, :ML_COLS], w_in[:, ML_COLS:]
    main, gif = ml[:, :ML_COLS - 2 * ML_HEADS], ml[:, ML_COLS - 2 * ML_HEADS:]
    w = jnp.concatenate([main, rw, gif], axis=1)
    return jnp.pad(w, ((0, 0), (0, AB_COLS_PAD - w.shape[1])))


def _mlstm_kernel(gb_ref, qk_ref, v_ref, o_ref, gcol_ref, grow_ref, cw_ref, cb_ref, y_ref,
                  halo_ref, c_ref, n_ref, m_ref, qc_ref, kc_ref, *, tb, nb):
    H, DK, DV, L = ML_HEADS, ML_DK, ML_DV, ML_CHUNK
    HALO = SUBLANES

    @pl.when(pl.program_id(1) == 0)
    def _():
        halo_ref[...] = jnp.zeros_like(halo_ref)
        c_ref[...] = jnp.zeros_like(c_ref)
        n_ref[...] = jnp.zeros_like(n_ref)
        m_ref[...] = jnp.zeros_like(m_ref)

    for b in range(nb):
        x = qk_ref[b]
        xe = jnp.concatenate([halo_ref[b], x], axis=0)
        y = cb_ref[...]
        for j in range(ML_CONV):
            lo = HALO - (ML_CONV - 1) + j
            y = y + xe[lo:lo + tb] * cw_ref[j:j + 1, :]
        halo_ref[b] = x[tb - HALO:]
        y = y * jax.nn.sigmoid(y)
        qc_ref[b] = y[:, :H * DK] * (DK ** -0.5)
        kc_ref[b] = y[:, H * DK:]

    r_i = lax.broadcasted_iota(jnp.int32, (L, L), 0)
    c_i = lax.broadcasted_iota(jnp.int32, (L, L), 1)
    tri = c_i <= r_i

    def chunk(c, carry):
        rows = pl.ds(pl.multiple_of(c * L, L), L)
        chains = [(b, h) for b in range(nb) for h in range(H)]
        n = len(chains)
        q = [qc_ref[b, rows, h * DK:(h + 1) * DK] for b, h in chains]
        k = [kc_ref[b, rows, h * DK:(h + 1) * DK] for b, h in chains]
        v = [v_ref[b, rows, h * DV:(h + 1) * DV].astype(BF16) for b, h in chains]
        gc = [gcol_ref[b, h, rows, :] for b, h in chains]
        gr = [grow_ref[b, h, c] for b, h in chains]
        i_col = [gc[j][:, 0:1] + gb_ref[h] for j, (b, h) in enumerate(chains)]
        lf_col = [jax.nn.log_sigmoid(gc[j][:, 1:2] + gb_ref[H + h]) for j, (b, h) in enumerate(chains)]
        i_row = [gr[j][0:1, :] + gb_ref[h] for j, (b, h) in enumerate(chains)]
        lf_row = [jax.nn.log_sigmoid(gr[j][1:2, :] + gb_ref[H + h]) for j, (b, h) in enumerate(chains)]
        b_col = [jnp.sum(jnp.where(tri, lf_row[j], 0.0), axis=1, keepdims=True) for j in range(n)]
        b_row = [jnp.sum(jnp.where(c_i >= r_i, lf_col[j], 0.0), axis=0, keepdims=True) for j in range(n)]
        dmat = [jnp.where(tri, b_col[j] - b_row[j] + i_row[j], NSA_NEG) for j in range(n)]
        m_old = [m_ref[b * H + h] for b, h in chains]
        inter = [b_col[j] + m_old[j] for j in range(n)]
        m_row = [jnp.maximum(inter[j], jnp.max(dmat[j], axis=1, keepdims=True)) for j in range(n)]
        w_in = [jnp.exp(dmat[j] - m_row[j]) for j in range(n)]
        w_st = [jnp.exp(inter[j] - m_row[j]) for j in range(n)]
        qb = [q[j].astype(BF16) for j in range(n)]
        s = [lax.dot_general(qb[j], k[j].astype(BF16), NT_DIMS, preferred_element_type=F32) * w_in[j]
             for j in range(n)]
        qc_state = [jnp.dot(qb[j], c_ref[b * H + h].astype(BF16), preferred_element_type=F32)
                    for j, (b, h) in enumerate(chains)]
        sv = [jnp.dot(s[j].astype(BF16), v[j], preferred_element_type=F32) for j in range(n)]
        for j, (b, h) in enumerate(chains):
            den = (w_st[j] * jnp.sum(q[j] * n_ref[b * H + h], axis=1, keepdims=True)
                   + jnp.sum(s[j], axis=1, keepdims=True))
            hid = (w_st[j] * qc_state[j] + sv[j]) * (1.0 / jnp.maximum(jnp.abs(den), jnp.exp(-m_row[j])))
            y_ref[b, rows, h * DV:(h + 1) * DV] = jax.nn.sigmoid(o_ref[b, rows, h * DV:(h + 1) * DV]) * hid
        b_last = [b_col[j][L - 1:L, :] for j in range(n)]
        g_key = [b_last[j] - b_col[j] + i_col[j] for j in range(n)]
        m_new = [jnp.maximum(b_last[j] + m_old[j], jnp.max(g_key[j], axis=0, keepdims=True)) for j in range(n)]
        wk = [jnp.exp(g_key[j] - m_new[j]) for j in range(n)]
        kv = [jnp.dot((k[j] * wk[j]).T.astype(BF16), v[j], preferred_element_type=F32) for j in range(n)]
        for j, (b, h) in enumerate(chains):
            bh = b * H + h
            decay = jnp.exp(b_last[j] + m_old[j] - m_new[j])
            c_ref[bh] = decay * c_ref[bh] + kv[j]
            n_ref[bh] = decay * n_ref[bh] + jnp.sum(wk[j] * k[j], axis=0, keepdims=True)
            m_ref[bh] = m_new[j]
        return carry

    lax.fori_loop(0, tb // L, chunk, 0)


def _mlstm(p3, conv_w, conv_b, gate_b, tb=256, nb=2):
    B, S, _ = p3.shape
    H, DK, DV, L = ML_HEADS, ML_DK, ML_DV, ML_CHUNK
    tb = min(tb, S)
    nb = min(nb, B)
    nc = S // L
    gif = p3[:, :, AB_COL_GATES:AB_COL_GATES + 2 * H].reshape(B, S, 2, H)
    gcol = gif.transpose(0, 3, 1, 2)
    grow = gif.reshape(B, nc, L, 2, H).transpose(0, 4, 1, 3, 2)
    qkw = 2 * H * DK
    return pl.pallas_call(
        functools.partial(_mlstm_kernel, tb=tb, nb=nb),
        grid_spec=pltpu.PrefetchScalarGridSpec(
            num_scalar_prefetch=1, grid=(B // nb, S // tb),
            in_specs=[pl.BlockSpec((nb, tb, qkw), lambda b, t, gb: (b, t, 0)),
                      pl.BlockSpec((nb, tb, H * DV), lambda b, t, gb: (b, t, 1)),
                      pl.BlockSpec((nb, tb, H * DV), lambda b, t, gb: (b, t, 2)),
                      pl.BlockSpec((nb, H, tb, 2), lambda b, t, gb: (b, 0, t, 0)),
                      pl.BlockSpec((nb, H, tb // L, 2, L), lambda b, t, gb: (b, 0, t, 0, 0)),
                      pl.BlockSpec((ML_CONV, qkw), lambda b, t, gb: (0, 0)),
                      pl.BlockSpec((1, qkw), lambda b, t, gb: (0, 0))],
            out_specs=pl.BlockSpec((nb, tb, H * DV), lambda b, t, gb: (b, t, 0)),
            scratch_shapes=[pltpu.VMEM((nb, SUBLANES, qkw), F32), pltpu.VMEM((nb * H, DK, DV), F32),
                            pltpu.VMEM((nb * H, 1, DK), F32), pltpu.VMEM((nb * H, 1, 1), F32),
                            pltpu.VMEM((nb, tb, H * DK), F32), pltpu.VMEM((nb, tb, H * DK), F32)]),
        out_shape=jax.ShapeDtypeStruct((B, S, H * DV), F32),
        compiler_params=_cparams(("parallel", "arbitrary")),
        name="mlstm",
    )(gate_b, p3, p3, p3, gcol, grow, conv_w, conv_b.reshape(1, qkw))


def _head_sum_bcast(a, hmat):
    parts = [jnp.dot(a[:, i * LANES:(i + 1) * LANES], hmat, preferred_element_type=F32,
                     precision=lax.Precision.HIGHEST) for i in range(a.shape[1] // LANES)]
    return jnp.concatenate(parts, axis=1)


def _head_hmat():
    row = lax.broadcasted_iota(jnp.int32, (LANES, LANES), 0)
    col = lax.broadcasted_iota(jnp.int32, (LANES, LANES), 1)
    return jnp.where((row // RW_HEAD) == (col // RW_HEAD), 1.0, 0.0).astype(F32)


def _rwkv_prep_kernel(r_ref, k_ref, v_ref, lo_ref, mu_ref, w0_ref, wup_ref, a0_ref, aup_ref, gup_ref, kk_ref2, ka_ref,
                      rk_ref, ro_ref, wo_ref, ko_ref, vo_ref, kko_ref, kbo_ref, go_ref, bo_ref, prev_ref, *, tb):
    @pl.when(pl.program_id(1) == 0)
    def _():
        prev_ref[...] = jnp.zeros_like(prev_ref)

    C = RW_DIM
    x = jnp.concatenate([r_ref[0], k_ref[0], v_ref[0], lo_ref[0]], axis=1)
    prev = prev_ref[...]
    row0 = lax.broadcasted_iota(jnp.int32, (tb, 1), 0) == 0
    shifted = jnp.where(row0, prev[SUBLANES - 1:SUBLANES, :], pltpu.roll(x, 1, 0))
    prev_ref[...] = x[tb - SUBLANES:]
    xm = x + (shifted - x) * mu_ref[...]
    r, k, v, lo = xm[:, :C], xm[:, C:2 * C], xm[:, 2 * C:3 * C], xm[:, 3 * C:]
    xwa = lo[:, :LANES]
    xg = lo[:, LANES:]
    lw = jnp.dot(jnp.tanh(xwa).astype(BF16), wup_ref[...], preferred_element_type=F32)
    la = jnp.dot(xwa.astype(BF16), aup_ref[...], preferred_element_type=F32)
    g = jnp.dot(jax.nn.sigmoid(xg).astype(BF16), gup_ref[...], preferred_element_type=F32)
    w = jnp.exp(-math.exp(-0.5) * jax.nn.sigmoid(w0_ref[...] + lw))
    a = jax.nn.sigmoid(a0_ref[...] + la)
    hmat = _head_hmat()
    kk = k * kk_ref2[...]
    kk = kk * lax.rsqrt(_head_sum_bcast(kk * kk, hmat) + 1e-12)
    k2 = k * (1.0 + (a - 1.0) * ka_ref[...])
    ro_ref[0] = r
    wo_ref[0] = w
    ko_ref[0] = k2
    vo_ref[0] = v
    kko_ref[0] = kk
    kbo_ref[0] = kk * a
    go_ref[0] = g
    bo_ref[0] = _head_sum_bcast(r * k2 * rk_ref[...], hmat) * v


def _rwkv_post_kernel(y_ref, b_ref, g_ref, lnw_ref, lnb_ref, o_ref, *, tb):
    nq = RW_DIM // LANES // 2
    yrows = nq * SUBLANES
    lane = lax.broadcasted_iota(jnp.int32, (tb, LANES), 1)
    parts = []
    for q in range(nq):
        a0, a1, a2, a3 = [y_ref[pl.ds(q * SUBLANES + i, tb, stride=yrows), :] for i in range(4)]
        parts.append(jnp.where(lane < RW_HEAD, a0, pltpu.roll(a1, RW_HEAD, 1)))
        parts.append(jnp.where(lane < RW_HEAD, pltpu.roll(a2, RW_HEAD, 1), a3))
    y = jnp.concatenate(parts, axis=1)
    hmat = _head_hmat()
    mu = _head_sum_bcast(y, hmat) * (1.0 / RW_HEAD)
    yc = y - mu
    var = _head_sum_bcast(yc * yc, hmat) * (1.0 / RW_HEAD)
    yn = yc * lax.rsqrt(var + RW_LN_EPS) * lnw_ref[...] + lnb_ref[...]
    o_ref[0] = (yn + b_ref[0]) * g_ref[0]


def _rwkv_branch(p3, mu, w0, w_up, a0, a_up, g_up, k_k, k_a, r_k, ln_w, ln_b, tb=256):
    B, S, _ = p3.shape
    C = RW_DIM
    tb = min(tb, S)
    row = lambda t: t.reshape(1, -1)
    wup = jnp.concatenate([w_up, jnp.zeros_like(a_up)], axis=0).astype(BF16)
    aup = jnp.concatenate([jnp.zeros_like(w_up), a_up], axis=0).astype(BF16)
    seq = pl.BlockSpec((1, tb, C), lambda b, t: (b, t, 0))
    par = pl.BlockSpec((1, C), lambda b, t: (0, 0))
    r, w, k, v, kk, kb, g, bonus = pl.pallas_call(
        functools.partial(_rwkv_prep_kernel, tb=tb),
        grid=(B, S // tb),
        in_specs=[pl.BlockSpec((1, tb, C), lambda b, t: (b, t, 3)), pl.BlockSpec((1, tb, C), lambda b, t: (b, t, 4)),
                  pl.BlockSpec((1, tb, C), lambda b, t: (b, t, 5)),
                  pl.BlockSpec((1, tb, 2 * LANES), lambda b, t: (b, t, 24)),
                  pl.BlockSpec((1, RW_COLS), lambda b, t: (0, 0)), par,
                  pl.BlockSpec((LANES, C), lambda b, t: (0, 0)), par, pl.BlockSpec((LANES, C), lambda b, t: (0, 0)),
                  pl.BlockSpec((RW_LORA_G, C), lambda b, t: (0, 0)), par, par, par],
        out_specs=[seq] * 8,
        out_shape=[jax.ShapeDtypeStruct((B, S, C), F32)] * 8,
        scratch_shapes=[pltpu.VMEM((SUBLANES, RW_COLS), F32)],
        compiler_params=_cparams(("parallel", "arbitrary")),
        name="rwkv_prep",
    )(p3, p3, p3, p3, row(mu), row(w0), wup, row(a0), aup, g_up.astype(BF16), row(k_k), row(k_a), row(r_k))
    y_raw = _rwkv_scan(r, w, k, v, kk, kb)
    yrows = y_raw.shape[1] // S
    return pl.pallas_call(
        functools.partial(_rwkv_post_kernel, tb=tb),
        grid=(B, S // tb),
        in_specs=[pl.BlockSpec((None, tb * yrows, LANES), lambda b, t: (b, t, 0)), seq, seq, par, par],
        out_specs=seq,
        out_shape=jax.ShapeDtypeStruct((B, S, C), F32),
        compiler_params=_cparams(("parallel", "parallel")),
        name="rwkv_post",
    )(y_raw, bonus, g, row(ln_w), row(ln_b))


def _ab_mixer(x, norm_w, w_in, conv_w, conv_b, gate_b, mu, w0, w_up, a0, a_up, g_up, k_k, k_a, r_k, ln_w, ln_b,
              w_out):
    B, S, D = x.shape
    T = B * S
    x2 = x.reshape(T, D)
    p3 = _matmul(x2, _ab_permute_cols(w_in).astype(BF16), norm_w=norm_w).reshape(B, S, AB_COLS_PAD)
    y_m = _mlstm(p3, conv_w, conv_b, gate_b)
    y_r = _rwkv_branch(p3, mu, w0, w_up, a0, a_up, g_up, k_k, k_a, r_k, ln_w, ln_b)
    hm = ML_HEADS * ML_DV
    return _matmul_pair(y_m.reshape(T, hm), w_out[:hm].astype(BF16), y_r.reshape(T, RW_DIM),
                        w_out[hm:].astype(BF16), x2).reshape(B, S, D)


def _nsa(x, norm_w, w_in, gate_b, cmp_pos, cmp_w1, cmp_w2, w_out):
    B, S, D = x.shape
    T = B * S
    x2 = x.reshape(T, D)
    p = _matmul(x2, _pad_cols(w_in, 512).astype(BF16), norm_w=norm_w)
    o = _nsa_attention(p.reshape(B, S, -1), gate_b, cmp_pos, cmp_w1, cmp_w2)
    return _matmul(o.reshape(T, NSA_HEADS * NSA_HD), w_out.astype(BF16), residual=x2).reshape(B, S, D)


def _cross_attn_kernel(q_ref, k_ref, v_ref, o_ref):
    heads = [slice(h * CA_HD, (h + 1) * CA_HD) for h in range(CA_HEADS)]
    s = [lax.dot_general(q_ref[:, c].astype(BF16), k_ref[:, c].astype(BF16), NT_DIMS,
                         preferred_element_type=F32) * (CA_HD ** -0.5) for c in heads]
    e = [jnp.exp(a - jnp.max(a, axis=-1, keepdims=True)) for a in s]
    p = [a * (1.0 / jnp.sum(a, axis=-1, keepdims=True)) for a in e]
    for c, a in zip(heads, p):
        o_ref[:, c] = jnp.dot(a.astype(BF16), v_ref[:, c].astype(BF16), preferred_element_type=F32)


def _cross_attn(x, norm_w, mem, norm_mem, wq, wk, wv, wo, tq=512):
    B, S, D = x.shape
    T = B * S
    M = mem.shape[1]
    x2 = x.reshape(T, D)
    q = _matmul(x2, wq.astype(BF16), norm_w=norm_w)
    kv = _matmul(mem.reshape(B * M, D), jnp.concatenate([wk, wv], axis=1).astype(BF16), norm_w=norm_mem)
    nq = S // tq
    o = pl.pallas_call(
        _cross_attn_kernel,
        grid=(B, nq),
        in_specs=[pl.BlockSpec((tq, D), lambda b, i: (b * nq + i, 0)),
                  pl.BlockSpec((M, D), lambda b, i: (b, 0)), pl.BlockSpec((M, D), lambda b, i: (b, 1))],
        out_specs=pl.BlockSpec((tq, D), lambda b, i: (b * nq + i, 0)),
        out_shape=jax.ShapeDtypeStruct((T, D), F32),
        compiler_params=_cparams(("parallel", "parallel")),
        name="cross_attn",
    )(q, kv, kv)
    return _matmul(o, wo.astype(BF16), residual=x2).reshape(B, S, D)


MOE_TM = 256
MOE_NEG = -1e30


def _moe_route_kernel(x_ref, nw_ref, wr_ref, br_ref, xn_ref, ri_ref, rw_ref, cnt_ref, cnt_scr, *, tm):
    @pl.when(pl.program_id(0) == 0)
    def _():
        cnt_scr[...] = jnp.zeros_like(cnt_scr)

    x = x_ref[...]
    xn = x * lax.rsqrt(jnp.mean(x * x, axis=-1, keepdims=True) + EPS) * nw_ref[...]
    xn_ref[...] = xn
    logits = jnp.dot(xn, wr_ref[...], preferred_element_type=F32, precision=lax.Precision.HIGHEST) + br_ref[...]
    lane = lax.broadcasted_iota(jnp.int32, (tm, LANES), 1)
    gmask = lane < MOE_GROUPS
    lg = jnp.where(gmask, logits, MOE_NEG)
    gmax = jnp.max(lg, axis=-1, keepdims=True)
    grp = jnp.min(jnp.where(lg == gmax, lane, LANES), axis=-1, keepdims=True)
    p_grp = 1.0 / jnp.sum(jnp.where(gmask, jnp.exp(lg - gmax), 0.0), axis=-1, keepdims=True)
    lo = MOE_GROUPS + grp * MOE_PER_GROUP
    le = jnp.where((lane >= lo) & (lane < lo + MOE_PER_GROUP), logits, MOE_NEG)
    t1 = jnp.max(le, axis=-1, keepdims=True)
    i1 = jnp.min(jnp.where(le == t1, lane, LANES), axis=-1, keepdims=True)
    le2 = jnp.where(lane == i1, MOE_NEG, le)
    t2 = jnp.max(le2, axis=-1, keepdims=True)
    i2 = jnp.min(jnp.where(le2 == t2, lane, LANES), axis=-1, keepdims=True)
    e21 = jnp.exp(t2 - t1)
    w1 = p_grp / (1.0 + e21)
    w2 = w1 * e21
    e1 = i1 - MOE_GROUPS
    e2 = i2 - MOE_GROUPS
    oh1 = jnp.where(lane == e1, 1.0, 0.0)
    oh2 = jnp.where(lane == e2, 1.0, 0.0)
    both = oh1 + oh2
    r_i = lax.broadcasted_iota(jnp.int32, (tm, tm), 0)
    c_i = lax.broadcasted_iota(jnp.int32, (tm, tm), 1)
    ltri = jnp.where(c_i < r_i, 1.0, 0.0).astype(BF16)
    before = jnp.dot(ltri, both.astype(BF16), preferred_element_type=F32) + cnt_scr[...]
    pos1 = jnp.sum(oh1 * before, axis=-1, keepdims=True).astype(jnp.int32)
    pos2 = jnp.sum(oh2 * before, axis=-1, keepdims=True).astype(jnp.int32)
    cnt_scr[...] = cnt_scr[...] + jnp.sum(both, axis=0, keepdims=True)
    ri_ref[...] = jnp.where(lane == 0, e1, jnp.where(lane == 1, e2, jnp.where(lane == 2, pos1,
                            jnp.where(lane == 3, pos2, 0))))
    rw_ref[...] = jnp.where(lane == 0, w1, jnp.where(lane == 1, w2, 0.0))
    cnt_ref[...] = cnt_scr[...]


def _moe_dest(tok, e1_ref, e2_ref, p1_ref, p2_ref, off_ref):
    return off_ref[e1_ref[tok]] + p1_ref[tok], off_ref[e2_ref[tok]] + p2_ref[tok]


def _moe_rowmap_kernel(e1_ref, e2_ref, p1_ref, p2_ref, off_ref, cnt_ref, rt_ref, *, n_tok, rows):
    def clear(i, carry):
        rt_ref[i] = 0
        return carry

    def clear_pad(e, carry):
        used = off_ref[e] + cnt_ref[e]
        end = jnp.where(e + 1 < MOE_EXPERTS, off_ref[jnp.minimum(e + 1, MOE_EXPERTS - 1)], rows)
        lax.fori_loop(used, end, clear, 0)
        return carry

    def place(t, carry):
        d1, d2 = _moe_dest(t, e1_ref, e2_ref, p1_ref, p2_ref, off_ref)
        rt_ref[d1] = t
        rt_ref[d2] = t
        return carry

    lax.fori_loop(0, MOE_EXPERTS, clear_pad, 0)
    lax.fori_loop(0, n_tok, place, 0, unroll=8)


def _moe_expert_kernel(te_ref, nu_ref, rt_ref, xn_hbm, wg_ref, wu_ref, wd_ref, y_ref, xbuf, sems):
    del te_ref
    i = pl.program_id(0)
    nu = nu_ref[0]
    slot = i % 2

    def row_copy(tile, r, s):
        return pltpu.make_async_copy(xn_hbm.at[pl.ds(rt_ref[tile * MOE_TM + r], 1)], xbuf.at[s, pl.ds(r, 1)],
                                     sems.at[s])

    def drain(s):
        def body(r, carry):
            row_copy(0, 0, s).wait()
            return carry
        lax.fori_loop(0, MOE_TM, body, 0, unroll=8)

    @pl.when(i == 0)
    def _():
        def body(r, carry):
            row_copy(0, r, 0).start()
            return carry
        lax.fori_loop(0, MOE_TM, body, 0, unroll=8)

    @pl.when(i < nu)
    def _():
        drain(slot)
        nxt = jnp.minimum(i + 1, nu - 1)
        for r in range(MOE_TM):
            row_copy(nxt, r, 1 - slot).start()
        x = xbuf[slot].astype(BF16)
        gate = jnp.dot(x, wg_ref[...].astype(BF16), preferred_element_type=F32)
        up = jnp.dot(x, wu_ref[...].astype(BF16), preferred_element_type=F32)
        hid = gate * jax.nn.sigmoid(gate) * up
        y_ref[...] = jnp.dot(hid.astype(BF16), wd_ref[...].astype(BF16), preferred_element_type=F32)

        @pl.when(i == nu - 1)
        def _():
            drain(1 - slot)

    @pl.when(i >= nu)
    def _():
        y_ref[...] = jnp.zeros_like(y_ref)


def _moe_combine_kernel(e1_ref, e2_ref, p1_ref, p2_ref, off_ref, x_ref, rw_ref, ys_hbm, o_ref, buf1, buf2, sems, *, tm):
    i = pl.program_id(0)
    slot = i % 2

    def row_copy(src, t, buf, s):
        return pltpu.make_async_copy(ys_hbm.at[pl.ds(src, 1)], buf.at[s, pl.ds(t, 1)], sems.at[s])

    def issue_tile(tile, s):
        def body(t, carry):
            d1, d2 = _moe_dest(tile * tm + t, e1_ref, e2_ref, p1_ref, p2_ref, off_ref)
            row_copy(d1, t, buf1, s).start(priority=0)
            row_copy(d2, t, buf2, s).start(priority=1)
            return carry
        lax.fori_loop(0, tm, body, 0, unroll=4)

    def drain(t, carry):
        row_copy(0, 0, buf1, slot).wait()
        row_copy(0, 0, buf2, slot).wait()
        return carry

    @pl.when(i == 0)
    def _():
        issue_tile(0, 0)

    lax.fori_loop(0, tm, drain, 0, unroll=8)

    @pl.when(i + 1 < pl.num_programs(0))
    def _():
        issue_tile(i + 1, 1 - slot)

    w = rw_ref[...]
    o_ref[...] = x_ref[...] + w[:, 0:1] * buf1[slot] + w[:, 1:2] * buf2[slot]


def _hier_moe(x, norm_w, wg, bg, we, be, w_gate, w_up, w_down, layer):
    B, S, D = x.shape
    T = B * S
    x2 = x.reshape(T, D)
    FF = w_gate.shape[-1]
    tm_r = 512
    wr = jnp.pad(jnp.concatenate([wg, we], axis=1), ((0, 0), (0, LANES - MOE_GROUPS - MOE_EXPERTS)))
    br = jnp.pad(jnp.concatenate([bg, be]), (0, LANES - MOE_GROUPS - MOE_EXPERTS)).reshape(1, LANES)
    xn, ri, rw, cnt = pl.pallas_call(
        functools.partial(_moe_route_kernel, tm=tm_r),
        grid=(T // tm_r,),
        in_specs=[pl.BlockSpec((tm_r, D), lambda i: (i, 0)), pl.BlockSpec((1, D), lambda i: (0, 0)),
                  pl.BlockSpec((D, LANES), lambda i: (0, 0)), pl.BlockSpec((1, LANES), lambda i: (0, 0))],
        out_specs=[pl.BlockSpec((tm_r, D), lambda i: (i, 0)), pl.BlockSpec((tm_r, LANES), lambda i: (i, 0)),
                   pl.BlockSpec((tm_r, LANES), lambda i: (i, 0)), pl.BlockSpec((1, LANES), lambda i: (0, 0))],
        out_shape=[jax.ShapeDtypeStruct((T, D), F32), jax.ShapeDtypeStruct((T, LANES), jnp.int32),
                   jax.ShapeDtypeStruct((T, LANES), F32), jax.ShapeDtypeStruct((1, LANES), F32)],
        scratch_shapes=[pltpu.VMEM((1, LANES), F32)],
        compiler_params=_cparams(("arbitrary",)),
        name="moe_route",
    )(x2, norm_w.reshape(1, D), wr, br)

    counts = cnt[0, :MOE_EXPERTS].astype(jnp.int32)
    padded = (counts + MOE_TM - 1) // MOE_TM * MOE_TM
    ends = jnp.cumsum(padded)
    off = (ends - padded).astype(jnp.int32)
    n_tiles = (T * MOE_TOPK) // MOE_TM + MOE_EXPERTS
    rows = n_tiles * MOE_TM
    n_used = (ends[-1] // MOE_TM).astype(jnp.int32).reshape(1)
    tile_e = jnp.minimum(jnp.searchsorted(ends, jnp.arange(n_tiles, dtype=jnp.int32) * MOE_TM, side='right'),
                         MOE_EXPERTS - 1).astype(jnp.int32)
    e1, e2, p1, p2 = ri[:, 0], ri[:, 1], ri[:, 2], ri[:, 3]

    row_tok = pl.pallas_call(
        functools.partial(_moe_rowmap_kernel, n_tok=T, rows=rows),
        grid_spec=pltpu.PrefetchScalarGridSpec(
            num_scalar_prefetch=6, grid=(1,), in_specs=[],
            out_specs=pl.BlockSpec(memory_space=pltpu.SMEM)),
        out_shape=jax.ShapeDtypeStruct((rows,), jnp.int32),
        compiler_params=_cparams(("arbitrary",)),
        name="moe_rowmap",
    )(e1, e2, p1, p2, off, counts)

    def w_ix(i, te, nu, rt):
        return (layer, te[jnp.minimum(i, nu[0] - 1)], 0, 0)

    ys = pl.pallas_call(
        _moe_expert_kernel,
        grid_spec=pltpu.PrefetchScalarGridSpec(
            num_scalar_prefetch=3, grid=(n_tiles,),
            in_specs=[pl.BlockSpec(memory_space=pl.ANY),
                      pl.BlockSpec((None, None, D, FF), w_ix), pl.BlockSpec((None, None, D, FF), w_ix),
                      pl.BlockSpec((None, None, FF, D), w_ix)],
            out_specs=pl.BlockSpec((MOE_TM, D), lambda i, te, nu, rt: (i, 0)),
            scratch_shapes=[pltpu.VMEM((2, MOE_TM, D), F32), pltpu.SemaphoreType.DMA((2,))]),
        out_shape=jax.ShapeDtypeStruct((rows, D), F32),
        compiler_params=_cparams(("arbitrary",)),
        name="moe_experts",
    )(tile_e, n_used, row_tok, xn, w_gate, w_up, w_down)

    tm_c = 256
    out = pl.pallas_call(
        functools.partial(_moe_combine_kernel, tm=tm_c),
        grid_spec=pltpu.PrefetchScalarGridSpec(
            num_scalar_prefetch=5, grid=(T // tm_c,),
            in_specs=[pl.BlockSpec((tm_c, D), lambda i, *_: (i, 0)), pl.BlockSpec((tm_c, LANES), lambda i, *_: (i, 0)),
                      pl.BlockSpec(memory_space=pl.ANY)],
            out_specs=pl.BlockSpec((tm_c, D), lambda i, *_: (i, 0)),
            scratch_shapes=[pltpu.VMEM((2, tm_c, D), F32), pltpu.VMEM((2, tm_c, D), F32),
                            pltpu.SemaphoreType.DMA((2,))]),
        out_shape=jax.ShapeDtypeStruct((T, D), F32),
        compiler_params=_cparams(("arbitrary",)),
        name="moe_combine",
    )(e1, e2, p1, p2, off, x2, rw, ys)
    return out.reshape(B, S, D)


def kernel(x, mem, norm_mix, norm_cross, norm_mem, norm_ffn, norm_final,
           ab_w_in, ml_conv_w, ml_conv_b, ml_gate_b, rw_mu, rw_w0, rw_w_up, rw_a0, rw_a_up, rw_g_up,
           rw_k_k, rw_k_a, rw_r_k, rw_ln_w, rw_ln_b, ab_w_out,
           nsa_w_in, nsa_gate_b, cmp_pos, cmp_w1, cmp_w2, nsa_w_out,
           ca_wq, ca_wk, ca_wv, ca_wo,
           moe_wg, moe_bg, moe_we, moe_be, moe_w_gate, moe_w_up, moe_w_down):
    B, S, D = x.shape
    for l in range(DEPTH):
        j = l // 2
        if l % 2 == 0:
            x = _ab_mixer(x, norm_mix[l], ab_w_in[j], ml_conv_w[j], ml_conv_b[j], ml_gate_b[j], rw_mu[j], rw_w0[j],
                          rw_w_up[j], rw_a0[j], rw_a_up[j], rw_g_up[j], rw_k_k[j], rw_k_a[j], rw_r_k[j],
                          rw_ln_w[j], rw_ln_b[j], ab_w_out[j])
        else:
            x = _nsa(x, norm_mix[l], nsa_w_in[j], nsa_gate_b[j], cmp_pos[j], cmp_w1[j], cmp_w2[j], nsa_w_out[j])
        x = _cross_attn(x, norm_cross[l], mem, norm_mem[l], ca_wq[l], ca_wk[l], ca_wv[l], ca_wo[l])
        x = _hier_moe(x, norm_ffn[l], moe_wg[l], moe_bg[l], moe_we[l], moe_be[l],
                      moe_w_gate, moe_w_up, moe_w_down, l)
    return _rmsnorm_rows(x.reshape(B * S, D), norm_final).reshape(B, S, D)
```
